```python
import jax, jax.numpy as jnp
from jax import lax
import numpy as np

D_MODEL = 4096
BATCH = 2
SEQ = 4096
DEPTH = 2

GRID_W = 64
CTX_LEN = 256
CHUNK = 128
HEAD_DIM = 128
EPS = 1e-6
ROPE_THETA = 10000.0
A_HEADS = 8
A_WIDTH = A_HEADS * HEAD_DIM
B_HEADS = 12
B_KV_HEADS = 4
B_GROUP = B_HEADS // B_KV_HEADS
B_WIDTH = B_HEADS * HEAD_DIM
C_HEADS = 12
C_Q_RANK = 768
C_KV_RANK = 512
C_NOPE = 128
C_ROPE = 64
C_V = 128
C_QK = C_NOPE + C_ROPE
C_WIDTH = C_HEADS * C_V
D_MIX = A_WIDTH + B_WIDTH + C_WIDTH
IN_SIZES = (A_WIDTH, A_WIDTH, B_WIDTH, B_KV_HEADS * HEAD_DIM, B_KV_HEADS * HEAD_DIM, C_Q_RANK, C_KV_RANK, C_ROPE)
IN_COLS = sum(IN_SIZES)
N_EXPERTS = 64
TOP_K = 6
N_GROUPS = 8
TOPK_GROUPS = 4
D_EXPERT = 256
D_SHARED = 256
ROUTED_SCALE = 2.5
MOE_BLOCK = 128
N_MOD = 6

kernel_name = 'hybrid_dit_hgroups_gmlp_gqa_mla_moe'


def rmsnorm(x, w):
    xf = x.astype(jnp.float32)
    y = xf * lax.rsqrt(jnp.mean(xf * xf, axis=-1, keepdims=True) + EPS)
    return (y * w.astype(jnp.float32)).astype(x.dtype)


def modulate(x, shift, scale):
    return x * (1.0 + scale) + shift


def grid_positions(n_lat):
    rows = n_lat // GRID_W
    t = jnp.arange(n_lat, dtype=jnp.int32)
    return t // GRID_W - rows // 2, t % GRID_W - GRID_W // 2


def rope_2d(x, row, col):
    d = x.shape[-1]
    half = d // 2
    nf = half // 2
    freq = ROPE_THETA ** (-jnp.arange(nf, dtype=jnp.float32) / nf)

    def rot(xh, pos):
        ang = pos.astype(jnp.float32)[:, None] * freq
        cos = jnp.cos(ang)[None, :, None, :]
        sin = jnp.sin(ang)[None, :, None, :]
        x1, x2 = xh[..., :nf], xh[..., nf:]
        return jnp.concatenate([x1 * cos - x2 * sin, x1 * sin + x2 * cos], axis=-1)

    xf = x.astype(jnp.float32)
    out = jnp.concatenate([rot(xf[..., :half], row), rot(xf[..., half:], col)], axis=-1)
    return out.astype(x.dtype)


def attend(q, k, v):
    scale = q.shape[-1] ** -0.5
    s = jnp.einsum('bqkgd,bskd->bkgqs', q, k).astype(jnp.float32) * scale
    p = jax.nn.softmax(s, axis=-1).astype(v.dtype)
    return jnp.einsum('bkgqs,bskd->bqkgd', p, v)


def attend_blocked(q, k, v):
    b, sq = q.shape[:2]
    nb = sq // CHUNK
    qb = q.reshape(b, nb, CHUNK, *q.shape[2:]).swapaxes(0, 1)
    ob = lax.map(lambda qi: attend(qi, k, v), qb)
    return ob.swapaxes(0, 1).reshape(b, sq, *ob.shape[3:])


def chunk_gmlp(u, v, v_norm, w_s, b_s):
    u = jax.nn.gelu(u)
    v = rmsnorm(jax.nn.gelu(v), v_norm)
    b, n = v.shape[:2]
    vc = v.reshape(b, n // CHUNK, CHUNK, A_HEADS, HEAD_DIM)
    s = jnp.einsum('hpq,bcqhd->bcphd', w_s, vc) + jnp.transpose(b_s)[None, None, :, :, None]
    return u * s.reshape(b, n, A_HEADS, HEAD_DIM)


def token_mixer(h, n_ctx, row, col, need_ctx, w_in, gm_v_norm, gm_w_s, gm_b_s,
                gqa_q_norm, gqa_k_norm, mla_q_a_norm, mla_w_uq, mla_kv_a_norm, mla_w_ukv,
                mla_q_norm, mla_k_norm, w_out):
    b, n, _ = h.shape
    proj = jnp.einsum('bnd,dc->bnc', h, w_in)
    u, v, qb, kb, vb, cq, ckv, kr = jnp.split(proj, [int(o) for o in np.cumsum(IN_SIZES)[:-1]], axis=-1)

    def heads(t, nh):
        return t.reshape(b, n, nh, -1)

    def gq(t):
        return t.reshape(t.shape[0], t.shape[1], B_KV_HEADS, B_GROUP, HEAD_DIM)

    def rope_tail(t):
        return jnp.concatenate([t[..., :C_NOPE], rope_2d(t[..., C_NOPE:], row, col)], axis=-1)

    u = heads(u, A_HEADS)
    v = heads(v, A_HEADS)
    a_lat = chunk_gmlp(u[:, n_ctx:], v[:, n_ctx:], gm_v_norm, gm_w_s, gm_b_s)

    qg = rmsnorm(heads(qb, B_HEADS), gqa_q_norm)
    kg = rmsnorm(heads(kb, B_KV_HEADS), gqa_k_norm)
    vg = heads(vb, B_KV_HEADS)
    kg_all = jnp.concatenate([kg[:, :n_ctx], rope_2d(kg[:, n_ctx:], row, col)], axis=1)
    b_lat = attend_blocked(gq(rope_2d(qg[:, n_ctx:], row, col)), kg_all, vg)

    qc = jnp.einsum('bnr,rc->bnc', rmsnorm(cq, mla_q_a_norm), mla_w_uq).reshape(b, n, C_HEADS, C_QK)
    qc = rmsnorm(qc, mla_q_norm)
    kv = jnp.einsum('bnr,rc->bnc', rmsnorm(ckv, mla_kv_a_norm), mla_w_ukv).reshape(b, n, C_HEADS, C_NOPE + C_V)
    k_nope, vc = kv[..., :C_NOPE], kv[..., C_NOPE:]
    k_rope = jnp.broadcast_to(kr[:, :, None, :], (b, n, C_HEADS, C_ROPE))
    kc = rmsnorm(jnp.concatenate([k_nope, k_rope], axis=-1), mla_k_norm)
    kc_all = jnp.concatenate([kc[:, :n_ctx], rope_tail(kc[:, n_ctx:])], axis=1)
    c_lat = attend_blocked(rope_tail(qc[:, n_ctx:])[:, :, :, None, :], kc_all, vc)

    def merge(a, bb, cc):
        cat = jnp.concatenate([a.reshape(a.shape[0], a.shape[1], -1),
                               bb.reshape(bb.shape[0], bb.shape[1], -1),
                               cc.reshape(cc.shape[0], cc.shape[1], -1)], axis=-1)
        return jnp.einsum('bnm,md->bnd', cat, w_out)

    o_lat = merge(a_lat, b_lat, c_lat)
    o_ctx = None
    if need_ctx:
        a_ctx = chunk_gmlp(u[:, :n_ctx], v[:, :n_ctx], gm_v_norm, gm_w_s, gm_b_s)
        b_ctx = attend(gq(qg[:, :n_ctx]), kg[:, :n_ctx], vg[:, :n_ctx])
        c_ctx_o = attend(qc[:, :n_ctx, :, None, :], kc[:, :n_ctx], vc[:, :n_ctx])
        o_ctx = merge(a_ctx, b_ctx, c_ctx_o)
    return o_ctx, o_lat


def moe(x2d, router_w, router_bias, w_gate, w_up, w_down, sh_gate, sh_up, sh_down):
    t = x2d.shape[0]
    logits = jnp.einsum('td,de->te', x2d.astype(jnp.float32), router_w.astype(jnp.float32))
    scores = jax.nn.sigmoid(logits)
    sel = scores + router_bias.astype(jnp.float32)
    gscore = lax.top_k(sel.reshape(t, N_GROUPS, N_EXPERTS // N_GROUPS), 2)[0].sum(-1)
    _, gidx = lax.top_k(gscore, TOPK_GROUPS)
    gmask = jax.nn.one_hot(gidx, N_GROUPS, dtype=jnp.float32).sum(1) > 0
    emask = jnp.repeat(gmask, N_EXPERTS // N_GROUPS, axis=1)
    _, eidx = lax.top_k(jnp.where(emask, sel, -jnp.inf), TOP_K)
    w = jnp.take_along_axis(scores, eidx, axis=-1)
    w = w / jnp.sum(w, axis=-1, keepdims=True) * ROUTED_SCALE
    gates = jnp.einsum('tk,tke->te', w, jax.nn.one_hot(eidx, N_EXPERTS, dtype=jnp.float32)).astype(x2d.dtype)

    def block(args):
        xb, gb = args
        hg = jnp.einsum('td,edf->tef', xb, w_gate)
        hu = jnp.einsum('td,edf->tef', xb, w_up)
        hb = jax.nn.silu(hg) * hu * gb[:, :, None]
        return jnp.einsum('tef,efd->td', hb, w_down)

    nb = t // MOE_BLOCK
    routed = lax.map(block, (x2d.reshape(nb, MOE_BLOCK, -1), gates.reshape(nb, MOE_BLOCK, -1))).reshape(t, -1)
    shared = (jax.nn.silu(x2d @ sh_gate) * (x2d @ sh_up)) @ sh_down
    return routed + shared


def setup_inputs(seed: int = 0) -> dict:
    key = jax.random.key(seed)
    ks = iter(jax.random.split(key, 32))
    L, D = DEPTH, D_MODEL

    def nrm(shape, scale):
        return jax.random.normal(next(ks), shape, jnp.float32) * scale

    def gain(shape):
        return 1.0 + nrm(shape, 0.05)

    return {
        'x': nrm((BATCH, SEQ, D), 1.0),
        'c': nrm((BATCH, D), 1.0),
        'ctx': nrm((BATCH, CTX_LEN, D), 1.0),
        'c_ctx': nrm((D,), 1.0),
        'ada_w': nrm((L, D, N_MOD * D), 0.5 * D ** -0.5),
        'ada_b': nrm((L, N_MOD * D), 0.02),
        'norm1_w': gain((L, D)),
        'norm2_w': gain((L, D)),
        'w_in': nrm((L, D, IN_COLS), D ** -0.5),
        'gm_v_norm': gain((L, A_HEADS, HEAD_DIM)),
        'gm_w_s': nrm((L, A_HEADS, CHUNK, CHUNK), CHUNK ** -0.5),
        'gm_b_s': 1.0 + nrm((L, A_HEADS, CHUNK), 0.05),
        'gqa_q_norm': gain((L, HEAD_DIM)),
        'gqa_k_norm': gain((L, HEAD_DIM)),
        'mla_q_a_norm': gain((L, C_Q_RANK)),
        'mla_w_uq': nrm((L, C_Q_RANK, C_HEADS * C_QK), C_Q_RANK ** -0.5),
        'mla_kv_a_norm': gain((L, C_KV_RANK)),
        'mla_w_ukv': nrm((L, C_KV_RANK, C_HEADS * (C_NOPE + C_V)), C_KV_RANK ** -0.5),
        'mla_q_norm': gain((L, C_QK)),
        'mla_k_norm': gain((L, C_QK)),
        'w_out': nrm((L, D_MIX, D), D_MIX ** -0.5),
        'router_w': nrm((L, D, N_EXPERTS), D ** -0.5),
        'router_bias': nrm((L, N_EXPERTS), 0.01),
        'exp_w_gate': nrm((L, N_EXPERTS, D, D_EXPERT), D ** -0.5),
        'exp_w_up': nrm((L, N_EXPERTS, D, D_EXPERT), D ** -0.5),
        'exp_w_down': nrm((L, N_EXPERTS, D_EXPERT, D), D_EXPERT ** -0.5),
        'sh_w_gate': nrm((L, D, D_SHARED), D ** -0.5),
        'sh_w_up': nrm((L, D, D_SHARED), D ** -0.5),
        'sh_w_down': nrm((L, D_SHARED, D), D_SHARED ** -0.5),
    }


def reference(x, c, ctx, c_ctx, ada_w, ada_b, norm1_w, norm2_w, w_in, gm_v_norm, gm_w_s, gm_b_s,
              gqa_q_norm, gqa_k_norm, mla_q_a_norm, mla_w_uq, mla_kv_a_norm, mla_w_ukv,
              mla_q_norm, mla_k_norm, w_out, router_w, router_bias, exp_w_gate, exp_w_up, exp_w_down,
              sh_w_gate, sh_w_up, sh_w_down):
    b, n_lat, d = x.shape
    n_ctx = ctx.shape[1]
    row, col = grid_positions(n_lat)
    x_lat, x_ctx = x, ctx
    for l in range(DEPTH):
        need_ctx = l < DEPTH - 1
        mod_lat = (jnp.einsum('bd,de->be', jax.nn.silu(c), ada_w[l]) + ada_b[l]).reshape(b, N_MOD, 1, d)
        mod_ctx = (jnp.einsum('d,de->e', jax.nn.silu(c_ctx), ada_w[l]) + ada_b[l]).reshape(N_MOD, 1, 1, d)

        h = jnp.concatenate([modulate(rmsnorm(x_ctx, norm1_w[l]), mod_ctx[0], mod_ctx[1]),
                             modulate(rmsnorm(x_lat, norm1_w[l]), mod_lat[:, 0], mod_lat[:, 1])], axis=1)
        o_ctx, o_lat = token_mixer(h, n_ctx, row, col, need_ctx, w_in[l], gm_v_norm[l], gm_w_s[l], gm_b_s[l],
                                   gqa_q_norm[l], gqa_k_norm[l], mla_q_a_norm[l], mla_w_uq[l],
                                   mla_kv_a_norm[l], mla_w_ukv[l], mla_q_norm[l], mla_k_norm[l], w_out[l])
        x_lat = x_lat + mod_lat[:, 2] * o_lat
        if need_ctx:
            x_ctx = x_ctx + mod_ctx[2] * o_ctx

        h_lat = modulate(rmsnorm(x_lat, norm2_w[l]), mod_lat[:, 3], mod_lat[:, 4])
        moe_w = (router_w[l], router_bias[l], exp_w_gate[l], exp_w_up[l], exp_w_down[l],
                 sh_w_gate[l], sh_w_up[l], sh_w_down[l])
        if need_ctx:
            h_ctx = modulate(rmsnorm(x_ctx, norm2_w[l]), mod_ctx[3], mod_ctx[4])
            f = moe(jnp.concatenate([h_ctx, h_lat], axis=1).reshape(-1, d), *moe_w).reshape(b, n_ctx + n_lat, d)
            x_ctx = x_ctx + mod_ctx[5] * f[:, :n_ctx]
            f_lat = f[:, n_ctx:]
        else:
            f_lat = moe(h_lat.reshape(-1, d), *moe_w).reshape(b, n_lat, d)
        x_lat = x_lat + mod_lat[:, 5] * f_lat
    return x_lat
```

```python
import functools

import jax
import jax.numpy as jnp
from jax import lax
from jax.experimental import pallas as pl
from jax.experimental.pallas import tpu as pltpu

F32 = jnp.float32
BF16 = jnp.bfloat16
U32 = jnp.uint32
I32 = jnp.int32

GRID_W = 64
CHUNK = 128
HEAD_DIM = 128
EPS = 1e-6
ROPE_THETA = 10000.0
A_HEADS = 8
B_HEADS = 12
B_KV_HEADS = 4
B_GROUP = B_HEADS // B_KV_HEADS
C_HEADS = 12
C_Q_RANK = 768
C_KV_RANK = 512
C_NOPE = 128
C_ROPE = 64
C_V = 128
C_QK = C_NOPE + C_ROPE
A_WIDTH = A_HEADS * HEAD_DIM
B_WIDTH = B_HEADS * HEAD_DIM
C_WIDTH = C_HEADS * C_V
N_EXPERTS = 64
TOP_K = 6
N_GROUPS = 8
GROUP_SIZE = N_EXPERTS // N_GROUPS
TOPK_GROUPS = 4
ROUTED_SCALE = 2.5
N_MOD = 6

LANES = 128
SUBLANES = 8
ROW_TILE = 256
MM_TM = 512
MM_TN = 1024
ATT_TQ = 256
ATT_TK = 512
EXP_TM = 256
DISP_T = 256
COMB_T = 128
IN_COLS_PAD = 6144
VMEM_LIMIT = 56 * 1024 * 1024


def _params(sem, vmem=VMEM_LIMIT):
    return pltpu.CompilerParams(dimension_semantics=sem, vmem_limit_bytes=vmem)


def _rms(x, w):
    return x * lax.rsqrt(jnp.mean(x * x, axis=-1, keepdims=True) + EPS) * w


def _pack_bf16_pair(a, b):
    ua = lax.bitcast_convert_type(a.astype(BF16).astype(F32), U32)
    ub = lax.bitcast_convert_type(b.astype(BF16).astype(F32), U32)
    return (ua >> 16) | ub


def _unpack_bf16_pair(u):
    a = lax.bitcast_convert_type(u << 16, F32)
    b = lax.bitcast_convert_type(u & jnp.uint32(0xFFFF0000), F32)
    return a, b


def _ada_kernel(cb_ref, w_ref, b_ref, o_ref, *, n_rows, d, tn):
    nj = tn // LANES

    def body(kc, accs):
        k0 = pl.multiple_of(kc * SUBLANES, SUBLANES)
        wk = w_ref[0, pl.ds(k0, SUBLANES), :]
        out = []
        for r in range(n_rows):
            cv = cb_ref[r, pl.ds(k0, SUBLANES), :]
            sv = cv * jax.nn.sigmoid(cv)
            for j in range(nj):
                out.append(accs[r * nj + j] + wk[:, j * LANES:(j + 1) * LANES] * sv)
        return tuple(out)

    init = tuple(jnp.zeros((SUBLANES, LANES), F32) for _ in range(n_rows * nj))
    accs = lax.fori_loop(0, d // SUBLANES, body, init, unroll=4)
    o_ref[0] = jnp.zeros((SUBLANES, tn), F32)
    for r in range(n_rows):
        for j in range(nj):
            o_ref[0, r:r + 1, j * LANES:(j + 1) * LANES] = (
                jnp.sum(accs[r * nj + j], axis=0, keepdims=True) + b_ref[0, :, j * LANES:(j + 1) * LANES])


def _ada_mods(c, c_ctx, ada_w, ada_b):
    n_layers, d, nd = ada_w.shape
    n_rows = c.shape[0] + 1
    assert n_rows <= SUBLANES
    cvec = jnp.concatenate([c, c_ctx[None]], axis=0)
    cb = jnp.broadcast_to(cvec[:, :, None], (n_rows, d, LANES))
    tn = 512
    out = pl.pallas_call(
        functools.partial(_ada_kernel, n_rows=n_rows, d=d, tn=tn),
        grid=(n_layers, nd // tn),
        in_specs=[pl.BlockSpec((n_rows, d, LANES), lambda l, j: (0, 0, 0)),
                  pl.BlockSpec((1, d, tn), lambda l, j: (l, 0, j)),
                  pl.BlockSpec((1, 1, tn), lambda l, j: (l, 0, j))],
        out_specs=pl.BlockSpec((1, SUBLANES, tn), lambda l, j: (l, 0, j)),
        out_shape=jax.ShapeDtypeStruct((n_layers, SUBLANES, nd), F32),
        compiler_params=_params(("arbitrary", "arbitrary")),
        name="ada_mods",
    )(cb, ada_w, ada_b.reshape(n_layers, 1, nd))
    return out.reshape(n_layers, SUBLANES, N_MOD, 1, d)


def _mod_spec(layer, m, d, row_fn, tn=None, col_fn=None):
    if tn is None:
        return pl.BlockSpec((None, None, None, 1, d), lambda *a: (layer, row_fn(*a), m, 0, 0))
    return pl.BlockSpec((None, None, None, 1, tn), lambda *a: (layer, row_fn(*a), m, 0, col_fn(*a)))


def _norm_mod_kernel(x_ref, nw_ref, shift_ref, scale_ref, o_ref):
    y = _rms(x_ref[...], nw_ref[...])
    o_ref[...] = (y * (1.0 + scale_ref[...]) + shift_ref[...]).astype(o_ref.dtype)


def _norm_mod(x, norm_w, mods, layer, m_shift, m_scale, mod_row):
    t, d = x.shape
    return pl.pallas_call(
        _norm_mod_kernel,
        grid=(t // ROW_TILE,),
        in_specs=[pl.BlockSpec((ROW_TILE, d), lambda i: (i, 0)),
                  pl.BlockSpec((1, d), lambda i: (0, 0)),
                  _mod_spec(layer, m_shift, d, mod_row),
                  _mod_spec(layer, m_scale, d, mod_row)],
        out_specs=pl.BlockSpec((ROW_TILE, d), lambda i: (i, 0)),
        out_shape=jax.ShapeDtypeStruct((t, d), BF16),
        compiler_params=_params(("arbitrary",)),
        name="norm_mod",
    )(x, norm_w.reshape(1, d), mods, mods)


def _mm_kernel(x_ref, w_ref, o_ref):
    o_ref[...] = jnp.dot(x_ref[...], w_ref[...], preferred_element_type=F32).astype(o_ref.dtype)


def _matmul(x, w, out_dtype=BF16):
    m, k = x.shape
    n = w.shape[1]
    return pl.pallas_call(
        _mm_kernel,
        grid=(n // MM_TN, m // MM_TM),
        in_specs=[pl.BlockSpec((MM_TM, k), lambda j, i: (i, 0)),
                  pl.BlockSpec((k, MM_TN), lambda j, i: (0, j))],
        out_specs=pl.BlockSpec((MM_TM, MM_TN), lambda j, i: (i, j)),
        out_shape=jax.ShapeDtypeStruct((m, n), out_dtype),
        compiler_params=_params(("arbitrary", "arbitrary")),
        name="in_proj",
    )(x, w)


def _gmlp_kernel(p_ref, vn_ref, ws_ref, bt_ref, o_ref):
    for h in range(A_HEADS):
        hs = slice(h * HEAD_DIM, (h + 1) * HEAD_DIM)
        u = jax.nn.gelu(p_ref[:, hs].astype(F32))
        v = jax.nn.gelu(p_ref[:, A_WIDTH + h * HEAD_DIM:A_WIDTH + (h + 1) * HEAD_DIM].astype(F32))
        vb = _rms(v, vn_ref[:, hs]).astype(BF16)
        for c in range(ROW_TILE // CHUNK):
            cs = slice(c * CHUNK, (c + 1) * CHUNK)
            s = jnp.dot(ws_ref[h], vb[cs], preferred_element_type=F32) + bt_ref[:, h:h + 1]
            o_ref[cs, hs] = (u[cs] * s).astype(o_ref.dtype)


def _gmlp(proj, t_rows, v_norm, w_s, b_s):
    return pl.pallas_call(
        _gmlp_kernel,
        grid=(t_rows // ROW_TILE,),
        in_specs=[pl.BlockSpec((ROW_TILE, 2 * A_WIDTH), lambda i: (i, 0)),
                  pl.BlockSpec((1, A_WIDTH), lambda i: (0, 0)),
                  pl.BlockSpec((A_HEADS, CHUNK, CHUNK), lambda i: (0, 0, 0)),
                  pl.BlockSpec((CHUNK, A_HEADS), lambda i: (0, 0))],
        out_specs=pl.BlockSpec((ROW_TILE, A_WIDTH), lambda i: (i, 0)),
        out_shape=jax.ShapeDtypeStruct((t_rows, A_WIDTH), BF16),
        compiler_params=_params(("arbitrary",)),
        name="gmlp",
    )(proj, v_norm.reshape(1, A_WIDTH), w_s.astype(BF16), jnp.transpose(b_s))


def _rope_tables(n_lat, n_ctx, d_rot, reps):
    half = d_rot // 2
    nf = half // 2
    rows = n_lat // GRID_W
    tpos = jnp.arange(n_lat, dtype=I32)
    row = tpos // GRID_W - rows // 2
    col = tpos % GRID_W - GRID_W // 2
    freq = ROPE_THETA ** (-jnp.arange(nf, dtype=F32) / nf)
    ang_r = row.astype(F32)[:, None] * freq
    ang_c = col.astype(F32)[:, None] * freq
    cos = jnp.concatenate([jnp.cos(ang_r), jnp.cos(ang_r), jnp.cos(ang_c), jnp.cos(ang_c)], axis=1)
    sin = jnp.concatenate([-jnp.sin(ang_r), jnp.sin(ang_r), -jnp.sin(ang_c), jnp.sin(ang_c)], axis=1)
    cos = jnp.concatenate([cos, jnp.ones((n_ctx, d_rot), F32)], axis=0)
    sin = jnp.concatenate([sin, jnp.zeros((n_ctx, d_rot), F32)], axis=0)
    return jnp.tile(cos, (1, reps)), jnp.tile(sin, (1, reps))


def _rope(y, cos, sin, nf):
    lane = lax.broadcasted_iota(I32, y.shape, 1)
    partner = jnp.where((lane % (2 * nf)) < nf,
                        pltpu.roll(y, LANES - nf, 1), pltpu.roll(y, nf, 1))
    return y * cos + partner * sin


def _gqa_prep_kernel(p_ref, w_ref, cos_ref, sin_ref, o_ref):
    c = pl.program_id(1)
    heads = p_ref.shape[1] // HEAD_DIM

    @pl.when(c < 4)
    def _():
        for hh in range(heads):
            x = p_ref[:, hh * HEAD_DIM:(hh + 1) * HEAD_DIM].astype(F32)
            y = _rope(_rms(x, w_ref[...]), cos_ref[...], sin_ref[...], HEAD_DIM // 4)
            o_ref[hh] = y.astype(o_ref.dtype)

    @pl.when(c == 4)
    def _():
        for hh in range(heads):
            o_ref[hh] = p_ref[:, hh * HEAD_DIM:(hh + 1) * HEAD_DIM]


def _gqa_prep(proj, geom, q_norm, k_norm, cos, sin):
    batch, n_tok, b_fn, rb_fn = geom
    wvec = jnp.stack([q_norm * HEAD_DIM ** -0.5] * 3 + [k_norm, jnp.ones_like(k_norm)]).reshape(5, 1, HEAD_DIM)
    first = (2 * A_WIDTH) // 512
    return pl.pallas_call(
        _gqa_prep_kernel,
        grid=(proj.shape[0] // ROW_TILE, 5),
        in_specs=[pl.BlockSpec((ROW_TILE, 512), lambda i, c: (i, first + c)),
                  pl.BlockSpec((None, 1, HEAD_DIM), lambda i, c: (c, 0, 0)),
                  pl.BlockSpec((ROW_TILE, HEAD_DIM), lambda i, c: (rb_fn(i), 0)),
                  pl.BlockSpec((ROW_TILE, HEAD_DIM), lambda i, c: (rb_fn(i), 0))],
        out_specs=pl.BlockSpec((None, 4, ROW_TILE, HEAD_DIM), lambda i, c: (b_fn(i), c, rb_fn(i), 0)),
        out_shape=jax.ShapeDtypeStruct((batch, B_HEADS + 2 * B_KV_HEADS, n_tok, HEAD_DIM), BF16),
        compiler_params=_params(("arbitrary", "arbitrary")),
        name="gqa_prep",
    )(proj, wvec, cos, sin)


def _mla_prep_kernel(p_ref, qa_ref, kva_ref, wqn_ref, wqr_ref, wkk_ref, wkv_ref,
                     nqn_ref, nqr_ref, nkn_ref, nkr_ref, cos_ref, sin_ref,
                     qc_ref, kc_ref, vc_ref):
    cq = p_ref[:, :C_Q_RANK].astype(F32)
    ckv = p_ref[:, C_Q_RANK:C_Q_RANK + C_KV_RANK].astype(F32)
    kr = p_ref[:, C_Q_RANK + C_KV_RANK:C_Q_RANK + C_KV_RANK + LANES].astype(F32)
    cqn = _rms(cq, qa_ref[...]).astype(BF16)
    ckvn = _rms(ckv, kva_ref[...]).astype(BF16)
    qn = jnp.dot(cqn, wqn_ref[...], preferred_element_type=F32)
    qr = jnp.dot(cqn, wqr_ref[...], preferred_element_type=F32)
    kn = jnp.dot(ckvn, wkk_ref[...], preferred_element_type=F32)
    vv = jnp.dot(ckvn, wkv_ref[...], preferred_element_type=F32)
    cos = cos_ref[...]
    sin = sin_ref[...]
    lane = lax.broadcasted_iota(I32, kr.shape, 1)
    low = lane < C_ROPE
    inv_qk = 1.0 / C_QK

    kr_rot = _rope(kr * nkr_ref[...], cos, sin, C_ROPE // 4)
    kr_ss = jnp.sum(kr * kr, axis=-1, keepdims=True)
    for h in range(C_HEADS):
        hs = slice(h * C_NOPE, (h + 1) * C_NOPE)
        knh = kn[:, hs]
        r = lax.rsqrt((jnp.sum(knh * knh, axis=-1, keepdims=True) + kr_ss) * inv_qk + EPS)
        kc_ref[h, :, 0:C_NOPE] = (knh * r * nkn_ref[...]).astype(kc_ref.dtype)
        kc_ref[h, :, C_NOPE:C_QK] = (kr_rot * r)[:, 0:C_ROPE].astype(kc_ref.dtype)
        vc_ref[h] = vv[:, h * C_V:(h + 1) * C_V].astype(vc_ref.dtype)

    for pr in range(C_HEADS // 2):
        qrp = qr[:, pr * LANES:(pr + 1) * LANES]
        sq = qrp * qrp
        ss_pair = (jnp.sum(jnp.where(low, sq, 0.0), axis=-1, keepdims=True),
                   jnp.sum(jnp.where(low, 0.0, sq), axis=-1, keepdims=True))
        rs = []
        for e in range(2):
            h = 2 * pr + e
            qnh = qn[:, h * C_NOPE:(h + 1) * C_NOPE]
            r = lax.rsqrt((jnp.sum(qnh * qnh, axis=-1, keepdims=True) + ss_pair[e]) * inv_qk + EPS)
            rs.append(r)
            qc_ref[h, :, 0:C_NOPE] = (qnh * r * nqn_ref[...]).astype(qc_ref.dtype)
        q_rot = _rope(qrp * jnp.where(low, rs[0], rs[1]) * nqr_ref[...], cos, sin, C_ROPE // 4)
        qc_ref[2 * pr, :, C_NOPE:C_QK] = q_rot[:, 0:C_ROPE].astype(qc_ref.dtype)
        qc_ref[2 * pr + 1, :, C_NOPE:C_QK] = q_rot[:, C_ROPE:2 * C_ROPE].astype(qc_ref.dtype)


def _mla_prep(proj, geom, q_a_norm, w_uq, kv_a_norm, w_ukv, q_norm, k_norm, cos, sin):
    batch, n_tok, b_fn, rb_fn = geom
    first = (2 * A_WIDTH + B_WIDTH + 2 * B_KV_HEADS * HEAD_DIM)
    width = IN_COLS_PAD - first
    assert first % width == 0
    wq = w_uq.reshape(C_Q_RANK, C_HEADS, C_QK)
    wqn = wq[:, :, :C_NOPE].reshape(C_Q_RANK, C_HEADS * C_NOPE).astype(BF16)
    wqr = wq[:, :, C_NOPE:].reshape(C_Q_RANK, C_HEADS * C_ROPE).astype(BF16)
    wkv = w_ukv.reshape(C_KV_RANK, C_HEADS, C_NOPE + C_V)
    wkk = wkv[:, :, :C_NOPE].reshape(C_KV_RANK, C_HEADS * C_NOPE).astype(BF16)
    wkvv = wkv[:, :, C_NOPE:].reshape(C_KV_RANK, C_HEADS * C_V).astype(BF16)
    qs = C_QK ** -0.5
    nqn = (q_norm[:C_NOPE] * qs).reshape(1, C_NOPE)
    nqr = jnp.tile(q_norm[C_NOPE:] * qs, 2).reshape(1, LANES)
    nkn = k_norm[:C_NOPE].reshape(1, C_NOPE)
    nkr = jnp.tile(k_norm[C_NOPE:], 2).reshape(1, LANES)
    const = lambda shape: pl.BlockSpec(shape, lambda i: tuple(0 for _ in shape))
    head_spec = lambda dd: pl.BlockSpec((None, C_HEADS, ROW_TILE, dd), lambda i: (b_fn(i), 0, rb_fn(i), 0))
    return pl.pallas_call(
        _mla_prep_kernel,
        grid=(proj.shape[0] // ROW_TILE,),
        in_specs=[pl.BlockSpec((ROW_TILE, width), lambda i: (i, first // width)),
                  const((1, C_Q_RANK)), const((1, C_KV_RANK)),
                  const(wqn.shape), const(wqr.shape), const(wkk.shape), const(wkvv.shape),
                  const((1, C_NOPE)), const((1, LANES)), const((1, C_NOPE)), const((1, LANES)),
                  pl.BlockSpec((ROW_TILE, LANES), lambda i: (rb_fn(i), 0)),
                  pl.BlockSpec((ROW_TILE, LANES), lambda i: (rb_fn(i), 0))],
        out_specs=[head_spec(C_QK), head_spec(C_QK), head_spec(C_V)],
        out_shape=[jax.ShapeDtypeStruct((batch, C_HEADS, n_tok, C_QK), BF16),
                   jax.ShapeDtypeStruct((batch, C_HEADS, n_tok, C_QK), BF16),
                   jax.ShapeDtypeStruct((batch, C_HEADS, n_tok, C_V), BF16)],
        compiler_params=_params(("arbitrary",)),
        name="mla_prep",
    )(proj, q_a_norm.reshape(1, C_Q_RANK), kv_a_norm.reshape(1, C_KV_RANK), wqn, wqr, wkk, wkvv,
      nqn, nqr, nkn, nkr, cos, sin)


def _attn_kernel(q_ref, k_ref, v_ref, o_ref, m_sc, l_sc, acc_sc, *, group, n_lat, n_ctx, nq_lat):
    qi = pl.program_id(2)
    tq, dk = q_ref.shape[1], q_ref.shape[2]
    dv = v_ref.shape[1]
    q = q_ref[...].reshape(group * tq, dk)
    m_sc[...] = jnp.full(m_sc.shape, -jnp.inf, F32)
    l_sc[...] = jnp.zeros(l_sc.shape, F32)
    acc_sc[...] = jnp.zeros(acc_sc.shape, F32)

    def chunk(start, size):
        k = k_ref[pl.ds(start, size), :]
        v = v_ref[pl.ds(start, size), :]
        s = lax.dot_general(q, k, (((1,), (1,)), ((), ())), preferred_element_type=F32)
        m_prev = m_sc[...]
        m_new = jnp.maximum(m_prev, jnp.max(s, axis=-1, keepdims=True))
        alpha = jnp.exp(m_prev - m_new)
        p = jnp.exp(s - m_new)
        l_sc[...] = alpha * l_sc[...] + jnp.sum(p, axis=-1, keepdims=True)
        acc_sc[...] = alpha * acc_sc[...] + jnp.dot(p.astype(BF16), v, preferred_element_type=F32)
        m_sc[...] = m_new

    @pl.when(qi < nq_lat)
    def _():
        def body(c, carry):
            chunk(pl.multiple_of(c * ATT_TK, ATT_TK), ATT_TK)
            return carry
        lax.fori_loop(0, n_lat // ATT_TK, body, 0)

    chunk(n_lat, n_ctx)
    o = acc_sc[...] / l_sc[...]
    for g in range(group):
        o_ref[:, g * dv:(g + 1) * dv] = o[g * tq:(g + 1) * tq].astype(o_ref.dtype)


def _attention(q_arr, k_arr, v_arr, q_head0, k_head0, v_head0, n_kv, group, n_lat, n_ctx, with_ctx, t_rows, name):
    batch, _, n_tok, dk = q_arr.shape
    dv = v_arr.shape[-1]
    nq_lat = n_lat // ATT_TQ
    nq = nq_lat + (1 if with_ctx else 0)
    assert n_ctx == ATT_TQ and n_tok == n_lat + n_ctx and n_lat % ATT_TK == 0
    out_row = lambda b, g, qi: jnp.where(qi < nq_lat, b * nq_lat + qi, batch * nq_lat + b)
    return pl.pallas_call(
        functools.partial(_attn_kernel, group=group, n_lat=n_lat, n_ctx=n_ctx, nq_lat=nq_lat),
        grid=(batch, n_kv, nq),
        in_specs=[pl.BlockSpec((None, group, ATT_TQ, dk), lambda b, g, qi: (b, q_head0 // group + g, qi, 0)),
                  pl.BlockSpec((None, None, n_tok, dk), lambda b, g, qi: (b, k_head0 + g, 0, 0)),
                  pl.BlockSpec((None, None, n_tok, dv), lambda b, g, qi: (b, v_head0 + g, 0, 0))],
        out_specs=pl.BlockSpec((ATT_TQ, group * dv), lambda b, g, qi: (out_row(b, g, qi), g)),
        out_shape=jax.ShapeDtypeStruct((t_rows, n_kv * group * dv), BF16),
        scratch_shapes=[pltpu.VMEM((group * ATT_TQ, 1), F32),
                        pltpu.VMEM((group * ATT_TQ, 1), F32),
                        pltpu.VMEM((group * ATT_TQ, dv), F32)],
        compiler_params=_params(("arbitrary", "arbitrary", "arbitrary")),
        name=name,
    )(q_arr, k_arr, v_arr)


def _wout_kernel(a_ref, b_ref, c_ref, w_ref, x_ref, g_ref, o_ref):
    acc = jnp.dot(a_ref[...], w_ref[0:A_WIDTH, :], preferred_element_type=F32)
    acc += jnp.dot(b_ref[...], w_ref[A_WIDTH:A_WIDTH + B_WIDTH, :], preferred_element_type=F32)
    acc += jnp.dot(c_ref[...], w_ref[A_WIDTH + B_WIDTH:, :], preferred_element_type=F32)
    o_ref[...] = x_ref[...] + g_ref[...] * acc


def _out_proj(a_out, b_out, c_out, w, x, mods, layer, t_rows, mod_row):
    d = w.shape[1]
    return pl.pallas_call(
        _wout_kernel,
        grid=(d // MM_TN, t_rows // MM_TM),
        in_specs=[pl.BlockSpec((MM_TM, A_WIDTH), lambda j, i: (i, 0)),
                  pl.BlockSpec((MM_TM, B_WIDTH), lambda j, i: (i, 0)),
                  pl.BlockSpec((MM_TM, C_WIDTH), lambda j, i: (i, 0)),
                  pl.BlockSpec((w.shape[0], MM_TN), lambda j, i: (0, j)),
                  pl.BlockSpec((MM_TM, MM_TN), lambda j, i: (i, j)),
                  _mod_spec(layer, 2, d, lambda j, i: mod_row(i), MM_TN, lambda j, i: j)],
        out_specs=pl.BlockSpec((MM_TM, MM_TN), lambda j, i: (i, j)),
        out_shape=jax.ShapeDtypeStruct((t_rows, d), F32),
        compiler_params=_params(("arbitrary", "arbitrary")),
        name="out_proj",
    )(a_out, b_out, c_out, w, x, mods)


def _first_index(hit_val, x, iota, big):
    return jnp.min(jnp.where(x == hit_val, iota, big), axis=0, keepdims=True)


def _router_kernel(x_ref, nw_ref, shift_ref, scale_ref, rwt_ref, bias_ref,
                   hp_ref, eidx_ref, gate_ref, rank_ref, cnt_ref, carry_sc):
    i = pl.program_id(0)
    tm, d = x_ref.shape
    h = _rms(x_ref[...], nw_ref[...]) * (1.0 + scale_ref[...]) + shift_ref[...]
    hp_ref[...] = _pack_bf16_pair(h[:, :d // 2], h[:, d // 2:])

    logits = lax.dot_general(rwt_ref[...], h, (((1,), (1,)), ((), ())),
                             precision=lax.Precision.HIGHEST, preferred_element_type=F32)
    scores = jax.nn.sigmoid(logits)
    sel = scores + bias_ref[:, 0:1]
    neg = jnp.float32(-jnp.inf)

    iota_n = lax.broadcasted_iota(I32, (N_GROUPS, tm), 0).astype(F32)
    iota_e = lax.broadcasted_iota(I32, (N_EXPERTS, tm), 0).astype(F32)

    gscore = jnp.zeros((N_GROUPS, tm), F32)
    for g in range(N_GROUPS):
        blk = sel[g * GROUP_SIZE:(g + 1) * GROUP_SIZE]
        m1 = jnp.max(blk, axis=0, keepdims=True)
        first = _first_index(m1, blk, iota_n, float(GROUP_SIZE))
        m2 = jnp.max(jnp.where(iota_n == first, neg, blk), axis=0, keepdims=True)
        gscore = jnp.where(iota_n == float(g), m1 + m2, gscore)

    gsel = jnp.zeros((N_GROUPS, tm), F32)
    work = gscore
    for _ in range(TOPK_GROUPS):
        m = jnp.max(work, axis=0, keepdims=True)
        hit = iota_n == _first_index(m, work, iota_n, float(N_GROUPS))
        gsel = jnp.where(hit, 1.0, gsel)
        work = jnp.where(hit, neg, work)
    emask = jnp.concatenate(
        [jnp.broadcast_to(gsel[g:g + 1], (GROUP_SIZE, tm)) for g in range(N_GROUPS)], axis=0)

    work = jnp.where(emask > 0.5, sel, neg)
    picks = []
    onehot = jnp.zeros((N_EXPERTS, tm), F32)
    for _ in range(TOP_K):
        m = jnp.max(work, axis=0, keepdims=True)
        pick = _first_index(m, work, iota_e, float(N_EXPERTS))
        picks.append(pick)
        hit = iota_e == pick
        onehot = jnp.where(hit, 1.0, onehot)
        work = jnp.where(hit, neg, work)

    w_un = onehot * scores
    gates = w_un / jnp.sum(w_un, axis=0, keepdims=True) * ROUTED_SCALE

    @pl.when(i == 0)
    def _():
        carry_sc[...] = jnp.zeros(carry_sc.shape, F32)

    before = (lax.broadcasted_iota(I32, (tm, tm), 0) < lax.broadcasted_iota(I32, (tm, tm), 1)).astype(BF16)
    rank = carry_sc[:, 0:1] + jnp.dot(onehot.astype(BF16), before, preferred_element_type=F32)
    carry_sc[...] = carry_sc[...] + jnp.sum(onehot, axis=1, keepdims=True)
    cnt_ref[...] = carry_sc[...]

    eidx_ref[...] = jnp.zeros(eidx_ref.shape, I32)
    gate_ref[...] = jnp.zeros(gate_ref.shape, F32)
    rank_ref[...] = jnp.zeros(rank_ref.shape, I32)
    for k in range(TOP_K):
        hit = iota_e == picks[k]
        eidx_ref[k:k + 1, :] = picks[k].astype(I32)
        gate_ref[k:k + 1, :] = jnp.sum(jnp.where(hit, gates, 0.0), axis=0, keepdims=True)
        rank_ref[k:k + 1, :] = jnp.sum(jnp.where(hit, rank, 0.0), axis=0, keepdims=True).astype(I32)


def _norm_router(x, t_rows, norm_w, mods, layer, mod_row, router_w, router_bias):
    d = x.shape[1]
    col = lambda dt: (pl.BlockSpec((SUBLANES, ROW_TILE), lambda i: (0, i)),
                      jax.ShapeDtypeStruct((SUBLANES, t_rows), dt))
    specs, shapes = zip(col(I32), col(F32), col(I32))
    return pl.pallas_call(
        _router_kernel,
        grid=(t_rows // ROW_TILE,),
        in_specs=[pl.BlockSpec((ROW_TILE, d), lambda i: (i, 0)),
                  pl.BlockSpec((1, d), lambda i: (0, 0)),
                  _mod_spec(layer, 3, d, mod_row),
                  _mod_spec(layer, 4, d, mod_row),
                  pl.BlockSpec((N_EXPERTS, d), lambda i: (0, 0)),
                  pl.BlockSpec((N_EXPERTS, LANES), lambda i: (0, 0))],
        out_specs=[pl.BlockSpec((ROW_TILE, d // 2), lambda i: (i, 0)), *specs,
                   pl.BlockSpec((N_EXPERTS, LANES), lambda i: (0, 0))],
        out_shape=[jax.ShapeDtypeStruct((t_rows, d // 2), U32), *shapes,
                   jax.ShapeDtypeStruct((N_EXPERTS, LANES), F32)],
        scratch_shapes=[pltpu.VMEM((N_EXPERTS, LANES), F32)],
        compiler_params=_params(("arbitrary",)),
        name="norm_router",
    )(x, norm_w.reshape(1, d), mods, mods, jnp.transpose(router_w),
      jnp.broadcast_to(router_bias[:, None], (N_EXPERTS, LANES)))


def _row_copy(src, src_row, dst, dst_row, sem):
    return pltpu.make_async_copy(src.at[pl.ds(src_row, 1)], dst.at[pl.ds(dst_row, 1)], sem)


_FILL_PIECES = tuple(1 << b for b in range(EXP_TM.bit_length() - 2, -1, -1))


def _zero_fill(pstart_ref, plen_ref, tail_ref, xs_hbm, zero_sc, sem_z):
    zero_sc[...] = jnp.zeros(zero_sc.shape, zero_sc.dtype)
    n_tail = (xs_hbm.shape[0] - tail_ref[0]) // _FILL_PIECES[0]

    def pad_copies(e, wait):
        start = pstart_ref[e]
        head = (-start) & (SUBLANES - 1)
        for r in range(SUBLANES - 1):
            cp = pltpu.make_async_copy(zero_sc.at[pl.ds(0, 1)], xs_hbm.at[pl.ds(start + r, 1)], sem_z)

            @pl.when(r < head)
            def _():
                cp.wait() if wait else cp.start()

        off = start + head
        rem = plen_ref[e] - head
        for piece in _FILL_PIECES:
            if piece < SUBLANES:
                break
            on = (rem & piece) != 0
            cp = pltpu.make_async_copy(zero_sc.at[pl.ds(0, piece)],
                                       xs_hbm.at[pl.ds(pl.multiple_of(off, SUBLANES), piece)], sem_z)

            @pl.when(on)
            def _():
                cp.wait() if wait else cp.start()

            off = off + jnp.where(on, piece, 0)

    def tail_copy(j, wait):
        row = pl.multiple_of(tail_ref[0] + j * _FILL_PIECES[0], _FILL_PIECES[0])
        cp = pltpu.make_async_copy(zero_sc, xs_hbm.at[pl.ds(row, _FILL_PIECES[0])], sem_z)
        cp.wait() if wait else cp.start()

    for wait in (False, True):
        lax.fori_loop(0, N_EXPERTS, lambda e, c: (pad_copies(e, wait), c)[1], 0)
        lax.fori_loop(0, n_tail, lambda j, c: (tail_copy(j, wait), c)[1], 0)


def _dispatch_kernel(pstart_ref, plen_ref, tail_ref, slot_hbm, h_hbm, xs_hbm, slot_sm, zero_sc, sem_s, sem, sem_z):
    i = pl.program_id(0)
    n = slot_sm.shape[0]
    cp = pltpu.make_async_copy(slot_hbm.at[pl.ds(pl.multiple_of(i * n, n), n)], slot_sm, sem_s)
    cp.start()

    @pl.when(i == 0)
    def _():
        _zero_fill(pstart_ref, plen_ref, tail_ref, xs_hbm, zero_sc, sem_z)

    cp.wait()
    t0 = i * DISP_T

    def issue(t, carry):
        for k in range(TOP_K):
            _row_copy(h_hbm, t0 + t, xs_hbm, slot_sm[t * SUBLANES + k], sem).start()
        return carry

    lax.fori_loop(0, DISP_T, issue, 0)

    def drain(t, carry):
        for k in range(TOP_K):
            _row_copy(h_hbm, 0, xs_hbm, 0, sem).wait()
        return carry

    lax.fori_loop(0, DISP_T, drain, 0)


def _dispatch(hp, slots, pad_start, pad_len, tail_start, n_slots):
    t, w = hp.shape
    grid_spec = pltpu.PrefetchScalarGridSpec(
        num_scalar_prefetch=3,
        grid=(t // DISP_T,),
        in_specs=[pl.BlockSpec(memory_space=pl.ANY), pl.BlockSpec(memory_space=pl.ANY)],
        out_specs=pl.BlockSpec(memory_space=pl.ANY),
        scratch_shapes=[pltpu.SMEM((DISP_T * SUBLANES,), I32),
                        pltpu.VMEM((_FILL_PIECES[0], w), U32),
                        pltpu.SemaphoreType.DMA(()), pltpu.SemaphoreType.DMA(()), pltpu.SemaphoreType.DMA(())],
    )
    return pl.pallas_call(
        _dispatch_kernel,
        grid_spec=grid_spec,
        out_shape=jax.ShapeDtypeStruct((n_slots, w), U32),
        compiler_params=_params(("arbitrary",)),
        name="moe_dispatch",
    )(pad_start, pad_len, tail_start, slots, hp)


def _ffn(lo, hi, wg, wu, wd):
    half = lo.shape[1]
    hg = jnp.dot(lo, wg[:half], preferred_element_type=F32) + jnp.dot(hi, wg[half:], preferred_element_type=F32)
    hu = jnp.dot(lo, wu[:half], preferred_element_type=F32) + jnp.dot(hi, wu[half:], preferred_element_type=F32)
    hb = (jax.nn.silu(hg) * hu).astype(BF16)
    return jnp.dot(hb, wd, preferred_element_type=F32)


def _expert_kernel(te_ref, nu_ref, xs_ref, wg_ref, wu_ref, wd_ref, ys_ref, wg_sc, wu_sc, wd_sc):
    i = pl.program_id(0)
    prev = te_ref[jnp.maximum(i - 1, 0)]

    @pl.when(jnp.logical_or(i == 0, te_ref[i] != prev))
    def _():
        wg_sc[...] = wg_ref[...].astype(BF16)
        wu_sc[...] = wu_ref[...].astype(BF16)
        wd_sc[...] = wd_ref[...].astype(BF16)

    @pl.when(i < nu_ref[0])
    def _():
        lo, hi = _unpack_bf16_pair(xs_ref[...])
        y = _ffn(lo.astype(BF16), hi.astype(BF16), wg_sc[...], wu_sc[...], wd_sc[...])
        half = y.shape[1] // 2
        ys_ref[...] = _pack_bf16_pair(y[:, :half], y[:, half:])

    @pl.when(i >= nu_ref[0])
    def _():
        ys_ref[...] = jnp.zeros(ys_ref.shape, ys_ref.dtype)


def _experts(xs, tile_expert, n_used, w_gate, w_up, w_down):
    n_slots, w = xs.shape
    _, d, f = w_gate.shape
    n_tiles = n_slots // EXP_TM
    grid_spec = pltpu.PrefetchScalarGridSpec(
        num_scalar_prefetch=2,
        grid=(n_tiles,),
        in_specs=[pl.BlockSpec((EXP_TM, w), lambda i, te, nu: (jnp.minimum(i, nu[0] - 1), 0)),
                  pl.BlockSpec((None, d, f), lambda i, te, nu: (te[i], 0, 0)),
                  pl.BlockSpec((None, d, f), lambda i, te, nu: (te[i], 0, 0)),
                  pl.BlockSpec((None, f, d), lambda i, te, nu: (te[i], 0, 0))],
        out_specs=pl.BlockSpec((EXP_TM, w), lambda i, te, nu: (i, 0)),
        scratch_shapes=[pltpu.VMEM((d, f), BF16), pltpu.VMEM((d, f), BF16), pltpu.VMEM((f, d), BF16)],
    )
    return pl.pallas_call(
        _expert_kernel,
        grid_spec=grid_spec,
        out_shape=jax.ShapeDtypeStruct((n_slots, w), U32),
        compiler_params=_params(("arbitrary",)),
        name="moe_experts",
    )(tile_expert, n_used, xs, w_gate, w_up, w_down)


def _combine_kernel(slot_hbm, ys_hbm, hp_ref, gate_ref, sg_ref, su_ref, sd_ref, x_ref, g5_ref,
                    o_ref, slot_sm, buf, sem_s, sem):
    i = pl.program_id(0)
    n = slot_sm.shape[0]
    cp = pltpu.make_async_copy(slot_hbm.at[pl.ds(pl.multiple_of(i * n, n), n)], slot_sm, sem_s)
    cp.start()
    cp.wait()

    def issue(t, carry):
        for k in range(TOP_K):
            _row_copy(ys_hbm, slot_sm[t * SUBLANES + k], buf.at[k], t, sem).start()
        return carry

    lax.fori_loop(0, COMB_T, issue, 0)

    lo, hi = _unpack_bf16_pair(hp_ref[...])
    y = _ffn(lo.astype(BF16), hi.astype(BF16), sg_ref[...], su_ref[...], sd_ref[...])

    def drain(t, carry):
        for k in range(TOP_K):
            _row_copy(ys_hbm, 0, buf.at[k], 0, sem).wait()
        return carry

    lax.fori_loop(0, COMB_T, drain, 0)

    half = y.shape[1] // 2
    r_lo = y[:, :half]
    r_hi = y[:, half:]
    for k in range(TOP_K):
        e_lo, e_hi = _unpack_bf16_pair(buf[k])
        gk = gate_ref[:, k:k + 1]
        r_lo = r_lo + gk * e_lo
        r_hi = r_hi + gk * e_hi
    o_ref[:, :half] = x_ref[:, :half] + g5_ref[:, :half] * r_lo
    o_ref[:, half:] = x_ref[:, half:] + g5_ref[:, half:] * r_hi


def _combine(ys, hp, slots, gates, sh_gate, sh_up, sh_down, x, mods, layer, t_rows, mod_row):
    d = x.shape[1]
    w = hp.shape[1]
    f = sh_gate.shape[1]
    return pl.pallas_call(
        _combine_kernel,
        grid=(t_rows // COMB_T,),
        in_specs=[pl.BlockSpec(memory_space=pl.ANY), pl.BlockSpec(memory_space=pl.ANY),
                  pl.BlockSpec((COMB_T, w), lambda i: (i, 0)),
                  pl.BlockSpec((COMB_T, SUBLANES), lambda i: (i, 0)),
                  pl.BlockSpec((d, f), lambda i: (0, 0)),
                  pl.BlockSpec((d, f), lambda i: (0, 0)),
                  pl.BlockSpec((f, d), lambda i: (0, 0)),
                  pl.BlockSpec((COMB_T, d), lambda i: (i, 0)),
                  _mod_spec(layer, 5, d, lambda i: mod_row(i * COMB_T // ROW_TILE))],
        out_specs=pl.BlockSpec((COMB_T, d), lambda i: (i, 0)),
        out_shape=jax.ShapeDtypeStruct((t_rows, d), F32),
        scratch_shapes=[pltpu.SMEM((COMB_T * SUBLANES,), I32),
                        pltpu.VMEM((TOP_K, COMB_T, w), U32),
                        pltpu.SemaphoreType.DMA(()), pltpu.SemaphoreType.DMA(())],
        compiler_params=_params(("arbitrary",)),
        name="moe_combine",
    )(slots, ys, hp, gates, sh_gate.astype(BF16), sh_up.astype(BF16), sh_down.astype(BF16), x, mods)


def _moe(x, t_rows, norm_w, mods, layer, mod_row, router_w, router_bias,
         w_gate, w_up, w_down, sh_gate, sh_up, sh_down):
    hp, eidx, gate, rank, cnt = _norm_router(x, t_rows, norm_w, mods, layer, mod_row, router_w, router_bias)

    n_tiles = (t_rows * TOP_K + N_EXPERTS * (EXP_TM - 1) + EXP_TM - 1) // EXP_TM
    counts = cnt[:, 0].astype(I32)
    padded = (counts + EXP_TM - 1) // EXP_TM * EXP_TM
    ends = jnp.cumsum(padded)
    base = ends - padded
    slots = jnp.transpose(base[eidx] + rank).reshape(-1)
    n_used = (ends[-1] // EXP_TM).astype(I32)
    tile_start = jnp.arange(n_tiles, dtype=I32) * EXP_TM
    te = jnp.minimum(jnp.searchsorted(ends, tile_start, side="right").astype(I32), N_EXPERTS - 1)
    te = jnp.where(jnp.arange(n_tiles) < n_used, te, te[jnp.maximum(n_used - 1, 0)])

    xs = _dispatch(hp, slots, base + counts, padded - counts, ends[-1:], n_tiles * EXP_TM)
    ys = _experts(xs, te, n_used.reshape(1), w_gate, w_up, w_down)
    return _combine(ys, hp, slots, jnp.transpose(gate), sh_gate, sh_up, sh_down, x, mods, layer, t_rows, mod_row)


def kernel(x, c, ctx, c_ctx, ada_w, ada_b, norm1_w, norm2_w, w_in, gm_v_norm, gm_w_s, gm_b_s, gqa_q_norm, gqa_k_norm, mla_q_a_norm, mla_w_uq, mla_kv_a_norm, mla_w_ukv, mla_q_norm, mla_k_norm, w_out, router_w, router_bias, exp_w_gate, exp_w_up, exp_w_down, sh_w_gate, sh_w_up, sh_w_down):
    batch, n_lat, d = x.shape
    n_ctx = ctx.shape[1]
    depth = ada_w.shape[0]
    assert n_ctx == ROW_TILE and n_lat % MM_TM == 0 and (batch * n_ctx) % MM_TM == 0
    t_lat = batch * n_lat
    t_all = t_lat + batch * n_ctx
    n_tok = n_lat + n_ctx
    lat_tiles = n_lat // ROW_TILE

    mod_row = lambda i: jnp.minimum(i // lat_tiles, batch)
    is_lat = lambda i: i < batch * lat_tiles
    b_fn = lambda i: jnp.where(is_lat(i), i // lat_tiles, i - batch * lat_tiles)
    rb_fn = lambda i: jnp.where(is_lat(i), i % lat_tiles, lat_tiles)
    geom = (batch, n_tok, b_fn, rb_fn)
    mm_mod_row = lambda i: mod_row(i * (MM_TM // ROW_TILE))

    mods = _ada_mods(c, c_ctx, ada_w, ada_b)
    cos_b, sin_b = _rope_tables(n_lat, n_ctx, HEAD_DIM, 1)
    cos_c, sin_c = _rope_tables(n_lat, n_ctx, C_ROPE, LANES // C_ROPE)

    xs = jnp.concatenate([x.reshape(t_lat, d), ctx.reshape(batch * n_ctx, d)], axis=0)
    for l in range(depth):
        need_ctx = l < depth - 1
        t_rows = t_all if need_ctx else t_lat
        w_in_l = jnp.pad(w_in[l].astype(BF16), ((0, 0), (0, IN_COLS_PAD - w_in.shape[2])))

        h = _norm_mod(xs, norm1_w[l], mods, l, 0, 1, mod_row)
        proj = _matmul(h, w_in_l)
        a_out = _gmlp(proj, t_rows, gm_v_norm[l], gm_w_s[l], gm_b_s[l])
        qkv = _gqa_prep(proj, geom, gqa_q_norm[l], gqa_k_norm[l], cos_b, sin_b)
        b_out = _attention(qkv, qkv, qkv, 0, B_HEADS, B_HEADS + B_KV_HEADS, B_KV_HEADS, B_GROUP,
                           n_lat, n_ctx, need_ctx, t_rows, "gqa_attn")
        qc, kc, vc = _mla_prep(proj, geom, mla_q_a_norm[l], mla_w_uq[l], mla_kv_a_norm[l], mla_w_ukv[l],
                               mla_q_norm[l], mla_k_norm[l], cos_c, sin_c)
        c_out = _attention(qc, kc, vc, 0, 0, 0, C_HEADS, 1, n_lat, n_ctx, need_ctx, t_rows, "mla_attn")
        xs = _out_proj(a_out, b_out, c_out, w_out[l].astype(BF16), xs, mods, l, t_rows, mm_mod_row)
        xs = _moe(xs, t_rows, norm2_w[l], mods, l, mod_row, router_w[l], router_bias[l],
                  exp_w_gate[l], exp_w_up[l], exp_w_down[l], sh_w_gate[l], sh_w_up[l], sh_w_down[l])
    return xs[:t_lat].reshape(batch, n_lat, d)
```

```python
import functools

import jax
import jax.numpy as jnp
from jax import lax
from jax.experimental import pallas as pl
from jax.experimental.pallas import tpu as pltpu

F32 = jnp.float32
BF16 = jnp.bfloat16
U32 = jnp.uint32
I32 = jnp.int32

GRID_W = 64
CHUNK = 128
HEAD_DIM = 128
EPS = 1e-6
ROPE_THETA = 10000.0
A_HEADS = 8
B_HEADS = 12
B_KV_HEADS = 4
B_GROUP = B_HEADS // B_KV_HEADS
C_HEADS = 12
C_Q_RANK = 768
C_KV_RANK = 512
C_NOPE = 128
C_ROPE = 64
C_V = 128
C_QK = C_NOPE + C_ROPE
A_WIDTH = A_HEADS * HEAD_DIM
B_WIDTH = B_HEADS * HEAD_DIM
C_WIDTH = C_HEADS * C_V
N_EXPERTS = 64
TOP_K = 6
N_GROUPS = 8
GROUP_SIZE = N_EXPERTS // N_GROUPS
TOPK_GROUPS = 4
ROUTED_SCALE = 2.5
N_MOD = 6

LANES = 128
SUBLANES = 8
ROW_TILE = 256
MM_TM = 512
MM_TN = 1024
GQA_TQ = 512
MLA_TQ = 1024
ATT_TK = 512
LOG2E = 1.4426950408889634
BOUND_SLACK = 1.02
SCORE_BOUND_LIMIT = 60.0
EXP_TM = 256
DISP_T = 256
COMB_T = 128
IN_COLS_PAD = 6144
VMEM_LIMIT = 56 * 1024 * 1024


def _params(sem, vmem=VMEM_LIMIT):
    return pltpu.CompilerParams(dimension_semantics=sem, vmem_limit_bytes=vmem)


def _rms(x, w):
    return x * lax.rsqrt(jnp.mean(x * x, axis=-1, keepdims=True) + EPS) * w


def _pack_bf16_pair(a, b):
    ua = lax.bitcast_convert_type(a.astype(BF16).astype(F32), U32)
    ub = lax.bitcast_convert_type(b.astype(BF16).astype(F32), U32)
    return (ua >> 16) | ub


def _unpack_bf16_pair(u):
    a = lax.bitcast_convert_type(u << 16, F32)
    b = lax.bitcast_convert_type(u & jnp.uint32(0xFFFF0000), F32)
    return a, b


def _ada_kernel(cb_ref, w_ref, b_ref, o_ref, *, n_rows, d, tn):
    nj = tn // LANES

    def body(kc, accs):
        k0 = pl.multiple_of(kc * SUBLANES, SUBLANES)
        wk = w_ref[0, pl.ds(k0, SUBLANES), :]
        out = []
        for r in range(n_rows):
            cv = cb_ref[r, pl.ds(k0, SUBLANES), :]
            sv = cv * jax.nn.sigmoid(cv)
            for j in range(nj):
                out.append(accs[r * nj + j] + wk[:, j * LANES:(j + 1) * LANES] * sv)
        return tuple(out)

    init = tuple(jnp.zeros((SUBLANES, LANES), F32) for _ in range(n_rows * nj))
    accs = lax.fori_loop(0, d // SUBLANES, body, init, unroll=4)
    o_ref[0] = jnp.zeros((SUBLANES, tn), F32)
    for r in range(n_rows):
        for j in range(nj):
            o_ref[0, r:r + 1, j * LANES:(j + 1) * LANES] = (
                jnp.sum(accs[r * nj + j], axis=0, keepdims=True) + b_ref[0, :, j * LANES:(j + 1) * LANES])


def _ada_mods(c, c_ctx, ada_w, ada_b):
    n_layers, d, nd = ada_w.shape
    n_rows = c.shape[0] + 1
    assert n_rows <= SUBLANES
    cvec = jnp.concatenate([c, c_ctx[None]], axis=0)
    cb = jnp.broadcast_to(cvec[:, :, None], (n_rows, d, LANES))
    tn = 512
    out = pl.pallas_call(
        functools.partial(_ada_kernel, n_rows=n_rows, d=d, tn=tn),
        grid=(n_layers, nd // tn),
        in_specs=[pl.BlockSpec((n_rows, d, LANES), lambda l, j: (0, 0, 0)),
                  pl.BlockSpec((1, d, tn), lambda l, j: (l, 0, j)),
                  pl.BlockSpec((1, 1, tn), lambda l, j: (l, 0, j))],
        out_specs=pl.BlockSpec((1, SUBLANES, tn), lambda l, j: (l, 0, j)),
        out_shape=jax.ShapeDtypeStruct((n_layers, SUBLANES, nd), F32),
        compiler_params=_params(("arbitrary", "arbitrary")),
        name="ada_mods",
    )(cb, ada_w, ada_b.reshape(n_layers, 1, nd))
    return out.reshape(n_layers, SUBLANES, N_MOD, 1, d)


def _mod_spec(layer, m, d, row_fn, tn=None, col_fn=None):
    if tn is None:
        return pl.BlockSpec((None, None, None, 1, d), lambda *a: (layer, row_fn(*a), m, 0, 0))
    return pl.BlockSpec((None, None, None, 1, tn), lambda *a: (layer, row_fn(*a), m, 0, col_fn(*a)))


def _norm_mod_kernel(x_ref, nw_ref, shift_ref, scale_ref, o_ref):
    y = _rms(x_ref[...], nw_ref[...])
    o_ref[...] = (y * (1.0 + scale_ref[...]) + shift_ref[...]).astype(o_ref.dtype)


def _norm_mod(x, norm_w, mods, layer, m_shift, m_scale, mod_row):
    t, d = x.shape
    return pl.pallas_call(
        _norm_mod_kernel,
        grid=(t // ROW_TILE,),
        in_specs=[pl.BlockSpec((ROW_TILE, d), lambda i: (i, 0)),
                  pl.BlockSpec((1, d), lambda i: (0, 0)),
                  _mod_spec(layer, m_shift, d, mod_row),
                  _mod_spec(layer, m_scale, d, mod_row)],
        out_specs=pl.BlockSpec((ROW_TILE, d), lambda i: (i, 0)),
        out_shape=jax.ShapeDtypeStruct((t, d), BF16),
        compiler_params=_params(("arbitrary",)),
        name="norm_mod",
    )(x, norm_w.reshape(1, d), mods, mods)


def _mm_kernel(x_ref, w_ref, o_ref):
    o_ref[...] = jnp.dot(x_ref[...], w_ref[...], preferred_element_type=F32).astype(o_ref.dtype)


def _matmul(x, w, out_dtype=BF16):
    m, k = x.shape
    n = w.shape[1]
    return pl.pallas_call(
        _mm_kernel,
        grid=(n // MM_TN, m // MM_TM),
        in_specs=[pl.BlockSpec((MM_TM, k), lambda j, i: (i, 0)),
                  pl.BlockSpec((k, MM_TN), lambda j, i: (0, j))],
        out_specs=pl.BlockSpec((MM_TM, MM_TN), lambda j, i: (i, j)),
        out_shape=jax.ShapeDtypeStruct((m, n), out_dtype),
        compiler_params=_params(("arbitrary", "arbitrary")),
        name="in_proj",
    )(x, w)


def _gmlp_kernel(p_ref, vn_ref, ws_ref, bt_ref, o_ref):
    for h in range(A_HEADS):
        hs = slice(h * HEAD_DIM, (h + 1) * HEAD_DIM)
        u = jax.nn.gelu(p_ref[:, hs].astype(F32))
        v = jax.nn.gelu(p_ref[:, A_WIDTH + h * HEAD_DIM:A_WIDTH + (h + 1) * HEAD_DIM].astype(F32))
        vb = _rms(v, vn_ref[:, hs]).astype(BF16)
        for c in range(ROW_TILE // CHUNK):
            cs = slice(c * CHUNK, (c + 1) * CHUNK)
            s = jnp.dot(ws_ref[h], vb[cs], preferred_element_type=F32) + bt_ref[:, h:h + 1]
            o_ref[cs, hs] = (u[cs] * s).astype(o_ref.dtype)


def _gmlp(proj, t_rows, v_norm, w_s, b_s):
    return pl.pallas_call(
        _gmlp_kernel,
        grid=(t_rows // ROW_TILE,),
        in_specs=[pl.BlockSpec((ROW_TILE, 2 * A_WIDTH), lambda i: (i, 0)),
                  pl.BlockSpec((1, A_WIDTH), lambda i: (0, 0)),
                  pl.BlockSpec((A_HEADS, CHUNK, CHUNK), lambda i: (0, 0, 0)),
                  pl.BlockSpec((CHUNK, A_HEADS), lambda i: (0, 0))],
        out_specs=pl.BlockSpec((ROW_TILE, A_WIDTH), lambda i: (i, 0)),
        out_shape=jax.ShapeDtypeStruct((t_rows, A_WIDTH), BF16),
        compiler_params=_params(("arbitrary",)),
        name="gmlp",
    )(proj, v_norm.reshape(1, A_WIDTH), w_s.astype(BF16), jnp.transpose(b_s))


def _rope_tables(n_lat, n_ctx, d_rot, reps):
    half = d_rot // 2
    nf = half // 2
    rows = n_lat // GRID_W
    tpos = jnp.arange(n_lat, dtype=I32)
    row = tpos // GRID_W - rows // 2
    col = tpos % GRID_W - GRID_W // 2
    freq = ROPE_THETA ** (-jnp.arange(nf, dtype=F32) / nf)
    ang_r = row.astype(F32)[:, None] * freq
    ang_c = col.astype(F32)[:, None] * freq
    cos = jnp.concatenate([jnp.cos(ang_r), jnp.cos(ang_r), jnp.cos(ang_c), jnp.cos(ang_c)], axis=1)
    sin = jnp.concatenate([-jnp.sin(ang_r), jnp.sin(ang_r), -jnp.sin(ang_c), jnp.sin(ang_c)], axis=1)
    cos = jnp.concatenate([cos, jnp.ones((n_ctx, d_rot), F32)], axis=0)
    sin = jnp.concatenate([sin, jnp.zeros((n_ctx, d_rot), F32)], axis=0)
    return jnp.tile(cos, (1, reps)), jnp.tile(sin, (1, reps))


def _rope(y, cos, sin, nf):
    lane = lax.broadcasted_iota(I32, y.shape, 1)
    partner = jnp.where((lane % (2 * nf)) < nf,
                        pltpu.roll(y, LANES - nf, 1), pltpu.roll(y, nf, 1))
    return y * cos + partner * sin


def _gqa_prep_kernel(p_ref, w_ref, cos_ref, sin_ref, o_ref):
    c = pl.program_id(1)
    heads = p_ref.shape[1] // HEAD_DIM

    @pl.when(c < 4)
    def _():
        for hh in range(heads):
            x = p_ref[:, hh * HEAD_DIM:(hh + 1) * HEAD_DIM].astype(F32)
            y = _rope(_rms(x, w_ref[...]), cos_ref[...], sin_ref[...], HEAD_DIM // 4)
            o_ref[hh] = y.astype(o_ref.dtype)

    @pl.when(c == 4)
    def _():
        for hh in range(heads):
            o_ref[hh] = p_ref[:, hh * HEAD_DIM:(hh + 1) * HEAD_DIM]


def _gqa_prep(proj, geom, q_norm, k_norm, cos, sin):
    batch, n_tok, b_fn, rb_fn = geom
    wvec = jnp.stack([q_norm * (HEAD_DIM ** -0.5 * LOG2E)] * 3 + [k_norm, jnp.ones_like(k_norm)])
    wvec = wvec.reshape(5, 1, HEAD_DIM)
    first = (2 * A_WIDTH) // 512
    return pl.pallas_call(
        _gqa_prep_kernel,
        grid=(proj.shape[0] // ROW_TILE, 5),
        in_specs=[pl.BlockSpec((ROW_TILE, 512), lambda i, c: (i, first + c)),
                  pl.BlockSpec((None, 1, HEAD_DIM), lambda i, c: (c, 0, 0)),
                  pl.BlockSpec((ROW_TILE, HEAD_DIM), lambda i, c: (rb_fn(i), 0)),
                  pl.BlockSpec((ROW_TILE, HEAD_DIM), lambda i, c: (rb_fn(i), 0))],
        out_specs=pl.BlockSpec((None, 4, ROW_TILE, HEAD_DIM), lambda i, c: (b_fn(i), c, rb_fn(i), 0)),
        out_shape=jax.ShapeDtypeStruct((batch, B_HEADS + 2 * B_KV_HEADS, n_tok, HEAD_DIM), BF16),
        compiler_params=_params(("arbitrary", "arbitrary")),
        name="gqa_prep",
    )(proj, wvec, cos, sin)


def _mla_prep_kernel(p_ref, qa_ref, kva_ref, wqn_ref, wqr_ref, wkk_ref, wkv_ref,
                     nqn_ref, nqr_ref, nkn_ref, nkr_ref, cos_ref, sin_ref,
                     qc_ref, kc_ref, vc_ref):
    cq = p_ref[:, :C_Q_RANK].astype(F32)
    ckv = p_ref[:, C_Q_RANK:C_Q_RANK + C_KV_RANK].astype(F32)
    kr = p_ref[:, C_Q_RANK + C_KV_RANK:C_Q_RANK + C_KV_RANK + LANES].astype(F32)
    cqn = _rms(cq, qa_ref[...]).astype(BF16)
    ckvn = _rms(ckv, kva_ref[...]).astype(BF16)
    qn = jnp.dot(cqn, wqn_ref[...], preferred_element_type=F32)
    qr = jnp.dot(cqn, wqr_ref[...], preferred_element_type=F32)
    kn = jnp.dot(ckvn, wkk_ref[...], preferred_element_type=F32)
    vv = jnp.dot(ckvn, wkv_ref[...], preferred_element_type=F32)
    cos = cos_ref[...]
    sin = sin_ref[...]
    lane = lax.broadcasted_iota(I32, kr.shape, 1)
    low = lane < C_ROPE
    inv_qk = 1.0 / C_QK

    kr_rot = _rope(kr * nkr_ref[...], cos, sin, C_ROPE // 4)
    kr_ss = jnp.sum(kr * kr, axis=-1, keepdims=True)
    for h in range(C_HEADS):
        hs = slice(h * C_NOPE, (h + 1) * C_NOPE)
        knh = kn[:, hs]
        r = lax.rsqrt((jnp.sum(knh * knh, axis=-1, keepdims=True) + kr_ss) * inv_qk + EPS)
        kc_ref[h, :, 0:C_NOPE] = (knh * r * nkn_ref[...]).astype(kc_ref.dtype)
        kc_ref[h, :, C_NOPE:C_QK] = (kr_rot * r)[:, 0:C_ROPE].astype(kc_ref.dtype)
        vc_ref[h] = vv[:, h * C_V:(h + 1) * C_V].astype(vc_ref.dtype)

    for pr in range(C_HEADS // 2):
        qrp = qr[:, pr * LANES:(pr + 1) * LANES]
        sq = qrp * qrp
        ss_pair = (jnp.sum(jnp.where(low, sq, 0.0), axis=-1, keepdims=True),
                   jnp.sum(jnp.where(low, 0.0, sq), axis=-1, keepdims=True))
        rs = []
        for e in range(2):
            h = 2 * pr + e
            qnh = qn[:, h * C_NOPE:(h + 1) * C_NOPE]
            r = lax.rsqrt((jnp.sum(qnh * qnh, axis=-1, keepdims=True) + ss_pair[e]) * inv_qk + EPS)
            rs.append(r)
            qc_ref[h, :, 0:C_NOPE] = (qnh * r * nqn_ref[...]).astype(qc_ref.dtype)
        q_rot = _rope(qrp * jnp.where(low, rs[0], rs[1]) * nqr_ref[...], cos, sin, C_ROPE // 4)
        qc_ref[2 * pr, :, C_NOPE:C_QK] = q_rot[:, 0:C_ROPE].astype(qc_ref.dtype)
        qc_ref[2 * pr + 1, :, C_NOPE:C_QK] = q_rot[:, C_ROPE:2 * C_ROPE].astype(qc_ref.dtype)


def _mla_prep(proj, geom, q_a_norm, w_uq, kv_a_norm, w_ukv, q_norm, k_norm, cos, sin):
    batch, n_tok, b_fn, rb_fn = geom
    first = (2 * A_WIDTH + B_WIDTH + 2 * B_KV_HEADS * HEAD_DIM)
    width = IN_COLS_PAD - first
    assert first % width == 0
    wq = w_uq.reshape(C_Q_RANK, C_HEADS, C_QK)
    wqn = wq[:, :, :C_NOPE].reshape(C_Q_RANK, C_HEADS * C_NOPE).astype(BF16)
    wqr = wq[:, :, C_NOPE:].reshape(C_Q_RANK, C_HEADS * C_ROPE).astype(BF16)
    wkv = w_ukv.reshape(C_KV_RANK, C_HEADS, C_NOPE + C_V)
    wkk = wkv[:, :, :C_NOPE].reshape(C_KV_RANK, C_HEADS * C_NOPE).astype(BF16)
    wkvv = wkv[:, :, C_NOPE:].reshape(C_KV_RANK, C_HEADS * C_V).astype(BF16)
    qs = C_QK ** -0.5 * LOG2E
    nqn = (q_norm[:C_NOPE] * qs).reshape(1, C_NOPE)
    nqr = jnp.tile(q_norm[C_NOPE:] * qs, 2).reshape(1, LANES)
    nkn = k_norm[:C_NOPE].reshape(1, C_NOPE)
    nkr = jnp.tile(k_norm[C_NOPE:], 2).reshape(1, LANES)
    const = lambda shape: pl.BlockSpec(shape, lambda i: tuple(0 for _ in shape))
    head_spec = lambda dd: pl.BlockSpec((None, C_HEADS, ROW_TILE, dd), lambda i: (b_fn(i), 0, rb_fn(i), 0))
    return pl.pallas_call(
        _mla_prep_kernel,
        grid=(proj.shape[0] // ROW_TILE,),
        in_specs=[pl.BlockSpec((ROW_TILE, width), lambda i: (i, first // width)),
                  const((1, C_Q_RANK)), const((1, C_KV_RANK)),
                  const(wqn.shape), const(wqr.shape), const(wkk.shape), const(wkvv.shape),
                  const((1, C_NOPE)), const((1, LANES)), const((1, C_NOPE)), const((1, LANES)),
                  pl.BlockSpec((ROW_TILE, LANES), lambda i: (rb_fn(i), 0)),
                  pl.BlockSpec((ROW_TILE, LANES), lambda i: (rb_fn(i), 0))],
        out_specs=[head_spec(C_QK), head_spec(C_QK), head_spec(C_V)],
        out_shape=[jax.ShapeDtypeStruct((batch, C_HEADS, n_tok, C_QK), BF16),
                   jax.ShapeDtypeStruct((batch, C_HEADS, n_tok, C_QK), BF16),
                   jax.ShapeDtypeStruct((batch, C_HEADS, n_tok, C_V), BF16)],
        compiler_params=_params(("arbitrary",)),
        name="mla_prep",
    )(proj, q_a_norm.reshape(1, C_Q_RANK), kv_a_norm.reshape(1, C_KV_RANK), wqn, wqr, wkk, wkvv,
      nqn, nqr, nkn, nkr, cos, sin)


def _attn_kernel(bound_ref, q_ref, k_ref, v_ref, *rest, group, n_lat, bounded):
    o_ref, m_sc, l_sc, acc_sc = rest[-4:]
    tq, dk = q_ref.shape[1], q_ref.shape[2]
    dv = v_ref.shape[1]
    n_tail = k_ref.shape[0] - n_lat
    q = q_ref[...].reshape(group * tq, dk)
    l_sc[...] = jnp.zeros(l_sc.shape, F32)
    acc_sc[...] = jnp.zeros(acc_sc.shape, F32)
    if not bounded:
        m_sc[...] = jnp.full(m_sc.shape, -jnp.inf, F32)

    def lane_fold(p):
        out = p[:, 0:LANES]
        for j in range(1, p.shape[1] // LANES):
            out = out + p[:, j * LANES:(j + 1) * LANES]
        return out

    def chunk(start, size):
        k = k_ref[pl.ds(start, size), :]
        v = v_ref[pl.ds(start, size), :]
        s = lax.dot_general(q, k, (((1,), (1,)), ((), ())), preferred_element_type=F32)
        if bounded:
            p = jnp.exp2(s - bound_ref[0])
            l_sc[...] += lane_fold(p)
            acc_sc[...] += jnp.dot(p.astype(BF16), v, preferred_element_type=F32)
        else:
            m_prev = m_sc[...]
            m_new = jnp.maximum(m_prev, jnp.max(s, axis=-1, keepdims=True))
            alpha = jnp.exp2(m_prev - m_new)
            p = jnp.exp2(s - m_new[:, 0:1])
            l_sc[...] = alpha * l_sc[...] + lane_fold(p)
            acc_sc[...] = alpha[:, 0:1] * acc_sc[...] + jnp.dot(p.astype(BF16), v, preferred_element_type=F32)
            m_sc[...] = m_new

    if n_lat:
        def body(c, carry):
            chunk(pl.multiple_of(c * ATT_TK, ATT_TK), ATT_TK)
            return carry
        lax.fori_loop(0, n_lat // ATT_TK, body, 0, unroll=2)
    chunk(n_lat, n_tail)
    o = acc_sc[...] / jnp.sum(l_sc[...], axis=-1, keepdims=True)
    for g in range(group):
        o_ref[:, g * dv:(g + 1) * dv] = o[g * tq:(g + 1) * tq].astype(o_ref.dtype)


def _attention_call(bound, q_arr, k_arr, v_arr, prev_out, heads, n_kv, group, tq, n_lat, n_ctx, ctx_only,
                    t_rows, bounded, name):
    q_head0, k_head0, v_head0 = heads
    batch, _, n_tok, dk = q_arr.shape
    dv = v_arr.shape[-1]
    m = group * tq
    if ctx_only:
        assert tq == n_ctx and n_lat % n_ctx == 0
        nq, q0, key_rows, key_blk = 1, n_lat // tq, n_ctx, n_lat // n_ctx
        out_row = lambda b, qi: batch * (n_lat // tq) + b
    else:
        assert n_lat % tq == 0 and n_lat % ATT_TK == 0
        nq, q0, key_rows, key_blk = n_lat // tq, 0, n_tok, 0
        out_row = lambda b, qi: b * nq + qi
    in_specs = [pl.BlockSpec((None, group, tq, dk), lambda b, g, qi, *_: (b, q_head0 // group + g, q0 + qi, 0)),
                pl.BlockSpec((None, None, key_rows, dk), lambda b, g, qi, *_: (b, k_head0 + g, key_blk, 0)),
                pl.BlockSpec((None, None, key_rows, dv), lambda b, g, qi, *_: (b, v_head0 + g, key_blk, 0))]
    args = [bound, q_arr, k_arr, v_arr]
    aliases = {}
    if prev_out is not None:
        in_specs.append(pl.BlockSpec(memory_space=pl.ANY))
        args.append(prev_out)
        aliases = {len(args) - 1: 0}
    grid_spec = pltpu.PrefetchScalarGridSpec(
        num_scalar_prefetch=1,
        grid=(batch, n_kv, nq),
        in_specs=in_specs,
        out_specs=pl.BlockSpec((tq, group * dv), lambda b, g, qi, *_: (out_row(b, qi), g)),
        scratch_shapes=[pltpu.VMEM((m, LANES), F32), pltpu.VMEM((m, LANES), F32), pltpu.VMEM((m, dv), F32)],
    )
    return pl.pallas_call(
        functools.partial(_attn_kernel, group=group, n_lat=0 if ctx_only else n_lat, bounded=bounded),
        grid_spec=grid_spec,
        out_shape=jax.ShapeDtypeStruct((t_rows, n_kv * group * dv), BF16),
        input_output_aliases=aliases,
        compiler_params=_params(("arbitrary", "arbitrary", "arbitrary")),
        name=name,
    )(*args)


def _attention(bound, q_arr, k_arr, v_arr, heads, n_kv, group, tq, n_lat, n_ctx, with_ctx, t_rows, name):
    tq = min(tq, n_lat)

    def run(bounded):
        def f(bound, q_arr, k_arr, v_arr):
            out = _attention_call(bound, q_arr, k_arr, v_arr, None, heads, n_kv, group, tq, n_lat, n_ctx,
                                  False, t_rows, bounded, name)
            if with_ctx:
                out = _attention_call(bound, q_arr, k_arr, v_arr, out, heads, n_kv, group, n_ctx, n_lat, n_ctx,
                                      True, t_rows, bounded, name + "_ctx")
            return out
        return f

    return lax.cond(bound[0] <= SCORE_BOUND_LIMIT, run(True), run(False), bound, q_arr, k_arr, v_arr)


def _score_bound(q_norm, k_norm, dk):
    b = dk ** 0.5 * jnp.max(jnp.abs(q_norm)) * jnp.max(jnp.abs(k_norm)) * (LOG2E * BOUND_SLACK)
    return b.reshape(1).astype(F32)


def _wout_kernel(a_ref, b_ref, c_ref, w_ref, x_ref, g_ref, o_ref):
    acc = jnp.dot(a_ref[...], w_ref[0:A_WIDTH, :], preferred_element_type=F32)
    acc += jnp.dot(b_ref[...], w_ref[A_WIDTH:A_WIDTH + B_WIDTH, :], preferred_element_type=F32)
    acc += jnp.dot(c_ref[...], w_ref[A_WIDTH + B_WIDTH:, :], preferred_element_type=F32)
    o_ref[...] = x_ref[...] + g_ref[...] * acc


def _out_proj(a_out, b_out, c_out, w, x, mods, layer, t_rows, mod_row):
    d = w.shape[1]
    return pl.pallas_call(
        _wout_kernel,
        grid=(d // MM_TN, t_rows // MM_TM),
        in_specs=[pl.BlockSpec((MM_TM, A_WIDTH), lambda j, i: (i, 0)),
                  pl.BlockSpec((MM_TM, B_WIDTH), lambda j, i: (i, 0)),
                  pl.BlockSpec((MM_TM, C_WIDTH), lambda j, i: (i, 0)),
                  pl.BlockSpec((w.shape[0], MM_TN), lambda j, i: (0, j)),
                  pl.BlockSpec((MM_TM, MM_TN), lambda j, i: (i, j)),
                  _mod_spec(layer, 2, d, lambda j, i: mod_row(i), MM_TN, lambda j, i: j)],
        out_specs=pl.BlockSpec((MM_TM, MM_TN), lambda j, i: (i, j)),
        out_shape=jax.ShapeDtypeStruct((t_rows, d), F32),
        compiler_params=_params(("arbitrary", "arbitrary")),
        name="out_proj",
    )(a_out, b_out, c_out, w, x, mods)


def _first_index(hit_val, x, iota, big):
    return jnp.min(jnp.where(x == hit_val, iota, big), axis=0, keepdims=True)


def _router_kernel(x_ref, nw_ref, shift_ref, scale_ref, rwt_ref, bias_ref,
                   hp_ref, eidx_ref, gate_ref, rank_ref, cnt_ref, carry_sc):
    i = pl.program_id(0)
    tm, d = x_ref.shape
    h = _rms(x_ref[...], nw_ref[...]) * (1.0 + scale_ref[...]) + shift_ref[...]
    hp_ref[...] = _pack_bf16_pair(h[:, :d // 2], h[:, d // 2:])

    logits = lax.dot_general(rwt_ref[...], h, (((1,), (1,)), ((), ())),
                             precision=lax.Precision.HIGHEST, preferred_element_type=F32)
    scores = jax.nn.sigmoid(logits)
    sel = scores + bias_ref[:, 0:1]
    neg = jnp.float32(-jnp.inf)

    iota_n = lax.broadcasted_iota(I32, (N_GROUPS, tm), 0).astype(F32)
    iota_e = lax.broadcasted_iota(I32, (N_EXPERTS, tm), 0).astype(F32)

    gscore = jnp.zeros((N_GROUPS, tm), F32)
    for g in range(N_GROUPS):
        blk = sel[g * GROUP_SIZE:(g + 1) * GROUP_SIZE]
        m1 = jnp.max(blk, axis=0, keepdims=True)
        first = _first_index(m1, blk, iota_n, float(GROUP_SIZE))
        m2 = jnp.max(jnp.where(iota_n == first, neg, blk), axis=0, keepdims=True)
        gscore = jnp.where(iota_n == float(g), m1 + m2, gscore)

    gsel = jnp.zeros((N_GROUPS, tm), F32)
    work = gscore
    for _ in range(TOPK_GROUPS):
        m = jnp.max(work, axis=0, keepdims=True)
        hit = iota_n == _first_index(m, work, iota_n, float(N_GROUPS))
        gsel = jnp.where(hit, 1.0, gsel)
        work = jnp.where(hit, neg, work)
    emask = jnp.concatenate(
        [jnp.broadcast_to(gsel[g:g + 1], (GROUP_SIZE, tm)) for g in range(N_GROUPS)], axis=0)

    work = jnp.where(emask > 0.5, sel, neg)
    picks = []
    onehot = jnp.zeros((N_EXPERTS, tm), F32)
    for _ in range(TOP_K):
        m = jnp.max(work, axis=0, keepdims=True)
        pick = _first_index(m, work, iota_e, float(N_EXPERTS))
        picks.append(pick)
        hit = iota_e == pick
        onehot = jnp.where(hit, 1.0, onehot)
        work = jnp.where(hit, neg, work)

    w_un = onehot * scores
    gates = w_un / jnp.sum(w_un, axis=0, keepdims=True) * ROUTED_SCALE

    @pl.when(i == 0)
    def _():
        carry_sc[...] = jnp.zeros(carry_sc.shape, F32)

    before = (lax.broadcasted_iota(I32, (tm, tm), 0) < lax.broadcasted_iota(I32, (tm, tm), 1)).astype(BF16)
    rank = carry_sc[:, 0:1] + jnp.dot(onehot.astype(BF16), before, preferred_element_type=F32)
    carry_sc[...] = carry_sc[...] + jnp.sum(onehot, axis=1, keepdims=True)
    cnt_ref[...] = carry_sc[...]

    eidx_ref[...] = jnp.zeros(eidx_ref.shape, I32)
    gate_ref[...] = jnp.zeros(gate_ref.shape, F32)
    rank_ref[...] = jnp.zeros(rank_ref.shape, I32)
    for k in range(TOP_K):
        hit = iota_e == picks[k]
        eidx_ref[k:k + 1, :] = picks[k].astype(I32)
        gate_ref[k:k + 1, :] = jnp.sum(jnp.where(hit, gates, 0.0), axis=0, keepdims=True)
        rank_ref[k:k + 1, :] = jnp.sum(jnp.where(hit, rank, 0.0), axis=0, keepdims=True).astype(I32)


def _norm_router(x, t_rows, norm_w, mods, layer, mod_row, router_w, router_bias):
    d = x.shape[1]
    col = lambda dt: (pl.BlockSpec((SUBLANES, ROW_TILE), lambda i: (0, i)),
                      jax.ShapeDtypeStruct((SUBLANES, t_rows), dt))
    specs, shapes = zip(col(I32), col(F32), col(I32))
    return pl.pallas_call(
        _router_kernel,
        grid=(t_rows // ROW_TILE,),
        in_specs=[pl.BlockSpec((ROW_TILE, d), lambda i: (i, 0)),
                  pl.BlockSpec((1, d), lambda i: (0, 0)),
                  _mod_spec(layer, 3, d, mod_row),
                  _mod_spec(layer, 4, d, mod_row),
                  pl.BlockSpec((N_EXPERTS, d), lambda i: (0, 0)),
                  pl.BlockSpec((N_EXPERTS, LANES), lambda i: (0, 0))],
        out_specs=[pl.BlockSpec((ROW_TILE, d // 2), lambda i: (i, 0)), *specs,
                   pl.BlockSpec((N_EXPERTS, LANES), lambda i: (0, 0))],
        out_shape=[jax.ShapeDtypeStruct((t_rows, d // 2), U32), *shapes,
                   jax.ShapeDtypeStruct((N_EXPERTS, LANES), F32)],
        scratch_shapes=[pltpu.VMEM((N_EXPERTS, LANES), F32)],
        compiler_params=_params(("arbitrary",)),
        name="norm_router",
    )(x, norm_w.reshape(1, d), mods, mods, jnp.transpose(router_w),
      jnp.broadcast_to(router_bias[:, None], (N_EXPERTS, LANES)))


def _row_copy(src, src_row, dst, dst_row, sem):
    return pltpu.make_async_copy(src.at[pl.ds(src_row, 1)], dst.at[pl.ds(dst_row, 1)], sem)


_FILL_PIECES = tuple(1 << b for b in range(EXP_TM.bit_length() - 2, -1, -1))


def _zero_fill(pstart_ref, plen_ref, tail_ref, xs_hbm, zero_sc, sem_z):
    zero_sc[...] = jnp.zeros(zero_sc.shape, zero_sc.dtype)
    n_tail = (xs_hbm.shape[0] - tail_ref[0]) // _FILL_PIECES[0]

    def pad_copies(e, wait):
        start = pstart_ref[e]
        head = (-start) & (SUBLANES - 1)
        for r in range(SUBLANES - 1):
            cp = pltpu.make_async_copy(zero_sc.at[pl.ds(0, 1)], xs_hbm.at[pl.ds(start + r, 1)], sem_z)

            @pl.when(r < head)
            def _():
                cp.wait() if wait else cp.start()

        off = start + head
        rem = plen_ref[e] - head
        for piece in _FILL_PIECES:
            if piece < SUBLANES:
                break
            on = (rem & piece) != 0
            cp = pltpu.make_async_copy(zero_sc.at[pl.ds(0, piece)],
                                       xs_hbm.at[pl.ds(pl.multiple_of(off, SUBLANES), piece)], sem_z)

            @pl.when(on)
            def _():
                cp.wait() if wait else cp.start()

            off = off + jnp.where(on, piece, 0)

    def tail_copy(j, wait):
        row = pl.multiple_of(tail_ref[0] + j * _FILL_PIECES[0], _FILL_PIECES[0])
        cp = pltpu.make_async_copy(zero_sc, xs_hbm.at[pl.ds(row, _FILL_PIECES[0])], sem_z)
        cp.wait() if wait else cp.start()

    for wait in (False, True):
        lax.fori_loop(0, N_EXPERTS, lambda e, c: (pad_copies(e, wait), c)[1], 0)
        lax.fori_loop(0, n_tail, lambda j, c: (tail_copy(j, wait), c)[1], 0)


def _dispatch_kernel(pstart_ref, plen_ref, tail_ref, slot_hbm, h_ref, xs_hbm, slot_sm, zero_sc, sem_s, sem, sem_z):
    i = pl.program_id(0)
    n = slot_sm.shape[0]
    cp = pltpu.make_async_copy(slot_hbm.at[pl.ds(pl.multiple_of(i * n, n), n)], slot_sm, sem_s)
    cp.start()

    @pl.when(i == 0)
    def _():
        _zero_fill(pstart_ref, plen_ref, tail_ref, xs_hbm, zero_sc, sem_z)

    cp.wait()

    def issue(t, carry):
        for k in range(TOP_K):
            _row_copy(h_ref, t, xs_hbm, slot_sm[t * SUBLANES + k], sem).start()
        return carry

    lax.fori_loop(0, DISP_T, issue, 0)

    def drain(t, carry):
        for k in range(TOP_K):
            _row_copy(h_ref, 0, xs_hbm, 0, sem).wait()
        return carry

    lax.fori_loop(0, DISP_T, drain, 0)


def _dispatch(hp, slots, pad_start, pad_len, tail_start, n_slots):
    t, w = hp.shape
    grid_spec = pltpu.PrefetchScalarGridSpec(
        num_scalar_prefetch=3,
        grid=(t // DISP_T,),
        in_specs=[pl.BlockSpec(memory_space=pl.ANY), pl.BlockSpec((DISP_T, w), lambda i, *_: (i, 0))],
        out_specs=pl.BlockSpec(memory_space=pl.ANY),
        scratch_shapes=[pltpu.SMEM((DISP_T * SUBLANES,), I32),
                        pltpu.VMEM((_FILL_PIECES[0], w), U32),
                        pltpu.SemaphoreType.DMA(()), pltpu.SemaphoreType.DMA(()), pltpu.SemaphoreType.DMA(())],
    )
    return pl.pallas_call(
        _dispatch_kernel,
        grid_spec=grid_spec,
        out_shape=jax.ShapeDtypeStruct((n_slots, w), U32),
        compiler_params=_params(("arbitrary",)),
        name="moe_dispatch",
    )(pad_start, pad_len, tail_start, slots, hp)


def _ffn(lo, hi, wg, wu, wd):
    half = lo.shape[1]
    hg = jnp.dot(lo, wg[:half], preferred_element_type=F32) + jnp.dot(hi, wg[half:], preferred_element_type=F32)
    hu = jnp.dot(lo, wu[:half], preferred_element_type=F32) + jnp.dot(hi, wu[half:], preferred_element_type=F32)
    hb = (jax.nn.silu(hg) * hu).astype(BF16)
    return jnp.dot(hb, wd, preferred_element_type=F32)


def _expert_kernel(te_ref, nu_ref, xs_ref, wg_ref, wu_ref, wd_ref, ys_ref, wg_sc, wu_sc, wd_sc):
    i = pl.program_id(0)
    prev = te_ref[jnp.maximum(i - 1, 0)]

    @pl.when(jnp.logical_or(i == 0, te_ref[i] != prev))
    def _():
        wg_sc[...] = wg_ref[...].astype(BF16)
        wu_sc[...] = wu_ref[...].astype(BF16)
        wd_sc[...] = wd_ref[...].astype(BF16)

    @pl.when(i < nu_ref[0])
    def _():
        lo, hi = _unpack_bf16_pair(xs_ref[...])
        y = _ffn(lo.astype(BF16), hi.astype(BF16), wg_sc[...], wu_sc[...], wd_sc[...])
        half = y.shape[1] // 2
        ys_ref[...] = _pack_bf16_pair(y[:, :half], y[:, half:])

    @pl.when(i >= nu_ref[0])
    def _():
        ys_ref[...] = jnp.zeros(ys_ref.shape, ys_ref.dtype)


def _experts(xs, tile_expert, n_used, w_gate, w_up, w_down, layer):
    n_slots, w = xs.shape
    _, _, d, f = w_gate.shape
    n_tiles = n_slots // EXP_TM
    grid_spec = pltpu.PrefetchScalarGridSpec(
        num_scalar_prefetch=2,
        grid=(n_tiles,),
        in_specs=[pl.BlockSpec((EXP_TM, w), lambda i, te, nu: (jnp.minimum(i, nu[0] - 1), 0)),
                  pl.BlockSpec((None, None, d, f), lambda i, te, nu: (layer, te[i], 0, 0)),
                  pl.BlockSpec((None, None, d, f), lambda i, te, nu: (layer, te[i], 0, 0)),
                  pl.BlockSpec((None, None, f, d), lambda i, te, nu: (layer, te[i], 0, 0))],
        out_specs=pl.BlockSpec((EXP_TM, w), lambda i, te, nu: (i, 0)),
        scratch_shapes=[pltpu.VMEM((d, f), BF16), pltpu.VMEM((d, f), BF16), pltpu.VMEM((f, d), BF16)],
    )
    return pl.pallas_call(
        _expert_kernel,
        grid_spec=grid_spec,
        out_shape=jax.ShapeDtypeStruct((n_slots, w), U32),
        compiler_params=_params(("arbitrary",)),
        name="moe_experts",
    )(tile_expert, n_used, xs, w_gate, w_up, w_down)


def _combine_kernel(slot_hbm, ys_hbm, hp_ref, gate_ref, sg_ref, su_ref, sd_ref, x_ref, g5_ref,
                    o_ref, slot_sm, buf, sem_s, sem):
    i = pl.program_id(0)
    n = slot_sm.shape[0]
    cp = pltpu.make_async_copy(slot_hbm.at[pl.ds(pl.multiple_of(i * n, n), n)], slot_sm, sem_s)
    cp.start()
    cp.wait()

    def issue(t, carry):
        for k in range(TOP_K):
            _row_copy(ys_hbm, slot_sm[t * SUBLANES + k], buf.at[k], t, sem).start()
        return carry

    lax.fori_loop(0, COMB_T, issue, 0)

    lo, hi = _unpack_bf16_pair(hp_ref[...])
    y = _ffn(lo.astype(BF16), hi.astype(BF16), sg_ref[...], su_ref[...], sd_ref[...])

    def drain(t, carry):
        for k in range(TOP_K):
            _row_copy(ys_hbm, 0, buf.at[k], 0, sem).wait()
        return carry

    lax.fori_loop(0, COMB_T, drain, 0)

    half = y.shape[1] // 2
    r_lo = y[:, :half]
    r_hi = y[:, half:]
    for k in range(TOP_K):
        e_lo, e_hi = _unpack_bf16_pair(buf[k])
        gk = gate_ref[:, k:k + 1]
        r_lo = r_lo + gk * e_lo
        r_hi = r_hi + gk * e_hi
    o_ref[:, :half] = x_ref[:, :half] + g5_ref[:, :half] * r_lo
    o_ref[:, half:] = x_ref[:, half:] + g5_ref[:, half:] * r_hi


def _combine(ys, hp, slots, gates, sh_gate, sh_up, sh_down, x, mods, layer, t_rows, mod_row):
    d = x.shape[1]
    w = hp.shape[1]
    f = sh_gate.shape[1]
    return pl.pallas_call(
        _combine_kernel,
        grid=(t_rows // COMB_T,),
        in_specs=[pl.BlockSpec(memory_space=pl.ANY), pl.BlockSpec(memory_space=pl.ANY),
                  pl.BlockSpec((COMB_T, w), lambda i: (i, 0)),
                  pl.BlockSpec((COMB_T, SUBLANES), lambda i: (i, 0)),
                  pl.BlockSpec((d, f), lambda i: (0, 0)),
                  pl.BlockSpec((d, f), lambda i: (0, 0)),
                  pl.BlockSpec((f, d), lambda i: (0, 0)),
                  pl.BlockSpec((COMB_T, d), lambda i: (i, 0)),
                  _mod_spec(layer, 5, d, lambda i: mod_row(i * COMB_T // ROW_TILE))],
        out_specs=pl.BlockSpec((COMB_T, d), lambda i: (i, 0)),
        out_shape=jax.ShapeDtypeStruct((t_rows, d), F32),
        scratch_shapes=[pltpu.SMEM((COMB_T * SUBLANES,), I32),
                        pltpu.VMEM((TOP_K, COMB_T, w), U32),
                        pltpu.SemaphoreType.DMA(()), pltpu.SemaphoreType.DMA(())],
        compiler_params=_params(("arbitrary",)),
        name="moe_combine",
    )(slots, ys, hp, gates, sh_gate.astype(BF16), sh_up.astype(BF16), sh_down.astype(BF16), x, mods)


def _moe(x, t_rows, norm_w, mods, layer, mod_row, router_w, router_bias,
         w_gate, w_up, w_down, sh_gate, sh_up, sh_down):
    hp, eidx, gate, rank, cnt = _norm_router(x, t_rows, norm_w, mods, layer, mod_row, router_w, router_bias)

    n_tiles = (t_rows * TOP_K + N_EXPERTS * (EXP_TM - 1) + EXP_TM - 1) // EXP_TM
    counts = cnt[:, 0].astype(I32)
    padded = (counts + EXP_TM - 1) // EXP_TM * EXP_TM
    ends = jnp.cumsum(padded)
    base = ends - padded
    experts = jnp.arange(N_EXPERTS, dtype=I32)
    base_of = jnp.sum(jnp.where(eidx[:, :, None] == experts, base, 0), axis=-1)
    slots = jnp.transpose(base_of + rank).reshape(-1)
    n_used = (ends[-1] // EXP_TM).astype(I32)
    tile_start = jnp.arange(n_tiles, dtype=I32) * EXP_TM
    last_start = (n_used - 1) * EXP_TM
    te = jnp.sum(jnp.minimum(tile_start, last_start)[:, None] >= ends[None, :], axis=1).astype(I32)
    te = jnp.minimum(te, N_EXPERTS - 1)

    xs = _dispatch(hp, slots, base + counts, padded - counts, ends[-1:], n_tiles * EXP_TM)
    ys = _experts(xs, te, n_used.reshape(1), w_gate, w_up, w_down, layer)
    return _combine(ys, hp, slots, jnp.transpose(gate), sh_gate, sh_up, sh_down, x, mods, layer, t_rows, mod_row)


def kernel(x, c, ctx, c_ctx, ada_w, ada_b, norm1_w, norm2_w, w_in, gm_v_norm, gm_w_s, gm_b_s, gqa_q_norm, gqa_k_norm, mla_q_a_norm, mla_w_uq, mla_kv_a_norm, mla_w_ukv, mla_q_norm, mla_k_norm, w_out, router_w, router_bias, exp_w_gate, exp_w_up, exp_w_down, sh_w_gate, sh_w_up, sh_w_down):
    batch, n_lat, d = x.shape
    n_ctx = ctx.shape[1]
    depth = ada_w.shape[0]
    assert n_ctx == ROW_TILE and n_lat % MM_TM == 0 and (batch * n_ctx) % MM_TM == 0
    t_lat = batch * n_lat
    t_all = t_lat + batch * n_ctx
    n_tok = n_lat + n_ctx
    lat_tiles = n_lat // ROW_TILE

    mod_row = lambda i: jnp.minimum(i // lat_tiles, batch)
    is_lat = lambda i: i < batch * lat_tiles
    b_fn = lambda i: jnp.where(is_lat(i), i // lat_tiles, i - batch * lat_tiles)
    rb_fn = lambda i: jnp.where(is_lat(i), i % lat_tiles, lat_tiles)
    geom = (batch, n_tok, b_fn, rb_fn)
    mm_mod_row = lambda i: mod_row(i * (MM_TM // ROW_TILE))

    mods = _ada_mods(c, c_ctx, ada_w, ada_b)
    cos_b, sin_b = _rope_tables(n_lat, n_ctx, HEAD_DIM, 1)
    cos_c, sin_c = _rope_tables(n_lat, n_ctx, C_ROPE, LANES // C_ROPE)

    xs = jnp.concatenate([x.reshape(t_lat, d), ctx.reshape(batch * n_ctx, d)], axis=0)
    for l in range(depth):
        need_ctx = l < depth - 1
        t_rows = t_all if need_ctx else t_lat
        w_in_l = jnp.pad(w_in[l].astype(BF16), ((0, 0), (0, IN_COLS_PAD - w_in.shape[2])))

        h = _norm_mod(xs, norm1_w[l], mods, l, 0, 1, mod_row)
        proj = _matmul(h, w_in_l)
        a_out = _gmlp(proj, t_rows, gm_v_norm[l], gm_w_s[l], gm_b_s[l])
        qkv = _gqa_prep(proj, geom, gqa_q_norm[l], gqa_k_norm[l], cos_b, sin_b)
        b_out = _attention(_score_bound(gqa_q_norm[l], gqa_k_norm[l], HEAD_DIM), qkv, qkv, qkv,
                           (0, B_HEADS, B_HEADS + B_KV_HEADS), B_KV_HEADS, B_GROUP, GQA_TQ,
                           n_lat, n_ctx, need_ctx, t_rows, "gqa_attn")
        qc, kc, vc = _mla_prep(proj, geom, mla_q_a_norm[l], mla_w_uq[l], mla_kv_a_norm[l], mla_w_ukv[l],
                               mla_q_norm[l], mla_k_norm[l], cos_c, sin_c)
        c_out = _attention(_score_bound(mla_q_norm[l], mla_k_norm[l], C_QK), qc, kc, vc,
                           (0, 0, 0), C_HEADS, 1, MLA_TQ, n_lat, n_ctx, need_ctx, t_rows, "mla_attn")
        xs = _out_proj(a_out, b_out, c_out, w_out[l].astype(BF16), xs, mods, l, t_rows, mm_mod_row)
        xs = _moe(xs, t_rows, norm2_w[l], mods, l, mod_row, router_w[l], router_bias[l],
                  exp_w_gate, exp_w_up, exp_w_down, sh_w_gate[l], sh_w_up[l], sh_w_down[l])
    return xs[:t_lat].reshape(batch, n_lat, d)
```

```python
import functools

import jax
import jax.numpy as jnp
from jax import lax
from jax.experimental import pallas as pl
from jax.experimental.pallas import tpu as pltpu

F32 = jnp.float32
BF16 = jnp.bfloat16
U32 = jnp.uint32
I32 = jnp.int32

GRID_W = 64
CHUNK = 128
HEAD_DIM = 128
EPS = 1e-6
ROPE_THETA = 10000.0
A_HEADS = 8
B_HEADS = 12
B_KV_HEADS = 4
B_GROUP = B_HEADS // B_KV_HEADS
C_HEADS = 12
C_Q_RANK = 768
C_KV_RANK = 512
C_NOPE = 128
C_ROPE = 64
C_V = 128
C_QK = C_NOPE + C_ROPE
A_WIDTH = A_HEADS * HEAD_DIM
B_WIDTH = B_HEADS * HEAD_DIM
C_WIDTH = C_HEADS * C_V
N_EXPERTS = 64
TOP_K = 6
N_GROUPS = 8
GROUP_SIZE = N_EXPERTS // N_GROUPS
TOPK_GROUPS = 4
ROUTED_SCALE = 2.5
N_MOD = 6

LANES = 128
SUBLANES = 8
ROW_TILE = 256
MM_TM = 512
MM_TN = 1024
GQA_TQ = 512
MLA_TQ = 1024
ATT_TK = 512
LOG2E = 1.4426950408889634
BOUND_SLACK = 1.02
SCORE_BOUND_LIMIT = 60.0
EXP_TM = 256
DISP_T = 256
COMB_T = 128
IN_COLS_PAD = 6144
VMEM_LIMIT = 56 * 1024 * 1024


def _params(sem, vmem=VMEM_LIMIT):
    return pltpu.CompilerParams(dimension_semantics=sem, vmem_limit_bytes=vmem)


def _rms(x, w):
    return x * lax.rsqrt(jnp.mean(x * x, axis=-1, keepdims=True) + EPS) * w


def _pack_bf16_pair(a, b):
    ua = lax.bitcast_convert_type(a.astype(BF16).astype(F32), U32)
    ub = lax.bitcast_convert_type(b.astype(BF16).astype(F32), U32)
    return (ua >> 16) | ub


def _unpack_bf16_pair(u):
    a = lax.bitcast_convert_type(u << 16, F32)
    b = lax.bitcast_convert_type(u & jnp.uint32(0xFFFF0000), F32)
    return a, b


def _ada_kernel(cb_ref, w_ref, b_ref, o_ref, silu_sc, *, n_rows, d, tn):
    nj = tn // LANES

    @pl.when(jnp.logical_and(pl.program_id(0) == 0, pl.program_id(1) == 0))
    def _():
        cv = cb_ref[...]
        silu_sc[...] = cv * jax.nn.sigmoid(cv)

    def body(kc, accs):
        k0 = pl.multiple_of(kc * SUBLANES, SUBLANES)
        wk = w_ref[0, pl.ds(k0, SUBLANES), :]
        out = []
        for r in range(n_rows):
            sv = silu_sc[r, pl.ds(k0, SUBLANES), :]
            for j in range(nj):
                out.append(accs[r * nj + j] + wk[:, j * LANES:(j + 1) * LANES] * sv)
        return tuple(out)

    init = tuple(jnp.zeros((SUBLANES, LANES), F32) for _ in range(n_rows * nj))
    accs = lax.fori_loop(0, d // SUBLANES, body, init, unroll=8)
    o_ref[0] = jnp.zeros((SUBLANES, tn), F32)
    for r in range(n_rows):
        for j in range(nj):
            o_ref[0, r:r + 1, j * LANES:(j + 1) * LANES] = (
                jnp.sum(accs[r * nj + j], axis=0, keepdims=True) + b_ref[0, :, j * LANES:(j + 1) * LANES])


def _ada_mods(c, c_ctx, ada_w, ada_b):
    n_layers, d, nd = ada_w.shape
    n_rows = c.shape[0] + 1
    assert n_rows <= SUBLANES
    cvec = jnp.concatenate([c, c_ctx[None]], axis=0)
    cb = jnp.broadcast_to(cvec[:, :, None], (n_rows, d, LANES))
    tn = 512
    out = pl.pallas_call(
        functools.partial(_ada_kernel, n_rows=n_rows, d=d, tn=tn),
        grid=(n_layers, nd // tn),
        in_specs=[pl.BlockSpec((n_rows, d, LANES), lambda l, j: (0, 0, 0)),
                  pl.BlockSpec((1, d, tn), lambda l, j: (l, 0, j)),
                  pl.BlockSpec((1, 1, tn), lambda l, j: (l, 0, j))],
        out_specs=pl.BlockSpec((1, SUBLANES, tn), lambda l, j: (l, 0, j)),
        out_shape=jax.ShapeDtypeStruct((n_layers, SUBLANES, nd), F32),
        scratch_shapes=[pltpu.VMEM((n_rows, d, LANES), F32)],
        compiler_params=_params(("arbitrary", "arbitrary")),
        name="ada_mods",
    )(cb, ada_w, ada_b.reshape(n_layers, 1, nd))
    return out.reshape(n_layers, SUBLANES, N_MOD, 1, d)


def _mod_spec(layer, m, d, row_fn, tn=None, col_fn=None):
    if tn is None:
        return pl.BlockSpec((None, None, None, 1, d), lambda *a: (layer, row_fn(*a), m, 0, 0))
    return pl.BlockSpec((None, None, None, 1, tn), lambda *a: (layer, row_fn(*a), m, 0, col_fn(*a)))


def _in_proj_kernel(x_ref, nw_ref, shift_ref, scale_ref, w_ref, o_ref, h_sc):
    @pl.when(pl.program_id(1) == 0)
    def _():
        y = _rms(x_ref[...], nw_ref[...])
        h_sc[...] = (y * (1.0 + scale_ref[...]) + shift_ref[...]).astype(h_sc.dtype)

    o_ref[...] = jnp.dot(h_sc[...], w_ref[...], preferred_element_type=F32).astype(o_ref.dtype)


def _in_proj(x, norm_w, mods, layer, mod_row, w):
    m, d = x.shape
    n = w.shape[1]
    return pl.pallas_call(
        _in_proj_kernel,
        grid=(m // MM_TM, n // MM_TN),
        in_specs=[pl.BlockSpec((MM_TM, d), lambda i, j: (i, 0)),
                  pl.BlockSpec((1, d), lambda i, j: (0, 0)),
                  _mod_spec(layer, 0, d, lambda i, j: mod_row(i)),
                  _mod_spec(layer, 1, d, lambda i, j: mod_row(i)),
                  pl.BlockSpec((d, MM_TN), lambda i, j: (0, j))],
        out_specs=pl.BlockSpec((MM_TM, MM_TN), lambda i, j: (i, j)),
        out_shape=jax.ShapeDtypeStruct((m, n), BF16),
        scratch_shapes=[pltpu.VMEM((MM_TM, d), BF16)],
        compiler_params=_params(("arbitrary", "arbitrary")),
        name="in_proj",
    )(x, norm_w.reshape(1, d), mods, mods, w)


def _gmlp_kernel(p_ref, vn_ref, ws_ref, bt_ref, o_ref):
    for h in range(A_HEADS):
        hs = slice(h * HEAD_DIM, (h + 1) * HEAD_DIM)
        u = jax.nn.gelu(p_ref[:, hs].astype(F32))
        v = jax.nn.gelu(p_ref[:, A_WIDTH + h * HEAD_DIM:A_WIDTH + (h + 1) * HEAD_DIM].astype(F32))
        vb = _rms(v, vn_ref[:, hs]).astype(BF16)
        for c in range(ROW_TILE // CHUNK):
            cs = slice(c * CHUNK, (c + 1) * CHUNK)
            s = jnp.dot(ws_ref[h], vb[cs], preferred_element_type=F32) + bt_ref[:, h:h + 1]
            o_ref[cs, hs] = (u[cs] * s).astype(o_ref.dtype)


def _gmlp(proj, t_rows, v_norm, w_s, b_s):
    return pl.pallas_call(
        _gmlp_kernel,
        grid=(t_rows // ROW_TILE,),
        in_specs=[pl.BlockSpec((ROW_TILE, 2 * A_WIDTH), lambda i: (i, 0)),
                  pl.BlockSpec((1, A_WIDTH), lambda i: (0, 0)),
                  pl.BlockSpec((A_HEADS, CHUNK, CHUNK), lambda i: (0, 0, 0)),
                  pl.BlockSpec((CHUNK, A_HEADS), lambda i: (0, 0))],
        out_specs=pl.BlockSpec((ROW_TILE, A_WIDTH), lambda i: (i, 0)),
        out_shape=jax.ShapeDtypeStruct((t_rows, A_WIDTH), BF16),
        compiler_params=_params(("arbitrary",)),
        name="gmlp",
    )(proj, v_norm.reshape(1, A_WIDTH), w_s.astype(BF16), jnp.transpose(b_s))


def _rope_tables(n_lat, n_ctx, d_rot, reps):
    half = d_rot // 2
    nf = half // 2
    rows = n_lat // GRID_W
    tpos = jnp.arange(n_lat, dtype=I32)
    row = tpos // GRID_W - rows // 2
    col = tpos % GRID_W - GRID_W // 2
    freq = ROPE_THETA ** (-jnp.arange(nf, dtype=F32) / nf)
    ang_r = row.astype(F32)[:, None] * freq
    ang_c = col.astype(F32)[:, None] * freq
    cos = jnp.concatenate([jnp.cos(ang_r), jnp.cos(ang_r), jnp.cos(ang_c), jnp.cos(ang_c)], axis=1)
    sin = jnp.concatenate([-jnp.sin(ang_r), jnp.sin(ang_r), -jnp.sin(ang_c), jnp.sin(ang_c)], axis=1)
    cos = jnp.concatenate([cos, jnp.ones((n_ctx, d_rot), F32)], axis=0)
    sin = jnp.concatenate([sin, jnp.zeros((n_ctx, d_rot), F32)], axis=0)
    return jnp.tile(cos, (1, reps)), jnp.tile(sin, (1, reps))


def _rope(y, cos, sin, nf):
    lane = lax.broadcasted_iota(I32, y.shape, 1)
    partner = jnp.where((lane % (2 * nf)) < nf,
                        pltpu.roll(y, LANES - nf, 1), pltpu.roll(y, nf, 1))
    return y * cos + partner * sin


def _gqa_prep_kernel(p_ref, w_ref, cos_ref, sin_ref, o_ref):
    c = pl.program_id(1)
    heads = p_ref.shape[1] // HEAD_DIM

    @pl.when(c < 4)
    def _():
        for hh in range(heads):
            x = p_ref[:, hh * HEAD_DIM:(hh + 1) * HEAD_DIM].astype(F32)
            y = _rope(_rms(x, w_ref[...]), cos_ref[...], sin_ref[...], HEAD_DIM // 4)
            o_ref[hh] = y.astype(o_ref.dtype)

    @pl.when(c == 4)
    def _():
        for hh in range(heads):
            o_ref[hh] = p_ref[:, hh * HEAD_DIM:(hh + 1) * HEAD_DIM]


def _gqa_prep(proj, geom, q_norm, k_norm, cos, sin):
    batch, n_tok, b_fn, rb_fn = geom
    wvec = jnp.stack([q_norm * (HEAD_DIM ** -0.5 * LOG2E)] * 3 + [k_norm, jnp.ones_like(k_norm)])
    wvec = wvec.reshape(5, 1, HEAD_DIM)
    first = (2 * A_WIDTH) // 512
    return pl.pallas_call(
        _gqa_prep_kernel,
        grid=(proj.shape[0] // ROW_TILE, 5),
        in_specs=[pl.BlockSpec((ROW_TILE, 512), lambda i, c: (i, first + c)),
                  pl.BlockSpec((None, 1, HEAD_DIM), lambda i, c: (c, 0, 0)),
                  pl.BlockSpec((ROW_TILE, HEAD_DIM), lambda i, c: (rb_fn(i), 0)),
                  pl.BlockSpec((ROW_TILE, HEAD_DIM), lambda i, c: (rb_fn(i), 0))],
        out_specs=pl.BlockSpec((None, 4, ROW_TILE, HEAD_DIM), lambda i, c: (b_fn(i), c, rb_fn(i), 0)),
        out_shape=jax.ShapeDtypeStruct((batch, B_HEADS + 2 * B_KV_HEADS, n_tok, HEAD_DIM), BF16),
        compiler_params=_params(("arbitrary", "arbitrary")),
        name="gqa_prep",
    )(proj, wvec, cos, sin)


def _mla_prep_kernel(p_ref, qa_ref, kva_ref, wqn_ref, wqr_ref, wkk_ref, wkv_ref,
                     nqn_ref, nqr_ref, nkn_ref, nkr_ref, cos_ref, sin_ref,
                     qc_ref, kc_ref, vc_ref):
    cq = p_ref[:, :C_Q_RANK].astype(F32)
    ckv = p_ref[:, C_Q_RANK:C_Q_RANK + C_KV_RANK].astype(F32)
    kr = p_ref[:, C_Q_RANK + C_KV_RANK:C_Q_RANK + C_KV_RANK + LANES].astype(F32)
    cqn = _rms(cq, qa_ref[...]).astype(BF16)
    ckvn = _rms(ckv, kva_ref[...]).astype(BF16)
    qn = jnp.dot(cqn, wqn_ref[...], preferred_element_type=F32)
    qr = jnp.dot(cqn, wqr_ref[...], preferred_element_type=F32)
    kn = jnp.dot(ckvn, wkk_ref[...], preferred_element_type=F32)
    vv = jnp.dot(ckvn, wkv_ref[...], preferred_element_type=F32)
    cos = cos_ref[...]
    sin = sin_ref[...]
    lane = lax.broadcasted_iota(I32, kr.shape, 1)
    low = lane < C_ROPE
    inv_qk = 1.0 / C_QK

    kr_rot = _rope(kr * nkr_ref[...], cos, sin, C_ROPE // 4)
    kr_ss = jnp.sum(kr * kr, axis=-1, keepdims=True)
    for h in range(C_HEADS):
        hs = slice(h * C_NOPE, (h + 1) * C_NOPE)
        knh = kn[:, hs]
        r = lax.rsqrt((jnp.sum(knh * knh, axis=-1, keepdims=True) + kr_ss) * inv_qk + EPS)
        kc_ref[h, :, 0:C_NOPE] = (knh * r * nkn_ref[...]).astype(kc_ref.dtype)
        kc_ref[h, :, C_NOPE:C_QK] = (kr_rot * r)[:, 0:C_ROPE].astype(kc_ref.dtype)
        vc_ref[h] = vv[:, h * C_V:(h + 1) * C_V].astype(vc_ref.dtype)

    for pr in range(C_HEADS // 2):
        qrp = qr[:, pr * LANES:(pr + 1) * LANES]
        sq = qrp * qrp
        ss_pair = (jnp.sum(jnp.where(low, sq, 0.0), axis=-1, keepdims=True),
                   jnp.sum(jnp.where(low, 0.0, sq), axis=-1, keepdims=True))
        rs = []
        for e in range(2):
            h = 2 * pr + e
            qnh = qn[:, h * C_NOPE:(h + 1) * C_NOPE]
            r = lax.rsqrt((jnp.sum(qnh * qnh, axis=-1, keepdims=True) + ss_pair[e]) * inv_qk + EPS)
            rs.append(r)
            qc_ref[h, :, 0:C_NOPE] = (qnh * r * nqn_ref[...]).astype(qc_ref.dtype)
        q_rot = _rope(qrp * jnp.where(low, rs[0], rs[1]) * nqr_ref[...], cos, sin, C_ROPE // 4)
        qc_ref[2 * pr, :, C_NOPE:C_QK] = q_rot[:, 0:C_ROPE].astype(qc_ref.dtype)
        qc_ref[2 * pr + 1, :, C_NOPE:C_QK] = q_rot[:, C_ROPE:2 * C_ROPE].astype(qc_ref.dtype)


def _mla_prep(proj, geom, q_a_norm, w_uq, kv_a_norm, w_ukv, q_norm, k_norm, cos, sin):
    batch, n_tok, b_fn, rb_fn = geom
    first = (2 * A_WIDTH + B_WIDTH + 2 * B_KV_HEADS * HEAD_DIM)
    width = IN_COLS_PAD - first
    assert first % width == 0
    wq = w_uq.reshape(C_Q_RANK, C_HEADS, C_QK)
    wqn = wq[:, :, :C_NOPE].reshape(C_Q_RANK, C_HEADS * C_NOPE).astype(BF16)
    wqr = wq[:, :, C_NOPE:].reshape(C_Q_RANK, C_HEADS * C_ROPE).astype(BF16)
    wkv = w_ukv.reshape(C_KV_RANK, C_HEADS, C_NOPE + C_V)
    wkk = wkv[:, :, :C_NOPE].reshape(C_KV_RANK, C_HEADS * C_NOPE).astype(BF16)
    wkvv = wkv[:, :, C_NOPE:].reshape(C_KV_RANK, C_HEADS * C_V).astype(BF16)
    qs = C_QK ** -0.5 * LOG2E
    nqn = (q_norm[:C_NOPE] * qs).reshape(1, C_NOPE)
    nqr = jnp.tile(q_norm[C_NOPE:] * qs, 2).reshape(1, LANES)
    nkn = k_norm[:C_NOPE].reshape(1, C_NOPE)
    nkr = jnp.tile(k_norm[C_NOPE:], 2).reshape(1, LANES)
    const = lambda shape: pl.BlockSpec(shape, lambda i: tuple(0 for _ in shape))
    head_spec = lambda dd: pl.BlockSpec((None, C_HEADS, ROW_TILE, dd), lambda i: (b_fn(i), 0, rb_fn(i), 0))
    return pl.pallas_call(
        _mla_prep_kernel,
        grid=(proj.shape[0] // ROW_TILE,),
        in_specs=[pl.BlockSpec((ROW_TILE, width), lambda i: (i, first // width)),
                  const((1, C_Q_RANK)), const((1, C_KV_RANK)),
                  const(wqn.shape), const(wqr.shape), const(wkk.shape), const(wkvv.shape),
                  const((1, C_NOPE)), const((1, LANES)), const((1, C_NOPE)), const((1, LANES)),
                  pl.BlockSpec((ROW_TILE, LANES), lambda i: (rb_fn(i), 0)),
                  pl.BlockSpec((ROW_TILE, LANES), lambda i: (rb_fn(i), 0))],
        out_specs=[head_spec(C_QK), head_spec(C_QK), head_spec(C_V)],
        out_shape=[jax.ShapeDtypeStruct((batch, C_HEADS, n_tok, C_QK), BF16),
                   jax.ShapeDtypeStruct((batch, C_HEADS, n_tok, C_QK), BF16),
                   jax.ShapeDtypeStruct((batch, C_HEADS, n_tok, C_V), BF16)],
        compiler_params=_params(("arbitrary",)),
        name="mla_prep",
    )(proj, q_a_norm.reshape(1, C_Q_RANK), kv_a_norm.reshape(1, C_KV_RANK), wqn, wqr, wkk, wkvv,
      nqn, nqr, nkn, nkr, cos, sin)


def _attn_kernel(bound_ref, q_ref, k_ref, v_ref, *rest, group, n_lat, bounded):
    o_ref, m_sc, l_sc, acc_sc = rest[-4:]
    tq, dk = q_ref.shape[1], q_ref.shape[2]
    dv = v_ref.shape[1]
    n_tail = k_ref.shape[0] - n_lat
    q = q_ref[...].reshape(group * tq, dk)
    l_sc[...] = jnp.zeros(l_sc.shape, F32)
    acc_sc[...] = jnp.zeros(acc_sc.shape, F32)
    if not bounded:
        m_sc[...] = jnp.full(m_sc.shape, -jnp.inf, F32)

    def lane_fold(p):
        out = p[:, 0:LANES]
        for j in range(1, p.shape[1] // LANES):
            out = out + p[:, j * LANES:(j + 1) * LANES]
        return out

    def chunk(start, size):
        k = k_ref[pl.ds(start, size), :]
        v = v_ref[pl.ds(start, size), :]
        s = lax.dot_general(q, k, (((1,), (1,)), ((), ())), preferred_element_type=F32)
        if bounded:
            p = jnp.exp2(s - bound_ref[0])
            l_sc[...] += lane_fold(p)
            acc_sc[...] += jnp.dot(p.astype(BF16), v, preferred_element_type=F32)
        else:
            m_prev = m_sc[...]
            m_new = jnp.maximum(m_prev, jnp.max(s, axis=-1, keepdims=True))
            alpha = jnp.exp2(m_prev - m_new)
            p = jnp.exp2(s - m_new[:, 0:1])
            l_sc[...] = alpha * l_sc[...] + lane_fold(p)
            acc_sc[...] = alpha[:, 0:1] * acc_sc[...] + jnp.dot(p.astype(BF16), v, preferred_element_type=F32)
            m_sc[...] = m_new

    if n_lat:
        def body(c, carry):
            chunk(pl.multiple_of(c * ATT_TK, ATT_TK), ATT_TK)
            return carry
        lax.fori_loop(0, n_lat // ATT_TK, body, 0, unroll=2)
    chunk(n_lat, n_tail)
    o = acc_sc[...] / jnp.sum(l_sc[...], axis=-1, keepdims=True)
    for g in range(group):
        o_ref[:, g * dv:(g + 1) * dv] = o[g * tq:(g + 1) * tq].astype(o_ref.dtype)


def _attention_call(bound, q_arr, k_arr, v_arr, prev_out, heads, n_kv, group, tq, n_lat, n_ctx, ctx_only,
                    t_rows, bounded, name):
    q_head0, k_head0, v_head0 = heads
    batch, _, n_tok, dk = q_arr.shape
    dv = v_arr.shape[-1]
    m = group * tq
    if ctx_only:
        assert tq == n_ctx and n_lat % n_ctx == 0
        nq, q0, key_rows, key_blk = 1, n_lat // tq, n_ctx, n_lat // n_ctx
        out_row = lambda b, qi: batch * (n_lat // tq) + b
    else:
        assert n_lat % tq == 0 and n_lat % ATT_TK == 0
        nq, q0, key_rows, key_blk = n_lat // tq, 0, n_tok, 0
        out_row = lambda b, qi: b * nq + qi
    in_specs = [pl.BlockSpec((None, group, tq, dk), lambda b, g, qi, *_: (b, q_head0 // group + g, q0 + qi, 0)),
                pl.BlockSpec((None, None, key_rows, dk), lambda b, g, qi, *_: (b, k_head0 + g, key_blk, 0)),
                pl.BlockSpec((None, None, key_rows, dv), lambda b, g, qi, *_: (b, v_head0 + g, key_blk, 0))]
    args = [bound, q_arr, k_arr, v_arr]
    aliases = {}
    if prev_out is not None:
        in_specs.append(pl.BlockSpec(memory_space=pl.ANY))
        args.append(prev_out)
        aliases = {len(args) - 1: 0}
    grid_spec = pltpu.PrefetchScalarGridSpec(
        num_scalar_prefetch=1,
        grid=(batch, n_kv, nq),
        in_specs=in_specs,
        out_specs=pl.BlockSpec((tq, group * dv), lambda b, g, qi, *_: (out_row(b, qi), g)),
        scratch_shapes=[pltpu.VMEM((m, LANES), F32), pltpu.VMEM((m, LANES), F32), pltpu.VMEM((m, dv), F32)],
    )
    return pl.pallas_call(
        functools.partial(_attn_kernel, group=group, n_lat=0 if ctx_only else n_lat, bounded=bounded),
        grid_spec=grid_spec,
        out_shape=jax.ShapeDtypeStruct((t_rows, n_kv * group * dv), BF16),
        input_output_aliases=aliases,
        compiler_params=_params(("arbitrary", "arbitrary", "arbitrary")),
        name=name,
    )(*args)


def _attention(bound, q_arr, k_arr, v_arr, heads, n_kv, group, tq, n_lat, n_ctx, with_ctx, t_rows, name):
    tq = min(tq, n_lat)

    def run(bounded):
        def f(bound, q_arr, k_arr, v_arr):
            out = _attention_call(bound, q_arr, k_arr, v_arr, None, heads, n_kv, group, tq, n_lat, n_ctx,
                                  False, t_rows, bounded, name)
            if with_ctx:
                out = _attention_call(bound, q_arr, k_arr, v_arr, out, heads, n_kv, group, n_ctx, n_lat, n_ctx,
                                      True, t_rows, bounded, name + "_ctx")
            return out
        return f

    return lax.cond(bound[0] <= SCORE_BOUND_LIMIT, run(True), run(False), bound, q_arr, k_arr, v_arr)


def _score_bound(q_norm, k_norm, dk):
    b = dk ** 0.5 * jnp.max(jnp.abs(q_norm)) * jnp.max(jnp.abs(k_norm)) * (LOG2E * BOUND_SLACK)
    return b.reshape(1).astype(F32)


def _wout_kernel(a_ref, b_ref, c_ref, w_ref, x_ref, g_ref, o_ref):
    acc = jnp.dot(a_ref[...], w_ref[0:A_WIDTH, :], preferred_element_type=F32)
    acc += jnp.dot(b_ref[...], w_ref[A_WIDTH:A_WIDTH + B_WIDTH, :], preferred_element_type=F32)
    acc += jnp.dot(c_ref[...], w_ref[A_WIDTH + B_WIDTH:, :], preferred_element_type=F32)
    o_ref[...] = x_ref[...] + g_ref[...] * acc


def _out_proj(a_out, b_out, c_out, w, x, mods, layer, t_rows, mod_row):
    d = w.shape[1]
    return pl.pallas_call(
        _wout_kernel,
        grid=(d // MM_TN, t_rows // MM_TM),
        in_specs=[pl.BlockSpec((MM_TM, A_WIDTH), lambda j, i: (i, 0)),
                  pl.BlockSpec((MM_TM, B_WIDTH), lambda j, i: (i, 0)),
                  pl.BlockSpec((MM_TM, C_WIDTH), lambda j, i: (i, 0)),
                  pl.BlockSpec((w.shape[0], MM_TN), lambda j, i: (0, j)),
                  pl.BlockSpec((MM_TM, MM_TN), lambda j, i: (i, j)),
                  _mod_spec(layer, 2, d, lambda j, i: mod_row(i), MM_TN, lambda j, i: j)],
        out_specs=pl.BlockSpec((MM_TM, MM_TN), lambda j, i: (i, j)),
        out_shape=jax.ShapeDtypeStruct((t_rows, d), F32),
        compiler_params=_params(("arbitrary", "arbitrary")),
        name="out_proj",
    )(a_out, b_out, c_out, w, x, mods)


def _first_index(hit_val, x, iota, big):
    return jnp.min(jnp.where(x == hit_val, iota, big), axis=0, keepdims=True)


def _router_kernel(x_ref, nw_ref, shift_ref, scale_ref, rwt_ref, bias_ref,
                   hp_ref, eidx_ref, gate_ref, rank_ref, cnt_ref, carry_sc):
    i = pl.program_id(0)
    tm, d = x_ref.shape
    h = _rms(x_ref[...], nw_ref[...]) * (1.0 + scale_ref[...]) + shift_ref[...]
    hp_ref[...] = _pack_bf16_pair(h[:, :d // 2], h[:, d // 2:])

    logits = lax.dot_general(rwt_ref[...], h, (((1,), (1,)), ((), ())),
                             precision=lax.Precision.HIGHEST, preferred_element_type=F32)
    scores = jax.nn.sigmoid(logits)
    sel = scores + bias_ref[:, 0:1]
    neg = jnp.float32(-jnp.inf)

    iota_n = lax.broadcasted_iota(I32, (N_GROUPS, tm), 0).astype(F32)
    iota_e = lax.broadcasted_iota(I32, (N_EXPERTS, tm), 0).astype(F32)

    gscore = jnp.zeros((N_GROUPS, tm), F32)
    for g in range(N_GROUPS):
        blk = sel[g * GROUP_SIZE:(g + 1) * GROUP_SIZE]
        m1 = jnp.max(blk, axis=0, keepdims=True)
        first = _first_index(m1, blk, iota_n, float(GROUP_SIZE))
        m2 = jnp.max(jnp.where(iota_n == first, neg, blk), axis=0, keepdims=True)
        gscore = jnp.where(iota_n == float(g), m1 + m2, gscore)

    gsel = jnp.zeros((N_GROUPS, tm), F32)
    work = gscore
    for _ in range(TOPK_GROUPS):
        m = jnp.max(work, axis=0, keepdims=True)
        hit = iota_n == _first_index(m, work, iota_n, float(N_GROUPS))
        gsel = jnp.where(hit, 1.0, gsel)
        work = jnp.where(hit, neg, work)
    emask = jnp.concatenate(
        [jnp.broadcast_to(gsel[g:g + 1], (GROUP_SIZE, tm)) for g in range(N_GROUPS)], axis=0)

    work = jnp.where(emask > 0.5, sel, neg)
    picks = []
    onehot = jnp.zeros((N_EXPERTS, tm), F32)
    for _ in range(TOP_K):
        m = jnp.max(work, axis=0, keepdims=True)
        pick = _first_index(m, work, iota_e, float(N_EXPERTS))
        picks.append(pick)
        hit = iota_e == pick
        onehot = jnp.where(hit, 1.0, onehot)
        work = jnp.where(hit, neg, work)

    w_un = onehot * scores
    gates = w_un / jnp.sum(w_un, axis=0, keepdims=True) * ROUTED_SCALE

    @pl.when(i == 0)
    def _():
        carry_sc[...] = jnp.zeros(carry_sc.shape, F32)

    before = (lax.broadcasted_iota(I32, (tm, tm), 0) < lax.broadcasted_iota(I32, (tm, tm), 1)).astype(BF16)
    rank = carry_sc[:, 0:1] + jnp.dot(onehot.astype(BF16), before, preferred_element_type=F32)
    carry_sc[...] = carry_sc[...] + jnp.sum(onehot, axis=1, keepdims=True)
    cnt_ref[...] = carry_sc[...]

    eidx_ref[...] = jnp.zeros(eidx_ref.shape, I32)
    gate_ref[...] = jnp.zeros(gate_ref.shape, F32)
    rank_ref[...] = jnp.zeros(rank_ref.shape, I32)
    for k in range(TOP_K):
        hit = iota_e == picks[k]
        eidx_ref[k:k + 1, :] = picks[k].astype(I32)
        gate_ref[k:k + 1, :] = jnp.sum(jnp.where(hit, gates, 0.0), axis=0, keepdims=True)
        rank_ref[k:k + 1, :] = jnp.sum(jnp.where(hit, rank, 0.0), axis=0, keepdims=True).astype(I32)


def _norm_router(x, t_rows, norm_w, mods, layer, mod_row, router_w, router_bias):
    d = x.shape[1]
    col = lambda dt: (pl.BlockSpec((SUBLANES, ROW_TILE), lambda i: (0, i)),
                      jax.ShapeDtypeStruct((SUBLANES, t_rows), dt))
    specs, shapes = zip(col(I32), col(F32), col(I32))
    return pl.pallas_call(
        _router_kernel,
        grid=(t_rows // ROW_TILE,),
        in_specs=[pl.BlockSpec((ROW_TILE, d), lambda i: (i, 0)),
                  pl.BlockSpec((1, d), lambda i: (0, 0)),
                  _mod_spec(layer, 3, d, mod_row),
                  _mod_spec(layer, 4, d, mod_row),
                  pl.BlockSpec((N_EXPERTS, d), lambda i: (0, 0)),
                  pl.BlockSpec((N_EXPERTS, LANES), lambda i: (0, 0))],
        out_specs=[pl.BlockSpec((ROW_TILE, d // 2), lambda i: (i, 0)), *specs,
                   pl.BlockSpec((N_EXPERTS, LANES), lambda i: (0, 0))],
        out_shape=[jax.ShapeDtypeStruct((t_rows, d // 2), U32), *shapes,
                   jax.ShapeDtypeStruct((N_EXPERTS, LANES), F32)],
        scratch_shapes=[pltpu.VMEM((N_EXPERTS, LANES), F32)],
        compiler_params=_params(("arbitrary",)),
        name="norm_router",
    )(x, norm_w.reshape(1, d), mods, mods, jnp.transpose(router_w),
      jnp.broadcast_to(router_bias[:, None], (N_EXPERTS, LANES)))


def _row_copy(src, src_row, dst, dst_row, sem):
    return pltpu.make_async_copy(src.at[pl.ds(src_row, 1)], dst.at[pl.ds(dst_row, 1)], sem)


_FILL_PIECES = tuple(1 << b for b in range(EXP_TM.bit_length() - 2, -1, -1))


def _zero_fill(pstart_ref, plen_ref, tail_ref, xs_hbm, zero_sc, sem_z):
    zero_sc[...] = jnp.zeros(zero_sc.shape, zero_sc.dtype)
    n_tail = (xs_hbm.shape[0] - tail_ref[0]) // _FILL_PIECES[0]

    def pad_copies(e, wait):
        start = pstart_ref[e]
        head = (-start) & (SUBLANES - 1)
        for r in range(SUBLANES - 1):
            cp = pltpu.make_async_copy(zero_sc.at[pl.ds(0, 1)], xs_hbm.at[pl.ds(start + r, 1)], sem_z)

            @pl.when(r < head)
            def _():
                cp.wait() if wait else cp.start()

        off = start + head
        rem = plen_ref[e] - head
        for piece in _FILL_PIECES:
            if piece < SUBLANES:
                break
            on = (rem & piece) != 0
            cp = pltpu.make_async_copy(zero_sc.at[pl.ds(0, piece)],
                                       xs_hbm.at[pl.ds(pl.multiple_of(off, SUBLANES), piece)], sem_z)

            @pl.when(on)
            def _():
                cp.wait() if wait else cp.start()

            off = off + jnp.where(on, piece, 0)

    def tail_copy(j, wait):
        row = pl.multiple_of(tail_ref[0] + j * _FILL_PIECES[0], _FILL_PIECES[0])
        cp = pltpu.make_async_copy(zero_sc, xs_hbm.at[pl.ds(row, _FILL_PIECES[0])], sem_z)
        cp.wait() if wait else cp.start()

    for wait in (False, True):
        lax.fori_loop(0, N_EXPERTS, lambda e, c: (pad_copies(e, wait), c)[1], 0)
        lax.fori_loop(0, n_tail, lambda j, c: (tail_copy(j, wait), c)[1], 0)


def _dispatch_kernel(pstart_ref, plen_ref, tail_ref, slot_hbm, h_ref, xs_hbm, slot_sm, zero_sc, sem_s, sem, sem_z):
    i = pl.program_id(0)
    n = slot_sm.shape[0]
    cp = pltpu.make_async_copy(slot_hbm.at[pl.ds(pl.multiple_of(i * n, n), n)], slot_sm, sem_s)
    cp.start()

    @pl.when(i == 0)
    def _():
        _zero_fill(pstart_ref, plen_ref, tail_ref, xs_hbm, zero_sc, sem_z)

    cp.wait()

    def issue(t, carry):
        for k in range(TOP_K):
            _row_copy(h_ref, t, xs_hbm, slot_sm[t * SUBLANES + k], sem).start(priority=k % 2)
        return carry

    lax.fori_loop(0, DISP_T, issue, 0)

    def drain(t, carry):
        for k in range(TOP_K):
            _row_copy(h_ref, 0, xs_hbm, 0, sem).wait()
        return carry

    lax.fori_loop(0, DISP_T, drain, 0)


def _dispatch(hp, slots, pad_start, pad_len, tail_start, n_slots):
    t, w = hp.shape
    grid_spec = pltpu.PrefetchScalarGridSpec(
        num_scalar_prefetch=3,
        grid=(t // DISP_T,),
        in_specs=[pl.BlockSpec(memory_space=pl.ANY), pl.BlockSpec((DISP_T, w), lambda i, *_: (i, 0))],
        out_specs=pl.BlockSpec(memory_space=pl.ANY),
        scratch_shapes=[pltpu.SMEM((DISP_T * SUBLANES,), I32),
                        pltpu.VMEM((_FILL_PIECES[0], w), U32),
                        pltpu.SemaphoreType.DMA(()), pltpu.SemaphoreType.DMA(()), pltpu.SemaphoreType.DMA(())],
    )
    return pl.pallas_call(
        _dispatch_kernel,
        grid_spec=grid_spec,
        out_shape=jax.ShapeDtypeStruct((n_slots, w), U32),
        compiler_params=_params(("arbitrary",)),
        name="moe_dispatch",
    )(pad_start, pad_len, tail_start, slots, hp)


def _ffn(lo, hi, wg, wu, wd):
    half = lo.shape[1]
    hg = jnp.dot(lo, wg[:half], preferred_element_type=F32) + jnp.dot(hi, wg[half:], preferred_element_type=F32)
    hu = jnp.dot(lo, wu[:half], preferred_element_type=F32) + jnp.dot(hi, wu[half:], preferred_element_type=F32)
    hb = (jax.nn.silu(hg) * hu).astype(BF16)
    return jnp.dot(hb, wd, preferred_element_type=F32)


def _expert_kernel(te_ref, par_ref, nxt_ref, nu_ref, xs_ref, wg_hbm, wu_hbm, wd_hbm, ys_ref,
                   wg_f, wu_f, wd_f, wg_sc, wu_sc, wd_sc, sem, *, layer):
    i = pl.program_id(0)
    expert = te_ref[i]
    par = par_ref[i]

    def weight_copies(e, slot):
        return (pltpu.make_async_copy(wg_hbm.at[layer, e], wg_f.at[slot], sem.at[slot, 0]),
                pltpu.make_async_copy(wu_hbm.at[layer, e], wu_f.at[slot], sem.at[slot, 1]),
                pltpu.make_async_copy(wd_hbm.at[layer, e], wd_f.at[slot], sem.at[slot, 2]))

    @pl.when(i == 0)
    def _():
        for cp in weight_copies(expert, par):
            cp.start()

    @pl.when(jnp.logical_or(i == 0, expert != te_ref[jnp.maximum(i - 1, 0)]))
    def _():
        for cp in weight_copies(expert, par):
            cp.wait()

        @pl.when(nxt_ref[i] >= 0)
        def _():
            for cp in weight_copies(nxt_ref[i], 1 - par):
                cp.start()

        wg_sc[...] = wg_f[par].astype(BF16)
        wu_sc[...] = wu_f[par].astype(BF16)
        wd_sc[...] = wd_f[par].astype(BF16)

    @pl.when(i < nu_ref[0])
    def _():
        lo, hi = _unpack_bf16_pair(xs_ref[...])
        y = _ffn(lo.astype(BF16), hi.astype(BF16), wg_sc[...], wu_sc[...], wd_sc[...])
        half = y.shape[1] // 2
        ys_ref[...] = _pack_bf16_pair(y[:, :half], y[:, half:])

    @pl.when(i >= nu_ref[0])
    def _():
        ys_ref[...] = jnp.zeros(ys_ref.shape, ys_ref.dtype)


def _experts(xs, tile_expert, tile_slot, tile_next, n_used, w_gate, w_up, w_down, layer):
    n_slots, w = xs.shape
    _, _, d, f = w_gate.shape
    n_tiles = n_slots // EXP_TM
    hbm = pl.BlockSpec(memory_space=pl.ANY)
    grid_spec = pltpu.PrefetchScalarGridSpec(
        num_scalar_prefetch=4,
        grid=(n_tiles,),
        in_specs=[pl.BlockSpec((EXP_TM, w), lambda i, te, par, nxt, nu: (jnp.minimum(i, nu[0] - 1), 0)),
                  hbm, hbm, hbm],
        out_specs=pl.BlockSpec((EXP_TM, w), lambda i, te, par, nxt, nu: (i, 0)),
        scratch_shapes=[pltpu.VMEM((2, d, f), F32), pltpu.VMEM((2, d, f), F32), pltpu.VMEM((2, f, d), F32),
                        pltpu.VMEM((d, f), BF16), pltpu.VMEM((d, f), BF16), pltpu.VMEM((f, d), BF16),
                        pltpu.SemaphoreType.DMA((2, 3))],
    )
    return pl.pallas_call(
        functools.partial(_expert_kernel, layer=layer),
        grid_spec=grid_spec,
        out_shape=jax.ShapeDtypeStruct((n_slots, w), U32),
        compiler_params=_params(("arbitrary",)),
        name="moe_experts",
    )(tile_expert, tile_slot, tile_next, n_used, xs, w_gate, w_up, w_down)


def _combine_kernel(slot_hbm, ys_hbm, hp_ref, gate_ref, sg_ref, su_ref, sd_ref, x_ref, g5_ref,
                    o_ref, slot_sm, buf, sem_s, sem):
    i = pl.program_id(0)
    n_steps = pl.num_programs(0)
    n = COMB_T * SUBLANES

    def gather(h, half_index):
        cp = pltpu.make_async_copy(slot_hbm.at[pl.ds(pl.multiple_of(half_index * n, n), n)],
                                   slot_sm.at[h], sem_s.at[h])
        cp.start()
        cp.wait()

        def issue(t, carry):
            for k in range(TOP_K):
                _row_copy(ys_hbm, slot_sm[h, t * SUBLANES + k], buf.at[h, k], t, sem.at[h]).start(priority=k % 2)
            return carry

        lax.fori_loop(0, COMB_T, issue, 0)

    def finish(h):
        rows = slice(h * COMB_T, (h + 1) * COMB_T)
        lo, hi = _unpack_bf16_pair(hp_ref[rows, :])
        y = _ffn(lo.astype(BF16), hi.astype(BF16), sg_ref[...], su_ref[...], sd_ref[...])

        def drain(t, carry):
            for k in range(TOP_K):
                _row_copy(ys_hbm, 0, buf.at[h, k], 0, sem.at[h]).wait()
            return carry

        lax.fori_loop(0, COMB_T, drain, 0)

        half = y.shape[1] // 2
        r_lo = y[:, :half]
        r_hi = y[:, half:]
        for k in range(TOP_K):
            e_lo, e_hi = _unpack_bf16_pair(buf[h, k])
            gk = gate_ref[rows, k:k + 1]
            r_lo = r_lo + gk * e_lo
            r_hi = r_hi + gk * e_hi
        o_ref[rows, :half] = x_ref[rows, :half] + g5_ref[:, :half] * r_lo
        o_ref[rows, half:] = x_ref[rows, half:] + g5_ref[:, half:] * r_hi

    @pl.when(i == 0)
    def _():
        gather(0, 0)

    gather(1, 2 * i + 1)
    finish(0)

    @pl.when(i + 1 < n_steps)
    def _():
        gather(0, 2 * i + 2)

    finish(1)


def _combine(ys, hp, slots, gates, sh_gate, sh_up, sh_down, x, mods, layer, t_rows, mod_row):
    d = x.shape[1]
    w = hp.shape[1]
    f = sh_gate.shape[1]
    rows = 2 * COMB_T
    return pl.pallas_call(
        _combine_kernel,
        grid=(t_rows // rows,),
        in_specs=[pl.BlockSpec(memory_space=pl.ANY), pl.BlockSpec(memory_space=pl.ANY),
                  pl.BlockSpec((rows, w), lambda i: (i, 0)),
                  pl.BlockSpec((rows, SUBLANES), lambda i: (i, 0)),
                  pl.BlockSpec((d, f), lambda i: (0, 0)),
                  pl.BlockSpec((d, f), lambda i: (0, 0)),
                  pl.BlockSpec((f, d), lambda i: (0, 0)),
                  pl.BlockSpec((rows, d), lambda i: (i, 0)),
                  _mod_spec(layer, 5, d, lambda i: mod_row(i * rows // ROW_TILE))],
        out_specs=pl.BlockSpec((rows, d), lambda i: (i, 0)),
        out_shape=jax.ShapeDtypeStruct((t_rows, d), F32),
        scratch_shapes=[pltpu.SMEM((2, COMB_T * SUBLANES), I32),
                        pltpu.VMEM((2, TOP_K, COMB_T, w), U32),
                        pltpu.SemaphoreType.DMA((2,)), pltpu.SemaphoreType.DMA((2,))],
        compiler_params=_params(("arbitrary",)),
        name="moe_combine",
    )(slots, ys, hp, gates, sh_gate.astype(BF16), sh_up.astype(BF16), sh_down.astype(BF16), x, mods)


def _moe(x, t_rows, norm_w, mods, layer, mod_row, router_w, router_bias,
         w_gate, w_up, w_down, sh_gate, sh_up, sh_down):
    hp, eidx, gate, rank, cnt = _norm_router(x, t_rows, norm_w, mods, layer, mod_row, router_w, router_bias)

    n_tiles = (t_rows * TOP_K + N_EXPERTS * (EXP_TM - 1) + EXP_TM - 1) // EXP_TM
    counts = cnt[:, 0].astype(I32)
    padded = (counts + EXP_TM - 1) // EXP_TM * EXP_TM
    ends = jnp.cumsum(padded)
    base = ends - padded
    experts = jnp.arange(N_EXPERTS, dtype=I32)
    base_of = jnp.sum(jnp.where(eidx[:, :, None] == experts, base, 0), axis=-1)
    slots = jnp.transpose(base_of + rank).reshape(-1)
    n_used = (ends[-1] // EXP_TM).astype(I32)
    tile_start = jnp.arange(n_tiles, dtype=I32) * EXP_TM
    last_start = (n_used - 1) * EXP_TM
    te = jnp.sum(jnp.minimum(tile_start, last_start)[:, None] >= ends[None, :], axis=1).astype(I32)
    te = jnp.minimum(te, N_EXPERTS - 1)
    used = counts > 0
    slot_of = ((jnp.cumsum(used.astype(I32)) - 1) % 2).astype(I32)
    later = jnp.logical_and(experts[None, :] > experts[:, None], used[None, :])
    next_of = jnp.min(jnp.where(later, experts[None, :], N_EXPERTS), axis=1)
    next_of = jnp.where(next_of == N_EXPERTS, -1, next_of).astype(I32)

    xs = _dispatch(hp, slots, base + counts, padded - counts, ends[-1:], n_tiles * EXP_TM)
    ys = _experts(xs, te, slot_of[te], next_of[te], n_used.reshape(1), w_gate, w_up, w_down, layer)
    return _combine(ys, hp, slots, jnp.transpose(gate), sh_gate, sh_up, sh_down, x, mods, layer, t_rows, mod_row)


def kernel(x, c, ctx, c_ctx, ada_w, ada_b, norm1_w, norm2_w, w_in, gm_v_norm, gm_w_s, gm_b_s, gqa_q_norm, gqa_k_norm, mla_q_a_norm, mla_w_uq, mla_kv_a_norm, mla_w_ukv, mla_q_norm, mla_k_norm, w_out, router_w, router_bias, exp_w_gate, exp_w_up, exp_w_down, sh_w_gate, sh_w_up, sh_w_down):
    batch, n_lat, d = x.shape
    n_ctx = ctx.shape[1]
    depth = ada_w.shape[0]
    assert n_ctx == ROW_TILE and n_lat % MM_TM == 0 and (batch * n_ctx) % MM_TM == 0
    t_lat = batch * n_lat
    t_all = t_lat + batch * n_ctx
    n_tok = n_lat + n_ctx
    lat_tiles = n_lat // ROW_TILE

    mod_row = lambda i: jnp.minimum(i // lat_tiles, batch)
    is_lat = lambda i: i < batch * lat_tiles
    b_fn = lambda i: jnp.where(is_lat(i), i // lat_tiles, i - batch * lat_tiles)
    rb_fn = lambda i: jnp.where(is_lat(i), i % lat_tiles, lat_tiles)
    geom = (batch, n_tok, b_fn, rb_fn)
    mm_mod_row = lambda i: mod_row(i * (MM_TM // ROW_TILE))

    mods = _ada_mods(c, c_ctx, ada_w, ada_b)
    cos_b, sin_b = _rope_tables(n_lat, n_ctx, HEAD_DIM, 1)
    cos_c, sin_c = _rope_tables(n_lat, n_ctx, C_ROPE, LANES // C_ROPE)

    xs = jnp.concatenate([x.reshape(t_lat, d), ctx.reshape(batch * n_ctx, d)], axis=0)
    for l in range(depth):
        need_ctx = l < depth - 1
        t_rows = t_all if need_ctx else t_lat
        w_in_l = jnp.pad(w_in[l].astype(BF16), ((0, 0), (0, IN_COLS_PAD - w_in.shape[2])))

        proj = _in_proj(xs, norm1_w[l], mods, l, mm_mod_row, w_in_l)
        a_out = _gmlp(proj, t_rows, gm_v_norm[l], gm_w_s[l], gm_b_s[l])
        qkv = _gqa_prep(proj, geom, gqa_q_norm[l], gqa_k_norm[l], cos_b, sin_b)
        b_out = _attention(_score_bound(gqa_q_norm[l], gqa_k_norm[l], HEAD_DIM), qkv, qkv, qkv,
                           (0, B_HEADS, B_HEADS + B_KV_HEADS), B_KV_HEADS, B_GROUP, GQA_TQ,
                           n_lat, n_ctx, need_ctx, t_rows, "gqa_attn")
        qc, kc, vc = _mla_prep(proj, geom, mla_q_a_norm[l], mla_w_uq[l], mla_kv_a_norm[l], mla_w_ukv[l],
                               mla_q_norm[l], mla_k_norm[l], cos_c, sin_c)
        c_out = _attention(_score_bound(mla_q_norm[l], mla_k_norm[l], C_QK), qc, kc, vc,
                           (0, 0, 0), C_HEADS, 1, MLA_TQ, n_lat, n_ctx, need_ctx, t_rows, "mla_attn")
        xs = _out_proj(a_out, b_out, c_out, w_out[l].astype(BF16), xs, mods, l, t_rows, mm_mod_row)
        xs = _moe(xs, t_rows, norm2_w[l], mods, l, mod_row, router_w[l], router_bias[l],
                  exp_w_gate, exp_w_up, exp_w_down, sh_w_gate[l], sh_w_up[l], sh_w_down[l])
    return xs[:t_lat].reshape(batch, n_lat, d)
```

```python
import functools

import jax
import jax.numpy as jnp
from jax import lax
from jax.experimental import pallas as pl
from jax.experimental.pallas import tpu as pltpu

F32 = jnp.float32
BF16 = jnp.bfloat16
U32 = jnp.uint32
I32 = jnp.int32

GRID_W = 64
CHUNK = 128
HEAD_DIM = 128
EPS = 1e-6
ROPE_THETA = 10000.0
A_HEADS = 8
B_HEADS = 12
B_KV_HEADS = 4
B_GROUP = B_HEADS // B_KV_HEADS
C_HEADS = 12
C_Q_RANK = 768
C_KV_RANK = 512
C_NOPE = 128
C_ROPE = 64
C_V = 128
C_QK = C_NOPE + C_ROPE
A_WIDTH = A_HEADS * HEAD_DIM
B_WIDTH = B_HEADS * HEAD_DIM
C_WIDTH = C_HEADS * C_V
N_EXPERTS = 64
TOP_K = 6
N_GROUPS = 8
GROUP_SIZE = N_EXPERTS // N_GROUPS
TOPK_GROUPS = 4
ROUTED_SCALE = 2.5
N_MOD = 6

LANES = 128
SUBLANES = 8
ROW_TILE = 256
MM_TM = 512
MM_TN = 1024
GQA_TQ = 512
MLA_TQ = 1024
ATT_TK = 512
LOG2E = 1.4426950408889634
BOUND_SLACK = 1.02
SCORE_BOUND_LIMIT = 60.0
EXP_TM = 256
DISP_T = 256
COMB_T = 128
IN_COLS_PAD = 6144
VMEM_LIMIT = 56 * 1024 * 1024


def _params(sem, vmem=VMEM_LIMIT):
    return pltpu.CompilerParams(dimension_semantics=sem, vmem_limit_bytes=vmem)


def _rms(x, w):
    return x * lax.rsqrt(jnp.mean(x * x, axis=-1, keepdims=True) + EPS) * w


def _pack_bf16_pair(a, b):
    ua = lax.bitcast_convert_type(a.astype(BF16).astype(F32), U32)
    ub = lax.bitcast_convert_type(b.astype(BF16).astype(F32), U32)
    return (ua >> 16) | ub


def _unpack_bf16_pair(u):
    a = lax.bitcast_convert_type(u << 16, F32)
    b = lax.bitcast_convert_type(u & jnp.uint32(0xFFFF0000), F32)
    return a, b


def _slab_rows(d):
    return d // 2 // LANES


def _slab_piece(ref, row0, n_tok, n_sl, a):
    return ref[pl.ds(row0 + a, n_tok, stride=n_sl), :]


def _store_slabs(ref, row0, v):
    n_tok, d = v.shape
    n_sl = _slab_rows(d)
    for a in range(n_sl):
        piece = _pack_bf16_pair(v[:, a * LANES:(a + 1) * LANES], v[:, d // 2 + a * LANES:d // 2 + (a + 1) * LANES])
        ref[pl.ds(row0 + a, n_tok, stride=n_sl), :] = piece


def _slab_weight_rows(w):
    d = w.shape[0]
    n_sl = _slab_rows(d)
    parts = []
    for a in range(n_sl):
        parts += [w[a * LANES:(a + 1) * LANES], w[d // 2 + a * LANES:d // 2 + (a + 1) * LANES]]
    return parts


def _ffn_slabs(ref, row0, n_tok, n_sl, wg_ref, wu_ref, wd_ref):
    hg = None
    for a in range(n_sl):
        lo, hi = _unpack_bf16_pair(_slab_piece(ref, row0, n_tok, n_sl, a))
        xa = jnp.concatenate([lo.astype(BF16), hi.astype(BF16)], axis=1)
        rows = slice(2 * a * LANES, 2 * (a + 1) * LANES)
        g = jnp.dot(xa, wg_ref[rows, :], preferred_element_type=F32)
        u = jnp.dot(xa, wu_ref[rows, :], preferred_element_type=F32)
        hg, hu = (g, u) if hg is None else (hg + g, hu + u)
    hb = (jax.nn.silu(hg) * hu).astype(BF16)
    return jnp.dot(hb, wd_ref[...], preferred_element_type=F32)


def _ada_kernel(cb_ref, w_ref, b_ref, o_ref, silu_sc, *, n_rows, d, tn):
    nj = tn // LANES

    @pl.when(jnp.logical_and(pl.program_id(0) == 0, pl.program_id(1) == 0))
    def _():
        cv = cb_ref[...]
        silu_sc[...] = cv * jax.nn.sigmoid(cv)

    def body(kc, accs):
        k0 = pl.multiple_of(kc * SUBLANES, SUBLANES)
        wk = w_ref[0, pl.ds(k0, SUBLANES), :]
        out = []
        for r in range(n_rows):
            sv = silu_sc[r, pl.ds(k0, SUBLANES), :]
            for j in range(nj):
                out.append(accs[r * nj + j] + wk[:, j * LANES:(j + 1) * LANES] * sv)
        return tuple(out)

    init = tuple(jnp.zeros((SUBLANES, LANES), F32) for _ in range(n_rows * nj))
    accs = lax.fori_loop(0, d // SUBLANES, body, init, unroll=8)
    o_ref[0] = jnp.zeros((SUBLANES, tn), F32)
    for r in range(n_rows):
        for j in range(nj):
            o_ref[0, r:r + 1, j * LANES:(j + 1) * LANES] = (
                jnp.sum(accs[r * nj + j], axis=0, keepdims=True) + b_ref[0, :, j * LANES:(j + 1) * LANES])


def _ada_mods(c, c_ctx, ada_w, ada_b):
    n_layers, d, nd = ada_w.shape
    n_rows = c.shape[0] + 1
    assert n_rows <= SUBLANES
    cvec = jnp.concatenate([c, c_ctx[None]], axis=0)
    cb = jnp.broadcast_to(cvec[:, :, None], (n_rows, d, LANES))
    tn = 512
    out = pl.pallas_call(
        functools.partial(_ada_kernel, n_rows=n_rows, d=d, tn=tn),
        grid=(n_layers, nd // tn),
        in_specs=[pl.BlockSpec((n_rows, d, LANES), lambda l, j: (0, 0, 0)),
                  pl.BlockSpec((1, d, tn), lambda l, j: (l, 0, j)),
                  pl.BlockSpec((1, 1, tn), lambda l, j: (l, 0, j))],
        out_specs=pl.BlockSpec((1, SUBLANES, tn), lambda l, j: (l, 0, j)),
        out_shape=jax.ShapeDtypeStruct((n_layers, SUBLANES, nd), F32),
        scratch_shapes=[pltpu.VMEM((n_rows, d, LANES), F32)],
        compiler_params=_params(("arbitrary", "arbitrary")),
        name="ada_mods",
    )(cb, ada_w, ada_b.reshape(n_layers, 1, nd))
    return out.reshape(n_layers, SUBLANES, N_MOD, 1, d)


def _mod_spec(layer, m, d, row_fn, tn=None, col_fn=None):
    if tn is None:
        return pl.BlockSpec((None, None, None, 1, d), lambda *a: (layer, row_fn(*a), m, 0, 0))
    return pl.BlockSpec((None, None, None, 1, tn), lambda *a: (layer, row_fn(*a), m, 0, col_fn(*a)))


def _in_proj_kernel(x_ref, nw_ref, shift_ref, scale_ref, w_ref, o_ref, h_sc):
    @pl.when(pl.program_id(1) == 0)
    def _():
        y = _rms(x_ref[...], nw_ref[...])
        h_sc[...] = (y * (1.0 + scale_ref[...]) + shift_ref[...]).astype(h_sc.dtype)

    o_ref[...] = jnp.dot(h_sc[...], w_ref[...], preferred_element_type=F32).astype(o_ref.dtype)


def _in_proj(x, norm_w, mods, layer, mod_row, w):
    m, d = x.shape
    n = w.shape[1]
    return pl.pallas_call(
        _in_proj_kernel,
        grid=(m // MM_TM, n // MM_TN),
        in_specs=[pl.BlockSpec((MM_TM, d), lambda i, j: (i, 0)),
                  pl.BlockSpec((1, d), lambda i, j: (0, 0)),
                  _mod_spec(layer, 0, d, lambda i, j: mod_row(i)),
                  _mod_spec(layer, 1, d, lambda i, j: mod_row(i)),
                  pl.BlockSpec((d, MM_TN), lambda i, j: (0, j))],
        out_specs=pl.BlockSpec((MM_TM, MM_TN), lambda i, j: (i, j)),
        out_shape=jax.ShapeDtypeStruct((m, n), BF16),
        scratch_shapes=[pltpu.VMEM((MM_TM, d), BF16)],
        compiler_params=_params(("arbitrary", "arbitrary")),
        name="in_proj",
    )(x, norm_w.reshape(1, d), mods, mods, w)


def _gmlp_kernel(p_ref, vn_ref, ws_ref, bt_ref, o_ref):
    for h in range(A_HEADS):
        hs = slice(h * HEAD_DIM, (h + 1) * HEAD_DIM)
        u = jax.nn.gelu(p_ref[:, hs].astype(F32))
        v = jax.nn.gelu(p_ref[:, A_WIDTH + h * HEAD_DIM:A_WIDTH + (h + 1) * HEAD_DIM].astype(F32))
        vb = _rms(v, vn_ref[:, hs]).astype(BF16)
        for c in range(ROW_TILE // CHUNK):
            cs = slice(c * CHUNK, (c + 1) * CHUNK)
            s = jnp.dot(ws_ref[h], vb[cs], preferred_element_type=F32) + bt_ref[:, h:h + 1]
            o_ref[cs, hs] = (u[cs] * s).astype(o_ref.dtype)


def _gmlp(proj, t_rows, v_norm, w_s, b_s):
    return pl.pallas_call(
        _gmlp_kernel,
        grid=(t_rows // ROW_TILE,),
        in_specs=[pl.BlockSpec((ROW_TILE, 2 * A_WIDTH), lambda i: (i, 0)),
                  pl.BlockSpec((1, A_WIDTH), lambda i: (0, 0)),
                  pl.BlockSpec((A_HEADS, CHUNK, CHUNK), lambda i: (0, 0, 0)),
                  pl.BlockSpec((CHUNK, A_HEADS), lambda i: (0, 0))],
        out_specs=pl.BlockSpec((ROW_TILE, A_WIDTH), lambda i: (i, 0)),
        out_shape=jax.ShapeDtypeStruct((t_rows, A_WIDTH), BF16),
        compiler_params=_params(("arbitrary",)),
        name="gmlp",
    )(proj, v_norm.reshape(1, A_WIDTH), w_s.astype(BF16), jnp.transpose(b_s))


def _rope_tables(n_lat, n_ctx, d_rot, reps):
    half = d_rot // 2
    nf = half // 2
    rows = n_lat // GRID_W
    tpos = jnp.arange(n_lat, dtype=I32)
    row = tpos // GRID_W - rows // 2
    col = tpos % GRID_W - GRID_W // 2
    freq = ROPE_THETA ** (-jnp.arange(nf, dtype=F32) / nf)
    ang_r = row.astype(F32)[:, None] * freq
    ang_c = col.astype(F32)[:, None] * freq
    cos = jnp.concatenate([jnp.cos(ang_r), jnp.cos(ang_r), jnp.cos(ang_c), jnp.cos(ang_c)], axis=1)
    sin = jnp.concatenate([-jnp.sin(ang_r), jnp.sin(ang_r), -jnp.sin(ang_c), jnp.sin(ang_c)], axis=1)
    cos = jnp.concatenate([cos, jnp.ones((n_ctx, d_rot), F32)], axis=0)
    sin = jnp.concatenate([sin, jnp.zeros((n_ctx, d_rot), F32)], axis=0)
    return jnp.tile(cos, (1, reps)), jnp.tile(sin, (1, reps))


def _rope(y, cos, sin, nf):
    lane = lax.broadcasted_iota(I32, y.shape, 1)
    partner = jnp.where((lane % (2 * nf)) < nf,
                        pltpu.roll(y, LANES - nf, 1), pltpu.roll(y, nf, 1))
    return y * cos + partner * sin


def _gqa_prep_kernel(p_ref, w_ref, cos_ref, sin_ref, o_ref):
    c = pl.program_id(1)
    heads = p_ref.shape[1] // HEAD_DIM

    @pl.when(c < 4)
    def _():
        for hh in range(heads):
            x = p_ref[:, hh * HEAD_DIM:(hh + 1) * HEAD_DIM].astype(F32)
            y = _rope(_rms(x, w_ref[...]), cos_ref[...], sin_ref[...], HEAD_DIM // 4)
            o_ref[hh] = y.astype(o_ref.dtype)

    @pl.when(c == 4)
    def _():
        for hh in range(heads):
            o_ref[hh] = p_ref[:, hh * HEAD_DIM:(hh + 1) * HEAD_DIM]


def _gqa_prep(proj, geom, q_norm, k_norm, cos, sin):
    batch, n_tok, b_fn, rb_fn = geom
    wvec = jnp.stack([q_norm * (HEAD_DIM ** -0.5 * LOG2E)] * 3 + [k_norm, jnp.ones_like(k_norm)])
    wvec = wvec.reshape(5, 1, HEAD_DIM)
    first = (2 * A_WIDTH) // 512
    return pl.pallas_call(
        _gqa_prep_kernel,
        grid=(proj.shape[0] // ROW_TILE, 5),
        in_specs=[pl.BlockSpec((ROW_TILE, 512), lambda i, c: (i, first + c)),
                  pl.BlockSpec((None, 1, HEAD_DIM), lambda i, c: (c, 0, 0)),
                  pl.BlockSpec((ROW_TILE, HEAD_DIM), lambda i, c: (rb_fn(i), 0)),
                  pl.BlockSpec((ROW_TILE, HEAD_DIM), lambda i, c: (rb_fn(i), 0))],
        out_specs=pl.BlockSpec((None, 4, ROW_TILE, HEAD_DIM), lambda i, c: (b_fn(i), c, rb_fn(i), 0)),
        out_shape=jax.ShapeDtypeStruct((batch, B_HEADS + 2 * B_KV_HEADS, n_tok, HEAD_DIM), BF16),
        compiler_params=_params(("arbitrary", "arbitrary")),
        name="gqa_prep",
    )(proj, wvec, cos, sin)


def _mla_prep_kernel(p_ref, qa_ref, kva_ref, wqn_ref, wqr_ref, wkk_ref, wkv_ref,
                     nqn_ref, nqr_ref, nkn_ref, nkr_ref, cos_ref, sin_ref,
                     qc_ref, kc_ref, vc_ref):
    cq = p_ref[:, :C_Q_RANK].astype(F32)
    ckv = p_ref[:, C_Q_RANK:C_Q_RANK + C_KV_RANK].astype(F32)
    kr = p_ref[:, C_Q_RANK + C_KV_RANK:C_Q_RANK + C_KV_RANK + LANES].astype(F32)
    cqn = _rms(cq, qa_ref[...]).astype(BF16)
    ckvn = _rms(ckv, kva_ref[...]).astype(BF16)
    qn = jnp.dot(cqn, wqn_ref[...], preferred_element_type=F32)
    qr = jnp.dot(cqn, wqr_ref[...], preferred_element_type=F32)
    kn = jnp.dot(ckvn, wkk_ref[...], preferred_element_type=F32)
    vv = jnp.dot(ckvn, wkv_ref[...], preferred_element_type=F32)
    cos = cos_ref[...]
    sin = sin_ref[...]
    lane = lax.broadcasted_iota(I32, kr.shape, 1)
    low = lane < C_ROPE
    inv_qk = 1.0 / C_QK

    kr_rot = _rope(kr * nkr_ref[...], cos, sin, C_ROPE // 4)
    kr_ss = jnp.sum(kr * kr, axis=-1, keepdims=True)
    for h in range(C_HEADS):
        hs = slice(h * C_NOPE, (h + 1) * C_NOPE)
        knh = kn[:, hs]
        r = lax.rsqrt((jnp.sum(knh * knh, axis=-1, keepdims=True) + kr_ss) * inv_qk + EPS)
        kc_ref[h, :, 0:C_NOPE] = (knh * r * nkn_ref[...]).astype(kc_ref.dtype)
        kc_ref[h, :, C_NOPE:C_QK] = (kr_rot * r)[:, 0:C_ROPE].astype(kc_ref.dtype)
        vc_ref[h] = vv[:, h * C_V:(h + 1) * C_V].astype(vc_ref.dtype)

    for pr in range(C_HEADS // 2):
        qrp = qr[:, pr * LANES:(pr + 1) * LANES]
        sq = qrp * qrp
        ss_pair = (jnp.sum(jnp.where(low, sq, 0.0), axis=-1, keepdims=True),
                   jnp.sum(jnp.where(low, 0.0, sq), axis=-1, keepdims=True))
        rs = []
        for e in range(2):
            h = 2 * pr + e
            qnh = qn[:, h * C_NOPE:(h + 1) * C_NOPE]
            r = lax.rsqrt((jnp.sum(qnh * qnh, axis=-1, keepdims=True) + ss_pair[e]) * inv_qk + EPS)
            rs.append(r)
            qc_ref[h, :, 0:C_NOPE] = (qnh * r * nqn_ref[...]).astype(qc_ref.dtype)
        q_rot = _rope(qrp * jnp.where(low, rs[0], rs[1]) * nqr_ref[...], cos, sin, C_ROPE // 4)
        qc_ref[2 * pr, :, C_NOPE:C_QK] = q_rot[:, 0:C_ROPE].astype(qc_ref.dtype)
        qc_ref[2 * pr + 1, :, C_NOPE:C_QK] = q_rot[:, C_ROPE:2 * C_ROPE].astype(qc_ref.dtype)


def _mla_prep(proj, geom, q_a_norm, w_uq, kv_a_norm, w_ukv, q_norm, k_norm, cos, sin):
    batch, n_tok, b_fn, rb_fn = geom
    first = (2 * A_WIDTH + B_WIDTH + 2 * B_KV_HEADS * HEAD_DIM)
    width = IN_COLS_PAD - first
    assert first % width == 0
    wq = w_uq.reshape(C_Q_RANK, C_HEADS, C_QK)
    wqn = wq[:, :, :C_NOPE].reshape(C_Q_RANK, C_HEADS * C_NOPE).astype(BF16)
    wqr = wq[:, :, C_NOPE:].reshape(C_Q_RANK, C_HEADS * C_ROPE).astype(BF16)
    wkv = w_ukv.reshape(C_KV_RANK, C_HEADS, C_NOPE + C_V)
    wkk = wkv[:, :, :C_NOPE].reshape(C_KV_RANK, C_HEADS * C_NOPE).astype(BF16)
    wkvv = wkv[:, :, C_NOPE:].reshape(C_KV_RANK, C_HEADS * C_V).astype(BF16)
    qs = C_QK ** -0.5 * LOG2E
    nqn = (q_norm[:C_NOPE] * qs).reshape(1, C_NOPE)
    nqr = jnp.tile(q_norm[C_NOPE:] * qs, 2).reshape(1, LANES)
    nkn = k_norm[:C_NOPE].reshape(1, C_NOPE)
    nkr = jnp.tile(k_norm[C_NOPE:], 2).reshape(1, LANES)
    const = lambda shape: pl.BlockSpec(shape, lambda i: tuple(0 for _ in shape))
    head_spec = lambda dd: pl.BlockSpec((None, C_HEADS, ROW_TILE, dd), lambda i: (b_fn(i), 0, rb_fn(i), 0))
    return pl.pallas_call(
        _mla_prep_kernel,
        grid=(proj.shape[0] // ROW_TILE,),
        in_specs=[pl.BlockSpec((ROW_TILE, width), lambda i: (i, first // width)),
                  const((1, C_Q_RANK)), const((1, C_KV_RANK)),
                  const(wqn.shape), const(wqr.shape), const(wkk.shape), const(wkvv.shape),
                  const((1, C_NOPE)), const((1, LANES)), const((1, C_NOPE)), const((1, LANES)),
                  pl.BlockSpec((ROW_TILE, LANES), lambda i: (rb_fn(i), 0)),
                  pl.BlockSpec((ROW_TILE, LANES), lambda i: (rb_fn(i), 0))],
        out_specs=[head_spec(C_QK), head_spec(C_QK), head_spec(C_V)],
        out_shape=[jax.ShapeDtypeStruct((batch, C_HEADS, n_tok, C_QK), BF16),
                   jax.ShapeDtypeStruct((batch, C_HEADS, n_tok, C_QK), BF16),
                   jax.ShapeDtypeStruct((batch, C_HEADS, n_tok, C_V), BF16)],
        compiler_params=_params(("arbitrary",)),
        name="mla_prep",
    )(proj, q_a_norm.reshape(1, C_Q_RANK), kv_a_norm.reshape(1, C_KV_RANK), wqn, wqr, wkk, wkvv,
      nqn, nqr, nkn, nkr, cos, sin)


def _attn_kernel(bound_ref, q_ref, k_ref, v_ref, *rest, group, n_lat, bounded):
    o_ref, m_sc, l_sc, acc_sc = rest[-4:]
    tq, dk = q_ref.shape[1], q_ref.shape[2]
    dv = v_ref.shape[1]
    n_tail = k_ref.shape[0] - n_lat
    q = q_ref[...].reshape(group * tq, dk)
    l_sc[...] = jnp.zeros(l_sc.shape, F32)
    acc_sc[...] = jnp.zeros(acc_sc.shape, F32)
    if not bounded:
        m_sc[...] = jnp.full(m_sc.shape, -jnp.inf, F32)

    def lane_fold(p):
        out = p[:, 0:LANES]
        for j in range(1, p.shape[1] // LANES):
            out = out + p[:, j * LANES:(j + 1) * LANES]
        return out

    def chunk(start, size):
        k = k_ref[pl.ds(start, size), :]
        v = v_ref[pl.ds(start, size), :]
        s = lax.dot_general(q, k, (((1,), (1,)), ((), ())), preferred_element_type=F32)
        if bounded:
            p = jnp.exp2(s - bound_ref[0])
            l_sc[...] += lane_fold(p)
            acc_sc[...] += jnp.dot(p.astype(BF16), v, preferred_element_type=F32)
        else:
            m_prev = m_sc[...]
            m_new = jnp.maximum(m_prev, jnp.max(s, axis=-1, keepdims=True))
            alpha = jnp.exp2(m_prev - m_new)
            p = jnp.exp2(s - m_new[:, 0:1])
            l_sc[...] = alpha * l_sc[...] + lane_fold(p)
            acc_sc[...] = alpha[:, 0:1] * acc_sc[...] + jnp.dot(p.astype(BF16), v, preferred_element_type=F32)
            m_sc[...] = m_new

    if n_lat:
        def body(c, carry):
            chunk(pl.multiple_of(c * ATT_TK, ATT_TK), ATT_TK)
            return carry
        lax.fori_loop(0, n_lat // ATT_TK, body, 0, unroll=2)
    chunk(n_lat, n_tail)
    o = acc_sc[...] / jnp.sum(l_sc[...], axis=-1, keepdims=True)
    for g in range(group):
        o_ref[:, g * dv:(g + 1) * dv] = o[g * tq:(g + 1) * tq].astype(o_ref.dtype)


def _attention_call(bound, q_arr, k_arr, v_arr, prev_out, heads, n_kv, group, tq, n_lat, n_ctx, ctx_only,
                    t_rows, bounded, name):
    q_head0, k_head0, v_head0 = heads
    batch, _, n_tok, dk = q_arr.shape
    dv = v_arr.shape[-1]
    m = group * tq
    if ctx_only:
        assert tq == n_ctx and n_lat % n_ctx == 0
        nq, q0, key_rows, key_blk = 1, n_lat // tq, n_ctx, n_lat // n_ctx
        out_row = lambda b, qi: batch * (n_lat // tq) + b
    else:
        assert n_lat % tq == 0 and n_lat % ATT_TK == 0
        nq, q0, key_rows, key_blk = n_lat // tq, 0, n_tok, 0
        out_row = lambda b, qi: b * nq + qi
    in_specs = [pl.BlockSpec((None, group, tq, dk), lambda b, g, qi, *_: (b, q_head0 // group + g, q0 + qi, 0)),
                pl.BlockSpec((None, None, key_rows, dk), lambda b, g, qi, *_: (b, k_head0 + g, key_blk, 0)),
                pl.BlockSpec((None, None, key_rows, dv), lambda b, g, qi, *_: (b, v_head0 + g, key_blk, 0))]
    args = [bound, q_arr, k_arr, v_arr]
    aliases = {}
    if prev_out is not None:
        in_specs.append(pl.BlockSpec(memory_space=pl.ANY))
        args.append(prev_out)
        aliases = {len(args) - 1: 0}
    grid_spec = pltpu.PrefetchScalarGridSpec(
        num_scalar_prefetch=1,
        grid=(batch, n_kv, nq),
        in_specs=in_specs,
        out_specs=pl.BlockSpec((tq, group * dv), lambda b, g, qi, *_: (out_row(b, qi), g)),
        scratch_shapes=[pltpu.VMEM((m, LANES), F32), pltpu.VMEM((m, LANES), F32), pltpu.VMEM((m, dv), F32)],
    )
    return pl.pallas_call(
        functools.partial(_attn_kernel, group=group, n_lat=0 if ctx_only else n_lat, bounded=bounded),
        grid_spec=grid_spec,
        out_shape=jax.ShapeDtypeStruct((t_rows, n_kv * group * dv), BF16),
        input_output_aliases=aliases,
        compiler_params=_params(("arbitrary", "arbitrary", "arbitrary")),
        name=name,
    )(*args)


def _attention(bound, q_arr, k_arr, v_arr, heads, n_kv, group, tq, n_lat, n_ctx, with_ctx, t_rows, name):
    tq = min(tq, n_lat)

    def run(bounded):
        def f(bound, q_arr, k_arr, v_arr):
            out = _attention_call(bound, q_arr, k_arr, v_arr, None, heads, n_kv, group, tq, n_lat, n_ctx,
                                  False, t_rows, bounded, name)
            if with_ctx:
                out = _attention_call(bound, q_arr, k_arr, v_arr, out, heads, n_kv, group, n_ctx, n_lat, n_ctx,
                                      True, t_rows, bounded, name + "_ctx")
            return out
        return f

    return lax.cond(bound[0] <= SCORE_BOUND_LIMIT, run(True), run(False), bound, q_arr, k_arr, v_arr)


def _score_bound(q_norm, k_norm, dk):
    b = dk ** 0.5 * jnp.max(jnp.abs(q_norm)) * jnp.max(jnp.abs(k_norm)) * (LOG2E * BOUND_SLACK)
    return b.reshape(1).astype(F32)


def _wout_kernel(a_ref, b_ref, c_ref, w_ref, x_ref, g_ref, o_ref):
    acc = jnp.dot(a_ref[...], w_ref[0:A_WIDTH, :], preferred_element_type=F32)
    acc += jnp.dot(b_ref[...], w_ref[A_WIDTH:A_WIDTH + B_WIDTH, :], preferred_element_type=F32)
    acc += jnp.dot(c_ref[...], w_ref[A_WIDTH + B_WIDTH:, :], preferred_element_type=F32)
    o_ref[...] = x_ref[...] + g_ref[...] * acc


def _out_proj(a_out, b_out, c_out, w, x, mods, layer, t_rows, mod_row):
    d = w.shape[1]
    return pl.pallas_call(
        _wout_kernel,
        grid=(d // MM_TN, t_rows // MM_TM),
        in_specs=[pl.BlockSpec((MM_TM, A_WIDTH), lambda j, i: (i, 0)),
                  pl.BlockSpec((MM_TM, B_WIDTH), lambda j, i: (i, 0)),
                  pl.BlockSpec((MM_TM, C_WIDTH), lambda j, i: (i, 0)),
                  pl.BlockSpec((w.shape[0], MM_TN), lambda j, i: (0, j)),
                  pl.BlockSpec((MM_TM, MM_TN), lambda j, i: (i, j)),
                  _mod_spec(layer, 2, d, lambda j, i: mod_row(i), MM_TN, lambda j, i: j)],
        out_specs=pl.BlockSpec((MM_TM, MM_TN), lambda j, i: (i, j)),
        out_shape=jax.ShapeDtypeStruct((t_rows, d), F32),
        compiler_params=_params(("arbitrary", "arbitrary")),
        name="out_proj",
    )(a_out, b_out, c_out, w, x, mods)


def _first_index(hit_val, x, iota, big):
    return jnp.min(jnp.where(x == hit_val, iota, big), axis=0, keepdims=True)


def _router_kernel(x_ref, nw_ref, shift_ref, scale_ref, rwt_ref, bias_ref,
                   hp_ref, eidx_ref, gate_ref, rank_ref, cnt_ref, carry_sc):
    i = pl.program_id(0)
    tm, d = x_ref.shape
    h = _rms(x_ref[...], nw_ref[...]) * (1.0 + scale_ref[...]) + shift_ref[...]
    _store_slabs(hp_ref, 0, h)

    logits = lax.dot_general(rwt_ref[...], h, (((1,), (1,)), ((), ())),
                             precision=lax.Precision.HIGHEST, preferred_element_type=F32)
    scores = jax.nn.sigmoid(logits)
    sel = scores + bias_ref[:, 0:1]
    neg = jnp.float32(-jnp.inf)

    iota_n = lax.broadcasted_iota(I32, (N_GROUPS, tm), 0).astype(F32)
    iota_e = lax.broadcasted_iota(I32, (N_EXPERTS, tm), 0).astype(F32)

    gscore = jnp.zeros((N_GROUPS, tm), F32)
    for g in range(N_GROUPS):
        blk = sel[g * GROUP_SIZE:(g + 1) * GROUP_SIZE]
        m1 = jnp.max(blk, axis=0, keepdims=True)
        first = _first_index(m1, blk, iota_n, float(GROUP_SIZE))
        m2 = jnp.max(jnp.where(iota_n == first, neg, blk), axis=0, keepdims=True)
        gscore = jnp.where(iota_n == float(g), m1 + m2, gscore)

    gsel = jnp.zeros((N_GROUPS, tm), F32)
    work = gscore
    for _ in range(TOPK_GROUPS):
        m = jnp.max(work, axis=0, keepdims=True)
        hit = iota_n == _first_index(m, work, iota_n, float(N_GROUPS))
        gsel = jnp.where(hit, 1.0, gsel)
        work = jnp.where(hit, neg, work)
    emask = jnp.concatenate(
        [jnp.broadcast_to(gsel[g:g + 1], (GROUP_SIZE, tm)) for g in range(N_GROUPS)], axis=0)

    work = jnp.where(emask > 0.5, sel, neg)
    picks = []
    onehot = jnp.zeros((N_EXPERTS, tm), F32)
    for _ in range(TOP_K):
        m = jnp.max(work, axis=0, keepdims=True)
        pick = _first_index(m, work, iota_e, float(N_EXPERTS))
        picks.append(pick)
        hit = iota_e == pick
        onehot = jnp.where(hit, 1.0, onehot)
        work = jnp.where(hit, neg, work)

    w_un = onehot * scores
    gates = w_un / jnp.sum(w_un, axis=0, keepdims=True) * ROUTED_SCALE

    @pl.when(i == 0)
    def _():
        carry_sc[...] = jnp.zeros(carry_sc.shape, F32)

    before = (lax.broadcasted_iota(I32, (tm, tm), 0) < lax.broadcasted_iota(I32, (tm, tm), 1)).astype(BF16)
    rank = carry_sc[:, 0:1] + jnp.dot(onehot.astype(BF16), before, preferred_element_type=F32)
    carry_sc[...] = carry_sc[...] + jnp.sum(onehot, axis=1, keepdims=True)
    cnt_ref[...] = carry_sc[...]

    eidx_ref[...] = jnp.zeros(eidx_ref.shape, I32)
    gate_ref[...] = jnp.zeros(gate_ref.shape, F32)
    rank_ref[...] = jnp.zeros(rank_ref.shape, I32)
    for k in range(TOP_K):
        hit = iota_e == picks[k]
        eidx_ref[k:k + 1, :] = picks[k].astype(I32)
        gate_ref[k:k + 1, :] = jnp.sum(jnp.where(hit, gates, 0.0), axis=0, keepdims=True)
        rank_ref[k:k + 1, :] = jnp.sum(jnp.where(hit, rank, 0.0), axis=0, keepdims=True).astype(I32)


def _norm_router(x, t_rows, norm_w, mods, layer, mod_row, router_w, router_bias):
    d = x.shape[1]
    col = lambda dt: (pl.BlockSpec((SUBLANES, ROW_TILE), lambda i: (0, i)),
                      jax.ShapeDtypeStruct((SUBLANES, t_rows), dt))
    specs, shapes = zip(col(I32), col(F32), col(I32))
    return pl.pallas_call(
        _router_kernel,
        grid=(t_rows // ROW_TILE,),
        in_specs=[pl.BlockSpec((ROW_TILE, d), lambda i: (i, 0)),
                  pl.BlockSpec((1, d), lambda i: (0, 0)),
                  _mod_spec(layer, 3, d, mod_row),
                  _mod_spec(layer, 4, d, mod_row),
                  pl.BlockSpec((N_EXPERTS, d), lambda i: (0, 0)),
                  pl.BlockSpec((N_EXPERTS, LANES), lambda i: (0, 0))],
        out_specs=[pl.BlockSpec((ROW_TILE * _slab_rows(d), LANES), lambda i: (i, 0)), *specs,
                   pl.BlockSpec((N_EXPERTS, LANES), lambda i: (0, 0))],
        out_shape=[jax.ShapeDtypeStruct((t_rows * _slab_rows(d), LANES), U32), *shapes,
                   jax.ShapeDtypeStruct((N_EXPERTS, LANES), F32)],
        scratch_shapes=[pltpu.VMEM((N_EXPERTS, LANES), F32)],
        compiler_params=_params(("arbitrary",)),
        name="norm_router",
    )(x, norm_w.reshape(1, d), mods, mods, jnp.transpose(router_w),
      jnp.broadcast_to(router_bias[:, None], (N_EXPERTS, LANES)))


def _slab_copy(src, src_tok, dst, dst_tok, n_tok, n_sl, sem):
    rows = n_tok * n_sl
    first = lambda tok: tok * n_sl if isinstance(tok, int) else pl.multiple_of(tok * n_sl, n_sl)
    return pltpu.make_async_copy(src.at[pl.ds(first(src_tok), rows)], dst.at[pl.ds(first(dst_tok), rows)], sem)


_FILL_PIECES = tuple(1 << b for b in range(EXP_TM.bit_length() - 2, -1, -1))


def _zero_fill(pstart_ref, plen_ref, tail_ref, xs_hbm, zero_sc, n_sl, sem_z):
    zero_sc[...] = jnp.zeros(zero_sc.shape, zero_sc.dtype)
    n_tail = (xs_hbm.shape[0] // n_sl - tail_ref[0]) // _FILL_PIECES[0]

    def pad_copies(e, wait):
        off = pstart_ref[e]
        for piece in _FILL_PIECES:
            on = (plen_ref[e] & piece) != 0
            cp = _slab_copy(zero_sc, 0, xs_hbm, off, piece, n_sl, sem_z)

            @pl.when(on)
            def _():
                cp.wait() if wait else cp.start()

            off = off + jnp.where(on, piece, 0)

    def tail_copy(j, wait):
        cp = _slab_copy(zero_sc, 0, xs_hbm, tail_ref[0] + j * _FILL_PIECES[0], _FILL_PIECES[0], n_sl, sem_z)
        cp.wait() if wait else cp.start()

    for wait in (False, True):
        lax.fori_loop(0, N_EXPERTS, lambda e, c: (pad_copies(e, wait), c)[1], 0)
        lax.fori_loop(0, n_tail, lambda j, c: (tail_copy(j, wait), c)[1], 0)


def _dispatch_kernel(pstart_ref, plen_ref, tail_ref, slot_hbm, h_ref, xs_hbm, slot_sm, zero_sc, sem_s, sem, sem_z):
    i = pl.program_id(0)
    n = slot_sm.shape[0]
    n_sl = h_ref.shape[0] // DISP_T
    cp = pltpu.make_async_copy(slot_hbm.at[pl.ds(pl.multiple_of(i * n, n), n)], slot_sm, sem_s)
    cp.start()

    @pl.when(i == 0)
    def _():
        _zero_fill(pstart_ref, plen_ref, tail_ref, xs_hbm, zero_sc, n_sl, sem_z)

    cp.wait()

    def issue(t, carry):
        for k in range(TOP_K):
            _slab_copy(h_ref, t, xs_hbm, slot_sm[t * SUBLANES + k], 1, n_sl, sem).start(priority=k % 2)
        return carry

    lax.fori_loop(0, DISP_T, issue, 0)

    def drain(t, carry):
        for k in range(TOP_K):
            _slab_copy(h_ref, 0, xs_hbm, 0, 1, n_sl, sem).wait()
        return carry

    lax.fori_loop(0, DISP_T, drain, 0)


def _dispatch(hp, slots, pad_start, pad_len, tail_start, n_slots, n_sl):
    t = hp.shape[0] // n_sl
    grid_spec = pltpu.PrefetchScalarGridSpec(
        num_scalar_prefetch=3,
        grid=(t // DISP_T,),
        in_specs=[pl.BlockSpec(memory_space=pl.ANY), pl.BlockSpec((DISP_T * n_sl, LANES), lambda i, *_: (i, 0))],
        out_specs=pl.BlockSpec(memory_space=pl.ANY),
        scratch_shapes=[pltpu.SMEM((DISP_T * SUBLANES,), I32),
                        pltpu.VMEM((_FILL_PIECES[0] * n_sl, LANES), U32),
                        pltpu.SemaphoreType.DMA(()), pltpu.SemaphoreType.DMA(()), pltpu.SemaphoreType.DMA(())],
    )
    return pl.pallas_call(
        _dispatch_kernel,
        grid_spec=grid_spec,
        out_shape=jax.ShapeDtypeStruct((n_slots * n_sl, LANES), U32),
        compiler_params=_params(("arbitrary",)),
        name="moe_dispatch",
    )(pad_start, pad_len, tail_start, slots, hp)


def _expert_kernel(te_ref, par_ref, nxt_ref, nu_ref, xs_ref, wg_hbm, wu_hbm, wd_hbm, ys_ref,
                   wg_f, wu_f, wd_f, wg_sc, wu_sc, wd_sc, sem, *, layer):
    i = pl.program_id(0)
    expert = te_ref[i]
    par = par_ref[i]

    def weight_copies(e, slot):
        return (pltpu.make_async_copy(wg_hbm.at[layer, e], wg_f.at[slot], sem.at[slot, 0]),
                pltpu.make_async_copy(wu_hbm.at[layer, e], wu_f.at[slot], sem.at[slot, 1]),
                pltpu.make_async_copy(wd_hbm.at[layer, e], wd_f.at[slot], sem.at[slot, 2]))

    @pl.when(i == 0)
    def _():
        for cp in weight_copies(expert, par):
            cp.start()

    @pl.when(jnp.logical_or(i == 0, expert != te_ref[jnp.maximum(i - 1, 0)]))
    def _():
        for cp in weight_copies(expert, par):
            cp.wait()

        @pl.when(nxt_ref[i] >= 0)
        def _():
            for cp in weight_copies(nxt_ref[i], 1 - par):
                cp.start()

        d = wg_sc.shape[0]
        for a in range(_slab_rows(d)):
            for half, src0 in enumerate((a * LANES, d // 2 + a * LANES)):
                dst = slice((2 * a + half) * LANES, (2 * a + half + 1) * LANES)
                wg_sc[dst, :] = wg_f[par, src0:src0 + LANES, :].astype(BF16)
                wu_sc[dst, :] = wu_f[par, src0:src0 + LANES, :].astype(BF16)
        wd_sc[...] = wd_f[par].astype(BF16)

    @pl.when(i < nu_ref[0])
    def _():
        n_sl = xs_ref.shape[0] // EXP_TM
        y = _ffn_slabs(xs_ref, 0, EXP_TM, n_sl, wg_sc, wu_sc, wd_sc)
        _store_slabs(ys_ref, 0, y)

    @pl.when(i >= nu_ref[0])
    def _():
        ys_ref[...] = jnp.zeros(ys_ref.shape, ys_ref.dtype)


def _experts(xs, tile_expert, tile_slot, tile_next, n_used, w_gate, w_up, w_down, layer):
    _, _, d, f = w_gate.shape
    n_sl = _slab_rows(d)
    n_slots = xs.shape[0] // n_sl
    n_tiles = n_slots // EXP_TM
    hbm = pl.BlockSpec(memory_space=pl.ANY)
    tile = (EXP_TM * n_sl, LANES)
    grid_spec = pltpu.PrefetchScalarGridSpec(
        num_scalar_prefetch=4,
        grid=(n_tiles,),
        in_specs=[pl.BlockSpec(tile, lambda i, te, par, nxt, nu: (jnp.minimum(i, nu[0] - 1), 0)),
                  hbm, hbm, hbm],
        out_specs=pl.BlockSpec(tile, lambda i, te, par, nxt, nu: (i, 0)),
        scratch_shapes=[pltpu.VMEM((2, d, f), F32), pltpu.VMEM((2, d, f), F32), pltpu.VMEM((2, f, d), F32),
                        pltpu.VMEM((d, f), BF16), pltpu.VMEM((d, f), BF16), pltpu.VMEM((f, d), BF16),
                        pltpu.SemaphoreType.DMA((2, 3))],
    )
    return pl.pallas_call(
        functools.partial(_expert_kernel, layer=layer),
        grid_spec=grid_spec,
        out_shape=jax.ShapeDtypeStruct(xs.shape, U32),
        compiler_params=_params(("arbitrary",)),
        name="moe_experts",
    )(tile_expert, tile_slot, tile_next, n_used, xs, w_gate, w_up, w_down)


def _combine_kernel(slot_hbm, ys_hbm, hp_ref, gate_ref, sg_ref, su_ref, sd_ref, x_ref, g5_ref,
                    o_ref, slot_sm, buf, sem_s, sem):
    i = pl.program_id(0)
    n_steps = pl.num_programs(0)
    n = COMB_T * SUBLANES
    half = x_ref.shape[1] // 2
    n_sl = half // LANES

    def gather(h, half_index):
        cp = pltpu.make_async_copy(slot_hbm.at[pl.ds(pl.multiple_of(half_index * n, n), n)],
                                   slot_sm.at[h], sem_s.at[h])
        cp.start()
        cp.wait()

        def issue(t, carry):
            for k in range(TOP_K):
                _slab_copy(ys_hbm, slot_sm[h, t * SUBLANES + k], buf.at[h, k], t, 1, n_sl,
                           sem.at[h]).start(priority=k % 2)
            return carry

        lax.fori_loop(0, COMB_T, issue, 0)

    def finish(h):
        rows = slice(h * COMB_T, (h + 1) * COMB_T)
        y = _ffn_slabs(hp_ref, h * COMB_T * n_sl, COMB_T, n_sl, sg_ref, su_ref, sd_ref)

        def drain(t, carry):
            for k in range(TOP_K):
                _slab_copy(ys_hbm, 0, buf.at[h, k], 0, 1, n_sl, sem.at[h]).wait()
            return carry

        lax.fori_loop(0, COMB_T, drain, 0)

        gates = [gate_ref[rows, k:k + 1] for k in range(TOP_K)]
        for a in range(n_sl):
            lo_cols = slice(a * LANES, (a + 1) * LANES)
            hi_cols = slice(half + a * LANES, half + (a + 1) * LANES)
            r_lo = y[:, lo_cols]
            r_hi = y[:, hi_cols]
            for k in range(TOP_K):
                e_lo, e_hi = _unpack_bf16_pair(_slab_piece(buf.at[h, k], 0, COMB_T, n_sl, a))
                r_lo = r_lo + gates[k] * e_lo
                r_hi = r_hi + gates[k] * e_hi
            o_ref[rows, lo_cols] = x_ref[rows, lo_cols] + g5_ref[:, lo_cols] * r_lo
            o_ref[rows, hi_cols] = x_ref[rows, hi_cols] + g5_ref[:, hi_cols] * r_hi

    @pl.when(i == 0)
    def _():
        gather(0, 0)

    gather(1, 2 * i + 1)
    finish(0)

    @pl.when(i + 1 < n_steps)
    def _():
        gather(0, 2 * i + 2)

    finish(1)


def _combine(ys, hp, slots, gates, sh_gate, sh_up, sh_down, x, mods, layer, t_rows, mod_row):
    d = x.shape[1]
    n_sl = _slab_rows(d)
    f = sh_gate.shape[1]
    rows = 2 * COMB_T
    slab_order = lambda w: jnp.concatenate(_slab_weight_rows(w), axis=0).astype(BF16)
    return pl.pallas_call(
        _combine_kernel,
        grid=(t_rows // rows,),
        in_specs=[pl.BlockSpec(memory_space=pl.ANY), pl.BlockSpec(memory_space=pl.ANY),
                  pl.BlockSpec((rows * n_sl, LANES), lambda i: (i, 0)),
                  pl.BlockSpec((rows, SUBLANES), lambda i: (i, 0)),
                  pl.BlockSpec((d, f), lambda i: (0, 0)),
                  pl.BlockSpec((d, f), lambda i: (0, 0)),
                  pl.BlockSpec((f, d), lambda i: (0, 0)),
                  pl.BlockSpec((rows, d), lambda i: (i, 0)),
                  _mod_spec(layer, 5, d, lambda i: mod_row(i * rows // ROW_TILE))],
        out_specs=pl.BlockSpec((rows, d), lambda i: (i, 0)),
        out_shape=jax.ShapeDtypeStruct((t_rows, d), F32),
        scratch_shapes=[pltpu.SMEM((2, COMB_T * SUBLANES), I32),
                        pltpu.VMEM((2, TOP_K, COMB_T * n_sl, LANES), U32),
                        pltpu.SemaphoreType.DMA((2,)), pltpu.SemaphoreType.DMA((2,))],
        compiler_params=_params(("arbitrary",)),
        name="moe_combine",
    )(slots, ys, hp, gates, slab_order(sh_gate), slab_order(sh_up), sh_down.astype(BF16), x, mods)


def _moe(x, t_rows, norm_w, mods, layer, mod_row, router_w, router_bias,
         w_gate, w_up, w_down, sh_gate, sh_up, sh_down):
    hp, eidx, gate, rank, cnt = _norm_router(x, t_rows, norm_w, mods, layer, mod_row, router_w, router_bias)

    n_tiles = (t_rows * TOP_K + N_EXPERTS * (EXP_TM - 1) + EXP_TM - 1) // EXP_TM
    counts = cnt[:, 0].astype(I32)
    padded = (counts + EXP_TM - 1) // EXP_TM * EXP_TM
    ends = jnp.cumsum(padded)
    base = ends - padded
    experts = jnp.arange(N_EXPERTS, dtype=I32)
    base_of = jnp.sum(jnp.where(eidx[:, :, None] == experts, base, 0), axis=-1)
    slots = jnp.transpose(base_of + rank).reshape(-1)
    n_used = (ends[-1] // EXP_TM).astype(I32)
    tile_start = jnp.arange(n_tiles, dtype=I32) * EXP_TM
    last_start = (n_used - 1) * EXP_TM
    te = jnp.sum(jnp.minimum(tile_start, last_start)[:, None] >= ends[None, :], axis=1).astype(I32)
    te = jnp.minimum(te, N_EXPERTS - 1)
    used = counts > 0
    slot_of = ((jnp.cumsum(used.astype(I32)) - 1) % 2).astype(I32)
    later = jnp.logical_and(experts[None, :] > experts[:, None], used[None, :])
    next_of = jnp.min(jnp.where(later, experts[None, :], N_EXPERTS), axis=1)
    next_of = jnp.where(next_of == N_EXPERTS, -1, next_of).astype(I32)

    n_sl = _slab_rows(x.shape[1])
    assert n_sl % SUBLANES == 0
    xs = _dispatch(hp, slots, base + counts, padded - counts, ends[-1:], n_tiles * EXP_TM, n_sl)
    ys = _experts(xs, te, slot_of[te], next_of[te], n_used.reshape(1), w_gate, w_up, w_down, layer)
    return _combine(ys, hp, slots, jnp.transpose(gate), sh_gate, sh_up, sh_down, x, mods, layer, t_rows, mod_row)


def kernel(x, c, ctx, c_ctx, ada_w, ada_b, norm1_w, norm2_w, w_in, gm_v_norm, gm_w_s, gm_b_s, gqa_q_norm, gqa_k_norm, mla_q_a_norm, mla_w_uq, mla_kv_a_norm, mla_w_ukv, mla_q_norm, mla_k_norm, w_out, router_w, router_bias, exp_w_gate, exp_w_up, exp_w_down, sh_w_gate, sh_w_up, sh_w_down):
    batch, n_lat, d = x.shape
    n_ctx = ctx.shape[1]
    depth = ada_w.shape[0]
    assert n_ctx == ROW_TILE and n_lat % MM_TM == 0 and (batch * n_ctx) % MM_TM == 0
    t_lat = batch * n_lat
    t_all = t_lat + batch * n_ctx
    n_tok = n_lat + n_ctx
    lat_tiles = n_lat // ROW_TILE

    mod_row = lambda i: jnp.minimum(i // lat_tiles, batch)
    is_lat = lambda i: i < batch * lat_tiles
    b_fn = lambda i: jnp.where(is_lat(i), i // lat_tiles, i - batch * lat_tiles)
    rb_fn = lambda i: jnp.where(is_lat(i), i % lat_tiles, lat_tiles)
    geom = (batch, n_tok, b_fn, rb_fn)
    mm_mod_row = lambda i: mod_row(i * (MM_TM // ROW_TILE))

    mods = _ada_mods(c, c_ctx, ada_w, ada_b)
    cos_b, sin_b = _rope_tables(n_lat, n_ctx, HEAD_DIM, 1)
    cos_c, sin_c = _rope_tables(n_lat, n_ctx, C_ROPE, LANES // C_ROPE)

    xs = jnp.concatenate([x.reshape(t_lat, d), ctx.reshape(batch * n_ctx, d)], axis=0)
    for l in range(depth):
        need_ctx = l < depth - 1
        t_rows = t_all if need_ctx else t_lat
        w_in_l = jnp.pad(w_in[l].astype(BF16), ((0, 0), (0, IN_COLS_PAD - w_in.shape[2])))

        proj = _in_proj(xs, norm1_w[l], mods, l, mm_mod_row, w_in_l)
        a_out = _gmlp(proj, t_rows, gm_v_norm[l], gm_w_s[l], gm_b_s[l])
        qkv = _gqa_prep(proj, geom, gqa_q_norm[l], gqa_k_norm[l], cos_b, sin_b)
        b_out = _attention(_score_bound(gqa_q_norm[l], gqa_k_norm[l], HEAD_DIM), qkv, qkv, qkv,
                           (0, B_HEADS, B_HEADS + B_KV_HEADS), B_KV_HEADS, B_GROUP, GQA_TQ,
                           n_lat, n_ctx, need_ctx, t_rows, "gqa_attn")
        qc, kc, vc = _mla_prep(proj, geom, mla_q_a_norm[l], mla_w_uq[l], mla_kv_a_norm[l], mla_w_ukv[l],
                               mla_q_norm[l], mla_k_norm[l], cos_c, sin_c)
        c_out = _attention(_score_bound(mla_q_norm[l], mla_k_norm[l], C_QK), qc, kc, vc,
                           (0, 0, 0), C_HEADS, 1, MLA_TQ, n_lat, n_ctx, need_ctx, t_rows, "mla_attn")
        xs = _out_proj(a_out, b_out, c_out, w_out[l].astype(BF16), xs, mods, l, t_rows, mm_mod_row)
        xs = _moe(xs, t_rows, norm2_w[l], mods, l, mod_row, router_w[l], router_bias[l],
                  exp_w_gate, exp_w_up, exp_w_down, sh_w_gate[l], sh_w_up[l], sh_w_down[l])
    return xs[:t_lat].reshape(batch, n_lat, d)
```

```python
import functools

import jax
import jax.numpy as jnp
from jax import lax
from jax.experimental import pallas as pl
from jax.experimental.pallas import tpu as pltpu

F32 = jnp.float32
BF16 = jnp.bfloat16
U32 = jnp.uint32
I32 = jnp.int32

GRID_W = 64
CHUNK = 128
HEAD_DIM = 128
EPS = 1e-6
ROPE_THETA = 10000.0
A_HEADS = 8
B_HEADS = 12
B_KV_HEADS = 4
B_GROUP = B_HEADS // B_KV_HEADS
C_HEADS = 12
C_Q_RANK = 768
C_KV_RANK = 512
C_NOPE = 128
C_ROPE = 64
C_V = 128
C_QK = C_NOPE + C_ROPE
A_WIDTH = A_HEADS * HEAD_DIM
B_WIDTH = B_HEADS * HEAD_DIM
C_WIDTH = C_HEADS * C_V
N_EXPERTS = 64
TOP_K = 6
N_GROUPS = 8
GROUP_SIZE = N_EXPERTS // N_GROUPS
TOPK_GROUPS = 4
ROUTED_SCALE = 2.5
N_MOD = 6

LANES = 128
SUBLANES = 8
ROW_TILE = 256
MM_TM = 512
MM_TN = 1024
GQA_TQ = 512
MLA_TQ = 1024
ATT_TK = 512
LOG2E = 1.4426950408889634
BOUND_SLACK = 1.02
SCORE_BOUND_LIMIT = 60.0
EXP_TM = 256
DISP_T = 256
COMB_T = 128
COMB_COLS = 128
IN_COLS_PAD = 6144
VMEM_LIMIT = 56 * 1024 * 1024


def _params(sem, vmem=VMEM_LIMIT):
    return pltpu.CompilerParams(dimension_semantics=sem, vmem_limit_bytes=vmem)


def _rms(x, w):
    return x * lax.rsqrt(jnp.mean(x * x, axis=-1, keepdims=True) + EPS) * w


def _pack_bf16_pair(a, b):
    ua = lax.bitcast_convert_type(a.astype(BF16).astype(F32), U32)
    ub = lax.bitcast_convert_type(b.astype(BF16).astype(F32), U32)
    return (ua >> 16) | ub


def _unpack_bf16_pair(u):
    a = lax.bitcast_convert_type(u << 16, F32)
    b = lax.bitcast_convert_type(u & jnp.uint32(0xFFFF0000), F32)
    return a, b


def _ffn(lo, hi, wg, wu, wd):
    half = lo.shape[1]
    hg = jnp.dot(lo, wg[:half], preferred_element_type=F32) + jnp.dot(hi, wg[half:], preferred_element_type=F32)
    hu = jnp.dot(lo, wu[:half], preferred_element_type=F32) + jnp.dot(hi, wu[half:], preferred_element_type=F32)
    hb = (jax.nn.silu(hg) * hu).astype(BF16)
    return jnp.dot(hb, wd, preferred_element_type=F32)


def _ada_kernel(cb_ref, w_ref, b_ref, o_ref, silu_sc, *, n_rows, d, tn):
    nj = tn // LANES

    @pl.when(jnp.logical_and(pl.program_id(0) == 0, pl.program_id(1) == 0))
    def _():
        cv = cb_ref[...]
        silu_sc[...] = cv * jax.nn.sigmoid(cv)

    def body(kc, accs):
        k0 = pl.multiple_of(kc * SUBLANES, SUBLANES)
        wk = w_ref[0, pl.ds(k0, SUBLANES), :]
        out = []
        for r in range(n_rows):
            sv = silu_sc[r, pl.ds(k0, SUBLANES), :]
            for j in range(nj):
                out.append(accs[r * nj + j] + wk[:, j * LANES:(j + 1) * LANES] * sv)
        return tuple(out)

    init = tuple(jnp.zeros((SUBLANES, LANES), F32) for _ in range(n_rows * nj))
    accs = lax.fori_loop(0, d // SUBLANES, body, init, unroll=8)
    o_ref[0] = jnp.zeros((SUBLANES, tn), F32)
    for r in range(n_rows):
        for j in range(nj):
            o_ref[0, r:r + 1, j * LANES:(j + 1) * LANES] = (
                jnp.sum(accs[r * nj + j], axis=0, keepdims=True) + b_ref[0, :, j * LANES:(j + 1) * LANES])


def _ada_mods(c, c_ctx, ada_w, ada_b):
    n_layers, d, nd = ada_w.shape
    n_rows = c.shape[0] + 1
    assert n_rows <= SUBLANES
    cvec = jnp.concatenate([c, c_ctx[None]], axis=0)
    cb = jnp.broadcast_to(cvec[:, :, None], (n_rows, d, LANES))
    tn = 512
    out = pl.pallas_call(
        functools.partial(_ada_kernel, n_rows=n_rows, d=d, tn=tn),
        grid=(n_layers, nd // tn),
        in_specs=[pl.BlockSpec((n_rows, d, LANES), lambda l, j: (0, 0, 0)),
                  pl.BlockSpec((1, d, tn), lambda l, j: (l, 0, j)),
                  pl.BlockSpec((1, 1, tn), lambda l, j: (l, 0, j))],
        out_specs=pl.BlockSpec((1, SUBLANES, tn), lambda l, j: (l, 0, j)),
        out_shape=jax.ShapeDtypeStruct((n_layers, SUBLANES, nd), F32),
        scratch_shapes=[pltpu.VMEM((n_rows, d, LANES), F32)],
        compiler_params=_params(("arbitrary", "arbitrary")),
        name="ada_mods",
    )(cb, ada_w, ada_b.reshape(n_layers, 1, nd))
    return out.reshape(n_layers, SUBLANES, N_MOD, 1, d)


def _mod_spec(layer, m, d, row_fn, tn=None, col_fn=None):
    if tn is None:
        return pl.BlockSpec((None, None, None, 1, d), lambda *a: (layer, row_fn(*a), m, 0, 0))
    return pl.BlockSpec((None, None, None, 1, tn), lambda *a: (layer, row_fn(*a), m, 0, col_fn(*a)))


def _in_proj_kernel(x_ref, nw_ref, shift_ref, scale_ref, w_ref, o_ref, h_sc):
    @pl.when(pl.program_id(1) == 0)
    def _():
        y = _rms(x_ref[...], nw_ref[...])
        h_sc[...] = (y * (1.0 + scale_ref[...]) + shift_ref[...]).astype(h_sc.dtype)

    o_ref[...] = jnp.dot(h_sc[...], w_ref[...], preferred_element_type=F32).astype(o_ref.dtype)


def _in_proj(x, norm_w, mods, layer, mod_row, w):
    m, d = x.shape
    n = w.shape[1]
    return pl.pallas_call(
        _in_proj_kernel,
        grid=(m // MM_TM, n // MM_TN),
        in_specs=[pl.BlockSpec((MM_TM, d), lambda i, j: (i, 0)),
                  pl.BlockSpec((1, d), lambda i, j: (0, 0)),
                  _mod_spec(layer, 0, d, lambda i, j: mod_row(i)),
                  _mod_spec(layer, 1, d, lambda i, j: mod_row(i)),
                  pl.BlockSpec((d, MM_TN), lambda i, j: (0, j))],
        out_specs=pl.BlockSpec((MM_TM, MM_TN), lambda i, j: (i, j)),
        out_shape=jax.ShapeDtypeStruct((m, n), BF16),
        scratch_shapes=[pltpu.VMEM((MM_TM, d), BF16)],
        compiler_params=_params(("arbitrary", "arbitrary")),
        name="in_proj",
    )(x, norm_w.reshape(1, d), mods, mods, w)


def _gmlp_kernel(p_ref, vn_ref, ws_ref, bt_ref, o_ref):
    for h in range(A_HEADS):
        hs = slice(h * HEAD_DIM, (h + 1) * HEAD_DIM)
        u = jax.nn.gelu(p_ref[:, hs].astype(F32))
        v = jax.nn.gelu(p_ref[:, A_WIDTH + h * HEAD_DIM:A_WIDTH + (h + 1) * HEAD_DIM].astype(F32))
        vb = _rms(v, vn_ref[:, hs]).astype(BF16)
        for c in range(ROW_TILE // CHUNK):
            cs = slice(c * CHUNK, (c + 1) * CHUNK)
            s = jnp.dot(ws_ref[h], vb[cs], preferred_element_type=F32) + bt_ref[:, h:h + 1]
            o_ref[cs, hs] = (u[cs] * s).astype(o_ref.dtype)


def _gmlp(proj, t_rows, v_norm, w_s, b_s):
    return pl.pallas_call(
        _gmlp_kernel,
        grid=(t_rows // ROW_TILE,),
        in_specs=[pl.BlockSpec((ROW_TILE, 2 * A_WIDTH), lambda i: (i, 0)),
                  pl.BlockSpec((1, A_WIDTH), lambda i: (0, 0)),
                  pl.BlockSpec((A_HEADS, CHUNK, CHUNK), lambda i: (0, 0, 0)),
                  pl.BlockSpec((CHUNK, A_HEADS), lambda i: (0, 0))],
        out_specs=pl.BlockSpec((ROW_TILE, A_WIDTH), lambda i: (i, 0)),
        out_shape=jax.ShapeDtypeStruct((t_rows, A_WIDTH), BF16),
        compiler_params=_params(("arbitrary",)),
        name="gmlp",
    )(proj, v_norm.reshape(1, A_WIDTH), w_s.astype(BF16), jnp.transpose(b_s))


def _rope_tables(n_lat, n_ctx, d_rot, reps):
    half = d_rot // 2
    nf = half // 2
    rows = n_lat // GRID_W
    tpos = jnp.arange(n_lat, dtype=I32)
    row = tpos // GRID_W - rows // 2
    col = tpos % GRID_W - GRID_W // 2
    freq = ROPE_THETA ** (-jnp.arange(nf, dtype=F32) / nf)
    ang_r = row.astype(F32)[:, None] * freq
    ang_c = col.astype(F32)[:, None] * freq
    cos = jnp.concatenate([jnp.cos(ang_r), jnp.cos(ang_r), jnp.cos(ang_c), jnp.cos(ang_c)], axis=1)
    sin = jnp.concatenate([-jnp.sin(ang_r), jnp.sin(ang_r), -jnp.sin(ang_c), jnp.sin(ang_c)], axis=1)
    cos = jnp.concatenate([cos, jnp.ones((n_ctx, d_rot), F32)], axis=0)
    sin = jnp.concatenate([sin, jnp.zeros((n_ctx, d_rot), F32)], axis=0)
    return jnp.tile(cos, (1, reps)), jnp.tile(sin, (1, reps))


def _rope(y, cos, sin, nf):
    lane = lax.broadcasted_iota(I32, y.shape, 1)
    partner = jnp.where((lane % (2 * nf)) < nf,
                        pltpu.roll(y, LANES - nf, 1), pltpu.roll(y, nf, 1))
    return y * cos + partner * sin


def _gqa_prep_kernel(p_ref, w_ref, cos_ref, sin_ref, o_ref):
    c = pl.program_id(1)
    heads = p_ref.shape[1] // HEAD_DIM

    @pl.when(c < 4)
    def _():
        for hh in range(heads):
            x = p_ref[:, hh * HEAD_DIM:(hh + 1) * HEAD_DIM].astype(F32)
            y = _rope(_rms(x, w_ref[...]), cos_ref[...], sin_ref[...], HEAD_DIM // 4)
            o_ref[hh] = y.astype(o_ref.dtype)

    @pl.when(c == 4)
    def _():
        for hh in range(heads):
            o_ref[hh] = p_ref[:, hh * HEAD_DIM:(hh + 1) * HEAD_DIM]


def _gqa_prep(proj, geom, q_norm, k_norm, cos, sin):
    batch, n_tok, b_fn, rb_fn = geom
    wvec = jnp.stack([q_norm * (HEAD_DIM ** -0.5 * LOG2E)] * 3 + [k_norm, jnp.ones_like(k_norm)])
    wvec = wvec.reshape(5, 1, HEAD_DIM)
    first = (2 * A_WIDTH) // 512
    return pl.pallas_call(
        _gqa_prep_kernel,
        grid=(proj.shape[0] // ROW_TILE, 5),
        in_specs=[pl.BlockSpec((ROW_TILE, 512), lambda i, c: (i, first + c)),
                  pl.BlockSpec((None, 1, HEAD_DIM), lambda i, c: (c, 0, 0)),
                  pl.BlockSpec((ROW_TILE, HEAD_DIM), lambda i, c: (rb_fn(i), 0)),
                  pl.BlockSpec((ROW_TILE, HEAD_DIM), lambda i, c: (rb_fn(i), 0))],
        out_specs=pl.BlockSpec((None, 4, ROW_TILE, HEAD_DIM), lambda i, c: (b_fn(i), c, rb_fn(i), 0)),
        out_shape=jax.ShapeDtypeStruct((batch, B_HEADS + 2 * B_KV_HEADS, n_tok, HEAD_DIM), BF16),
        compiler_params=_params(("arbitrary", "arbitrary")),
        name="gqa_prep",
    )(proj, wvec, cos, sin)


def _mla_prep_kernel(p_ref, qa_ref, kva_ref, wqn_ref, wqr_ref, wkk_ref, wkv_ref,
                     nqn_ref, nqr_ref, nkn_ref, nkr_ref, cos_ref, sin_ref,
                     qc_ref, kc_ref, vc_ref):
    cq = p_ref[:, :C_Q_RANK].astype(F32)
    ckv = p_ref[:, C_Q_RANK:C_Q_RANK + C_KV_RANK].astype(F32)
    kr = p_ref[:, C_Q_RANK + C_KV_RANK:C_Q_RANK + C_KV_RANK + LANES].astype(F32)
    cqn = _rms(cq, qa_ref[...]).astype(BF16)
    ckvn = _rms(ckv, kva_ref[...]).astype(BF16)
    qn = jnp.dot(cqn, wqn_ref[...], preferred_element_type=F32)
    qr = jnp.dot(cqn, wqr_ref[...], preferred_element_type=F32)
    kn = jnp.dot(ckvn, wkk_ref[...], preferred_element_type=F32)
    vv = jnp.dot(ckvn, wkv_ref[...], preferred_element_type=F32)
    cos = cos_ref[...]
    sin = sin_ref[...]
    lane = lax.broadcasted_iota(I32, kr.shape, 1)
    low = lane < C_ROPE
    inv_qk = 1.0 / C_QK

    kr_rot = _rope(kr * nkr_ref[...], cos, sin, C_ROPE // 4)
    kr_ss = jnp.sum(kr * kr, axis=-1, keepdims=True)
    for h in range(C_HEADS):
        hs = slice(h * C_NOPE, (h + 1) * C_NOPE)
        knh = kn[:, hs]
        r = lax.rsqrt((jnp.sum(knh * knh, axis=-1, keepdims=True) + kr_ss) * inv_qk + EPS)
        kc_ref[h, :, 0:C_NOPE] = (knh * r * nkn_ref[...]).astype(kc_ref.dtype)
        kc_ref[h, :, C_NOPE:C_QK] = (kr_rot * r)[:, 0:C_ROPE].astype(kc_ref.dtype)
        vc_ref[h] = vv[:, h * C_V:(h + 1) * C_V].astype(vc_ref.dtype)

    for pr in range(C_HEADS // 2):
        qrp = qr[:, pr * LANES:(pr + 1) * LANES]
        sq = qrp * qrp
        ss_pair = (jnp.sum(jnp.where(low, sq, 0.0), axis=-1, keepdims=True),
                   jnp.sum(jnp.where(low, 0.0, sq), axis=-1, keepdims=True))
        rs = []
        for e in range(2):
            h = 2 * pr + e
            qnh = qn[:, h * C_NOPE:(h + 1) * C_NOPE]
            r = lax.rsqrt((jnp.sum(qnh * qnh, axis=-1, keepdims=True) + ss_pair[e]) * inv_qk + EPS)
            rs.append(r)
            qc_ref[h, :, 0:C_NOPE] = (qnh * r * nqn_ref[...]).astype(qc_ref.dtype)
        q_rot = _rope(qrp * jnp.where(low, rs[0], rs[1]) * nqr_ref[...], cos, sin, C_ROPE // 4)
        qc_ref[2 * pr, :, C_NOPE:C_QK] = q_rot[:, 0:C_ROPE].astype(qc_ref.dtype)
        qc_ref[2 * pr + 1, :, C_NOPE:C_QK] = q_rot[:, C_ROPE:2 * C_ROPE].astype(qc_ref.dtype)


def _mla_prep(proj, geom, q_a_norm, w_uq, kv_a_norm, w_ukv, q_norm, k_norm, cos, sin):
    batch, n_tok, b_fn, rb_fn = geom
    first = (2 * A_WIDTH + B_WIDTH + 2 * B_KV_HEADS * HEAD_DIM)
    width = IN_COLS_PAD - first
    assert first % width == 0
    wq = w_uq.reshape(C_Q_RANK, C_HEADS, C_QK)
    wqn = wq[:, :, :C_NOPE].reshape(C_Q_RANK, C_HEADS * C_NOPE).astype(BF16)
    wqr = wq[:, :, C_NOPE:].reshape(C_Q_RANK, C_HEADS * C_ROPE).astype(BF16)
    wkv = w_ukv.reshape(C_KV_RANK, C_HEADS, C_NOPE + C_V)
    wkk = wkv[:, :, :C_NOPE].reshape(C_KV_RANK, C_HEADS * C_NOPE).astype(BF16)
    wkvv = wkv[:, :, C_NOPE:].reshape(C_KV_RANK, C_HEADS * C_V).astype(BF16)
    qs = C_QK ** -0.5 * LOG2E
    nqn = (q_norm[:C_NOPE] * qs).reshape(1, C_NOPE)
    nqr = jnp.tile(q_norm[C_NOPE:] * qs, 2).reshape(1, LANES)
    nkn = k_norm[:C_NOPE].reshape(1, C_NOPE)
    nkr = jnp.tile(k_norm[C_NOPE:], 2).reshape(1, LANES)
    const = lambda shape: pl.BlockSpec(shape, lambda i: tuple(0 for _ in shape))
    head_spec = lambda dd: pl.BlockSpec((None, C_HEADS, ROW_TILE, dd), lambda i: (b_fn(i), 0, rb_fn(i), 0))
    return pl.pallas_call(
        _mla_prep_kernel,
        grid=(proj.shape[0] // ROW_TILE,),
        in_specs=[pl.BlockSpec((ROW_TILE, width), lambda i: (i, first // width)),
                  const((1, C_Q_RANK)), const((1, C_KV_RANK)),
                  const(wqn.shape), const(wqr.shape), const(wkk.shape), const(wkvv.shape),
                  const((1, C_NOPE)), const((1, LANES)), const((1, C_NOPE)), const((1, LANES)),
                  pl.BlockSpec((ROW_TILE, LANES), lambda i: (rb_fn(i), 0)),
                  pl.BlockSpec((ROW_TILE, LANES), lambda i: (rb_fn(i), 0))],
        out_specs=[head_spec(C_QK), head_spec(C_QK), head_spec(C_V)],
        out_shape=[jax.ShapeDtypeStruct((batch, C_HEADS, n_tok, C_QK), BF16),
                   jax.ShapeDtypeStruct((batch, C_HEADS, n_tok, C_QK), BF16),
                   jax.ShapeDtypeStruct((batch, C_HEADS, n_tok, C_V), BF16)],
        compiler_params=_params(("arbitrary",)),
        name="mla_prep",
    )(proj, q_a_norm.reshape(1, C_Q_RANK), kv_a_norm.reshape(1, C_KV_RANK), wqn, wqr, wkk, wkvv,
      nqn, nqr, nkn, nkr, cos, sin)


def _attn_kernel(bound_ref, q_ref, k_ref, v_ref, *rest, group, n_lat, bounded):
    o_ref, m_sc, l_sc, acc_sc = rest[-4:]
    tq, dk = q_ref.shape[1], q_ref.shape[2]
    dv = v_ref.shape[1]
    n_tail = k_ref.shape[0] - n_lat
    q = q_ref[...].reshape(group * tq, dk)
    l_sc[...] = jnp.zeros(l_sc.shape, F32)
    acc_sc[...] = jnp.zeros(acc_sc.shape, F32)
    if not bounded:
        m_sc[...] = jnp.full(m_sc.shape, -jnp.inf, F32)

    def lane_fold(p):
        out = p[:, 0:LANES]
        for j in range(1, p.shape[1] // LANES):
            out = out + p[:, j * LANES:(j + 1) * LANES]
        return out

    def chunk(start, size):
        k = k_ref[pl.ds(start, size), :]
        v = v_ref[pl.ds(start, size), :]
        s = lax.dot_general(q, k, (((1,), (1,)), ((), ())), preferred_element_type=F32)
        if bounded:
            p = jnp.exp2(s - bound_ref[0])
            l_sc[...] += lane_fold(p)
            acc_sc[...] += jnp.dot(p.astype(BF16), v, preferred_element_type=F32)
        else:
            m_prev = m_sc[...]
            m_new = jnp.maximum(m_prev, jnp.max(s, axis=-1, keepdims=True))
            alpha = jnp.exp2(m_prev - m_new)
            p = jnp.exp2(s - m_new[:, 0:1])
            l_sc[...] = alpha * l_sc[...] + lane_fold(p)
            acc_sc[...] = alpha[:, 0:1] * acc_sc[...] + jnp.dot(p.astype(BF16), v, preferred_element_type=F32)
            m_sc[...] = m_new

    if n_lat:
        def body(c, carry):
            chunk(pl.multiple_of(c * ATT_TK, ATT_TK), ATT_TK)
            return carry
        lax.fori_loop(0, n_lat // ATT_TK, body, 0, unroll=2)
    chunk(n_lat, n_tail)
    o = acc_sc[...] / jnp.sum(l_sc[...], axis=-1, keepdims=True)
    for g in range(group):
        o_ref[:, g * dv:(g + 1) * dv] = o[g * tq:(g + 1) * tq].astype(o_ref.dtype)


def _attention_call(bound, q_arr, k_arr, v_arr, prev_out, heads, n_kv, group, tq, n_lat, n_ctx, ctx_only,
                    t_rows, bounded, name):
    q_head0, k_head0, v_head0 = heads
    batch, _, n_tok, dk = q_arr.shape
    dv = v_arr.shape[-1]
    m = group * tq
    if ctx_only:
        assert tq == n_ctx and n_lat % n_ctx == 0
        nq, q0, key_rows, key_blk = 1, n_lat // tq, n_ctx, n_lat // n_ctx
        out_row = lambda b, qi: batch * (n_lat // tq) + b
    else:
        assert n_lat % tq == 0 and n_lat % ATT_TK == 0
        nq, q0, key_rows, key_blk = n_lat // tq, 0, n_tok, 0
        out_row = lambda b, qi: b * nq + qi
    in_specs = [pl.BlockSpec((None, group, tq, dk), lambda b, g, qi, *_: (b, q_head0 // group + g, q0 + qi, 0)),
                pl.BlockSpec((None, None, key_rows, dk), lambda b, g, qi, *_: (b, k_head0 + g, key_blk, 0)),
                pl.BlockSpec((None, None, key_rows, dv), lambda b, g, qi, *_: (b, v_head0 + g, key_blk, 0))]
    args = [bound, q_arr, k_arr, v_arr]
    aliases = {}
    if prev_out is not None:
        in_specs.append(pl.BlockSpec(memory_space=pl.ANY))
        args.append(prev_out)
        aliases = {len(args) - 1: 0}
    grid_spec = pltpu.PrefetchScalarGridSpec(
        num_scalar_prefetch=1,
        grid=(batch, n_kv, nq),
        in_specs=in_specs,
        out_specs=pl.BlockSpec((tq, group * dv), lambda b, g, qi, *_: (out_row(b, qi), g)),
        scratch_shapes=[pltpu.VMEM((m, LANES), F32), pltpu.VMEM((m, LANES), F32), pltpu.VMEM((m, dv), F32)],
    )
    return pl.pallas_call(
        functools.partial(_attn_kernel, group=group, n_lat=0 if ctx_only else n_lat, bounded=bounded),
        grid_spec=grid_spec,
        out_shape=jax.ShapeDtypeStruct((t_rows, n_kv * group * dv), BF16),
        input_output_aliases=aliases,
        compiler_params=_params(("arbitrary", "arbitrary", "arbitrary")),
        name=name,
    )(*args)


def _attention(bound, q_arr, k_arr, v_arr, heads, n_kv, group, tq, n_lat, n_ctx, with_ctx, t_rows, name):
    tq = min(tq, n_lat)

    def run(bounded):
        def f(bound, q_arr, k_arr, v_arr):
            out = _attention_call(bound, q_arr, k_arr, v_arr, None, heads, n_kv, group, tq, n_lat, n_ctx,
                                  False, t_rows, bounded, name)
            if with_ctx:
                out = _attention_call(bound, q_arr, k_arr, v_arr, out, heads, n_kv, group, n_ctx, n_lat, n_ctx,
                                      True, t_rows, bounded, name + "_ctx")
            return out
        return f

    return lax.cond(bound[0] <= SCORE_BOUND_LIMIT, run(True), run(False), bound, q_arr, k_arr, v_arr)


def _score_bound(q_norm, k_norm, dk):
    b = dk ** 0.5 * jnp.max(jnp.abs(q_norm)) * jnp.max(jnp.abs(k_norm)) * (LOG2E * BOUND_SLACK)
    return b.reshape(1).astype(F32)


def _wout_kernel(a_ref, b_ref, c_ref, w_ref, x_ref, g_ref, o_ref):
    acc = jnp.dot(a_ref[...], w_ref[0:A_WIDTH, :], preferred_element_type=F32)
    acc += jnp.dot(b_ref[...], w_ref[A_WIDTH:A_WIDTH + B_WIDTH, :], preferred_element_type=F32)
    acc += jnp.dot(c_ref[...], w_ref[A_WIDTH + B_WIDTH:, :], preferred_element_type=F32)
    o_ref[...] = x_ref[...] + g_ref[...] * acc


def _out_proj(a_out, b_out, c_out, w, x, mods, layer, t_rows, mod_row):
    d = w.shape[1]
    return pl.pallas_call(
        _wout_kernel,
        grid=(d // MM_TN, t_rows // MM_TM),
        in_specs=[pl.BlockSpec((MM_TM, A_WIDTH), lambda j, i: (i, 0)),
                  pl.BlockSpec((MM_TM, B_WIDTH), lambda j, i: (i, 0)),
                  pl.BlockSpec((MM_TM, C_WIDTH), lambda j, i: (i, 0)),
                  pl.BlockSpec((w.shape[0], MM_TN), lambda j, i: (0, j)),
                  pl.BlockSpec((MM_TM, MM_TN), lambda j, i: (i, j)),
                  _mod_spec(layer, 2, d, lambda j, i: mod_row(i), MM_TN, lambda j, i: j)],
        out_specs=pl.BlockSpec((MM_TM, MM_TN), lambda j, i: (i, j)),
        out_shape=jax.ShapeDtypeStruct((t_rows, d), F32),
        compiler_params=_params(("arbitrary", "arbitrary")),
        name="out_proj",
    )(a_out, b_out, c_out, w, x, mods)


def _first_index(hit_val, x, iota, big):
    return jnp.min(jnp.where(x == hit_val, iota, big), axis=0, keepdims=True)


def _router_kernel(x_ref, nw_ref, shift_ref, scale_ref, rwt_ref, bias_ref,
                   hp_ref, eidx_ref, gate_ref, rank_ref, cnt_ref, carry_sc):
    i = pl.program_id(0)
    tm, d = x_ref.shape
    h = _rms(x_ref[...], nw_ref[...]) * (1.0 + scale_ref[...]) + shift_ref[...]
    hp_ref[...] = _pack_bf16_pair(h[:, :d // 2], h[:, d // 2:])

    logits = lax.dot_general(rwt_ref[...], h, (((1,), (1,)), ((), ())),
                             precision=lax.Precision.HIGHEST, preferred_element_type=F32)
    scores = jax.nn.sigmoid(logits)
    sel = scores + bias_ref[:, 0:1]
    neg = jnp.float32(-jnp.inf)

    iota_n = lax.broadcasted_iota(I32, (N_GROUPS, tm), 0).astype(F32)
    iota_e = lax.broadcasted_iota(I32, (N_EXPERTS, tm), 0).astype(F32)

    gscore = jnp.zeros((N_GROUPS, tm), F32)
    for g in range(N_GROUPS):
        blk = sel[g * GROUP_SIZE:(g + 1) * GROUP_SIZE]
        m1 = jnp.max(blk, axis=0, keepdims=True)
        first = _first_index(m1, blk, iota_n, float(GROUP_SIZE))
        m2 = jnp.max(jnp.where(iota_n == first, neg, blk), axis=0, keepdims=True)
        gscore = jnp.where(iota_n == float(g), m1 + m2, gscore)

    gsel = jnp.zeros((N_GROUPS, tm), F32)
    work = gscore
    for _ in range(TOPK_GROUPS):
        m = jnp.max(work, axis=0, keepdims=True)
        hit = iota_n == _first_index(m, work, iota_n, float(N_GROUPS))
        gsel = jnp.where(hit, 1.0, gsel)
        work = jnp.where(hit, neg, work)
    emask = jnp.concatenate(
        [jnp.broadcast_to(gsel[g:g + 1], (GROUP_SIZE, tm)) for g in range(N_GROUPS)], axis=0)

    work = jnp.where(emask > 0.5, sel, neg)
    picks = []
    onehot = jnp.zeros((N_EXPERTS, tm), F32)
    for _ in range(TOP_K):
        m = jnp.max(work, axis=0, keepdims=True)
        pick = _first_index(m, work, iota_e, float(N_EXPERTS))
        picks.append(pick)
        hit = iota_e == pick
        onehot = jnp.where(hit, 1.0, onehot)
        work = jnp.where(hit, neg, work)

    w_un = onehot * scores
    gates = w_un / jnp.sum(w_un, axis=0, keepdims=True) * ROUTED_SCALE

    @pl.when(i == 0)
    def _():
        carry_sc[...] = jnp.zeros(carry_sc.shape, F32)

    before = (lax.broadcasted_iota(I32, (tm, tm), 0) < lax.broadcasted_iota(I32, (tm, tm), 1)).astype(BF16)
    rank = carry_sc[:, 0:1] + jnp.dot(onehot.astype(BF16), before, preferred_element_type=F32)
    carry_sc[...] = carry_sc[...] + jnp.sum(onehot, axis=1, keepdims=True)
    cnt_ref[...] = carry_sc[...]

    eidx_ref[...] = jnp.zeros(eidx_ref.shape, I32)
    gate_ref[...] = jnp.zeros(gate_ref.shape, F32)
    rank_ref[...] = jnp.zeros(rank_ref.shape, I32)
    for k in range(TOP_K):
        hit = iota_e == picks[k]
        eidx_ref[k:k + 1, :] = picks[k].astype(I32)
        gate_ref[k:k + 1, :] = jnp.sum(jnp.where(hit, gates, 0.0), axis=0, keepdims=True)
        rank_ref[k:k + 1, :] = jnp.sum(jnp.where(hit, rank, 0.0), axis=0, keepdims=True).astype(I32)


def _norm_router(x, t_rows, norm_w, mods, layer, mod_row, router_w, router_bias):
    d = x.shape[1]
    col = lambda dt: (pl.BlockSpec((SUBLANES, ROW_TILE), lambda i: (0, i)),
                      jax.ShapeDtypeStruct((SUBLANES, t_rows), dt))
    specs, shapes = zip(col(I32), col(F32), col(I32))
    return pl.pallas_call(
        _router_kernel,
        grid=(t_rows // ROW_TILE,),
        in_specs=[pl.BlockSpec((ROW_TILE, d), lambda i: (i, 0)),
                  pl.BlockSpec((1, d), lambda i: (0, 0)),
                  _mod_spec(layer, 3, d, mod_row),
                  _mod_spec(layer, 4, d, mod_row),
                  pl.BlockSpec((N_EXPERTS, d), lambda i: (0, 0)),
                  pl.BlockSpec((N_EXPERTS, LANES), lambda i: (0, 0))],
        out_specs=[pl.BlockSpec((ROW_TILE, d // 2), lambda i: (i, 0)), *specs,
                   pl.BlockSpec((N_EXPERTS, LANES), lambda i: (0, 0))],
        out_shape=[jax.ShapeDtypeStruct((t_rows, d // 2), U32), *shapes,
                   jax.ShapeDtypeStruct((N_EXPERTS, LANES), F32)],
        scratch_shapes=[pltpu.VMEM((N_EXPERTS, LANES), F32)],
        compiler_params=_params(("arbitrary",)),
        name="norm_router",
    )(x, norm_w.reshape(1, d), mods, mods, jnp.transpose(router_w),
      jnp.broadcast_to(router_bias[:, None], (N_EXPERTS, LANES)))


def _row_copy(src, src_row, dst, dst_row, sem):
    return pltpu.make_async_copy(src.at[pl.ds(src_row, 1)], dst.at[pl.ds(dst_row, 1)], sem)


_FILL_PIECES = tuple(1 << b for b in range(EXP_TM.bit_length() - 2, -1, -1))


def _zero_fill(pstart_ref, plen_ref, tail_ref, xs_hbm, zero_sc, sem_z):
    zero_sc[...] = jnp.zeros(zero_sc.shape, zero_sc.dtype)
    n_tail = (xs_hbm.shape[0] - tail_ref[0]) // _FILL_PIECES[0]

    def pad_copies(e, wait):
        start = pstart_ref[e]
        head = (-start) & (SUBLANES - 1)
        for r in range(SUBLANES - 1):
            cp = pltpu.make_async_copy(zero_sc.at[pl.ds(0, 1)], xs_hbm.at[pl.ds(start + r, 1)], sem_z)

            @pl.when(r < head)
            def _():
                cp.wait() if wait else cp.start()

        off = start + head
        rem = plen_ref[e] - head
        for piece in _FILL_PIECES:
            if piece < SUBLANES:
                break
            on = (rem & piece) != 0
            cp = pltpu.make_async_copy(zero_sc.at[pl.ds(0, piece)],
                                       xs_hbm.at[pl.ds(pl.multiple_of(off, SUBLANES), piece)], sem_z)

            @pl.when(on)
            def _():
                cp.wait() if wait else cp.start()

            off = off + jnp.where(on, piece, 0)

    def tail_copy(j, wait):
        row = pl.multiple_of(tail_ref[0] + j * _FILL_PIECES[0], _FILL_PIECES[0])
        cp = pltpu.make_async_copy(zero_sc, xs_hbm.at[pl.ds(row, _FILL_PIECES[0])], sem_z)
        cp.wait() if wait else cp.start()

    for wait in (False, True):
        lax.fori_loop(0, N_EXPERTS, lambda e, c: (pad_copies(e, wait), c)[1], 0)
        lax.fori_loop(0, n_tail, lambda j, c: (tail_copy(j, wait), c)[1], 0)


def _dispatch_kernel(pstart_ref, plen_ref, tail_ref, slot_hbm, h_ref, xs_hbm, slot_sm, zero_sc, sem_s, sem, sem_z):
    i = pl.program_id(0)
    n = slot_sm.shape[0]
    cp = pltpu.make_async_copy(slot_hbm.at[pl.ds(pl.multiple_of(i * n, n), n)], slot_sm, sem_s)
    cp.start()

    @pl.when(i == 0)
    def _():
        _zero_fill(pstart_ref, plen_ref, tail_ref, xs_hbm, zero_sc, sem_z)

    cp.wait()

    def issue(t, carry):
        for k in range(TOP_K):
            _row_copy(h_ref, t, xs_hbm, slot_sm[t * SUBLANES + k], sem).start(priority=k % 2)
        return carry

    lax.fori_loop(0, DISP_T, issue, 0)

    def drain(t, carry):
        for k in range(TOP_K):
            _row_copy(h_ref, 0, xs_hbm, 0, sem).wait()
        return carry

    lax.fori_loop(0, DISP_T, drain, 0)


def _dispatch(hp, slots, pad_start, pad_len, tail_start, n_slots):
    t, w = hp.shape
    grid_spec = pltpu.PrefetchScalarGridSpec(
        num_scalar_prefetch=3,
        grid=(t // DISP_T,),
        in_specs=[pl.BlockSpec(memory_space=pl.ANY), pl.BlockSpec((DISP_T, w), lambda i, *_: (i, 0))],
        out_specs=pl.BlockSpec(memory_space=pl.ANY),
        scratch_shapes=[pltpu.SMEM((DISP_T * SUBLANES,), I32),
                        pltpu.VMEM((_FILL_PIECES[0], w), U32),
                        pltpu.SemaphoreType.DMA(()), pltpu.SemaphoreType.DMA(()), pltpu.SemaphoreType.DMA(())],
    )
    return pl.pallas_call(
        _dispatch_kernel,
        grid_spec=grid_spec,
        out_shape=jax.ShapeDtypeStruct((n_slots, w), U32),
        compiler_params=_params(("arbitrary",)),
        name="moe_dispatch",
    )(pad_start, pad_len, tail_start, slots, hp)


def _expert_kernel(te_ref, par_ref, nxt_ref, nu_ref, xs_ref, wg_hbm, wu_hbm, wd_hbm, ys_ref,
                   wg_f, wu_f, wd_f, wg_sc, wu_sc, wd_sc, sem, *, layer):
    i = pl.program_id(0)
    expert = te_ref[i]
    par = par_ref[i]

    def weight_copies(e, slot):
        return (pltpu.make_async_copy(wg_hbm.at[layer, e], wg_f.at[slot], sem.at[slot, 0]),
                pltpu.make_async_copy(wu_hbm.at[layer, e], wu_f.at[slot], sem.at[slot, 1]),
                pltpu.make_async_copy(wd_hbm.at[layer, e], wd_f.at[slot], sem.at[slot, 2]))

    @pl.when(i == 0)
    def _():
        for cp in weight_copies(expert, par):
            cp.start()

    @pl.when(jnp.logical_or(i == 0, expert != te_ref[jnp.maximum(i - 1, 0)]))
    def _():
        for cp in weight_copies(expert, par):
            cp.wait()

        @pl.when(nxt_ref[i] >= 0)
        def _():
            for cp in weight_copies(nxt_ref[i], 1 - par):
                cp.start()

        wg_sc[...] = wg_f[par].astype(BF16)
        wu_sc[...] = wu_f[par].astype(BF16)
        wd_sc[...] = wd_f[par].astype(BF16)

    @pl.when(i < nu_ref[0])
    def _():
        lo, hi = _unpack_bf16_pair(xs_ref[...])
        y = _ffn(lo.astype(BF16), hi.astype(BF16), wg_sc[...], wu_sc[...], wd_sc[...])
        half = y.shape[1] // 2
        ys_ref[...] = _pack_bf16_pair(y[:, :half], y[:, half:])

    @pl.when(i >= nu_ref[0])
    def _():
        ys_ref[...] = jnp.zeros(ys_ref.shape, ys_ref.dtype)


def _experts(xs, tile_expert, tile_slot, tile_next, n_used, w_gate, w_up, w_down, layer):
    _, _, d, f = w_gate.shape
    n_tiles = xs.shape[0] // EXP_TM
    hbm = pl.BlockSpec(memory_space=pl.ANY)
    tile = (EXP_TM, xs.shape[1])
    grid_spec = pltpu.PrefetchScalarGridSpec(
        num_scalar_prefetch=4,
        grid=(n_tiles,),
        in_specs=[pl.BlockSpec(tile, lambda i, te, par, nxt, nu: (jnp.minimum(i, nu[0] - 1), 0)),
                  hbm, hbm, hbm],
        out_specs=pl.BlockSpec(tile, lambda i, te, par, nxt, nu: (i, 0)),
        scratch_shapes=[pltpu.VMEM((2, d, f), F32), pltpu.VMEM((2, d, f), F32), pltpu.VMEM((2, f, d), F32),
                        pltpu.VMEM((d, f), BF16), pltpu.VMEM((d, f), BF16), pltpu.VMEM((f, d), BF16),
                        pltpu.SemaphoreType.DMA((2, 3))],
    )
    return pl.pallas_call(
        functools.partial(_expert_kernel, layer=layer),
        grid_spec=grid_spec,
        out_shape=jax.ShapeDtypeStruct(xs.shape, U32),
        compiler_params=_params(("arbitrary",)),
        name="moe_experts",
    )(tile_expert, tile_slot, tile_next, n_used, xs, w_gate, w_up, w_down)


def _combine_kernel(slot_hbm, ys_hbm, hp_ref, gate_ref, sg_ref, su_ref, sd_ref, x_ref, g5_ref,
                    o_ref, slot_sm, buf, sem_s, sem):
    i = pl.program_id(0)
    n_steps = pl.num_programs(0)
    n = COMB_T * SUBLANES
    half = x_ref.shape[1] // 2

    def load_slots(h, half_index):
        cp = pltpu.make_async_copy(slot_hbm.at[pl.ds(pl.multiple_of(half_index * n, n), n)],
                                   slot_sm.at[h], sem_s.at[h])
        cp.start()
        cp.wait()

    def issue(h, tokens):
        for t in tokens:
            for k in range(TOP_K):
                _row_copy(ys_hbm, slot_sm[h, t * SUBLANES + k], buf.at[h, k], t, sem.at[h]).start(priority=k % 2)

    def drain(h):
        def body(t, carry):
            for k in range(TOP_K):
                _row_copy(ys_hbm, 0, buf.at[h, k], 0, sem.at[h]).wait()
            return carry

        lax.fori_loop(0, COMB_T, body, 0)

    def finish(h, other):
        rows = slice(h * COMB_T, (h + 1) * COMB_T)
        issue(other, range(0, COMB_T // 2))
        lo, hi = _unpack_bf16_pair(hp_ref[rows, :])
        y = _ffn(lo.astype(BF16), hi.astype(BF16), sg_ref[...], su_ref[...], sd_ref[...])
        drain(h)
        issue(other, range(COMB_T // 2, COMB_T))
        gates = [gate_ref[rows, k:k + 1] for k in range(TOP_K)]
        for c in range(half // COMB_COLS):
            lo_cols = slice(c * COMB_COLS, (c + 1) * COMB_COLS)
            hi_cols = slice(half + c * COMB_COLS, half + (c + 1) * COMB_COLS)
            r_lo = y[:, lo_cols]
            r_hi = y[:, hi_cols]
            for k in range(TOP_K):
                e_lo, e_hi = _unpack_bf16_pair(buf[h, k, :, lo_cols])
                r_lo = r_lo + gates[k] * e_lo
                r_hi = r_hi + gates[k] * e_hi
            o_ref[rows, lo_cols] = x_ref[rows, lo_cols] + g5_ref[:, lo_cols] * r_lo
            o_ref[rows, hi_cols] = x_ref[rows, hi_cols] + g5_ref[:, hi_cols] * r_hi

    @pl.when(i == 0)
    def _():
        load_slots(0, 0)
        lax.fori_loop(0, COMB_T, lambda t, c: (issue(0, (t,)), c)[1], 0)

    load_slots(1, 2 * i + 1)
    finish(0, 1)
    load_slots(0, jnp.minimum(2 * i + 2, 2 * n_steps - 2))
    finish(1, 0)

    @pl.when(i == n_steps - 1)
    def _():
        drain(0)


def _combine(ys, hp, slots, gates, sh_gate, sh_up, sh_down, x, mods, layer, t_rows, mod_row):
    d = x.shape[1]
    w = hp.shape[1]
    f = sh_gate.shape[1]
    rows = 2 * COMB_T
    return pl.pallas_call(
        _combine_kernel,
        grid=(t_rows // rows,),
        in_specs=[pl.BlockSpec(memory_space=pl.ANY), pl.BlockSpec(memory_space=pl.ANY),
                  pl.BlockSpec((rows, w), lambda i: (i, 0)),
                  pl.BlockSpec((rows, SUBLANES), lambda i: (i, 0)),
                  pl.BlockSpec((d, f), lambda i: (0, 0)),
                  pl.BlockSpec((d, f), lambda i: (0, 0)),
                  pl.BlockSpec((f, d), lambda i: (0, 0)),
                  pl.BlockSpec((rows, d), lambda i: (i, 0)),
                  _mod_spec(layer, 5, d, lambda i: mod_row(i * rows // ROW_TILE))],
        out_specs=pl.BlockSpec((rows, d), lambda i: (i, 0)),
        out_shape=jax.ShapeDtypeStruct((t_rows, d), F32),
        scratch_shapes=[pltpu.SMEM((2, COMB_T * SUBLANES), I32),
                        pltpu.VMEM((2, TOP_K, COMB_T, w), U32),
                        pltpu.SemaphoreType.DMA((2,)), pltpu.SemaphoreType.DMA((2,))],
        compiler_params=_params(("arbitrary",)),
        name="moe_combine",
    )(slots, ys, hp, gates, sh_gate.astype(BF16), sh_up.astype(BF16), sh_down.astype(BF16), x, mods)


def _moe(x, t_rows, norm_w, mods, layer, mod_row, router_w, router_bias,
         w_gate, w_up, w_down, sh_gate, sh_up, sh_down):
    hp, eidx, gate, rank, cnt = _norm_router(x, t_rows, norm_w, mods, layer, mod_row, router_w, router_bias)

    n_tiles = (t_rows * TOP_K + N_EXPERTS * (EXP_TM - 1) + EXP_TM - 1) // EXP_TM
    counts = cnt[:, 0].astype(I32)
    padded = (counts + EXP_TM - 1) // EXP_TM * EXP_TM
    ends = jnp.cumsum(padded)
    base = ends - padded
    experts = jnp.arange(N_EXPERTS, dtype=I32)
    base_of = jnp.sum(jnp.where(eidx[:, :, None] == experts, base, 0), axis=-1)
    slots = jnp.transpose(base_of + rank).reshape(-1)
    n_used = (ends[-1] // EXP_TM).astype(I32)
    tile_start = jnp.arange(n_tiles, dtype=I32) * EXP_TM
    last_start = (n_used - 1) * EXP_TM
    te = jnp.sum(jnp.minimum(tile_start, last_start)[:, None] >= ends[None, :], axis=1).astype(I32)
    te = jnp.minimum(te, N_EXPERTS - 1)
    used = counts > 0
    slot_of = ((jnp.cumsum(used.astype(I32)) - 1) % 2).astype(I32)
    later = jnp.logical_and(experts[None, :] > experts[:, None], used[None, :])
    next_of = jnp.min(jnp.where(later, experts[None, :], N_EXPERTS), axis=1)
    next_of = jnp.where(next_of == N_EXPERTS, -1, next_of).astype(I32)

    xs = _dispatch(hp, slots, base + counts, padded - counts, ends[-1:], n_tiles * EXP_TM)
    ys = _experts(xs, te, slot_of[te], next_of[te], n_used.reshape(1), w_gate, w_up, w_down, layer)
    return _combine(ys, hp, slots, jnp.transpose(gate), sh_gate, sh_up, sh_down, x, mods, layer, t_rows, mod_row)


def kernel(x, c, ctx, c_ctx, ada_w, ada_b, norm1_w, norm2_w, w_in, gm_v_norm, gm_w_s, gm_b_s, gqa_q_norm, gqa_k_norm, mla_q_a_norm, mla_w_uq, mla_kv_a_norm, mla_w_ukv, mla_q_norm, mla_k_norm, w_out, router_w, router_bias, exp_w_gate, exp_w_up, exp_w_down, sh_w_gate, sh_w_up, sh_w_down):
    batch, n_lat, d = x.shape
    n_ctx = ctx.shape[1]
    depth = ada_w.shape[0]
    assert n_ctx == ROW_TILE and n_lat % MM_TM == 0 and (batch * n_ctx) % MM_TM == 0
    t_lat = batch * n_lat
    t_all = t_lat + batch * n_ctx
    n_tok = n_lat + n_ctx
    lat_tiles = n_lat // ROW_TILE

    mod_row = lambda i: jnp.minimum(i // lat_tiles, batch)
    is_lat = lambda i: i < batch * lat_tiles
    b_fn = lambda i: jnp.where(is_lat(i), i // lat_tiles, i - batch * lat_tiles)
    rb_fn = lambda i: jnp.where(is_lat(i), i % lat_tiles, lat_tiles)
    geom = (batch, n_tok, b_fn, rb_fn)
    mm_mod_row = lambda i: mod_row(i * (MM_TM // ROW_TILE))

    mods = _ada_mods(c, c_ctx, ada_w, ada_b)
    cos_b, sin_b = _rope_tables(n_lat, n_ctx, HEAD_DIM, 1)
    cos_c, sin_c = _rope_tables(n_lat, n_ctx, C_ROPE, LANES // C_ROPE)

    xs = jnp.concatenate([x.reshape(t_lat, d), ctx.reshape(batch * n_ctx, d)], axis=0)
    for l in range(depth):
        need_ctx = l < depth - 1
        t_rows = t_all if need_ctx else t_lat
        w_in_l = jnp.pad(w_in[l].astype(BF16), ((0, 0), (0, IN_COLS_PAD - w_in.shape[2])))

        proj = _in_proj(xs, norm1_w[l], mods, l, mm_mod_row, w_in_l)
        a_out = _gmlp(proj, t_rows, gm_v_norm[l], gm_w_s[l], gm_b_s[l])
        qkv = _gqa_prep(proj, geom, gqa_q_norm[l], gqa_k_norm[l], cos_b, sin_b)
        b_out = _attention(_score_bound(gqa_q_norm[l], gqa_k_norm[l], HEAD_DIM), qkv, qkv, qkv,
                           (0, B_HEADS, B_HEADS + B_KV_HEADS), B_KV_HEADS, B_GROUP, GQA_TQ,
                           n_lat, n_ctx, need_ctx, t_rows, "gqa_attn")
        qc, kc, vc = _mla_prep(proj, geom, mla_q_a_norm[l], mla_w_uq[l], mla_kv_a_norm[l], mla_w_ukv[l],
                               mla_q_norm[l], mla_k_norm[l], cos_c, sin_c)
        c_out = _attention(_score_bound(mla_q_norm[l], mla_k_norm[l], C_QK), qc, kc, vc,
                           (0, 0, 0), C_HEADS, 1, MLA_TQ, n_lat, n_ctx, need_ctx, t_rows, "mla_attn")
        xs = _out_proj(a_out, b_out, c_out, w_out[l].astype(BF16), xs, mods, l, t_rows, mm_mod_row)
        xs = _moe(xs, t_rows, norm2_w[l], mods, l, mod_row, router_w[l], router_bias[l],
                  exp_w_gate, exp_w_up, exp_w_down, sh_w_gate[l], sh_w_up[l], sh_w_down[l])
    return xs[:t_lat].reshape(batch, n_lat, d)
```

```python
import functools

import jax
import jax.numpy as jnp
from jax import lax
from jax.experimental import pallas as pl
from jax.experimental.pallas import tpu as pltpu

F32 = jnp.float32
BF16 = jnp.bfloat16
U32 = jnp.uint32
I32 = jnp.int32

GRID_W = 64
CHUNK = 128
HEAD_DIM = 128
EPS = 1e-6
ROPE_THETA = 10000.0
A_HEADS = 8
B_HEADS = 12
B_KV_HEADS = 4
B_GROUP = B_HEADS // B_KV_HEADS
C_HEADS = 12
C_Q_RANK = 768
C_KV_RANK = 512
C_NOPE = 128
C_ROPE = 64
C_V = 128
C_QK = C_NOPE + C_ROPE
A_WIDTH = A_HEADS * HEAD_DIM
B_WIDTH = B_HEADS * HEAD_DIM
C_WIDTH = C_HEADS * C_V
N_EXPERTS = 64
TOP_K = 6
N_GROUPS = 8
GROUP_SIZE = N_EXPERTS // N_GROUPS
TOPK_GROUPS = 4
ROUTED_SCALE = 2.5
N_MOD = 6

LANES = 128
SUBLANES = 8
ROW_TILE = 256
MM_TM = 512
MM_TN = 1024
GQA_TQ = 1024
MLA_TQ = 2048
ATT_TK = 512
LOG2E = 1.4426950408889634
BOUND_SLACK = 1.02
SCORE_BOUND_LIMIT = 60.0
EXP_TM = 256
DISP_T = 256
COMB_T = 128
COMB_COLS = 128
IN_COLS_PAD = 6144
VMEM_LIMIT = 56 * 1024 * 1024


def _params(sem, vmem=VMEM_LIMIT):
    return pltpu.CompilerParams(dimension_semantics=sem, vmem_limit_bytes=vmem)


def _rms(x, w):
    return x * lax.rsqrt(jnp.mean(x * x, axis=-1, keepdims=True) + EPS) * w


def _pack_bf16_pair(a, b):
    ua = lax.bitcast_convert_type(a.astype(BF16).astype(F32), U32)
    ub = lax.bitcast_convert_type(b.astype(BF16).astype(F32), U32)
    return (ua >> 16) | ub


def _unpack_bf16_pair(u):
    a = lax.bitcast_convert_type(u << 16, F32)
    b = lax.bitcast_convert_type(u & jnp.uint32(0xFFFF0000), F32)
    return a, b


def _ffn(lo, hi, wg, wu, wd):
    half = lo.shape[1]
    hg = jnp.dot(lo, wg[:half], preferred_element_type=F32) + jnp.dot(hi, wg[half:], preferred_element_type=F32)
    hu = jnp.dot(lo, wu[:half], preferred_element_type=F32) + jnp.dot(hi, wu[half:], preferred_element_type=F32)
    hb = (jax.nn.silu(hg) * hu).astype(BF16)
    return jnp.dot(hb, wd, preferred_element_type=F32)


def _ada_kernel(cb_ref, w_ref, b_ref, o_ref, silu_sc, *, n_rows, d, tn):
    nj = tn // LANES

    @pl.when(jnp.logical_and(pl.program_id(0) == 0, pl.program_id(1) == 0))
    def _():
        cv = cb_ref[...]
        silu_sc[...] = cv * jax.nn.sigmoid(cv)

    def body(kc, accs):
        k0 = pl.multiple_of(kc * SUBLANES, SUBLANES)
        wk = w_ref[0, pl.ds(k0, SUBLANES), :]
        out = []
        for r in range(n_rows):
            sv = silu_sc[r, pl.ds(k0, SUBLANES), :]
            for j in range(nj):
                out.append(accs[r * nj + j] + wk[:, j * LANES:(j + 1) * LANES] * sv)
        return tuple(out)

    init = tuple(jnp.zeros((SUBLANES, LANES), F32) for _ in range(n_rows * nj))
    accs = lax.fori_loop(0, d // SUBLANES, body, init, unroll=8)
    o_ref[0] = jnp.zeros((SUBLANES, tn), F32)
    for r in range(n_rows):
        for j in range(nj):
            o_ref[0, r:r + 1, j * LANES:(j + 1) * LANES] = (
                jnp.sum(accs[r * nj + j], axis=0, keepdims=True) + b_ref[0, :, j * LANES:(j + 1) * LANES])


def _ada_mods(c, c_ctx, ada_w, ada_b):
    n_layers, d, nd = ada_w.shape
    n_rows = c.shape[0] + 1
    assert n_rows <= SUBLANES
    cvec = jnp.concatenate([c, c_ctx[None]], axis=0)
    cb = jnp.broadcast_to(cvec[:, :, None], (n_rows, d, LANES))
    tn = 512
    out = pl.pallas_call(
        functools.partial(_ada_kernel, n_rows=n_rows, d=d, tn=tn),
        grid=(n_layers, nd // tn),
        in_specs=[pl.BlockSpec((n_rows, d, LANES), lambda l, j: (0, 0, 0)),
                  pl.BlockSpec((1, d, tn), lambda l, j: (l, 0, j)),
                  pl.BlockSpec((1, 1, tn), lambda l, j: (l, 0, j))],
        out_specs=pl.BlockSpec((1, SUBLANES, tn), lambda l, j: (l, 0, j)),
        out_shape=jax.ShapeDtypeStruct((n_layers, SUBLANES, nd), F32),
        scratch_shapes=[pltpu.VMEM((n_rows, d, LANES), F32)],
        compiler_params=_params(("arbitrary", "arbitrary")),
        name="ada_mods",
    )(cb, ada_w, ada_b.reshape(n_layers, 1, nd))
    return out.reshape(n_layers, SUBLANES, N_MOD, 1, d)


def _mod_spec(layer, m, d, row_fn, tn=None, col_fn=None):
    if tn is None:
        return pl.BlockSpec((None, None, None, 1, d), lambda *a: (layer, row_fn(*a), m, 0, 0))
    return pl.BlockSpec((None, None, None, 1, tn), lambda *a: (layer, row_fn(*a), m, 0, col_fn(*a)))


def _in_proj_kernel(x_ref, nw_ref, shift_ref, scale_ref, w_ref, o_ref, h_sc):
    @pl.when(pl.program_id(1) == 0)
    def _():
        y = _rms(x_ref[...], nw_ref[...])
        h_sc[...] = (y * (1.0 + scale_ref[...]) + shift_ref[...]).astype(h_sc.dtype)

    o_ref[...] = jnp.dot(h_sc[...], w_ref[...], preferred_element_type=F32).astype(o_ref.dtype)


def _in_proj(x, norm_w, mods, layer, mod_row, w):
    m, d = x.shape
    n = w.shape[1]
    return pl.pallas_call(
        _in_proj_kernel,
        grid=(m // MM_TM, n // MM_TN),
        in_specs=[pl.BlockSpec((MM_TM, d), lambda i, j: (i, 0)),
                  pl.BlockSpec((1, d), lambda i, j: (0, 0)),
                  _mod_spec(layer, 0, d, lambda i, j: mod_row(i)),
                  _mod_spec(layer, 1, d, lambda i, j: mod_row(i)),
                  pl.BlockSpec((d, MM_TN), lambda i, j: (0, j))],
        out_specs=pl.BlockSpec((MM_TM, MM_TN), lambda i, j: (i, j)),
        out_shape=jax.ShapeDtypeStruct((m, n), BF16),
        scratch_shapes=[pltpu.VMEM((MM_TM, d), BF16)],
        compiler_params=_params(("arbitrary", "arbitrary")),
        name="in_proj",
    )(x, norm_w.reshape(1, d), mods, mods, w)


def _gmlp_kernel(p_ref, vn_ref, ws_ref, bt_ref, o_ref):
    for h in range(A_HEADS):
        hs = slice(h * HEAD_DIM, (h + 1) * HEAD_DIM)
        u = jax.nn.gelu(p_ref[:, hs].astype(F32))
        v = jax.nn.gelu(p_ref[:, A_WIDTH + h * HEAD_DIM:A_WIDTH + (h + 1) * HEAD_DIM].astype(F32))
        vb = _rms(v, vn_ref[:, hs]).astype(BF16)
        for c in range(ROW_TILE // CHUNK):
            cs = slice(c * CHUNK, (c + 1) * CHUNK)
            s = jnp.dot(ws_ref[h], vb[cs], preferred_element_type=F32) + bt_ref[:, h:h + 1]
            o_ref[cs, hs] = (u[cs] * s).astype(o_ref.dtype)


def _gmlp(proj, t_rows, v_norm, w_s, b_s):
    return pl.pallas_call(
        _gmlp_kernel,
        grid=(t_rows // ROW_TILE,),
        in_specs=[pl.BlockSpec((ROW_TILE, 2 * A_WIDTH), lambda i: (i, 0)),
                  pl.BlockSpec((1, A_WIDTH), lambda i: (0, 0)),
                  pl.BlockSpec((A_HEADS, CHUNK, CHUNK), lambda i: (0, 0, 0)),
                  pl.BlockSpec((CHUNK, A_HEADS), lambda i: (0, 0))],
        out_specs=pl.BlockSpec((ROW_TILE, A_WIDTH), lambda i: (i, 0)),
        out_shape=jax.ShapeDtypeStruct((t_rows, A_WIDTH), BF16),
        compiler_params=_params(("arbitrary",)),
        name="gmlp",
    )(proj, v_norm.reshape(1, A_WIDTH), w_s.astype(BF16), jnp.transpose(b_s))


def _rope_tables(n_lat, n_ctx, d_rot, reps):
    half = d_rot // 2
    nf = half // 2
    rows = n_lat // GRID_W
    tpos = jnp.arange(n_lat, dtype=I32)
    row = tpos // GRID_W - rows // 2
    col = tpos % GRID_W - GRID_W // 2
    freq = ROPE_THETA ** (-jnp.arange(nf, dtype=F32) / nf)
    ang_r = row.astype(F32)[:, None] * freq
    ang_c = col.astype(F32)[:, None] * freq
    cos = jnp.concatenate([jnp.cos(ang_r), jnp.cos(ang_r), jnp.cos(ang_c), jnp.cos(ang_c)], axis=1)
    sin = jnp.concatenate([-jnp.sin(ang_r), jnp.sin(ang_r), -jnp.sin(ang_c), jnp.sin(ang_c)], axis=1)
    cos = jnp.concatenate([cos, jnp.ones((n_ctx, d_rot), F32)], axis=0)
    sin = jnp.concatenate([sin, jnp.zeros((n_ctx, d_rot), F32)], axis=0)
    return jnp.tile(cos, (1, reps)), jnp.tile(sin, (1, reps))


def _rope(y, cos, sin, nf):
    lane = lax.broadcasted_iota(I32, y.shape, 1)
    partner = jnp.where((lane % (2 * nf)) < nf,
                        pltpu.roll(y, LANES - nf, 1), pltpu.roll(y, nf, 1))
    return y * cos + partner * sin


def _gqa_prep_kernel(p_ref, w_ref, cos_ref, sin_ref, o_ref):
    c = pl.program_id(1)
    heads = p_ref.shape[1] // HEAD_DIM

    @pl.when(c < 4)
    def _():
        for hh in range(heads):
            x = p_ref[:, hh * HEAD_DIM:(hh + 1) * HEAD_DIM].astype(F32)
            y = _rope(_rms(x, w_ref[...]), cos_ref[...], sin_ref[...], HEAD_DIM // 4)
            o_ref[hh] = y.astype(o_ref.dtype)

    @pl.when(c == 4)
    def _():
        for hh in range(heads):
            o_ref[hh] = p_ref[:, hh * HEAD_DIM:(hh + 1) * HEAD_DIM]


def _gqa_prep(proj, geom, q_norm, k_norm, cos, sin):
    batch, n_tok, b_fn, rb_fn = geom
    wvec = jnp.stack([q_norm * (HEAD_DIM ** -0.5 * LOG2E)] * 3 + [k_norm, jnp.ones_like(k_norm)])
    wvec = wvec.reshape(5, 1, HEAD_DIM)
    first = (2 * A_WIDTH) // 512
    return pl.pallas_call(
        _gqa_prep_kernel,
        grid=(proj.shape[0] // ROW_TILE, 5),
        in_specs=[pl.BlockSpec((ROW_TILE, 512), lambda i, c: (i, first + c)),
                  pl.BlockSpec((None, 1, HEAD_DIM), lambda i, c: (c, 0, 0)),
                  pl.BlockSpec((ROW_TILE, HEAD_DIM), lambda i, c: (rb_fn(i), 0)),
                  pl.BlockSpec((ROW_TILE, HEAD_DIM), lambda i, c: (rb_fn(i), 0))],
        out_specs=pl.BlockSpec((None, 4, ROW_TILE, HEAD_DIM), lambda i, c: (b_fn(i), c, rb_fn(i), 0)),
        out_shape=jax.ShapeDtypeStruct((batch, B_HEADS + 2 * B_KV_HEADS, n_tok, HEAD_DIM), BF16),
        compiler_params=_params(("arbitrary", "arbitrary")),
        name="gqa_prep",
    )(proj, wvec, cos, sin)


def _mla_prep_kernel(p_ref, qa_ref, kva_ref, wqn_ref, wqr_ref, wkk_ref, wkv_ref,
                     nqn_ref, nqr_ref, nkn_ref, nkr_ref, cos_ref, sin_ref,
                     qc_ref, kc_ref, vc_ref):
    cq = p_ref[:, :C_Q_RANK].astype(F32)
    ckv = p_ref[:, C_Q_RANK:C_Q_RANK + C_KV_RANK].astype(F32)
    kr = p_ref[:, C_Q_RANK + C_KV_RANK:C_Q_RANK + C_KV_RANK + LANES].astype(F32)
    cqn = _rms(cq, qa_ref[...]).astype(BF16)
    ckvn = _rms(ckv, kva_ref[...]).astype(BF16)
    qn = jnp.dot(cqn, wqn_ref[...], preferred_element_type=F32)
    qr = jnp.dot(cqn, wqr_ref[...], preferred_element_type=F32)
    kn = jnp.dot(ckvn, wkk_ref[...], preferred_element_type=F32)
    vv = jnp.dot(ckvn, wkv_ref[...], preferred_element_type=F32)
    cos = cos_ref[...]
    sin = sin_ref[...]
    lane = lax.broadcasted_iota(I32, kr.shape, 1)
    low = lane < C_ROPE
    inv_qk = 1.0 / C_QK

    kr_rot = _rope(kr * nkr_ref[...], cos, sin, C_ROPE // 4)
    kr_ss = jnp.sum(kr * kr, axis=-1, keepdims=True)
    for h in range(C_HEADS):
        hs = slice(h * C_NOPE, (h + 1) * C_NOPE)
        knh = kn[:, hs]
        r = lax.rsqrt((jnp.sum(knh * knh, axis=-1, keepdims=True) + kr_ss) * inv_qk + EPS)
        kc_ref[h, :, 0:C_NOPE] = (knh * r * nkn_ref[...]).astype(kc_ref.dtype)
        kc_ref[h, :, C_NOPE:C_QK] = (kr_rot * r)[:, 0:C_ROPE].astype(kc_ref.dtype)
        vc_ref[h] = vv[:, h * C_V:(h + 1) * C_V].astype(vc_ref.dtype)

    for pr in range(C_HEADS // 2):
        qrp = qr[:, pr * LANES:(pr + 1) * LANES]
        sq = qrp * qrp
        ss_pair = (jnp.sum(jnp.where(low, sq, 0.0), axis=-1, keepdims=True),
                   jnp.sum(jnp.where(low, 0.0, sq), axis=-1, keepdims=True))
        rs = []
        for e in range(2):
            h = 2 * pr + e
            qnh = qn[:, h * C_NOPE:(h + 1) * C_NOPE]
            r = lax.rsqrt((jnp.sum(qnh * qnh, axis=-1, keepdims=True) + ss_pair[e]) * inv_qk + EPS)
            rs.append(r)
            qc_ref[h, :, 0:C_NOPE] = (qnh * r * nqn_ref[...]).astype(qc_ref.dtype)
        q_rot = _rope(qrp * jnp.where(low, rs[0], rs[1]) * nqr_ref[...], cos, sin, C_ROPE // 4)
        qc_ref[2 * pr, :, C_NOPE:C_QK] = q_rot[:, 0:C_ROPE].astype(qc_ref.dtype)
        qc_ref[2 * pr + 1, :, C_NOPE:C_QK] = q_rot[:, C_ROPE:2 * C_ROPE].astype(qc_ref.dtype)


def _mla_prep(proj, geom, q_a_norm, w_uq, kv_a_norm, w_ukv, q_norm, k_norm, cos, sin):
    batch, n_tok, b_fn, rb_fn = geom
    first = (2 * A_WIDTH + B_WIDTH + 2 * B_KV_HEADS * HEAD_DIM)
    width = IN_COLS_PAD - first
    assert first % width == 0
    wq = w_uq.reshape(C_Q_RANK, C_HEADS, C_QK)
    wqn = wq[:, :, :C_NOPE].reshape(C_Q_RANK, C_HEADS * C_NOPE).astype(BF16)
    wqr = wq[:, :, C_NOPE:].reshape(C_Q_RANK, C_HEADS * C_ROPE).astype(BF16)
    wkv = w_ukv.reshape(C_KV_RANK, C_HEADS, C_NOPE + C_V)
    wkk = wkv[:, :, :C_NOPE].reshape(C_KV_RANK, C_HEADS * C_NOPE).astype(BF16)
    wkvv = wkv[:, :, C_NOPE:].reshape(C_KV_RANK, C_HEADS * C_V).astype(BF16)
    qs = C_QK ** -0.5 * LOG2E
    nqn = (q_norm[:C_NOPE] * qs).reshape(1, C_NOPE)
    nqr = jnp.tile(q_norm[C_NOPE:] * qs, 2).reshape(1, LANES)
    nkn = k_norm[:C_NOPE].reshape(1, C_NOPE)
    nkr = jnp.tile(k_norm[C_NOPE:], 2).reshape(1, LANES)
    const = lambda shape: pl.BlockSpec(shape, lambda i: tuple(0 for _ in shape))
    head_spec = lambda dd: pl.BlockSpec((None, C_HEADS, ROW_TILE, dd), lambda i: (b_fn(i), 0, rb_fn(i), 0))
    return pl.pallas_call(
        _mla_prep_kernel,
        grid=(proj.shape[0] // ROW_TILE,),
        in_specs=[pl.BlockSpec((ROW_TILE, width), lambda i: (i, first // width)),
                  const((1, C_Q_RANK)), const((1, C_KV_RANK)),
                  const(wqn.shape), const(wqr.shape), const(wkk.shape), const(wkvv.shape),
                  const((1, C_NOPE)), const((1, LANES)), const((1, C_NOPE)), const((1, LANES)),
                  pl.BlockSpec((ROW_TILE, LANES), lambda i: (rb_fn(i), 0)),
                  pl.BlockSpec((ROW_TILE, LANES), lambda i: (rb_fn(i), 0))],
        out_specs=[head_spec(C_QK), head_spec(C_QK), head_spec(C_V)],
        out_shape=[jax.ShapeDtypeStruct((batch, C_HEADS, n_tok, C_QK), BF16),
                   jax.ShapeDtypeStruct((batch, C_HEADS, n_tok, C_QK), BF16),
                   jax.ShapeDtypeStruct((batch, C_HEADS, n_tok, C_V), BF16)],
        compiler_params=_params(("arbitrary",)),
        name="mla_prep",
    )(proj, q_a_norm.reshape(1, C_Q_RANK), kv_a_norm.reshape(1, C_KV_RANK), wqn, wqr, wkk, wkvv,
      nqn, nqr, nkn, nkr, cos, sin)


def _attn_kernel(bound_ref, q_ref, k_ref, v_ref, *rest, group, n_lat):
    use_bound = bound_ref[0] <= SCORE_BOUND_LIMIT
    for bounded, pred in ((True, use_bound), (False, jnp.logical_not(use_bound))):
        pl.when(pred)(functools.partial(_attn_body, bound_ref, q_ref, k_ref, v_ref, rest,
                                        group=group, n_lat=n_lat, bounded=bounded))


def _attn_body(bound_ref, q_ref, k_ref, v_ref, rest, *, group, n_lat, bounded):
    o_ref, m_sc, l_sc, acc_sc = rest[-4:]
    tq, dk = q_ref.shape[1], q_ref.shape[2]
    dv = v_ref.shape[1]
    n_tail = k_ref.shape[0] - n_lat
    q = q_ref[...].reshape(group * tq, dk)
    l_sc[...] = jnp.zeros(l_sc.shape, F32)
    acc_sc[...] = jnp.zeros(acc_sc.shape, F32)
    if not bounded:
        m_sc[...] = jnp.full(m_sc.shape, -jnp.inf, F32)

    def lane_fold(p):
        out = p[:, 0:LANES]
        for j in range(1, p.shape[1] // LANES):
            out = out + p[:, j * LANES:(j + 1) * LANES]
        return out

    def chunk(start, size):
        k = k_ref[pl.ds(start, size), :]
        v = v_ref[pl.ds(start, size), :]
        s = lax.dot_general(q, k, (((1,), (1,)), ((), ())), preferred_element_type=F32)
        if bounded:
            p = jnp.exp2(s - bound_ref[0])
            l_sc[...] += lane_fold(p)
            acc_sc[...] += jnp.dot(p.astype(BF16), v, preferred_element_type=F32)
        else:
            m_prev = m_sc[...]
            m_new = jnp.maximum(m_prev, jnp.max(s, axis=-1, keepdims=True))
            alpha = jnp.exp2(m_prev - m_new)
            p = jnp.exp2(s - m_new[:, 0:1])
            l_sc[...] = alpha * l_sc[...] + lane_fold(p)
            acc_sc[...] = alpha[:, 0:1] * acc_sc[...] + jnp.dot(p.astype(BF16), v, preferred_element_type=F32)
            m_sc[...] = m_new

    if n_lat:
        def body(c, carry):
            chunk(pl.multiple_of(c * ATT_TK, ATT_TK), ATT_TK)
            return carry
        lax.fori_loop(0, n_lat // ATT_TK, body, 0, unroll=2)
    chunk(n_lat, n_tail)
    o = acc_sc[...] / jnp.sum(l_sc[...], axis=-1, keepdims=True)
    for g in range(group):
        o_ref[:, g * dv:(g + 1) * dv] = o[g * tq:(g + 1) * tq].astype(o_ref.dtype)


def _attention_call(bound, q_arr, k_arr, v_arr, prev_out, heads, n_kv, group, tq, n_lat, n_ctx, ctx_only,
                    t_rows, name):
    q_head0, k_head0, v_head0 = heads
    batch, _, n_tok, dk = q_arr.shape
    dv = v_arr.shape[-1]
    m = group * tq
    if ctx_only:
        assert tq == n_ctx and n_lat % n_ctx == 0
        nq, q0, key_rows, key_blk = 1, n_lat // tq, n_ctx, n_lat // n_ctx
        out_row = lambda b, qi: batch * (n_lat // tq) + b
    else:
        assert n_lat % tq == 0 and n_lat % ATT_TK == 0
        nq, q0, key_rows, key_blk = n_lat // tq, 0, n_tok, 0
        out_row = lambda b, qi: b * nq + qi
    in_specs = [pl.BlockSpec((None, group, tq, dk), lambda b, g, qi, *_: (b, q_head0 // group + g, q0 + qi, 0)),
                pl.BlockSpec((None, None, key_rows, dk), lambda b, g, qi, *_: (b, k_head0 + g, key_blk, 0)),
                pl.BlockSpec((None, None, key_rows, dv), lambda b, g, qi, *_: (b, v_head0 + g, key_blk, 0))]
    args = [bound, q_arr, k_arr, v_arr]
    aliases = {}
    if prev_out is not None:
        in_specs.append(pl.BlockSpec(memory_space=pl.ANY))
        args.append(prev_out)
        aliases = {len(args) - 1: 0}
    grid_spec = pltpu.PrefetchScalarGridSpec(
        num_scalar_prefetch=1,
        grid=(batch, n_kv, nq),
        in_specs=in_specs,
        out_specs=pl.BlockSpec((tq, group * dv), lambda b, g, qi, *_: (out_row(b, qi), g)),
        scratch_shapes=[pltpu.VMEM((m, LANES), F32), pltpu.VMEM((m, LANES), F32), pltpu.VMEM((m, dv), F32)],
    )
    return pl.pallas_call(
        functools.partial(_attn_kernel, group=group, n_lat=0 if ctx_only else n_lat),
        grid_spec=grid_spec,
        out_shape=jax.ShapeDtypeStruct((t_rows, n_kv * group * dv), BF16),
        input_output_aliases=aliases,
        compiler_params=_params(("arbitrary", "arbitrary", "arbitrary")),
        name=name,
    )(*args)


def _attention(bound, q_arr, k_arr, v_arr, heads, n_kv, group, tq, n_lat, n_ctx, with_ctx, t_rows, name):
    tq = min(tq, n_lat)
    out = _attention_call(bound, q_arr, k_arr, v_arr, None, heads, n_kv, group, tq, n_lat, n_ctx,
                          False, t_rows, name)
    if with_ctx:
        out = _attention_call(bound, q_arr, k_arr, v_arr, out, heads, n_kv, group, n_ctx, n_lat, n_ctx,
                              True, t_rows, name + "_ctx")
    return out


def _score_bound(q_norm, k_norm, dk):
    b = dk ** 0.5 * jnp.max(jnp.abs(q_norm)) * jnp.max(jnp.abs(k_norm)) * (LOG2E * BOUND_SLACK)
    return b.reshape(1).astype(F32)


def _wout_kernel(a_ref, b_ref, c_ref, w_ref, x_ref, g_ref, o_ref):
    acc = jnp.dot(a_ref[...], w_ref[0:A_WIDTH, :], preferred_element_type=F32)
    acc += jnp.dot(b_ref[...], w_ref[A_WIDTH:A_WIDTH + B_WIDTH, :], preferred_element_type=F32)
    acc += jnp.dot(c_ref[...], w_ref[A_WIDTH + B_WIDTH:, :], preferred_element_type=F32)
    o_ref[...] = x_ref[...] + g_ref[...] * acc


def _out_proj(a_out, b_out, c_out, w, x, mods, layer, t_rows, mod_row):
    d = w.shape[1]
    return pl.pallas_call(
        _wout_kernel,
        grid=(d // MM_TN, t_rows // MM_TM),
        in_specs=[pl.BlockSpec((MM_TM, A_WIDTH), lambda j, i: (i, 0)),
                  pl.BlockSpec((MM_TM, B_WIDTH), lambda j, i: (i, 0)),
                  pl.BlockSpec((MM_TM, C_WIDTH), lambda j, i: (i, 0)),
                  pl.BlockSpec((w.shape[0], MM_TN), lambda j, i: (0, j)),
                  pl.BlockSpec((MM_TM, MM_TN), lambda j, i: (i, j)),
                  _mod_spec(layer, 2, d, lambda j, i: mod_row(i), MM_TN, lambda j, i: j)],
        out_specs=pl.BlockSpec((MM_TM, MM_TN), lambda j, i: (i, j)),
        out_shape=jax.ShapeDtypeStruct((t_rows, d), F32),
        compiler_params=_params(("arbitrary", "arbitrary")),
        name="out_proj",
    )(a_out, b_out, c_out, w, x, mods)


def _first_index(hit_val, x, iota, big):
    return jnp.min(jnp.where(x == hit_val, iota, big), axis=0, keepdims=True)


def _router_kernel(x_ref, nw_ref, shift_ref, scale_ref, rwt_ref, bias_ref,
                   hp_ref, eidx_ref, gate_ref, rank_ref, cnt_ref, carry_sc):
    i = pl.program_id(0)
    tm, d = x_ref.shape
    h = _rms(x_ref[...], nw_ref[...]) * (1.0 + scale_ref[...]) + shift_ref[...]
    hp_ref[...] = _pack_bf16_pair(h[:, :d // 2], h[:, d // 2:])

    logits = lax.dot_general(rwt_ref[...], h, (((1,), (1,)), ((), ())),
                             precision=lax.Precision.HIGHEST, preferred_element_type=F32)
    scores = jax.nn.sigmoid(logits)
    sel = scores + bias_ref[:, 0:1]
    neg = jnp.float32(-jnp.inf)

    iota_n = lax.broadcasted_iota(I32, (N_GROUPS, tm), 0).astype(F32)
    iota_e = lax.broadcasted_iota(I32, (N_EXPERTS, tm), 0).astype(F32)

    gscore = jnp.zeros((N_GROUPS, tm), F32)
    for g in range(N_GROUPS):
        blk = sel[g * GROUP_SIZE:(g + 1) * GROUP_SIZE]
        m1 = jnp.max(blk, axis=0, keepdims=True)
        first = _first_index(m1, blk, iota_n, float(GROUP_SIZE))
        m2 = jnp.max(jnp.where(iota_n == first, neg, blk), axis=0, keepdims=True)
        gscore = jnp.where(iota_n == float(g), m1 + m2, gscore)

    gsel = jnp.zeros((N_GROUPS, tm), F32)
    work = gscore
    for _ in range(TOPK_GROUPS):
        m = jnp.max(work, axis=0, keepdims=True)
        hit = iota_n == _first_index(m, work, iota_n, float(N_GROUPS))
        gsel = jnp.where(hit, 1.0, gsel)
        work = jnp.where(hit, neg, work)
    emask = jnp.concatenate(
        [jnp.broadcast_to(gsel[g:g + 1], (GROUP_SIZE, tm)) for g in range(N_GROUPS)], axis=0)

    work = jnp.where(emask > 0.5, sel, neg)
    picks = []
    onehot = jnp.zeros((N_EXPERTS, tm), F32)
    for _ in range(TOP_K):
        m = jnp.max(work, axis=0, keepdims=True)
        pick = _first_index(m, work, iota_e, float(N_EXPERTS))
        picks.append(pick)
        hit = iota_e == pick
        onehot = jnp.where(hit, 1.0, onehot)
        work = jnp.where(hit, neg, work)

    w_un = onehot * scores
    gates = w_un / jnp.sum(w_un, axis=0, keepdims=True) * ROUTED_SCALE

    @pl.when(i == 0)
    def _():
        carry_sc[...] = jnp.zeros(carry_sc.shape, F32)

    before = (lax.broadcasted_iota(I32, (tm, tm), 0) < lax.broadcasted_iota(I32, (tm, tm), 1)).astype(BF16)
    rank = carry_sc[:, 0:1] + jnp.dot(onehot.astype(BF16), before, preferred_element_type=F32)
    carry_sc[...] = carry_sc[...] + jnp.sum(onehot, axis=1, keepdims=True)
    cnt_ref[...] = carry_sc[...]

    eidx_ref[...] = jnp.zeros(eidx_ref.shape, I32)
    gate_ref[...] = jnp.zeros(gate_ref.shape, F32)
    rank_ref[...] = jnp.zeros(rank_ref.shape, I32)
    for k in range(TOP_K):
        hit = iota_e == picks[k]
        eidx_ref[k:k + 1, :] = picks[k].astype(I32)
        gate_ref[k:k + 1, :] = jnp.sum(jnp.where(hit, gates, 0.0), axis=0, keepdims=True)
        rank_ref[k:k + 1, :] = jnp.sum(jnp.where(hit, rank, 0.0), axis=0, keepdims=True).astype(I32)


def _norm_router(x, t_rows, norm_w, mods, layer, mod_row, router_w, router_bias):
    d = x.shape[1]
    col = lambda dt: (pl.BlockSpec((SUBLANES, ROW_TILE), lambda i: (0, i)),
                      jax.ShapeDtypeStruct((SUBLANES, t_rows), dt))
    specs, shapes = zip(col(I32), col(F32), col(I32))
    return pl.pallas_call(
        _router_kernel,
        grid=(t_rows // ROW_TILE,),
        in_specs=[pl.BlockSpec((ROW_TILE, d), lambda i: (i, 0)),
                  pl.BlockSpec((1, d), lambda i: (0, 0)),
                  _mod_spec(layer, 3, d, mod_row),
                  _mod_spec(layer, 4, d, mod_row),
                  pl.BlockSpec((N_EXPERTS, d), lambda i: (0, 0)),
                  pl.BlockSpec((N_EXPERTS, LANES), lambda i: (0, 0))],
        out_specs=[pl.BlockSpec((ROW_TILE, d // 2), lambda i: (i, 0)), *specs,
                   pl.BlockSpec((N_EXPERTS, LANES), lambda i: (0, 0))],
        out_shape=[jax.ShapeDtypeStruct((t_rows, d // 2), U32), *shapes,
                   jax.ShapeDtypeStruct((N_EXPERTS, LANES), F32)],
        scratch_shapes=[pltpu.VMEM((N_EXPERTS, LANES), F32)],
        compiler_params=_params(("arbitrary",)),
        name="norm_router",
    )(x, norm_w.reshape(1, d), mods, mods, jnp.transpose(router_w),
      jnp.broadcast_to(router_bias[:, None], (N_EXPERTS, LANES)))


def _row_copy(src, src_row, dst, dst_row, sem):
    return pltpu.make_async_copy(src.at[pl.ds(src_row, 1)], dst.at[pl.ds(dst_row, 1)], sem)


_FILL_PIECES = tuple(1 << b for b in range(EXP_TM.bit_length() - 2, -1, -1))


def _zero_fill(pstart_ref, plen_ref, tail_ref, xs_hbm, zero_sc, sem_z):
    zero_sc[...] = jnp.zeros(zero_sc.shape, zero_sc.dtype)
    n_tail = (xs_hbm.shape[0] - tail_ref[0]) // _FILL_PIECES[0]

    def pad_copies(e, wait):
        start = pstart_ref[e]
        head = (-start) & (SUBLANES - 1)
        for r in range(SUBLANES - 1):
            cp = pltpu.make_async_copy(zero_sc.at[pl.ds(0, 1)], xs_hbm.at[pl.ds(start + r, 1)], sem_z)

            @pl.when(r < head)
            def _():
                cp.wait() if wait else cp.start()

        off = start + head
        rem = plen_ref[e] - head
        for piece in _FILL_PIECES:
            if piece < SUBLANES:
                break
            on = (rem & piece) != 0
            cp = pltpu.make_async_copy(zero_sc.at[pl.ds(0, piece)],
                                       xs_hbm.at[pl.ds(pl.multiple_of(off, SUBLANES), piece)], sem_z)

            @pl.when(on)
            def _():
                cp.wait() if wait else cp.start()

            off = off + jnp.where(on, piece, 0)

    def tail_copy(j, wait):
        row = pl.multiple_of(tail_ref[0] + j * _FILL_PIECES[0], _FILL_PIECES[0])
        cp = pltpu.make_async_copy(zero_sc, xs_hbm.at[pl.ds(row, _FILL_PIECES[0])], sem_z)
        cp.wait() if wait else cp.start()

    for wait in (False, True):
        lax.fori_loop(0, N_EXPERTS, lambda e, c: (pad_copies(e, wait), c)[1], 0)
        lax.fori_loop(0, n_tail, lambda j, c: (tail_copy(j, wait), c)[1], 0)


def _dispatch_kernel(pstart_ref, plen_ref, tail_ref, slot_hbm, h_ref, xs_hbm, slot_sm, zero_sc, sem_s, sem, sem_z):
    i = pl.program_id(0)
    n = slot_sm.shape[0]
    cp = pltpu.make_async_copy(slot_hbm.at[pl.ds(pl.multiple_of(i * n, n), n)], slot_sm, sem_s)
    cp.start()

    @pl.when(i == 0)
    def _():
        _zero_fill(pstart_ref, plen_ref, tail_ref, xs_hbm, zero_sc, sem_z)

    cp.wait()

    def issue(t, carry):
        for k in range(TOP_K):
            _row_copy(h_ref, t, xs_hbm, slot_sm[t * SUBLANES + k], sem).start(priority=k % 2)
        return carry

    lax.fori_loop(0, DISP_T, issue, 0)

    def drain(t, carry):
        for k in range(TOP_K):
            _row_copy(h_ref, 0, xs_hbm, 0, sem).wait()
        return carry

    lax.fori_loop(0, DISP_T, drain, 0)


def _dispatch(hp, slots, pad_start, pad_len, tail_start, n_slots):
    t, w = hp.shape
    grid_spec = pltpu.PrefetchScalarGridSpec(
        num_scalar_prefetch=3,
        grid=(t // DISP_T,),
        in_specs=[pl.BlockSpec(memory_space=pl.ANY), pl.BlockSpec((DISP_T, w), lambda i, *_: (i, 0))],
        out_specs=pl.BlockSpec(memory_space=pl.ANY),
        scratch_shapes=[pltpu.SMEM((DISP_T * SUBLANES,), I32),
                        pltpu.VMEM((_FILL_PIECES[0], w), U32),
                        pltpu.SemaphoreType.DMA(()), pltpu.SemaphoreType.DMA(()), pltpu.SemaphoreType.DMA(())],
    )
    return pl.pallas_call(
        _dispatch_kernel,
        grid_spec=grid_spec,
        out_shape=jax.ShapeDtypeStruct((n_slots, w), U32),
        compiler_params=_params(("arbitrary",)),
        name="moe_dispatch",
    )(pad_start, pad_len, tail_start, slots, hp)


def _expert_kernel(te_ref, par_ref, nxt_ref, nu_ref, xs_ref, wg_hbm, wu_hbm, wd_hbm, ys_ref,
                   wg_f, wu_f, wd_f, wg_sc, wu_sc, wd_sc, sem, *, layer):
    i = pl.program_id(0)
    expert = te_ref[i]
    par = par_ref[i]

    def weight_copies(e, slot):
        return (pltpu.make_async_copy(wg_hbm.at[layer, e], wg_f.at[slot], sem.at[slot, 0]),
                pltpu.make_async_copy(wu_hbm.at[layer, e], wu_f.at[slot], sem.at[slot, 1]),
                pltpu.make_async_copy(wd_hbm.at[layer, e], wd_f.at[slot], sem.at[slot, 2]))

    @pl.when(i == 0)
    def _():
        for cp in weight_copies(expert, par):
            cp.start()

    @pl.when(jnp.logical_or(i == 0, expert != te_ref[jnp.maximum(i - 1, 0)]))
    def _():
        for cp in weight_copies(expert, par):
            cp.wait()

        @pl.when(nxt_ref[i] >= 0)
        def _():
            for cp in weight_copies(nxt_ref[i], 1 - par):
                cp.start()

        wg_sc[...] = wg_f[par].astype(BF16)
        wu_sc[...] = wu_f[par].astype(BF16)
        wd_sc[...] = wd_f[par].astype(BF16)

    @pl.when(i < nu_ref[0])
    def _():
        lo, hi = _unpack_bf16_pair(xs_ref[...])
        y = _ffn(lo.astype(BF16), hi.astype(BF16), wg_sc[...], wu_sc[...], wd_sc[...])
        half = y.shape[1] // 2
        ys_ref[...] = _pack_bf16_pair(y[:, :half], y[:, half:])

    @pl.when(i >= nu_ref[0])
    def _():
        ys_ref[...] = jnp.zeros(ys_ref.shape, ys_ref.dtype)


def _experts(xs, tile_expert, tile_slot, tile_next, n_used, w_gate, w_up, w_down, layer):
    _, _, d, f = w_gate.shape
    n_tiles = xs.shape[0] // EXP_TM
    hbm = pl.BlockSpec(memory_space=pl.ANY)
    tile = (EXP_TM, xs.shape[1])
    grid_spec = pltpu.PrefetchScalarGridSpec(
        num_scalar_prefetch=4,
        grid=(n_tiles,),
        in_specs=[pl.BlockSpec(tile, lambda i, te, par, nxt, nu: (jnp.minimum(i, nu[0] - 1), 0)),
                  hbm, hbm, hbm],
        out_specs=pl.BlockSpec(tile, lambda i, te, par, nxt, nu: (i, 0)),
        scratch_shapes=[pltpu.VMEM((2, d, f), F32), pltpu.VMEM((2, d, f), F32), pltpu.VMEM((2, f, d), F32),
                        pltpu.VMEM((d, f), BF16), pltpu.VMEM((d, f), BF16), pltpu.VMEM((f, d), BF16),
                        pltpu.SemaphoreType.DMA((2, 3))],
    )
    return pl.pallas_call(
        functools.partial(_expert_kernel, layer=layer),
        grid_spec=grid_spec,
        out_shape=jax.ShapeDtypeStruct(xs.shape, U32),
        compiler_params=_params(("arbitrary",)),
        name="moe_experts",
    )(tile_expert, tile_slot, tile_next, n_used, xs, w_gate, w_up, w_down)


def _combine_kernel(slot_hbm, ys_hbm, hp_ref, gate_ref, sg_ref, su_ref, sd_ref, x_ref, g5_ref,
                    o_ref, slot_sm, buf, sem_s, sem):
    i = pl.program_id(0)
    n_steps = pl.num_programs(0)
    n = COMB_T * SUBLANES
    half = x_ref.shape[1] // 2

    def load_slots(h, half_index):
        cp = pltpu.make_async_copy(slot_hbm.at[pl.ds(pl.multiple_of(half_index * n, n), n)],
                                   slot_sm.at[h], sem_s.at[h])
        cp.start()
        cp.wait()

    def issue(h, tokens):
        for t in tokens:
            for k in range(TOP_K):
                _row_copy(ys_hbm, slot_sm[h, t * SUBLANES + k], buf.at[h, k], t, sem.at[h]).start(priority=k % 2)

    def drain(h):
        def body(t, carry):
            for k in range(TOP_K):
                _row_copy(ys_hbm, 0, buf.at[h, k], 0, sem.at[h]).wait()
            return carry

        lax.fori_loop(0, COMB_T, body, 0)

    def finish(h, other):
        rows = slice(h * COMB_T, (h + 1) * COMB_T)
        issue(other, range(0, COMB_T // 2))
        lo, hi = _unpack_bf16_pair(hp_ref[rows, :])
        y = _ffn(lo.astype(BF16), hi.astype(BF16), sg_ref[...], su_ref[...], sd_ref[...])
        drain(h)
        issue(other, range(COMB_T // 2, COMB_T))
        gates = [gate_ref[rows, k:k + 1] for k in range(TOP_K)]
        for c in range(half // COMB_COLS):
            lo_cols = slice(c * COMB_COLS, (c + 1) * COMB_COLS)
            hi_cols = slice(half + c * COMB_COLS, half + (c + 1) * COMB_COLS)
            r_lo = y[:, lo_cols]
            r_hi = y[:, hi_cols]
            for k in range(TOP_K):
                e_lo, e_hi = _unpack_bf16_pair(buf[h, k, :, lo_cols])
                r_lo = r_lo + gates[k] * e_lo
                r_hi = r_hi + gates[k] * e_hi
            o_ref[rows, lo_cols] = x_ref[rows, lo_cols] + g5_ref[:, lo_cols] * r_lo
            o_ref[rows, hi_cols] = x_ref[rows, hi_cols] + g5_ref[:, hi_cols] * r_hi

    @pl.when(i == 0)
    def _():
        load_slots(0, 0)
        lax.fori_loop(0, COMB_T, lambda t, c: (issue(0, (t,)), c)[1], 0)

    load_slots(1, 2 * i + 1)
    finish(0, 1)
    load_slots(0, jnp.minimum(2 * i + 2, 2 * n_steps - 2))
    finish(1, 0)

    @pl.when(i == n_steps - 1)
    def _():
        drain(0)


def _combine(ys, hp, slots, gates, sh_gate, sh_up, sh_down, x, mods, layer, t_rows, mod_row):
    d = x.shape[1]
    w = hp.shape[1]
    f = sh_gate.shape[1]
    rows = 2 * COMB_T
    return pl.pallas_call(
        _combine_kernel,
        grid=(t_rows // rows,),
        in_specs=[pl.BlockSpec(memory_space=pl.ANY), pl.BlockSpec(memory_space=pl.ANY),
                  pl.BlockSpec((rows, w), lambda i: (i, 0)),
                  pl.BlockSpec((rows, SUBLANES), lambda i: (i, 0)),
                  pl.BlockSpec((d, f), lambda i: (0, 0)),
                  pl.BlockSpec((d, f), lambda i: (0, 0)),
                  pl.BlockSpec((f, d), lambda i: (0, 0)),
                  pl.BlockSpec((rows, d), lambda i: (i, 0)),
                  _mod_spec(layer, 5, d, lambda i: mod_row(i * rows // ROW_TILE))],
        out_specs=pl.BlockSpec((rows, d), lambda i: (i, 0)),
        out_shape=jax.ShapeDtypeStruct((t_rows, d), F32),
        scratch_shapes=[pltpu.SMEM((2, COMB_T * SUBLANES), I32),
                        pltpu.VMEM((2, TOP_K, COMB_T, w), U32),
                        pltpu.SemaphoreType.DMA((2,)), pltpu.SemaphoreType.DMA((2,))],
        compiler_params=_params(("arbitrary",)),
        name="moe_combine",
    )(slots, ys, hp, gates, sh_gate.astype(BF16), sh_up.astype(BF16), sh_down.astype(BF16), x, mods)


def _moe(x, t_rows, norm_w, mods, layer, mod_row, router_w, router_bias,
         w_gate, w_up, w_down, sh_gate, sh_up, sh_down):
    hp, eidx, gate, rank, cnt = _norm_router(x, t_rows, norm_w, mods, layer, mod_row, router_w, router_bias)

    n_tiles = (t_rows * TOP_K + N_EXPERTS * (EXP_TM - 1) + EXP_TM - 1) // EXP_TM
    counts = cnt[:, 0].astype(I32)
    padded = (counts + EXP_TM - 1) // EXP_TM * EXP_TM
    ends = jnp.cumsum(padded)
    base = ends - padded
    experts = jnp.arange(N_EXPERTS, dtype=I32)
    base_of = jnp.sum(jnp.where(eidx[:, :, None] == experts, base, 0), axis=-1)
    slots = jnp.transpose(base_of + rank).reshape(-1)
    n_used = (ends[-1] // EXP_TM).astype(I32)
    tile_start = jnp.arange(n_tiles, dtype=I32) * EXP_TM
    last_start = (n_used - 1) * EXP_TM
    te = jnp.sum(jnp.minimum(tile_start, last_start)[:, None] >= ends[None, :], axis=1).astype(I32)
    te = jnp.minimum(te, N_EXPERTS - 1)
    used = counts > 0
    slot_of = ((jnp.cumsum(used.astype(I32)) - 1) % 2).astype(I32)
    later = jnp.logical_and(experts[None, :] > experts[:, None], used[None, :])
    next_of = jnp.min(jnp.where(later, experts[None, :], N_EXPERTS), axis=1)
    next_of = jnp.where(next_of == N_EXPERTS, -1, next_of).astype(I32)

    xs = _dispatch(hp, slots, base + counts, padded - counts, ends[-1:], n_tiles * EXP_TM)
    ys = _experts(xs, te, slot_of[te], next_of[te], n_used.reshape(1), w_gate, w_up, w_down, layer)
    return _combine(ys, hp, slots, jnp.transpose(gate), sh_gate, sh_up, sh_down, x, mods, layer, t_rows, mod_row)


def kernel(x, c, ctx, c_ctx, ada_w, ada_b, norm1_w, norm2_w, w_in, gm_v_norm, gm_w_s, gm_b_s, gqa_q_norm, gqa_k_norm, mla_q_a_norm, mla_w_uq, mla_kv_a_norm, mla_w_ukv, mla_q_norm, mla_k_norm, w_out, router_w, router_bias, exp_w_gate, exp_w_up, exp_w_down, sh_w_gate, sh_w_up, sh_w_down):
    batch, n_lat, d = x.shape
    n_ctx = ctx.shape[1]
    depth = ada_w.shape[0]
    assert n_ctx == ROW_TILE and n_lat % MM_TM == 0 and (batch * n_ctx) % MM_TM == 0
    t_lat = batch * n_lat
    t_all = t_lat + batch * n_ctx
    n_tok = n_lat + n_ctx
    lat_tiles = n_lat // ROW_TILE

    mod_row = lambda i: jnp.minimum(i // lat_tiles, batch)
    is_lat = lambda i: i < batch * lat_tiles
    b_fn = lambda i: jnp.where(is_lat(i), i // lat_tiles, i - batch * lat_tiles)
    rb_fn = lambda i: jnp.where(is_lat(i), i % lat_tiles, lat_tiles)
    geom = (batch, n_tok, b_fn, rb_fn)
    mm_mod_row = lambda i: mod_row(i * (MM_TM // ROW_TILE))

    mods = _ada_mods(c, c_ctx, ada_w, ada_b)
    cos_b, sin_b = _rope_tables(n_lat, n_ctx, HEAD_DIM, 1)
    cos_c, sin_c = _rope_tables(n_lat, n_ctx, C_ROPE, LANES // C_ROPE)

    xs = jnp.concatenate([x.reshape(t_lat, d), ctx.reshape(batch * n_ctx, d)], axis=0)
    for l in range(depth):
        need_ctx = l < depth - 1
        t_rows = t_all if need_ctx else t_lat
        w_in_l = jnp.pad(w_in[l].astype(BF16), ((0, 0), (0, IN_COLS_PAD - w_in.shape[2])))

        proj = _in_proj(xs, norm1_w[l], mods, l, mm_mod_row, w_in_l)
        a_out = _gmlp(proj, t_rows, gm_v_norm[l], gm_w_s[l], gm_b_s[l])
        qkv = _gqa_prep(proj, geom, gqa_q_norm[l], gqa_k_norm[l], cos_b, sin_b)
        b_out = _attention(_score_bound(gqa_q_norm[l], gqa_k_norm[l], HEAD_DIM), qkv, qkv, qkv,
                           (0, B_HEADS, B_HEADS + B_KV_HEADS), B_KV_HEADS, B_GROUP, GQA_TQ,
                           n_lat, n_ctx, need_ctx, t_rows, "gqa_attn")
        qc, kc, vc = _mla_prep(proj, geom, mla_q_a_norm[l], mla_w_uq[l], mla_kv_a_norm[l], mla_w_ukv[l],
                               mla_q_norm[l], mla_k_norm[l], cos_c, sin_c)
        c_out = _attention(_score_bound(mla_q_norm[l], mla_k_norm[l], C_QK), qc, kc, vc,
                           (0, 0, 0), C_HEADS, 1, MLA_TQ, n_lat, n_ctx, need_ctx, t_rows, "mla_attn")
        xs = _out_proj(a_out, b_out, c_out, w_out[l].astype(BF16), xs, mods, l, t_rows, mm_mod_row)
        xs = _moe(xs, t_rows, norm2_w[l], mods, l, mod_row, router_w[l], router_bias[l],
                  exp_w_gate, exp_w_up, exp_w_down, sh_w_gate[l], sh_w_up[l], sh_w_down[l])
    return xs[:t_lat].reshape(batch, n_lat, d)
```

```python
import functools

import jax
import jax.numpy as jnp
from jax import lax
from jax.experimental import pallas as pl
from jax.experimental.pallas import tpu as pltpu

F32 = jnp.float32
BF16 = jnp.bfloat16
U32 = jnp.uint32
I32 = jnp.int32

GRID_W = 64
CHUNK = 128
HEAD_DIM = 128
EPS = 1e-6
ROPE_THETA = 10000.0
A_HEADS = 8
B_HEADS = 12
B_KV_HEADS = 4
B_GROUP = B_HEADS // B_KV_HEADS
C_HEADS = 12
C_Q_RANK = 768
C_KV_RANK = 512
C_NOPE = 128
C_ROPE = 64
C_V = 128
C_QK = C_NOPE + C_ROPE
A_WIDTH = A_HEADS * HEAD_DIM
B_WIDTH = B_HEADS * HEAD_DIM
C_WIDTH = C_HEADS * C_V
N_EXPERTS = 64
TOP_K = 6
N_GROUPS = 8
GROUP_SIZE = N_EXPERTS // N_GROUPS
TOPK_GROUPS = 4
ROUTED_SCALE = 2.5
N_MOD = 6

LANES = 128
SUBLANES = 8
ROW_TILE = 256
MM_TM = 512
MM_TN = 1024
GQA_TQ = 1024
MLA_TQ = 2048
ATT_TK = 512
LOG2E = 1.4426950408889634
BOUND_SLACK = 1.02
SCORE_BOUND_LIMIT = 60.0
EXP_TM = 256
DISP_T = 256
COMB_T = 128
COMB_COLS = 128
IN_COLS_PAD = 6144
VMEM_LIMIT = 56 * 1024 * 1024


def _params(sem, vmem=VMEM_LIMIT):
    return pltpu.CompilerParams(dimension_semantics=sem, vmem_limit_bytes=vmem)


def _rms(x, w):
    return x * lax.rsqrt(jnp.mean(x * x, axis=-1, keepdims=True) + EPS) * w


def _pack_bf16_pair(a, b):
    ua = lax.bitcast_convert_type(a.astype(BF16).astype(F32), U32)
    ub = lax.bitcast_convert_type(b.astype(BF16).astype(F32), U32)
    return (ua >> 16) | ub


def _unpack_bf16_pair(u):
    a = lax.bitcast_convert_type(u << 16, F32)
    b = lax.bitcast_convert_type(u & jnp.uint32(0xFFFF0000), F32)
    return a, b


def _ffn(lo, hi, wg, wu, wd):
    half = lo.shape[1]
    hg = jnp.dot(lo, wg[:half], preferred_element_type=F32) + jnp.dot(hi, wg[half:], preferred_element_type=F32)
    hu = jnp.dot(lo, wu[:half], preferred_element_type=F32) + jnp.dot(hi, wu[half:], preferred_element_type=F32)
    hb = (jax.nn.silu(hg) * hu).astype(BF16)
    return jnp.dot(hb, wd, preferred_element_type=F32)


def _ada_kernel(cb_ref, w_ref, b_ref, o_ref, silu_sc, *, n_rows, d, tn):
    nj = tn // LANES

    @pl.when(jnp.logical_and(pl.program_id(0) == 0, pl.program_id(1) == 0))
    def _():
        cv = cb_ref[...]
        silu_sc[...] = cv * jax.nn.sigmoid(cv)

    def body(kc, accs):
        k0 = pl.multiple_of(kc * SUBLANES, SUBLANES)
        wk = w_ref[0, pl.ds(k0, SUBLANES), :]
        out = []
        for r in range(n_rows):
            sv = silu_sc[r, pl.ds(k0, SUBLANES), :]
            for j in range(nj):
                out.append(accs[r * nj + j] + wk[:, j * LANES:(j + 1) * LANES] * sv)
        return tuple(out)

    init = tuple(jnp.zeros((SUBLANES, LANES), F32) for _ in range(n_rows * nj))
    accs = lax.fori_loop(0, d // SUBLANES, body, init, unroll=8)
    o_ref[0] = jnp.zeros((SUBLANES, tn), F32)
    for r in range(n_rows):
        for j in range(nj):
            o_ref[0, r:r + 1, j * LANES:(j + 1) * LANES] = (
                jnp.sum(accs[r * nj + j], axis=0, keepdims=True) + b_ref[0, :, j * LANES:(j + 1) * LANES])


def _ada_mods(c, c_ctx, ada_w, ada_b):
    n_layers, d, nd = ada_w.shape
    n_rows = c.shape[0] + 1
    assert n_rows <= SUBLANES
    cvec = jnp.concatenate([c, c_ctx[None]], axis=0)
    cb = jnp.broadcast_to(cvec[:, :, None], (n_rows, d, LANES))
    tn = 1024
    out = pl.pallas_call(
        functools.partial(_ada_kernel, n_rows=n_rows, d=d, tn=tn),
        grid=(n_layers, nd // tn),
        in_specs=[pl.BlockSpec((n_rows, d, LANES), lambda l, j: (0, 0, 0)),
                  pl.BlockSpec((1, d, tn), lambda l, j: (l, 0, j)),
                  pl.BlockSpec((1, 1, tn), lambda l, j: (l, 0, j))],
        out_specs=pl.BlockSpec((1, SUBLANES, tn), lambda l, j: (l, 0, j)),
        out_shape=jax.ShapeDtypeStruct((n_layers, SUBLANES, nd), F32),
        scratch_shapes=[pltpu.VMEM((n_rows, d, LANES), F32)],
        compiler_params=_params(("arbitrary", "arbitrary")),
        name="ada_mods",
    )(cb, ada_w, ada_b.reshape(n_layers, 1, nd))
    return out.reshape(n_layers, SUBLANES, N_MOD, 1, d)


def _mod_spec(layer, m, d, row_fn, tn=None, col_fn=None):
    if tn is None:
        return pl.BlockSpec((None, None, None, 1, d), lambda *a: (layer, row_fn(*a), m, 0, 0))
    return pl.BlockSpec((None, None, None, 1, tn), lambda *a: (layer, row_fn(*a), m, 0, col_fn(*a)))


def _in_proj_kernel(x_ref, nw_ref, shift_ref, scale_ref, w_ref, o_ref):
    y = _rms(x_ref[...], nw_ref[...])
    h = (y * (1.0 + scale_ref[...]) + shift_ref[...]).astype(BF16)
    o_ref[...] = jnp.dot(h, w_ref[...], preferred_element_type=F32).astype(o_ref.dtype)


def _in_proj(x, norm_w, mods, layer, mod_row, w):
    m, d = x.shape
    n = w.shape[1]
    return pl.pallas_call(
        _in_proj_kernel,
        grid=(n // MM_TN, m // MM_TM),
        in_specs=[pl.BlockSpec((MM_TM, d), lambda j, i: (i, 0)),
                  pl.BlockSpec((1, d), lambda j, i: (0, 0)),
                  _mod_spec(layer, 0, d, lambda j, i: mod_row(i)),
                  _mod_spec(layer, 1, d, lambda j, i: mod_row(i)),
                  pl.BlockSpec((d, MM_TN), lambda j, i: (0, j))],
        out_specs=pl.BlockSpec((MM_TM, MM_TN), lambda j, i: (i, j)),
        out_shape=jax.ShapeDtypeStruct((m, n), BF16),
        compiler_params=_params(("arbitrary", "arbitrary")),
        name="in_proj",
    )(x, norm_w.reshape(1, d), mods, mods, w)


def _gmlp_kernel(p_ref, vn_ref, ws_ref, bt_ref, o_ref):
    for h in range(A_HEADS):
        hs = slice(h * HEAD_DIM, (h + 1) * HEAD_DIM)
        u = jax.nn.gelu(p_ref[:, hs].astype(F32))
        v = jax.nn.gelu(p_ref[:, A_WIDTH + h * HEAD_DIM:A_WIDTH + (h + 1) * HEAD_DIM].astype(F32))
        vb = _rms(v, vn_ref[:, hs]).astype(BF16)
        for c in range(ROW_TILE // CHUNK):
            cs = slice(c * CHUNK, (c + 1) * CHUNK)
            s = jnp.dot(ws_ref[h], vb[cs], preferred_element_type=F32) + bt_ref[:, h:h + 1]
            o_ref[cs, hs] = (u[cs] * s).astype(o_ref.dtype)


def _gmlp(proj, t_rows, v_norm, w_s, b_s):
    return pl.pallas_call(
        _gmlp_kernel,
        grid=(t_rows // ROW_TILE,),
        in_specs=[pl.BlockSpec((ROW_TILE, 2 * A_WIDTH), lambda i: (i, 0)),
                  pl.BlockSpec((1, A_WIDTH), lambda i: (0, 0)),
                  pl.BlockSpec((A_HEADS, CHUNK, CHUNK), lambda i: (0, 0, 0)),
                  pl.BlockSpec((CHUNK, A_HEADS), lambda i: (0, 0))],
        out_specs=pl.BlockSpec((ROW_TILE, A_WIDTH), lambda i: (i, 0)),
        out_shape=jax.ShapeDtypeStruct((t_rows, A_WIDTH), BF16),
        compiler_params=_params(("arbitrary",)),
        name="gmlp",
    )(proj, v_norm.reshape(1, A_WIDTH), w_s.astype(BF16), jnp.transpose(b_s))


def _rope_tables(n_lat, n_ctx, d_rot, reps):
    half = d_rot // 2
    nf = half // 2
    rows = n_lat // GRID_W
    tpos = jnp.arange(n_lat, dtype=I32)
    row = tpos // GRID_W - rows // 2
    col = tpos % GRID_W - GRID_W // 2
    freq = ROPE_THETA ** (-jnp.arange(nf, dtype=F32) / nf)
    ang_r = row.astype(F32)[:, None] * freq
    ang_c = col.astype(F32)[:, None] * freq
    cos = jnp.concatenate([jnp.cos(ang_r), jnp.cos(ang_r), jnp.cos(ang_c), jnp.cos(ang_c)], axis=1)
    sin = jnp.concatenate([-jnp.sin(ang_r), jnp.sin(ang_r), -jnp.sin(ang_c), jnp.sin(ang_c)], axis=1)
    cos = jnp.concatenate([cos, jnp.ones((n_ctx, d_rot), F32)], axis=0)
    sin = jnp.concatenate([sin, jnp.zeros((n_ctx, d_rot), F32)], axis=0)
    return jnp.tile(cos, (1, reps)), jnp.tile(sin, (1, reps))


def _rope(y, cos, sin, nf):
    lane = lax.broadcasted_iota(I32, y.shape, 1)
    partner = jnp.where((lane % (2 * nf)) < nf,
                        pltpu.roll(y, LANES - nf, 1), pltpu.roll(y, nf, 1))
    return y * cos + partner * sin


def _gqa_prep_kernel(p_ref, w_ref, cos_ref, sin_ref, o_ref):
    c = pl.program_id(1)
    heads = p_ref.shape[1] // HEAD_DIM

    @pl.when(c < 4)
    def _():
        for hh in range(heads):
            x = p_ref[:, hh * HEAD_DIM:(hh + 1) * HEAD_DIM].astype(F32)
            y = _rope(_rms(x, w_ref[...]), cos_ref[...], sin_ref[...], HEAD_DIM // 4)
            o_ref[hh] = y.astype(o_ref.dtype)

    @pl.when(c == 4)
    def _():
        for hh in range(heads):
            o_ref[hh] = p_ref[:, hh * HEAD_DIM:(hh + 1) * HEAD_DIM]


def _gqa_prep(proj, geom, q_norm, k_norm, cos, sin):
    batch, n_tok, b_fn, rb_fn = geom
    wvec = jnp.stack([q_norm * (HEAD_DIM ** -0.5 * LOG2E)] * 3 + [k_norm, jnp.ones_like(k_norm)])
    wvec = wvec.reshape(5, 1, HEAD_DIM)
    first = (2 * A_WIDTH) // 512
    return pl.pallas_call(
        _gqa_prep_kernel,
        grid=(proj.shape[0] // ROW_TILE, 5),
        in_specs=[pl.BlockSpec((ROW_TILE, 512), lambda i, c: (i, first + c)),
                  pl.BlockSpec((None, 1, HEAD_DIM), lambda i, c: (c, 0, 0)),
                  pl.BlockSpec((ROW_TILE, HEAD_DIM), lambda i, c: (rb_fn(i), 0)),
                  pl.BlockSpec((ROW_TILE, HEAD_DIM), lambda i, c: (rb_fn(i), 0))],
        out_specs=pl.BlockSpec((None, 4, ROW_TILE, HEAD_DIM), lambda i, c: (b_fn(i), c, rb_fn(i), 0)),
        out_shape=jax.ShapeDtypeStruct((batch, B_HEADS + 2 * B_KV_HEADS, n_tok, HEAD_DIM), BF16),
        compiler_params=_params(("arbitrary", "arbitrary")),
        name="gqa_prep",
    )(proj, wvec, cos, sin)


def _mla_prep_kernel(p_ref, qa_ref, kva_ref, wqn_ref, wqr_ref, wkk_ref, wkv_ref,
                     nqn_ref, nqr_ref, nkn_ref, nkr_ref, cos_ref, sin_ref,
                     qc_ref, kc_ref, vc_ref):
    cq = p_ref[:, :C_Q_RANK].astype(F32)
    ckv = p_ref[:, C_Q_RANK:C_Q_RANK + C_KV_RANK].astype(F32)
    kr = p_ref[:, C_Q_RANK + C_KV_RANK:C_Q_RANK + C_KV_RANK + LANES].astype(F32)
    cqn = _rms(cq, qa_ref[...]).astype(BF16)
    ckvn = _rms(ckv, kva_ref[...]).astype(BF16)
    qn = jnp.dot(cqn, wqn_ref[...], preferred_element_type=F32)
    qr = jnp.dot(cqn, wqr_ref[...], preferred_element_type=F32)
    kn = jnp.dot(ckvn, wkk_ref[...], preferred_element_type=F32)
    vv = jnp.dot(ckvn, wkv_ref[...], preferred_element_type=F32)
    cos = cos_ref[...]
    sin = sin_ref[...]
    lane = lax.broadcasted_iota(I32, kr.shape, 1)
    low = lane < C_ROPE
    inv_qk = 1.0 / C_QK

    kr_rot = _rope(kr * nkr_ref[...], cos, sin, C_ROPE // 4)
    kr_ss = jnp.sum(kr * kr, axis=-1, keepdims=True)
    for h in range(C_HEADS):
        hs = slice(h * C_NOPE, (h + 1) * C_NOPE)
        knh = kn[:, hs]
        r = lax.rsqrt((jnp.sum(knh * knh, axis=-1, keepdims=True) + kr_ss) * inv_qk + EPS)
        kc_ref[h, :, 0:C_NOPE] = (knh * r * nkn_ref[...]).astype(kc_ref.dtype)
        kc_ref[h, :, C_NOPE:C_QK] = (kr_rot * r)[:, 0:C_ROPE].astype(kc_ref.dtype)
        vc_ref[h] = vv[:, h * C_V:(h + 1) * C_V].astype(vc_ref.dtype)

    for pr in range(C_HEADS // 2):
        qrp = qr[:, pr * LANES:(pr + 1) * LANES]
        sq = qrp * qrp
        ss_pair = (jnp.sum(jnp.where(low, sq, 0.0), axis=-1, keepdims=True),
                   jnp.sum(jnp.where(low, 0.0, sq), axis=-1, keepdims=True))
        rs = []
        for e in range(2):
            h = 2 * pr + e
            qnh = qn[:, h * C_NOPE:(h + 1) * C_NOPE]
            r = lax.rsqrt((jnp.sum(qnh * qnh, axis=-1, keepdims=True) + ss_pair[e]) * inv_qk + EPS)
            rs.append(r)
            qc_ref[h, :, 0:C_NOPE] = (qnh * r * nqn_ref[...]).astype(qc_ref.dtype)
        q_rot = _rope(qrp * jnp.where(low, rs[0], rs[1]) * nqr_ref[...], cos, sin, C_ROPE // 4)
        qc_ref[2 * pr, :, C_NOPE:C_QK] = q_rot[:, 0:C_ROPE].astype(qc_ref.dtype)
        qc_ref[2 * pr + 1, :, C_NOPE:C_QK] = q_rot[:, C_ROPE:2 * C_ROPE].astype(qc_ref.dtype)


def _mla_prep(proj, geom, q_a_norm, w_uq, kv_a_norm, w_ukv, q_norm, k_norm, cos, sin):
    batch, n_tok, b_fn, rb_fn = geom
    first = (2 * A_WIDTH + B_WIDTH + 2 * B_KV_HEADS * HEAD_DIM)
    width = IN_COLS_PAD - first
    assert first % width == 0
    wq = w_uq.reshape(C_Q_RANK, C_HEADS, C_QK)
    wqn = wq[:, :, :C_NOPE].reshape(C_Q_RANK, C_HEADS * C_NOPE).astype(BF16)
    wqr = wq[:, :, C_NOPE:].reshape(C_Q_RANK, C_HEADS * C_ROPE).astype(BF16)
    wkv = w_ukv.reshape(C_KV_RANK, C_HEADS, C_NOPE + C_V)
    wkk = wkv[:, :, :C_NOPE].reshape(C_KV_RANK, C_HEADS * C_NOPE).astype(BF16)
    wkvv = wkv[:, :, C_NOPE:].reshape(C_KV_RANK, C_HEADS * C_V).astype(BF16)
    qs = C_QK ** -0.5 * LOG2E
    nqn = (q_norm[:C_NOPE] * qs).reshape(1, C_NOPE)
    nqr = jnp.tile(q_norm[C_NOPE:] * qs, 2).reshape(1, LANES)
    nkn = k_norm[:C_NOPE].reshape(1, C_NOPE)
    nkr = jnp.tile(k_norm[C_NOPE:], 2).reshape(1, LANES)
    const = lambda shape: pl.BlockSpec(shape, lambda i: tuple(0 for _ in shape))
    head_spec = lambda dd: pl.BlockSpec((None, C_HEADS, ROW_TILE, dd), lambda i: (b_fn(i), 0, rb_fn(i), 0))
    return pl.pallas_call(
        _mla_prep_kernel,
        grid=(proj.shape[0] // ROW_TILE,),
        in_specs=[pl.BlockSpec((ROW_TILE, width), lambda i: (i, first // width)),
                  const((1, C_Q_RANK)), const((1, C_KV_RANK)),
                  const(wqn.shape), const(wqr.shape), const(wkk.shape), const(wkvv.shape),
                  const((1, C_NOPE)), const((1, LANES)), const((1, C_NOPE)), const((1, LANES)),
                  pl.BlockSpec((ROW_TILE, LANES), lambda i: (rb_fn(i), 0)),
                  pl.BlockSpec((ROW_TILE, LANES), lambda i: (rb_fn(i), 0))],
        out_specs=[head_spec(C_QK), head_spec(C_QK), head_spec(C_V)],
        out_shape=[jax.ShapeDtypeStruct((batch, C_HEADS, n_tok, C_QK), BF16),
                   jax.ShapeDtypeStruct((batch, C_HEADS, n_tok, C_QK), BF16),
                   jax.ShapeDtypeStruct((batch, C_HEADS, n_tok, C_V), BF16)],
        compiler_params=_params(("arbitrary",)),
        name="mla_prep",
    )(proj, q_a_norm.reshape(1, C_Q_RANK), kv_a_norm.reshape(1, C_KV_RANK), wqn, wqr, wkk, wkvv,
      nqn, nqr, nkn, nkr, cos, sin)


def _attn_kernel(bound_ref, q_ref, k_ref, v_ref, *rest, group, n_lat):
    use_bound = bound_ref[0] <= SCORE_BOUND_LIMIT
    for bounded, pred in ((True, use_bound), (False, jnp.logical_not(use_bound))):
        pl.when(pred)(functools.partial(_attn_body, bound_ref, q_ref, k_ref, v_ref, rest,
                                        group=group, n_lat=n_lat, bounded=bounded))


def _attn_body(bound_ref, q_ref, k_ref, v_ref, rest, *, group, n_lat, bounded):
    o_ref, m_sc, l_sc, acc_sc = rest[-4:]
    tq, dk = q_ref.shape[1], q_ref.shape[2]
    dv = v_ref.shape[1]
    n_tail = k_ref.shape[0] - n_lat
    q = q_ref[...].reshape(group * tq, dk)
    l_sc[...] = jnp.zeros(l_sc.shape, F32)
    acc_sc[...] = jnp.zeros(acc_sc.shape, F32)
    if not bounded:
        m_sc[...] = jnp.full(m_sc.shape, -jnp.inf, F32)

    def lane_fold(p):
        out = p[:, 0:LANES]
        for j in range(1, p.shape[1] // LANES):
            out = out + p[:, j * LANES:(j + 1) * LANES]
        return out

    def chunk(start, size):
        k = k_ref[pl.ds(start, size), :]
        v = v_ref[pl.ds(start, size), :]
        s = lax.dot_general(q, k, (((1,), (1,)), ((), ())), preferred_element_type=F32)
        if bounded:
            p = jnp.exp2(s - bound_ref[0])
            l_sc[...] += lane_fold(p)
            acc_sc[...] += jnp.dot(p.astype(BF16), v, preferred_element_type=F32)
        else:
            m_prev = m_sc[...]
            m_new = jnp.maximum(m_prev, jnp.max(s, axis=-1, keepdims=True))
            alpha = jnp.exp2(m_prev - m_new)
            p = jnp.exp2(s - m_new[:, 0:1])
            l_sc[...] = alpha * l_sc[...] + lane_fold(p)
            acc_sc[...] = alpha[:, 0:1] * acc_sc[...] + jnp.dot(p.astype(BF16), v, preferred_element_type=F32)
            m_sc[...] = m_new

    if n_lat:
        def body(c, carry):
            chunk(pl.multiple_of(c * ATT_TK, ATT_TK), ATT_TK)
            return carry
        lax.fori_loop(0, n_lat // ATT_TK, body, 0, unroll=2)
    chunk(n_lat, n_tail)
    o = acc_sc[...] / jnp.sum(l_sc[...], axis=-1, keepdims=True)
    for g in range(group):
        o_ref[:, g * dv:(g + 1) * dv] = o[g * tq:(g + 1) * tq].astype(o_ref.dtype)


def _attention_call(bound, q_arr, k_arr, v_arr, prev_out, heads, n_kv, group, tq, n_lat, n_ctx, ctx_only,
                    t_rows, name):
    q_head0, k_head0, v_head0 = heads
    batch, _, n_tok, dk = q_arr.shape
    dv = v_arr.shape[-1]
    m = group * tq
    if ctx_only:
        assert tq == n_ctx and n_lat % n_ctx == 0
        nq, q0, key_rows, key_blk = 1, n_lat // tq, n_ctx, n_lat // n_ctx
        out_row = lambda b, qi: batch * (n_lat // tq) + b
    else:
        assert n_lat % tq == 0 and n_lat % ATT_TK == 0
        nq, q0, key_rows, key_blk = n_lat // tq, 0, n_tok, 0
        out_row = lambda b, qi: b * nq + qi
    in_specs = [pl.BlockSpec((None, group, tq, dk), lambda b, g, qi, *_: (b, q_head0 // group + g, q0 + qi, 0)),
                pl.BlockSpec((None, None, key_rows, dk), lambda b, g, qi, *_: (b, k_head0 + g, key_blk, 0)),
                pl.BlockSpec((None, None, key_rows, dv), lambda b, g, qi, *_: (b, v_head0 + g, key_blk, 0))]
    args = [bound, q_arr, k_arr, v_arr]
    aliases = {}
    if prev_out is not None:
        in_specs.append(pl.BlockSpec(memory_space=pl.ANY))
        args.append(prev_out)
        aliases = {len(args) - 1: 0}
    grid_spec = pltpu.PrefetchScalarGridSpec(
        num_scalar_prefetch=1,
        grid=(batch, n_kv, nq),
        in_specs=in_specs,
        out_specs=pl.BlockSpec((tq, group * dv), lambda b, g, qi, *_: (out_row(b, qi), g)),
        scratch_shapes=[pltpu.VMEM((m, LANES), F32), pltpu.VMEM((m, LANES), F32), pltpu.VMEM((m, dv), F32)],
    )
    return pl.pallas_call(
        functools.partial(_attn_kernel, group=group, n_lat=0 if ctx_only else n_lat),
        grid_spec=grid_spec,
        out_shape=jax.ShapeDtypeStruct((t_rows, n_kv * group * dv), BF16),
        input_output_aliases=aliases,
        compiler_params=_params(("arbitrary", "arbitrary", "arbitrary")),
        name=name,
    )(*args)


def _attention(bound, q_arr, k_arr, v_arr, heads, n_kv, group, tq, n_lat, n_ctx, with_ctx, t_rows, name):
    tq = min(tq, n_lat)
    out = _attention_call(bound, q_arr, k_arr, v_arr, None, heads, n_kv, group, tq, n_lat, n_ctx,
                          False, t_rows, name)
    if with_ctx:
        out = _attention_call(bound, q_arr, k_arr, v_arr, out, heads, n_kv, group, n_ctx, n_lat, n_ctx,
                              True, t_rows, name + "_ctx")
    return out


def _score_bound(q_norm, k_norm, dk):
    b = dk ** 0.5 * jnp.max(jnp.abs(q_norm)) * jnp.max(jnp.abs(k_norm)) * (LOG2E * BOUND_SLACK)
    return b.reshape(1).astype(F32)


def _wout_kernel(a_ref, b_ref, c_ref, w_ref, x_ref, g_ref, o_ref):
    acc = jnp.dot(a_ref[...], w_ref[0:A_WIDTH, :], preferred_element_type=F32)
    acc += jnp.dot(b_ref[...], w_ref[A_WIDTH:A_WIDTH + B_WIDTH, :], preferred_element_type=F32)
    acc += jnp.dot(c_ref[...], w_ref[A_WIDTH + B_WIDTH:, :], preferred_element_type=F32)
    o_ref[...] = x_ref[...] + g_ref[...] * acc


def _out_proj(a_out, b_out, c_out, w, x, mods, layer, t_rows, mod_row):
    d = w.shape[1]
    return pl.pallas_call(
        _wout_kernel,
        grid=(d // MM_TN, t_rows // MM_TM),
        in_specs=[pl.BlockSpec((MM_TM, A_WIDTH), lambda j, i: (i, 0)),
                  pl.BlockSpec((MM_TM, B_WIDTH), lambda j, i: (i, 0)),
                  pl.BlockSpec((MM_TM, C_WIDTH), lambda j, i: (i, 0)),
                  pl.BlockSpec((w.shape[0], MM_TN), lambda j, i: (0, j)),
                  pl.BlockSpec((MM_TM, MM_TN), lambda j, i: (i, j)),
                  _mod_spec(layer, 2, d, lambda j, i: mod_row(i), MM_TN, lambda j, i: j)],
        out_specs=pl.BlockSpec((MM_TM, MM_TN), lambda j, i: (i, j)),
        out_shape=jax.ShapeDtypeStruct((t_rows, d), F32),
        compiler_params=_params(("arbitrary", "arbitrary")),
        name="out_proj",
    )(a_out, b_out, c_out, w, x, mods)


def _first_index(hit_val, x, iota, big):
    return jnp.min(jnp.where(x == hit_val, iota, big), axis=0, keepdims=True)


def _router_kernel(x_ref, nw_ref, shift_ref, scale_ref, rwt_ref, bias_ref,
                   hp_ref, eidx_ref, gate_ref, rank_ref, cnt_ref, carry_sc):
    i = pl.program_id(0)
    tm, d = x_ref.shape
    h = _rms(x_ref[...], nw_ref[...]) * (1.0 + scale_ref[...]) + shift_ref[...]
    hp_ref[...] = _pack_bf16_pair(h[:, :d // 2], h[:, d // 2:])

    logits = lax.dot_general(rwt_ref[...], h, (((1,), (1,)), ((), ())),
                             precision=lax.Precision.HIGHEST, preferred_element_type=F32)
    scores = jax.nn.sigmoid(logits)
    sel = scores + bias_ref[:, 0:1]
    neg = jnp.float32(-jnp.inf)

    iota_n = lax.broadcasted_iota(I32, (N_GROUPS, tm), 0).astype(F32)
    iota_e = lax.broadcasted_iota(I32, (N_EXPERTS, tm), 0).astype(F32)

    gscore = jnp.zeros((N_GROUPS, tm), F32)
    for g in range(N_GROUPS):
        blk = sel[g * GROUP_SIZE:(g + 1) * GROUP_SIZE]
        m1 = jnp.max(blk, axis=0, keepdims=True)
        first = _first_index(m1, blk, iota_n, float(GROUP_SIZE))
        m2 = jnp.max(jnp.where(iota_n == first, neg, blk), axis=0, keepdims=True)
        gscore = jnp.where(iota_n == float(g), m1 + m2, gscore)

    gsel = jnp.zeros((N_GROUPS, tm), F32)
    work = gscore
    for _ in range(TOPK_GROUPS):
        m = jnp.max(work, axis=0, keepdims=True)
        hit = iota_n == _first_index(m, work, iota_n, float(N_GROUPS))
        gsel = jnp.where(hit, 1.0, gsel)
        work = jnp.where(hit, neg, work)
    emask = jnp.concatenate(
        [jnp.broadcast_to(gsel[g:g + 1], (GROUP_SIZE, tm)) for g in range(N_GROUPS)], axis=0)

    work = jnp.where(emask > 0.5, sel, neg)
    picks = []
    onehot = jnp.zeros((N_EXPERTS, tm), F32)
    for _ in range(TOP_K):
        m = jnp.max(work, axis=0, keepdims=True)
        pick = _first_index(m, work, iota_e, float(N_EXPERTS))
        picks.append(pick)
        hit = iota_e == pick
        onehot = jnp.where(hit, 1.0, onehot)
        work = jnp.where(hit, neg, work)

    w_un = onehot * scores
    gates = w_un / jnp.sum(w_un, axis=0, keepdims=True) * ROUTED_SCALE

    @pl.when(i == 0)
    def _():
        carry_sc[...] = jnp.zeros(carry_sc.shape, F32)

    before = (lax.broadcasted_iota(I32, (tm, tm), 0) < lax.broadcasted_iota(I32, (tm, tm), 1)).astype(BF16)
    rank = carry_sc[:, 0:1] + jnp.dot(onehot.astype(BF16), before, preferred_element_type=F32)
    carry_sc[...] = carry_sc[...] + jnp.sum(onehot, axis=1, keepdims=True)
    cnt_ref[...] = carry_sc[...]

    eidx_ref[...] = jnp.zeros(eidx_ref.shape, I32)
    gate_ref[...] = jnp.zeros(gate_ref.shape, F32)
    rank_ref[...] = jnp.zeros(rank_ref.shape, I32)
    for k in range(TOP_K):
        hit = iota_e == picks[k]
        eidx_ref[k:k + 1, :] = picks[k].astype(I32)
        gate_ref[k:k + 1, :] = jnp.sum(jnp.where(hit, gates, 0.0), axis=0, keepdims=True)
        rank_ref[k:k + 1, :] = jnp.sum(jnp.where(hit, rank, 0.0), axis=0, keepdims=True).astype(I32)


def _norm_router(x, t_rows, norm_w, mods, layer, mod_row, router_w, router_bias):
    d = x.shape[1]
    col = lambda dt: (pl.BlockSpec((SUBLANES, ROW_TILE), lambda i: (0, i)),
                      jax.ShapeDtypeStruct((SUBLANES, t_rows), dt))
    specs, shapes = zip(col(I32), col(F32), col(I32))
    return pl.pallas_call(
        _router_kernel,
        grid=(t_rows // ROW_TILE,),
        in_specs=[pl.BlockSpec((ROW_TILE, d), lambda i: (i, 0)),
                  pl.BlockSpec((1, d), lambda i: (0, 0)),
                  _mod_spec(layer, 3, d, mod_row),
                  _mod_spec(layer, 4, d, mod_row),
                  pl.BlockSpec((N_EXPERTS, d), lambda i: (0, 0)),
                  pl.BlockSpec((N_EXPERTS, LANES), lambda i: (0, 0))],
        out_specs=[pl.BlockSpec((ROW_TILE, d // 2), lambda i: (i, 0)), *specs,
                   pl.BlockSpec((N_EXPERTS, LANES), lambda i: (0, 0))],
        out_shape=[jax.ShapeDtypeStruct((t_rows, d // 2), U32), *shapes,
                   jax.ShapeDtypeStruct((N_EXPERTS, LANES), F32)],
        scratch_shapes=[pltpu.VMEM((N_EXPERTS, LANES), F32)],
        compiler_params=_params(("arbitrary",)),
        name="norm_router",
    )(x, norm_w.reshape(1, d), mods, mods, jnp.transpose(router_w),
      jnp.broadcast_to(router_bias[:, None], (N_EXPERTS, LANES)))


def _row_copy(src, src_row, dst, dst_row, sem):
    return pltpu.make_async_copy(src.at[pl.ds(src_row, 1)], dst.at[pl.ds(dst_row, 1)], sem)


_FILL_PIECES = tuple(1 << b for b in range(EXP_TM.bit_length() - 2, -1, -1))


def _zero_fill(pstart_ref, plen_ref, tail_ref, xs_hbm, zero_sc, sem_z):
    zero_sc[...] = jnp.zeros(zero_sc.shape, zero_sc.dtype)
    n_tail = (xs_hbm.shape[0] - tail_ref[0]) // _FILL_PIECES[0]

    def pad_copies(e, wait):
        start = pstart_ref[e]
        head = (-start) & (SUBLANES - 1)
        for r in range(SUBLANES - 1):
            cp = pltpu.make_async_copy(zero_sc.at[pl.ds(0, 1)], xs_hbm.at[pl.ds(start + r, 1)], sem_z)

            @pl.when(r < head)
            def _():
                cp.wait() if wait else cp.start()

        off = start + head
        rem = plen_ref[e] - head
        for piece in _FILL_PIECES:
            if piece < SUBLANES:
                break
            on = (rem & piece) != 0
            cp = pltpu.make_async_copy(zero_sc.at[pl.ds(0, piece)],
                                       xs_hbm.at[pl.ds(pl.multiple_of(off, SUBLANES), piece)], sem_z)

            @pl.when(on)
            def _():
                cp.wait() if wait else cp.start()

            off = off + jnp.where(on, piece, 0)

    def tail_copy(j, wait):
        row = pl.multiple_of(tail_ref[0] + j * _FILL_PIECES[0], _FILL_PIECES[0])
        cp = pltpu.make_async_copy(zero_sc, xs_hbm.at[pl.ds(row, _FILL_PIECES[0])], sem_z)
        cp.wait() if wait else cp.start()

    for wait in (False, True):
        lax.fori_loop(0, N_EXPERTS, lambda e, c: (pad_copies(e, wait), c)[1], 0)
        lax.fori_loop(0, n_tail, lambda j, c: (tail_copy(j, wait), c)[1], 0)


def _dispatch_kernel(pstart_ref, plen_ref, tail_ref, slot_hbm, h_ref, xs_hbm, slot_sm, zero_sc, sem_s, sem, sem_z):
    i = pl.program_id(0)
    n = slot_sm.shape[0]
    cp = pltpu.make_async_copy(slot_hbm.at[pl.ds(pl.multiple_of(i * n, n), n)], slot_sm, sem_s)
    cp.start()

    @pl.when(i == 0)
    def _():
        _zero_fill(pstart_ref, plen_ref, tail_ref, xs_hbm, zero_sc, sem_z)

    cp.wait()

    def issue(t, carry):
        for k in range(TOP_K):
            _row_copy(h_ref, t, xs_hbm, slot_sm[t * SUBLANES + k], sem).start(priority=k % 2)
        return carry

    lax.fori_loop(0, DISP_T, issue, 0)

    def drain(t, carry):
        for k in range(TOP_K):
            _row_copy(h_ref, 0, xs_hbm, 0, sem).wait()
        return carry

    lax.fori_loop(0, DISP_T, drain, 0)


def _dispatch(hp, slots, pad_start, pad_len, tail_start, n_slots):
    t, w = hp.shape
    grid_spec = pltpu.PrefetchScalarGridSpec(
        num_scalar_prefetch=3,
        grid=(t // DISP_T,),
        in_specs=[pl.BlockSpec(memory_space=pl.ANY), pl.BlockSpec((DISP_T, w), lambda i, *_: (i, 0))],
        out_specs=pl.BlockSpec(memory_space=pl.ANY),
        scratch_shapes=[pltpu.SMEM((DISP_T * SUBLANES,), I32),
                        pltpu.VMEM((_FILL_PIECES[0], w), U32),
                        pltpu.SemaphoreType.DMA(()), pltpu.SemaphoreType.DMA(()), pltpu.SemaphoreType.DMA(())],
    )
    return pl.pallas_call(
        _dispatch_kernel,
        grid_spec=grid_spec,
        out_shape=jax.ShapeDtypeStruct((n_slots, w), U32),
        compiler_params=_params(("arbitrary",)),
        name="moe_dispatch",
    )(pad_start, pad_len, tail_start, slots, hp)


def _expert_kernel(te_ref, par_ref, nxt_ref, nu_ref, xs_ref, wg_hbm, wu_hbm, wd_hbm, ys_ref,
                   wg_f, wu_f, wd_f, wg_sc, wu_sc, wd_sc, sem, *, layer):
    i = pl.program_id(0)
    expert = te_ref[i]
    par = par_ref[i]

    def weight_copies(e, slot):
        return (pltpu.make_async_copy(wg_hbm.at[layer, e], wg_f.at[slot], sem.at[slot, 0]),
                pltpu.make_async_copy(wu_hbm.at[layer, e], wu_f.at[slot], sem.at[slot, 1]),
                pltpu.make_async_copy(wd_hbm.at[layer, e], wd_f.at[slot], sem.at[slot, 2]))

    @pl.when(i == 0)
    def _():
        for cp in weight_copies(expert, par):
            cp.start()

    @pl.when(jnp.logical_or(i == 0, expert != te_ref[jnp.maximum(i - 1, 0)]))
    def _():
        for cp in weight_copies(expert, par):
            cp.wait()

        @pl.when(nxt_ref[i] >= 0)
        def _():
            for cp in weight_copies(nxt_ref[i], 1 - par):
                cp.start()

        wg_sc[...] = wg_f[par].astype(BF16)
        wu_sc[...] = wu_f[par].astype(BF16)
        wd_sc[...] = wd_f[par].astype(BF16)

    @pl.when(i < nu_ref[0])
    def _():
        lo, hi = _unpack_bf16_pair(xs_ref[...])
        y = _ffn(lo.astype(BF16), hi.astype(BF16), wg_sc[...], wu_sc[...], wd_sc[...])
        half = y.shape[1] // 2
        ys_ref[...] = _pack_bf16_pair(y[:, :half], y[:, half:])

    @pl.when(i >= nu_ref[0])
    def _():
        ys_ref[...] = jnp.zeros(ys_ref.shape, ys_ref.dtype)


def _experts(xs, tile_expert, tile_slot, tile_next, n_used, w_gate, w_up, w_down, layer):
    _, _, d, f = w_gate.shape
    n_tiles = xs.shape[0] // EXP_TM
    hbm = pl.BlockSpec(memory_space=pl.ANY)
    tile = (EXP_TM, xs.shape[1])
    grid_spec = pltpu.PrefetchScalarGridSpec(
        num_scalar_prefetch=4,
        grid=(n_tiles,),
        in_specs=[pl.BlockSpec(tile, lambda i, te, par, nxt, nu: (jnp.minimum(i, nu[0] - 1), 0)),
                  hbm, hbm, hbm],
        out_specs=pl.BlockSpec(tile, lambda i, te, par, nxt, nu: (i, 0)),
        scratch_shapes=[pltpu.VMEM((2, d, f), F32), pltpu.VMEM((2, d, f), F32), pltpu.VMEM((2, f, d), F32),
                        pltpu.VMEM((d, f), BF16), pltpu.VMEM((d, f), BF16), pltpu.VMEM((f, d), BF16),
                        pltpu.SemaphoreType.DMA((2, 3))],
    )
    return pl.pallas_call(
        functools.partial(_expert_kernel, layer=layer),
        grid_spec=grid_spec,
        out_shape=jax.ShapeDtypeStruct(xs.shape, U32),
        compiler_params=_params(("arbitrary",)),
        name="moe_experts",
    )(tile_expert, tile_slot, tile_next, n_used, xs, w_gate, w_up, w_down)


def _combine_kernel(slot_hbm, ys_hbm, hp_ref, gate_ref, sg_ref, su_ref, sd_ref, x_ref, g5_ref,
                    o_ref, slot_sm, buf, sem_s, sem):
    i = pl.program_id(0)
    n_steps = pl.num_programs(0)
    n = COMB_T * SUBLANES
    half = x_ref.shape[1] // 2

    def load_slots(h, half_index):
        cp = pltpu.make_async_copy(slot_hbm.at[pl.ds(pl.multiple_of(half_index * n, n), n)],
                                   slot_sm.at[h], sem_s.at[h])
        cp.start()
        cp.wait()

    def issue(h, tokens):
        for t in tokens:
            for k in range(TOP_K):
                _row_copy(ys_hbm, slot_sm[h, t * SUBLANES + k], buf.at[h, k], t, sem.at[h]).start(priority=k % 2)

    def drain(h):
        def body(t, carry):
            for k in range(TOP_K):
                _row_copy(ys_hbm, 0, buf.at[h, k], 0, sem.at[h]).wait()
            return carry

        lax.fori_loop(0, COMB_T, body, 0)

    def finish(h, other):
        rows = slice(h * COMB_T, (h + 1) * COMB_T)
        issue(other, range(0, COMB_T // 2))
        lo, hi = _unpack_bf16_pair(hp_ref[rows, :])
        y = _ffn(lo.astype(BF16), hi.astype(BF16), sg_ref[...], su_ref[...], sd_ref[...])
        drain(h)
        issue(other, range(COMB_T // 2, COMB_T))
        gates = [gate_ref[rows, k:k + 1] for k in range(TOP_K)]
        for c in range(half // COMB_COLS):
            lo_cols = slice(c * COMB_COLS, (c + 1) * COMB_COLS)
            hi_cols = slice(half + c * COMB_COLS, half + (c + 1) * COMB_COLS)
            r_lo = y[:, lo_cols]
            r_hi = y[:, hi_cols]
            for k in range(TOP_K):
                e_lo, e_hi = _unpack_bf16_pair(buf[h, k, :, lo_cols])
                r_lo = r_lo + gates[k] * e_lo
                r_hi = r_hi + gates[k] * e_hi
            o_ref[rows, lo_cols] = x_ref[rows, lo_cols] + g5_ref[:, lo_cols] * r_lo
            o_ref[rows, hi_cols] = x_ref[rows, hi_cols] + g5_ref[:, hi_cols] * r_hi

    @pl.when(i == 0)
    def _():
        load_slots(0, 0)
        lax.fori_loop(0, COMB_T, lambda t, c: (issue(0, (t,)), c)[1], 0)

    load_slots(1, 2 * i + 1)
    finish(0, 1)
    load_slots(0, jnp.minimum(2 * i + 2, 2 * n_steps - 2))
    finish(1, 0)

    @pl.when(i == n_steps - 1)
    def _():
        drain(0)


def _combine(ys, hp, slots, gates, sh_gate, sh_up, sh_down, x, mods, layer, t_rows, mod_row):
    d = x.shape[1]
    w = hp.shape[1]
    f = sh_gate.shape[1]
    rows = 2 * COMB_T
    return pl.pallas_call(
        _combine_kernel,
        grid=(t_rows // rows,),
        in_specs=[pl.BlockSpec(memory_space=pl.ANY), pl.BlockSpec(memory_space=pl.ANY),
                  pl.BlockSpec((rows, w), lambda i: (i, 0)),
                  pl.BlockSpec((rows, SUBLANES), lambda i: (i, 0)),
                  pl.BlockSpec((d, f), lambda i: (0, 0)),
                  pl.BlockSpec((d, f), lambda i: (0, 0)),
                  pl.BlockSpec((f, d), lambda i: (0, 0)),
                  pl.BlockSpec((rows, d), lambda i: (i, 0)),
                  _mod_spec(layer, 5, d, lambda i: mod_row(i * rows // ROW_TILE))],
        out_specs=pl.BlockSpec((rows, d), lambda i: (i, 0)),
        out_shape=jax.ShapeDtypeStruct((t_rows, d), F32),
        scratch_shapes=[pltpu.SMEM((2, COMB_T * SUBLANES), I32),
                        pltpu.VMEM((2, TOP_K, COMB_T, w), U32),
                        pltpu.SemaphoreType.DMA((2,)), pltpu.SemaphoreType.DMA((2,))],
        compiler_params=_params(("arbitrary",)),
        name="moe_combine",
    )(slots, ys, hp, gates, sh_gate.astype(BF16), sh_up.astype(BF16), sh_down.astype(BF16), x, mods)


def _moe(x, t_rows, norm_w, mods, layer, mod_row, router_w, router_bias,
         w_gate, w_up, w_down, sh_gate, sh_up, sh_down):
    hp, eidx, gate, rank, cnt = _norm_router(x, t_rows, norm_w, mods, layer, mod_row, router_w, router_bias)

    n_tiles = (t_rows * TOP_K + N_EXPERTS * (EXP_TM - 1) + EXP_TM - 1) // EXP_TM
    counts = cnt[:, 0].astype(I32)
    padded = (counts + EXP_TM - 1) // EXP_TM * EXP_TM
    ends = jnp.cumsum(padded)
    base = ends - padded
    experts = jnp.arange(N_EXPERTS, dtype=I32)
    base_of = jnp.sum(jnp.where(eidx[:, :, None] == experts, base, 0), axis=-1)
    slots = jnp.transpose(base_of + rank).reshape(-1)
    n_used = (ends[-1] // EXP_TM).astype(I32)
    tile_start = jnp.arange(n_tiles, dtype=I32) * EXP_TM
    last_start = (n_used - 1) * EXP_TM
    te = jnp.sum(jnp.minimum(tile_start, last_start)[:, None] >= ends[None, :], axis=1).astype(I32)
    te = jnp.minimum(te, N_EXPERTS - 1)
    used = counts > 0
    slot_of = ((jnp.cumsum(used.astype(I32)) - 1) % 2).astype(I32)
    later = jnp.logical_and(experts[None, :] > experts[:, None], used[None, :])
    next_of = jnp.min(jnp.where(later, experts[None, :], N_EXPERTS), axis=1)
    next_of = jnp.where(next_of == N_EXPERTS, -1, next_of).astype(I32)

    xs = _dispatch(hp, slots, base + counts, padded - counts, ends[-1:], n_tiles * EXP_TM)
    ys = _experts(xs, te, slot_of[te], next_of[te], n_used.reshape(1), w_gate, w_up, w_down, layer)
    return _combine(ys, hp, slots, jnp.transpose(gate), sh_gate, sh_up, sh_down, x, mods, layer, t_rows, mod_row)


def kernel(x, c, ctx, c_ctx, ada_w, ada_b, norm1_w, norm2_w, w_in, gm_v_norm, gm_w_s, gm_b_s, gqa_q_norm, gqa_k_norm, mla_q_a_norm, mla_w_uq, mla_kv_a_norm, mla_w_ukv, mla_q_norm, mla_k_norm, w_out, router_w, router_bias, exp_w_gate, exp_w_up, exp_w_down, sh_w_gate, sh_w_up, sh_w_down):
    batch, n_lat, d = x.shape
    n_ctx = ctx.shape[1]
    depth = ada_w.shape[0]
    assert n_ctx == ROW_TILE and n_lat % MM_TM == 0 and (batch * n_ctx) % MM_TM == 0
    t_lat = batch * n_lat
    t_all = t_lat + batch * n_ctx
    n_tok = n_lat + n_ctx
    lat_tiles = n_lat // ROW_TILE

    mod_row = lambda i: jnp.minimum(i // lat_tiles, batch)
    is_lat = lambda i: i < batch * lat_tiles
    b_fn = lambda i: jnp.where(is_lat(i), i // lat_tiles, i - batch * lat_tiles)
    rb_fn = lambda i: jnp.where(is_lat(i), i % lat_tiles, lat_tiles)
    geom = (batch, n_tok, b_fn, rb_fn)
    mm_mod_row = lambda i: mod_row(i * (MM_TM // ROW_TILE))

    mods = _ada_mods(c, c_ctx, ada_w, ada_b)
    cos_b, sin_b = _rope_tables(n_lat, n_ctx, HEAD_DIM, 1)
    cos_c, sin_c = _rope_tables(n_lat, n_ctx, C_ROPE, LANES // C_ROPE)

    xs = jnp.concatenate([x.reshape(t_lat, d), ctx.reshape(batch * n_ctx, d)], axis=0)
    for l in range(depth):
        need_ctx = l < depth - 1
        t_rows = t_all if need_ctx else t_lat
        w_in_l = jnp.pad(w_in[l].astype(BF16), ((0, 0), (0, IN_COLS_PAD - w_in.shape[2])))

        proj = _in_proj(xs, norm1_w[l], mods, l, mm_mod_row, w_in_l)
        a_out = _gmlp(proj, t_rows, gm_v_norm[l], gm_w_s[l], gm_b_s[l])
        qkv = _gqa_prep(proj, geom, gqa_q_norm[l], gqa_k_norm[l], cos_b, sin_b)
        b_out = _attention(_score_bound(gqa_q_norm[l], gqa_k_norm[l], HEAD_DIM), qkv, qkv, qkv,
                           (0, B_HEADS, B_HEADS + B_KV_HEADS), B_KV_HEADS, B_GROUP, GQA_TQ,
                           n_lat, n_ctx, need_ctx, t_rows, "gqa_attn")
        qc, kc, vc = _mla_prep(proj, geom, mla_q_a_norm[l], mla_w_uq[l], mla_kv_a_norm[l], mla_w_ukv[l],
                               mla_q_norm[l], mla_k_norm[l], cos_c, sin_c)
        c_out = _attention(_score_bound(mla_q_norm[l], mla_k_norm[l], C_QK), qc, kc, vc,
                           (0, 0, 0), C_HEADS, 1, MLA_TQ, n_lat, n_ctx, need_ctx, t_rows, "mla_attn")
        xs = _out_proj(a_out, b_out, c_out, w_out[l].astype(BF16), xs, mods, l, t_rows, mm_mod_row)
        xs = _moe(xs, t_rows, norm2_w[l], mods, l, mod_row, router_w[l], router_bias[l],
                  exp_w_gate, exp_w_up, exp_w_down, sh_w_gate[l], sh_w_up[l], sh_w_down[l])
    return xs[:t_lat].reshape(batch, n_lat, d)
```

```python
import functools

import jax
import jax.numpy as jnp
from jax import lax
from jax.experimental import pallas as pl
from jax.experimental.pallas import tpu as pltpu

F32 = jnp.float32
BF16 = jnp.bfloat16
U32 = jnp.uint32
I32 = jnp.int32

GRID_W = 64
CHUNK = 128
HEAD_DIM = 128
EPS = 1e-6
ROPE_THETA = 10000.0
A_HEADS = 8
B_HEADS = 12
B_KV_HEADS = 4
B_GROUP = B_HEADS // B_KV_HEADS
C_HEADS = 12
C_Q_RANK = 768
C_KV_RANK = 512
C_NOPE = 128
C_ROPE = 64
C_V = 128
C_QK = C_NOPE + C_ROPE
A_WIDTH = A_HEADS * HEAD_DIM
B_WIDTH = B_HEADS * HEAD_DIM
C_WIDTH = C_HEADS * C_V
N_EXPERTS = 64
TOP_K = 6
N_GROUPS = 8
GROUP_SIZE = N_EXPERTS // N_GROUPS
TOPK_GROUPS = 4
ROUTED_SCALE = 2.5
N_MOD = 6

LANES = 128
SUBLANES = 8
ROW_TILE = 256
MM_TM = 512
MM_TN = 1024
GQA_TQ = 1024
MLA_TQ = 2048
ATT_TK = 512
LOG2E = 1.4426950408889634
BOUND_SLACK = 1.02
SCORE_BOUND_LIMIT = 60.0
EXP_TM = 256
DISP_T = 256
COMB_T = 128
COMB_COLS = 128
IN_COLS_PAD = 6144
VMEM_LIMIT = 56 * 1024 * 1024


def _params(sem, vmem=VMEM_LIMIT):
    return pltpu.CompilerParams(dimension_semantics=sem, vmem_limit_bytes=vmem)


def _rms(x, w):
    return x * lax.rsqrt(jnp.mean(x * x, axis=-1, keepdims=True) + EPS) * w


def _pack_bf16_pair(a, b):
    ua = lax.bitcast_convert_type(a.astype(BF16).astype(F32), U32)
    ub = lax.bitcast_convert_type(b.astype(BF16).astype(F32), U32)
    return (ua >> 16) | ub


def _unpack_bf16_pair(u):
    a = lax.bitcast_convert_type(u << 16, F32)
    b = lax.bitcast_convert_type(u & jnp.uint32(0xFFFF0000), F32)
    return a, b


def _ffn(lo, hi, wg, wu, wd):
    half = lo.shape[1]
    hg = jnp.dot(lo, wg[:half], preferred_element_type=F32) + jnp.dot(hi, wg[half:], preferred_element_type=F32)
    hu = jnp.dot(lo, wu[:half], preferred_element_type=F32) + jnp.dot(hi, wu[half:], preferred_element_type=F32)
    hb = (jax.nn.silu(hg) * hu).astype(BF16)
    return jnp.dot(hb, wd, preferred_element_type=F32)


def _ada_kernel(cb_ref, w_ref, b_ref, o_ref, silu_sc, *, n_rows, d, tn):
    nj = tn // LANES

    @pl.when(jnp.logical_and(pl.program_id(0) == 0, pl.program_id(1) == 0))
    def _():
        cv = cb_ref[...]
        silu_sc[...] = cv * jax.nn.sigmoid(cv)

    def body(kc, accs):
        k0 = pl.multiple_of(kc * SUBLANES, SUBLANES)
        wk = w_ref[0, pl.ds(k0, SUBLANES), :]
        out = []
        for r in range(n_rows):
            sv = silu_sc[r, pl.ds(k0, SUBLANES), :]
            for j in range(nj):
                out.append(accs[r * nj + j] + wk[:, j * LANES:(j + 1) * LANES] * sv)
        return tuple(out)

    init = tuple(jnp.zeros((SUBLANES, LANES), F32) for _ in range(n_rows * nj))
    accs = lax.fori_loop(0, d // SUBLANES, body, init, unroll=8)
    o_ref[0] = jnp.zeros((SUBLANES, tn), F32)
    for r in range(n_rows):
        for j in range(nj):
            o_ref[0, r:r + 1, j * LANES:(j + 1) * LANES] = (
                jnp.sum(accs[r * nj + j], axis=0, keepdims=True) + b_ref[0, :, j * LANES:(j + 1) * LANES])


def _ada_mods(c, c_ctx, ada_w, ada_b):
    n_layers, d, nd = ada_w.shape
    n_rows = c.shape[0] + 1
    assert n_rows <= SUBLANES
    cvec = jnp.concatenate([c, c_ctx[None]], axis=0)
    cb = jnp.broadcast_to(cvec[:, :, None], (n_rows, d, LANES))
    tn = 1024
    out = pl.pallas_call(
        functools.partial(_ada_kernel, n_rows=n_rows, d=d, tn=tn),
        grid=(n_layers, nd // tn),
        in_specs=[pl.BlockSpec((n_rows, d, LANES), lambda l, j: (0, 0, 0)),
                  pl.BlockSpec((1, d, tn), lambda l, j: (l, 0, j)),
                  pl.BlockSpec((1, 1, tn), lambda l, j: (l, 0, j))],
        out_specs=pl.BlockSpec((1, SUBLANES, tn), lambda l, j: (l, 0, j)),
        out_shape=jax.ShapeDtypeStruct((n_layers, SUBLANES, nd), F32),
        scratch_shapes=[pltpu.VMEM((n_rows, d, LANES), F32)],
        compiler_params=_params(("arbitrary", "arbitrary")),
        name="ada_mods",
    )(cb, ada_w, ada_b.reshape(n_layers, 1, nd))
    return out.reshape(n_layers, SUBLANES, N_MOD, 1, d)


def _mod_spec(layer, m, d, row_fn, tn=None, col_fn=None):
    if tn is None:
        return pl.BlockSpec((None, None, None, 1, d), lambda *a: (layer, row_fn(*a), m, 0, 0))
    return pl.BlockSpec((None, None, None, 1, tn), lambda *a: (layer, row_fn(*a), m, 0, col_fn(*a)))


def _in_proj_kernel(x_ref, nw_ref, shift_ref, scale_ref, w_ref, *rest):
    o_ref = rest[-1]
    y = _rms(x_ref[...], nw_ref[...])
    h = (y * (1.0 + scale_ref[...]) + shift_ref[...]).astype(BF16)
    o_ref[...] = jnp.dot(h, w_ref[...], preferred_element_type=F32).astype(o_ref.dtype)


def _alias_prev(in_specs, args, prev):
    if prev is None:
        return {}
    in_specs.append(pl.BlockSpec(memory_space=pl.ANY))
    args.append(prev)
    return {len(args) - 1: 0}


def _in_proj(x, norm_w, mods, layer, mod_row, w, tile0, total_rows, prev):
    m, d = x.shape
    n = w.shape[1]
    in_specs = [pl.BlockSpec((MM_TM, d), lambda j, i: (i, 0)),
                pl.BlockSpec((1, d), lambda j, i: (0, 0)),
                _mod_spec(layer, 0, d, lambda j, i: mod_row(i + tile0)),
                _mod_spec(layer, 1, d, lambda j, i: mod_row(i + tile0)),
                pl.BlockSpec((d, MM_TN), lambda j, i: (0, j))]
    args = [x, norm_w.reshape(1, d), mods, mods, w]
    aliases = _alias_prev(in_specs, args, prev)
    return pl.pallas_call(
        _in_proj_kernel,
        grid=(n // MM_TN, m // MM_TM),
        in_specs=in_specs,
        out_specs=pl.BlockSpec((MM_TM, MM_TN), lambda j, i: (i + tile0, j)),
        out_shape=jax.ShapeDtypeStruct((total_rows, n), BF16),
        input_output_aliases=aliases,
        compiler_params=_params(("arbitrary", "arbitrary")),
        name="in_proj",
    )(*args)


def _gmlp_kernel(p_ref, vn_ref, ws_ref, bt_ref, o_ref):
    for h in range(A_HEADS):
        hs = slice(h * HEAD_DIM, (h + 1) * HEAD_DIM)
        u = jax.nn.gelu(p_ref[:, hs].astype(F32))
        v = jax.nn.gelu(p_ref[:, A_WIDTH + h * HEAD_DIM:A_WIDTH + (h + 1) * HEAD_DIM].astype(F32))
        vb = _rms(v, vn_ref[:, hs]).astype(BF16)
        for c in range(ROW_TILE // CHUNK):
            cs = slice(c * CHUNK, (c + 1) * CHUNK)
            s = jnp.dot(ws_ref[h], vb[cs], preferred_element_type=F32) + bt_ref[:, h:h + 1]
            o_ref[cs, hs] = (u[cs] * s).astype(o_ref.dtype)


def _gmlp(proj, t_rows, v_norm, w_s, b_s):
    return pl.pallas_call(
        _gmlp_kernel,
        grid=(t_rows // ROW_TILE,),
        in_specs=[pl.BlockSpec((ROW_TILE, 2 * A_WIDTH), lambda i: (i, 0)),
                  pl.BlockSpec((1, A_WIDTH), lambda i: (0, 0)),
                  pl.BlockSpec((A_HEADS, CHUNK, CHUNK), lambda i: (0, 0, 0)),
                  pl.BlockSpec((CHUNK, A_HEADS), lambda i: (0, 0))],
        out_specs=pl.BlockSpec((ROW_TILE, A_WIDTH), lambda i: (i, 0)),
        out_shape=jax.ShapeDtypeStruct((t_rows, A_WIDTH), BF16),
        compiler_params=_params(("arbitrary",)),
        name="gmlp",
    )(proj, v_norm.reshape(1, A_WIDTH), w_s.astype(BF16), jnp.transpose(b_s))


def _split_pairs(t, axis):
    shape = t.shape
    q = shape[axis] // 4
    t = t.reshape(shape[:axis] + (2, 2, q) + shape[axis + 1:])
    return jnp.swapaxes(t, axis, axis + 1).reshape(shape)


def _rope_tables(n_lat, n_ctx, d_rot, reps, split_pairs=False):
    half = d_rot // 2
    nf = half // 2
    rows = n_lat // GRID_W
    tpos = jnp.arange(n_lat, dtype=I32)
    row = tpos // GRID_W - rows // 2
    col = tpos % GRID_W - GRID_W // 2
    freq = ROPE_THETA ** (-jnp.arange(nf, dtype=F32) / nf)
    ang_r = row.astype(F32)[:, None] * freq
    ang_c = col.astype(F32)[:, None] * freq
    if split_pairs:
        cos = jnp.concatenate([jnp.cos(ang_r), jnp.cos(ang_c), jnp.cos(ang_r), jnp.cos(ang_c)], axis=1)
        sin = jnp.concatenate([-jnp.sin(ang_r), -jnp.sin(ang_c), jnp.sin(ang_r), jnp.sin(ang_c)], axis=1)
    else:
        cos = jnp.concatenate([jnp.cos(ang_r), jnp.cos(ang_r), jnp.cos(ang_c), jnp.cos(ang_c)], axis=1)
        sin = jnp.concatenate([-jnp.sin(ang_r), jnp.sin(ang_r), -jnp.sin(ang_c), jnp.sin(ang_c)], axis=1)
    cos = jnp.concatenate([cos, jnp.ones((n_ctx, d_rot), F32)], axis=0)
    sin = jnp.concatenate([sin, jnp.zeros((n_ctx, d_rot), F32)], axis=0)
    return jnp.tile(cos, (1, reps)), jnp.tile(sin, (1, reps))


def _rope(y, cos, sin, nf):
    if 2 * nf == LANES:
        partner = pltpu.roll(y, nf, 1)
    else:
        lane = lax.broadcasted_iota(I32, y.shape, 1)
        partner = jnp.where((lane % (2 * nf)) < nf,
                            pltpu.roll(y, LANES - nf, 1), pltpu.roll(y, nf, 1))
    return y * cos + partner * sin


def _gqa_prep_kernel(p_ref, w_ref, cos_ref, sin_ref, o_ref):
    c = pl.program_id(1)
    heads = p_ref.shape[1] // HEAD_DIM

    @pl.when(c < 4)
    def _():
        for hh in range(heads):
            x = p_ref[:, hh * HEAD_DIM:(hh + 1) * HEAD_DIM].astype(F32)
            y = _rope(_rms(x, w_ref[...]), cos_ref[...], sin_ref[...], HEAD_DIM // 2)
            o_ref[hh] = y.astype(o_ref.dtype)

    @pl.when(c == 4)
    def _():
        for hh in range(heads):
            o_ref[hh] = p_ref[:, hh * HEAD_DIM:(hh + 1) * HEAD_DIM]


def _gqa_prep(proj, geom, q_norm, k_norm, cos, sin):
    batch, n_tok, b_fn, rb_fn = geom
    q_norm, k_norm = _split_pairs(q_norm, 0), _split_pairs(k_norm, 0)
    wvec = jnp.stack([q_norm * (HEAD_DIM ** -0.5 * LOG2E)] * 3 + [k_norm, jnp.ones_like(k_norm)])
    wvec = wvec.reshape(5, 1, HEAD_DIM)
    first = (2 * A_WIDTH) // 512
    return pl.pallas_call(
        _gqa_prep_kernel,
        grid=(proj.shape[0] // ROW_TILE, 5),
        in_specs=[pl.BlockSpec((ROW_TILE, 512), lambda i, c: (i, first + c)),
                  pl.BlockSpec((None, 1, HEAD_DIM), lambda i, c: (c, 0, 0)),
                  pl.BlockSpec((ROW_TILE, HEAD_DIM), lambda i, c: (rb_fn(i), 0)),
                  pl.BlockSpec((ROW_TILE, HEAD_DIM), lambda i, c: (rb_fn(i), 0))],
        out_specs=pl.BlockSpec((None, 4, ROW_TILE, HEAD_DIM), lambda i, c: (b_fn(i), c, rb_fn(i), 0)),
        out_shape=jax.ShapeDtypeStruct((batch, B_HEADS + 2 * B_KV_HEADS, n_tok, HEAD_DIM), BF16),
        compiler_params=_params(("arbitrary", "arbitrary")),
        name="gqa_prep",
    )(proj, wvec, cos, sin)


def _mla_prep_kernel(p_ref, qa_ref, kva_ref, wqn_ref, wqr_ref, wkk_ref, wkv_ref,
                     nqn_ref, nqr_ref, nkn_ref, nkr_ref, cos_ref, sin_ref,
                     qc_ref, kc_ref, vc_ref):
    cq = p_ref[:, :C_Q_RANK].astype(F32)
    ckv = p_ref[:, C_Q_RANK:C_Q_RANK + C_KV_RANK].astype(F32)
    kr = p_ref[:, C_Q_RANK + C_KV_RANK:C_Q_RANK + C_KV_RANK + LANES].astype(F32)
    cqn = _rms(cq, qa_ref[...]).astype(BF16)
    ckvn = _rms(ckv, kva_ref[...]).astype(BF16)
    qn = jnp.dot(cqn, wqn_ref[...], preferred_element_type=F32)
    qr = jnp.dot(cqn, wqr_ref[...], preferred_element_type=F32)
    kn = jnp.dot(ckvn, wkk_ref[...], preferred_element_type=F32)
    vv = jnp.dot(ckvn, wkv_ref[...], preferred_element_type=F32)
    cos = cos_ref[...]
    sin = sin_ref[...]
    lane = lax.broadcasted_iota(I32, kr.shape, 1)
    low = lane < C_ROPE
    inv_qk = 1.0 / C_QK

    kr_rot = _rope(kr * nkr_ref[...], cos, sin, C_ROPE // 4)
    kr_ss = jnp.sum(kr * kr, axis=-1, keepdims=True)
    for h in range(C_HEADS):
        hs = slice(h * C_NOPE, (h + 1) * C_NOPE)
        knh = kn[:, hs]
        r = lax.rsqrt((jnp.sum(knh * knh, axis=-1, keepdims=True) + kr_ss) * inv_qk + EPS)
        kc_ref[h, :, 0:C_NOPE] = (knh * r * nkn_ref[...]).astype(kc_ref.dtype)
        kc_ref[h, :, C_NOPE:C_QK] = (kr_rot * r)[:, 0:C_ROPE].astype(kc_ref.dtype)
        vc_ref[h] = vv[:, h * C_V:(h + 1) * C_V].astype(vc_ref.dtype)

    for pr in range(C_HEADS // 2):
        qrp = qr[:, pr * LANES:(pr + 1) * LANES]
        sq = qrp * qrp
        ss_pair = (jnp.sum(jnp.where(low, sq, 0.0), axis=-1, keepdims=True),
                   jnp.sum(jnp.where(low, 0.0, sq), axis=-1, keepdims=True))
        rs = []
        for e in range(2):
            h = 2 * pr + e
            qnh = qn[:, h * C_NOPE:(h + 1) * C_NOPE]
            r = lax.rsqrt((jnp.sum(qnh * qnh, axis=-1, keepdims=True) + ss_pair[e]) * inv_qk + EPS)
            rs.append(r)
            qc_ref[h, :, 0:C_NOPE] = (qnh * r * nqn_ref[...]).astype(qc_ref.dtype)
        q_rot = _rope(qrp * jnp.where(low, rs[0], rs[1]) * nqr_ref[...], cos, sin, C_ROPE // 4)
        qc_ref[2 * pr, :, C_NOPE:C_QK] = q_rot[:, 0:C_ROPE].astype(qc_ref.dtype)
        qc_ref[2 * pr + 1, :, C_NOPE:C_QK] = q_rot[:, C_ROPE:2 * C_ROPE].astype(qc_ref.dtype)


def _mla_prep(proj, geom, q_a_norm, w_uq, kv_a_norm, w_ukv, q_norm, k_norm, cos, sin):
    batch, n_tok, b_fn, rb_fn = geom
    first = (2 * A_WIDTH + B_WIDTH + 2 * B_KV_HEADS * HEAD_DIM)
    width = IN_COLS_PAD - first
    assert first % width == 0
    wq = w_uq.reshape(C_Q_RANK, C_HEADS, C_QK)
    wqn = wq[:, :, :C_NOPE].reshape(C_Q_RANK, C_HEADS * C_NOPE).astype(BF16)
    wqr = wq[:, :, C_NOPE:].reshape(C_Q_RANK, C_HEADS * C_ROPE).astype(BF16)
    wkv = w_ukv.reshape(C_KV_RANK, C_HEADS, C_NOPE + C_V)
    wkk = wkv[:, :, :C_NOPE].reshape(C_KV_RANK, C_HEADS * C_NOPE).astype(BF16)
    wkvv = wkv[:, :, C_NOPE:].reshape(C_KV_RANK, C_HEADS * C_V).astype(BF16)
    qs = C_QK ** -0.5 * LOG2E
    nqn = (q_norm[:C_NOPE] * qs).reshape(1, C_NOPE)
    nqr = jnp.tile(q_norm[C_NOPE:] * qs, 2).reshape(1, LANES)
    nkn = k_norm[:C_NOPE].reshape(1, C_NOPE)
    nkr = jnp.tile(k_norm[C_NOPE:], 2).reshape(1, LANES)
    const = lambda shape: pl.BlockSpec(shape, lambda i: tuple(0 for _ in shape))
    head_spec = lambda dd: pl.BlockSpec((None, C_HEADS, ROW_TILE, dd), lambda i: (b_fn(i), 0, rb_fn(i), 0))
    return pl.pallas_call(
        _mla_prep_kernel,
        grid=(proj.shape[0] // ROW_TILE,),
        in_specs=[pl.BlockSpec((ROW_TILE, width), lambda i: (i, first // width)),
                  const((1, C_Q_RANK)), const((1, C_KV_RANK)),
                  const(wqn.shape), const(wqr.shape), const(wkk.shape), const(wkvv.shape),
                  const((1, C_NOPE)), const((1, LANES)), const((1, C_NOPE)), const((1, LANES)),
                  pl.BlockSpec((ROW_TILE, LANES), lambda i: (rb_fn(i), 0)),
                  pl.BlockSpec((ROW_TILE, LANES), lambda i: (rb_fn(i), 0))],
        out_specs=[head_spec(C_QK), head_spec(C_QK), head_spec(C_V)],
        out_shape=[jax.ShapeDtypeStruct((batch, C_HEADS, n_tok, C_QK), BF16),
                   jax.ShapeDtypeStruct((batch, C_HEADS, n_tok, C_QK), BF16),
                   jax.ShapeDtypeStruct((batch, C_HEADS, n_tok, C_V), BF16)],
        compiler_params=_params(("arbitrary",)),
        name="mla_prep",
    )(proj, q_a_norm.reshape(1, C_Q_RANK), kv_a_norm.reshape(1, C_KV_RANK), wqn, wqr, wkk, wkvv,
      nqn, nqr, nkn, nkr, cos, sin)


def _attn_kernel(bound_ref, q_ref, k_ref, v_ref, *rest, group, n_lat):
    use_bound = bound_ref[0] <= SCORE_BOUND_LIMIT
    for bounded, pred in ((True, use_bound), (False, jnp.logical_not(use_bound))):
        pl.when(pred)(functools.partial(_attn_body, bound_ref, q_ref, k_ref, v_ref, rest,
                                        group=group, n_lat=n_lat, bounded=bounded))


def _attn_body(bound_ref, q_ref, k_ref, v_ref, rest, *, group, n_lat, bounded):
    o_ref, m_sc, l_sc, acc_sc = rest[-4:]
    tq, dk = q_ref.shape[1], q_ref.shape[2]
    dv = v_ref.shape[1]
    n_tail = k_ref.shape[0] - n_lat
    q = q_ref[...].reshape(group * tq, dk)
    l_sc[...] = jnp.zeros(l_sc.shape, F32)
    acc_sc[...] = jnp.zeros(acc_sc.shape, F32)
    if not bounded:
        m_sc[...] = jnp.full(m_sc.shape, -jnp.inf, F32)

    def lane_fold(p):
        out = p[:, 0:LANES]
        for j in range(1, p.shape[1] // LANES):
            out = out + p[:, j * LANES:(j + 1) * LANES]
        return out

    def chunk(start, size):
        k = k_ref[pl.ds(start, size), :]
        v = v_ref[pl.ds(start, size), :]
        s = lax.dot_general(q, k, (((1,), (1,)), ((), ())), preferred_element_type=F32)
        if bounded:
            p = jnp.exp2(s - bound_ref[0])
            l_sc[...] += lane_fold(p)
            acc_sc[...] += jnp.dot(p.astype(BF16), v, preferred_element_type=F32)
        else:
            m_prev = m_sc[...]
            m_new = jnp.maximum(m_prev, jnp.max(s, axis=-1, keepdims=True))
            alpha = jnp.exp2(m_prev - m_new)
            p = jnp.exp2(s - m_new[:, 0:1])
            l_sc[...] = alpha * l_sc[...] + lane_fold(p)
            acc_sc[...] = alpha[:, 0:1] * acc_sc[...] + jnp.dot(p.astype(BF16), v, preferred_element_type=F32)
            m_sc[...] = m_new

    if n_lat:
        def body(c, carry):
            chunk(pl.multiple_of(c * ATT_TK, ATT_TK), ATT_TK)
            return carry
        lax.fori_loop(0, n_lat // ATT_TK, body, 0, unroll=2)
    chunk(n_lat, n_tail)
    o = acc_sc[...] / jnp.sum(l_sc[...], axis=-1, keepdims=True)
    for g in range(group):
        o_ref[:, g * dv:(g + 1) * dv] = o[g * tq:(g + 1) * tq].astype(o_ref.dtype)


def _attention_call(bound, q_arr, k_arr, v_arr, prev_out, heads, n_kv, group, tq, n_lat, n_ctx, ctx_only,
                    t_rows, name):
    q_head0, k_head0, v_head0 = heads
    batch, _, n_tok, dk = q_arr.shape
    dv = v_arr.shape[-1]
    m = group * tq
    if ctx_only:
        assert tq == n_ctx and n_lat % n_ctx == 0
        nq, q0, key_rows, key_blk = 1, n_lat // tq, n_ctx, n_lat // n_ctx
        out_row = lambda b, qi: batch * (n_lat // tq) + b
    else:
        assert n_lat % tq == 0 and n_lat % ATT_TK == 0
        nq, q0, key_rows, key_blk = n_lat // tq, 0, n_tok, 0
        out_row = lambda b, qi: b * nq + qi
    in_specs = [pl.BlockSpec((None, group, tq, dk), lambda b, g, qi, *_: (b, q_head0 // group + g, q0 + qi, 0)),
                pl.BlockSpec((None, None, key_rows, dk), lambda b, g, qi, *_: (b, k_head0 + g, key_blk, 0)),
                pl.BlockSpec((None, None, key_rows, dv), lambda b, g, qi, *_: (b, v_head0 + g, key_blk, 0))]
    args = [bound, q_arr, k_arr, v_arr]
    aliases = {}
    if prev_out is not None:
        in_specs.append(pl.BlockSpec(memory_space=pl.ANY))
        args.append(prev_out)
        aliases = {len(args) - 1: 0}
    grid_spec = pltpu.PrefetchScalarGridSpec(
        num_scalar_prefetch=1,
        grid=(batch, n_kv, nq),
        in_specs=in_specs,
        out_specs=pl.BlockSpec((tq, group * dv), lambda b, g, qi, *_: (out_row(b, qi), g)),
        scratch_shapes=[pltpu.VMEM((m, LANES), F32), pltpu.VMEM((m, LANES), F32), pltpu.VMEM((m, dv), F32)],
    )
    return pl.pallas_call(
        functools.partial(_attn_kernel, group=group, n_lat=0 if ctx_only else n_lat),
        grid_spec=grid_spec,
        out_shape=jax.ShapeDtypeStruct((t_rows, n_kv * group * dv), BF16),
        input_output_aliases=aliases,
        compiler_params=_params(("arbitrary", "arbitrary", "arbitrary")),
        name=name,
    )(*args)


def _attention(bound, q_arr, k_arr, v_arr, heads, n_kv, group, tq, n_lat, n_ctx, with_ctx, t_rows, name):
    tq = min(tq, n_lat)
    out = _attention_call(bound, q_arr, k_arr, v_arr, None, heads, n_kv, group, tq, n_lat, n_ctx,
                          False, t_rows, name)
    if with_ctx:
        out = _attention_call(bound, q_arr, k_arr, v_arr, out, heads, n_kv, group, n_ctx, n_lat, n_ctx,
                              True, t_rows, name + "_ctx")
    return out


def _score_bound(q_norm, k_norm, dk):
    b = dk ** 0.5 * jnp.max(jnp.abs(q_norm)) * jnp.max(jnp.abs(k_norm)) * (LOG2E * BOUND_SLACK)
    return b.reshape(1).astype(F32)


def _wout_kernel(a_ref, b_ref, c_ref, w_ref, x_ref, g_ref, *rest):
    o_ref = rest[-1]
    acc = jnp.dot(a_ref[...], w_ref[0:A_WIDTH, :], preferred_element_type=F32)
    acc += jnp.dot(b_ref[...], w_ref[A_WIDTH:A_WIDTH + B_WIDTH, :], preferred_element_type=F32)
    acc += jnp.dot(c_ref[...], w_ref[A_WIDTH + B_WIDTH:, :], preferred_element_type=F32)
    o_ref[...] = x_ref[...] + g_ref[...] * acc


def _out_proj(a_out, b_out, c_out, w, x, n_tiles, mods, layer, mod_row, tile0, total_rows, prev):
    d = w.shape[1]
    in_specs = [pl.BlockSpec((MM_TM, A_WIDTH), lambda j, i: (i + tile0, 0)),
                pl.BlockSpec((MM_TM, B_WIDTH), lambda j, i: (i + tile0, 0)),
                pl.BlockSpec((MM_TM, C_WIDTH), lambda j, i: (i + tile0, 0)),
                pl.BlockSpec((w.shape[0], MM_TN), lambda j, i: (0, j)),
                pl.BlockSpec((MM_TM, MM_TN), lambda j, i: (i, j)),
                _mod_spec(layer, 2, d, lambda j, i: mod_row(i + tile0), MM_TN, lambda j, i: j)]
    args = [a_out, b_out, c_out, w, x, mods]
    aliases = _alias_prev(in_specs, args, prev)
    return pl.pallas_call(
        _wout_kernel,
        grid=(d // MM_TN, n_tiles),
        in_specs=in_specs,
        out_specs=pl.BlockSpec((MM_TM, MM_TN), lambda j, i: (i + tile0, j)),
        out_shape=jax.ShapeDtypeStruct((total_rows, d), F32),
        input_output_aliases=aliases,
        compiler_params=_params(("arbitrary", "arbitrary")),
        name="out_proj",
    )(*args)


def _first_index(hit_val, x, iota, big):
    return jnp.min(jnp.where(x == hit_val, iota, big), axis=0, keepdims=True)


def _router_kernel(x_ref, nw_ref, shift_ref, scale_ref, rwt_ref, bias_ref,
                   hp_ref, eidx_ref, gate_ref, rank_ref, cnt_ref, carry_sc):
    i = pl.program_id(0)
    tm, d = x_ref.shape
    h = _rms(x_ref[...], nw_ref[...]) * (1.0 + scale_ref[...]) + shift_ref[...]
    hp_ref[...] = _pack_bf16_pair(h[:, :d // 2], h[:, d // 2:])

    logits = lax.dot_general(rwt_ref[...], h, (((1,), (1,)), ((), ())),
                             precision=lax.Precision.HIGHEST, preferred_element_type=F32)
    scores = jax.nn.sigmoid(logits)
    sel = scores + bias_ref[:, 0:1]
    neg = jnp.float32(-jnp.inf)

    iota_n = lax.broadcasted_iota(I32, (N_GROUPS, tm), 0).astype(F32)
    iota_e = lax.broadcasted_iota(I32, (N_EXPERTS, tm), 0).astype(F32)

    gscore = jnp.zeros((N_GROUPS, tm), F32)
    for g in range(N_GROUPS):
        blk = sel[g * GROUP_SIZE:(g + 1) * GROUP_SIZE]
        m1 = jnp.max(blk, axis=0, keepdims=True)
        first = _first_index(m1, blk, iota_n, float(GROUP_SIZE))
        m2 = jnp.max(jnp.where(iota_n == first, neg, blk), axis=0, keepdims=True)
        gscore = jnp.where(iota_n == float(g), m1 + m2, gscore)

    gsel = jnp.zeros((N_GROUPS, tm), F32)
    work = gscore
    for _ in range(TOPK_GROUPS):
        m = jnp.max(work, axis=0, keepdims=True)
        hit = iota_n == _first_index(m, work, iota_n, float(N_GROUPS))
        gsel = jnp.where(hit, 1.0, gsel)
        work = jnp.where(hit, neg, work)
    emask = jnp.concatenate(
        [jnp.broadcast_to(gsel[g:g + 1], (GROUP_SIZE, tm)) for g in range(N_GROUPS)], axis=0)

    work = jnp.where(emask > 0.5, sel, neg)
    picks = []
    onehot = jnp.zeros((N_EXPERTS, tm), F32)
    for _ in range(TOP_K):
        m = jnp.max(work, axis=0, keepdims=True)
        pick = _first_index(m, work, iota_e, float(N_EXPERTS))
        picks.append(pick)
        hit = iota_e == pick
        onehot = jnp.where(hit, 1.0, onehot)
        work = jnp.where(hit, neg, work)

    w_un = onehot * scores
    gates = w_un / jnp.sum(w_un, axis=0, keepdims=True) * ROUTED_SCALE

    @pl.when(i == 0)
    def _():
        carry_sc[...] = jnp.zeros(carry_sc.shape, F32)

    before = (lax.broadcasted_iota(I32, (tm, tm), 0) < lax.broadcasted_iota(I32, (tm, tm), 1)).astype(BF16)
    rank = carry_sc[:, 0:1] + jnp.dot(onehot.astype(BF16), before, preferred_element_type=F32)
    carry_sc[...] = carry_sc[...] + jnp.sum(onehot, axis=1, keepdims=True)
    cnt_ref[...] = carry_sc[...]

    eidx_ref[...] = jnp.zeros(eidx_ref.shape, I32)
    gate_ref[...] = jnp.zeros(gate_ref.shape, F32)
    rank_ref[...] = jnp.zeros(rank_ref.shape, I32)
    for k in range(TOP_K):
        hit = iota_e == picks[k]
        eidx_ref[k:k + 1, :] = picks[k].astype(I32)
        gate_ref[k:k + 1, :] = jnp.sum(jnp.where(hit, gates, 0.0), axis=0, keepdims=True)
        rank_ref[k:k + 1, :] = jnp.sum(jnp.where(hit, rank, 0.0), axis=0, keepdims=True).astype(I32)


def _norm_router(x, t_rows, norm_w, mods, layer, mod_row, router_w, router_bias):
    d = x.shape[1]
    col = lambda dt: (pl.BlockSpec((SUBLANES, ROW_TILE), lambda i: (0, i)),
                      jax.ShapeDtypeStruct((SUBLANES, t_rows), dt))
    specs, shapes = zip(col(I32), col(F32), col(I32))
    return pl.pallas_call(
        _router_kernel,
        grid=(t_rows // ROW_TILE,),
        in_specs=[pl.BlockSpec((ROW_TILE, d), lambda i: (i, 0)),
                  pl.BlockSpec((1, d), lambda i: (0, 0)),
                  _mod_spec(layer, 3, d, mod_row),
                  _mod_spec(layer, 4, d, mod_row),
                  pl.BlockSpec((N_EXPERTS, d), lambda i: (0, 0)),
                  pl.BlockSpec((N_EXPERTS, LANES), lambda i: (0, 0))],
        out_specs=[pl.BlockSpec((ROW_TILE, d // 2), lambda i: (i, 0)), *specs,
                   pl.BlockSpec((N_EXPERTS, LANES), lambda i: (0, 0))],
        out_shape=[jax.ShapeDtypeStruct((t_rows, d // 2), U32), *shapes,
                   jax.ShapeDtypeStruct((N_EXPERTS, LANES), F32)],
        scratch_shapes=[pltpu.VMEM((N_EXPERTS, LANES), F32)],
        compiler_params=_params(("arbitrary",)),
        name="norm_router",
    )(x, norm_w.reshape(1, d), mods, mods, jnp.transpose(router_w),
      jnp.broadcast_to(router_bias[:, None], (N_EXPERTS, LANES)))


def _row_copy(src, src_row, dst, dst_row, sem):
    return pltpu.make_async_copy(src.at[pl.ds(src_row, 1)], dst.at[pl.ds(dst_row, 1)], sem)


_FILL_PIECES = tuple(1 << b for b in range(EXP_TM.bit_length() - 2, -1, -1))


def _zero_fill(pstart_ref, plen_ref, tail_ref, xs_hbm, zero_sc, sem_z):
    zero_sc[...] = jnp.zeros(zero_sc.shape, zero_sc.dtype)
    n_tail = (xs_hbm.shape[0] - tail_ref[0]) // _FILL_PIECES[0]

    def pad_copies(e, wait):
        start = pstart_ref[e]
        head = (-start) & (SUBLANES - 1)
        for r in range(SUBLANES - 1):
            cp = pltpu.make_async_copy(zero_sc.at[pl.ds(0, 1)], xs_hbm.at[pl.ds(start + r, 1)], sem_z)

            @pl.when(r < head)
            def _():
                cp.wait() if wait else cp.start()

        off = start + head
        rem = plen_ref[e] - head
        for piece in _FILL_PIECES:
            if piece < SUBLANES:
                break
            on = (rem & piece) != 0
            cp = pltpu.make_async_copy(zero_sc.at[pl.ds(0, piece)],
                                       xs_hbm.at[pl.ds(pl.multiple_of(off, SUBLANES), piece)], sem_z)

            @pl.when(on)
            def _():
                cp.wait() if wait else cp.start()

            off = off + jnp.where(on, piece, 0)

    def tail_copy(j, wait):
        row = pl.multiple_of(tail_ref[0] + j * _FILL_PIECES[0], _FILL_PIECES[0])
        cp = pltpu.make_async_copy(zero_sc, xs_hbm.at[pl.ds(row, _FILL_PIECES[0])], sem_z)
        cp.wait() if wait else cp.start()

    for wait in (False, True):
        lax.fori_loop(0, N_EXPERTS, lambda e, c: (pad_copies(e, wait), c)[1], 0)
        lax.fori_loop(0, n_tail, lambda j, c: (tail_copy(j, wait), c)[1], 0)


def _dispatch_kernel(pstart_ref, plen_ref, tail_ref, slot_hbm, h_ref, xs_hbm, slot_sm, zero_sc, sem_s, sem, sem_z):
    i = pl.program_id(0)
    n = slot_sm.shape[0]
    cp = pltpu.make_async_copy(slot_hbm.at[pl.ds(pl.multiple_of(i * n, n), n)], slot_sm, sem_s)
    cp.start()

    @pl.when(i == 0)
    def _():
        _zero_fill(pstart_ref, plen_ref, tail_ref, xs_hbm, zero_sc, sem_z)

    cp.wait()

    def issue(t, carry):
        for k in range(TOP_K):
            _row_copy(h_ref, t, xs_hbm, slot_sm[t * SUBLANES + k], sem).start(priority=k % 2)
        return carry

    lax.fori_loop(0, DISP_T, issue, 0)

    def drain(t, carry):
        for k in range(TOP_K):
            _row_copy(h_ref, 0, xs_hbm, 0, sem).wait()
        return carry

    lax.fori_loop(0, DISP_T, drain, 0)


def _dispatch(hp, slots, pad_start, pad_len, tail_start, n_slots):
    t, w = hp.shape
    grid_spec = pltpu.PrefetchScalarGridSpec(
        num_scalar_prefetch=3,
        grid=(t // DISP_T,),
        in_specs=[pl.BlockSpec(memory_space=pl.ANY), pl.BlockSpec((DISP_T, w), lambda i, *_: (i, 0))],
        out_specs=pl.BlockSpec(memory_space=pl.ANY),
        scratch_shapes=[pltpu.SMEM((DISP_T * SUBLANES,), I32),
                        pltpu.VMEM((_FILL_PIECES[0], w), U32),
                        pltpu.SemaphoreType.DMA(()), pltpu.SemaphoreType.DMA(()), pltpu.SemaphoreType.DMA(())],
    )
    return pl.pallas_call(
        _dispatch_kernel,
        grid_spec=grid_spec,
        out_shape=jax.ShapeDtypeStruct((n_slots, w), U32),
        compiler_params=_params(("arbitrary",)),
        name="moe_dispatch",
    )(pad_start, pad_len, tail_start, slots, hp)


def _expert_kernel(te_ref, par_ref, nxt_ref, nu_ref, xs_ref, wg_hbm, wu_hbm, wd_hbm, ys_ref,
                   wg_f, wu_f, wd_f, wg_sc, wu_sc, wd_sc, sem, *, layer):
    i = pl.program_id(0)
    expert = te_ref[i]
    par = par_ref[i]

    def weight_copies(e, slot):
        return (pltpu.make_async_copy(wg_hbm.at[layer, e], wg_f.at[slot], sem.at[slot, 0]),
                pltpu.make_async_copy(wu_hbm.at[layer, e], wu_f.at[slot], sem.at[slot, 1]),
                pltpu.make_async_copy(wd_hbm.at[layer, e], wd_f.at[slot], sem.at[slot, 2]))

    @pl.when(i == 0)
    def _():
        for cp in weight_copies(expert, par):
            cp.start()

    @pl.when(jnp.logical_or(i == 0, expert != te_ref[jnp.maximum(i - 1, 0)]))
    def _():
        for cp in weight_copies(expert, par):
            cp.wait()

        @pl.when(nxt_ref[i] >= 0)
        def _():
            for cp in weight_copies(nxt_ref[i], 1 - par):
                cp.start()

        wg_sc[...] = wg_f[par].astype(BF16)
        wu_sc[...] = wu_f[par].astype(BF16)
        wd_sc[...] = wd_f[par].astype(BF16)

    @pl.when(i < nu_ref[0])
    def _():
        lo, hi = _unpack_bf16_pair(xs_ref[...])
        y = _ffn(lo.astype(BF16), hi.astype(BF16), wg_sc[...], wu_sc[...], wd_sc[...])
        half = y.shape[1] // 2
        ys_ref[...] = _pack_bf16_pair(y[:, :half], y[:, half:])

    @pl.when(i >= nu_ref[0])
    def _():
        ys_ref[...] = jnp.zeros(ys_ref.shape, ys_ref.dtype)


def _experts(xs, tile_expert, tile_slot, tile_next, n_used, w_gate, w_up, w_down, layer):
    _, _, d, f = w_gate.shape
    n_tiles = xs.shape[0] // EXP_TM
    hbm = pl.BlockSpec(memory_space=pl.ANY)
    tile = (EXP_TM, xs.shape[1])
    grid_spec = pltpu.PrefetchScalarGridSpec(
        num_scalar_prefetch=4,
        grid=(n_tiles,),
        in_specs=[pl.BlockSpec(tile, lambda i, te, par, nxt, nu: (jnp.minimum(i, nu[0] - 1), 0)),
                  hbm, hbm, hbm],
        out_specs=pl.BlockSpec(tile, lambda i, te, par, nxt, nu: (i, 0)),
        scratch_shapes=[pltpu.VMEM((2, d, f), F32), pltpu.VMEM((2, d, f), F32), pltpu.VMEM((2, f, d), F32),
                        pltpu.VMEM((d, f), BF16), pltpu.VMEM((d, f), BF16), pltpu.VMEM((f, d), BF16),
                        pltpu.SemaphoreType.DMA((2, 3))],
    )
    return pl.pallas_call(
        functools.partial(_expert_kernel, layer=layer),
        grid_spec=grid_spec,
        out_shape=jax.ShapeDtypeStruct(xs.shape, U32),
        compiler_params=_params(("arbitrary",)),
        name="moe_experts",
    )(tile_expert, tile_slot, tile_next, n_used, xs, w_gate, w_up, w_down)


def _combine_kernel(slot_hbm, ys_hbm, hp_ref, gate_ref, sg_ref, su_ref, sd_ref, x_ref, g5_ref,
                    o_ref, slot_sm, buf, sem_s, sem):
    i = pl.program_id(0)
    n_steps = pl.num_programs(0)
    n = COMB_T * SUBLANES
    half = x_ref.shape[1] // 2

    def load_slots(h, half_index):
        cp = pltpu.make_async_copy(slot_hbm.at[pl.ds(pl.multiple_of(half_index * n, n), n)],
                                   slot_sm.at[h], sem_s.at[h])
        cp.start()
        cp.wait()

    def issue(h, tokens):
        for t in tokens:
            for k in range(TOP_K):
                _row_copy(ys_hbm, slot_sm[h, t * SUBLANES + k], buf.at[h, k], t, sem.at[h]).start(priority=k % 2)

    def drain(h):
        def body(t, carry):
            for k in range(TOP_K):
                _row_copy(ys_hbm, 0, buf.at[h, k], 0, sem.at[h]).wait()
            return carry

        lax.fori_loop(0, COMB_T, body, 0)

    def finish(h, other):
        rows = slice(h * COMB_T, (h + 1) * COMB_T)
        issue(other, range(0, COMB_T // 2))
        lo, hi = _unpack_bf16_pair(hp_ref[rows, :])
        y = _ffn(lo.astype(BF16), hi.astype(BF16), sg_ref[...], su_ref[...], sd_ref[...])
        drain(h)
        issue(other, range(COMB_T // 2, COMB_T))
        gates = [gate_ref[rows, k:k + 1] for k in range(TOP_K)]
        for c in range(half // COMB_COLS):
            lo_cols = slice(c * COMB_COLS, (c + 1) * COMB_COLS)
            hi_cols = slice(half + c * COMB_COLS, half + (c + 1) * COMB_COLS)
            r_lo = y[:, lo_cols]
            r_hi = y[:, hi_cols]
            for k in range(TOP_K):
                e_lo, e_hi = _unpack_bf16_pair(buf[h, k, :, lo_cols])
                r_lo = r_lo + gates[k] * e_lo
                r_hi = r_hi + gates[k] * e_hi
            o_ref[rows, lo_cols] = x_ref[rows, lo_cols] + g5_ref[:, lo_cols] * r_lo
            o_ref[rows, hi_cols] = x_ref[rows, hi_cols] + g5_ref[:, hi_cols] * r_hi

    @pl.when(i == 0)
    def _():
        load_slots(0, 0)
        lax.fori_loop(0, COMB_T, lambda t, c: (issue(0, (t,)), c)[1], 0)

    load_slots(1, 2 * i + 1)
    finish(0, 1)
    load_slots(0, jnp.minimum(2 * i + 2, 2 * n_steps - 2))
    finish(1, 0)

    @pl.when(i == n_steps - 1)
    def _():
        drain(0)


def _combine(ys, hp, slots, gates, sh_gate, sh_up, sh_down, x, mods, layer, t_rows, mod_row):
    d = x.shape[1]
    w = hp.shape[1]
    f = sh_gate.shape[1]
    rows = 2 * COMB_T
    return pl.pallas_call(
        _combine_kernel,
        grid=(t_rows // rows,),
        in_specs=[pl.BlockSpec(memory_space=pl.ANY), pl.BlockSpec(memory_space=pl.ANY),
                  pl.BlockSpec((rows, w), lambda i: (i, 0)),
                  pl.BlockSpec((rows, SUBLANES), lambda i: (i, 0)),
                  pl.BlockSpec((d, f), lambda i: (0, 0)),
                  pl.BlockSpec((d, f), lambda i: (0, 0)),
                  pl.BlockSpec((f, d), lambda i: (0, 0)),
                  pl.BlockSpec((rows, d), lambda i: (i, 0)),
                  _mod_spec(layer, 5, d, lambda i: mod_row(i * rows // ROW_TILE))],
        out_specs=pl.BlockSpec((rows, d), lambda i: (i, 0)),
        out_shape=jax.ShapeDtypeStruct((t_rows, d), F32),
        scratch_shapes=[pltpu.SMEM((2, COMB_T * SUBLANES), I32),
                        pltpu.VMEM((2, TOP_K, COMB_T, w), U32),
                        pltpu.SemaphoreType.DMA((2,)), pltpu.SemaphoreType.DMA((2,))],
        compiler_params=_params(("arbitrary",)),
        name="moe_combine",
    )(slots, ys, hp, gates, sh_gate.astype(BF16), sh_up.astype(BF16), sh_down.astype(BF16), x, mods)


def _moe(x, t_rows, norm_w, mods, layer, mod_row, router_w, router_bias,
         w_gate, w_up, w_down, sh_gate, sh_up, sh_down):
    hp, eidx, gate, rank, cnt = _norm_router(x, t_rows, norm_w, mods, layer, mod_row, router_w, router_bias)

    n_tiles = (t_rows * TOP_K + N_EXPERTS * (EXP_TM - 1) + EXP_TM - 1) // EXP_TM
    counts = cnt[:, 0].astype(I32)
    padded = (counts + EXP_TM - 1) // EXP_TM * EXP_TM
    ends = jnp.cumsum(padded)
    base = ends - padded
    experts = jnp.arange(N_EXPERTS, dtype=I32)
    base_of = jnp.sum(jnp.where(eidx[:, :, None] == experts, base, 0), axis=-1)
    slots = jnp.transpose(base_of + rank).reshape(-1)
    n_used = (ends[-1] // EXP_TM).astype(I32)
    tile_start = jnp.arange(n_tiles, dtype=I32) * EXP_TM
    last_start = (n_used - 1) * EXP_TM
    te = jnp.sum(jnp.minimum(tile_start, last_start)[:, None] >= ends[None, :], axis=1).astype(I32)
    te = jnp.minimum(te, N_EXPERTS - 1)
    used = counts > 0
    slot_of = ((jnp.cumsum(used.astype(I32)) - 1) % 2).astype(I32)
    later = jnp.logical_and(experts[None, :] > experts[:, None], used[None, :])
    next_of = jnp.min(jnp.where(later, experts[None, :], N_EXPERTS), axis=1)
    next_of = jnp.where(next_of == N_EXPERTS, -1, next_of).astype(I32)

    xs = _dispatch(hp, slots, base + counts, padded - counts, ends[-1:], n_tiles * EXP_TM)
    ys = _experts(xs, te, slot_of[te], next_of[te], n_used.reshape(1), w_gate, w_up, w_down, layer)
    return _combine(ys, hp, slots, jnp.transpose(gate), sh_gate, sh_up, sh_down, x, mods, layer, t_rows, mod_row)


def kernel(x, c, ctx, c_ctx, ada_w, ada_b, norm1_w, norm2_w, w_in, gm_v_norm, gm_w_s, gm_b_s, gqa_q_norm, gqa_k_norm, mla_q_a_norm, mla_w_uq, mla_kv_a_norm, mla_w_ukv, mla_q_norm, mla_k_norm, w_out, router_w, router_bias, exp_w_gate, exp_w_up, exp_w_down, sh_w_gate, sh_w_up, sh_w_down):
    batch, n_lat, d = x.shape
    n_ctx = ctx.shape[1]
    depth = ada_w.shape[0]
    assert n_ctx == ROW_TILE and n_lat % MM_TM == 0 and (batch * n_ctx) % MM_TM == 0
    t_lat = batch * n_lat
    t_all = t_lat + batch * n_ctx
    n_tok = n_lat + n_ctx
    lat_tiles = n_lat // ROW_TILE

    mod_row = lambda i: jnp.minimum(i // lat_tiles, batch)
    is_lat = lambda i: i < batch * lat_tiles
    b_fn = lambda i: jnp.where(is_lat(i), i // lat_tiles, i - batch * lat_tiles)
    rb_fn = lambda i: jnp.where(is_lat(i), i % lat_tiles, lat_tiles)
    geom = (batch, n_tok, b_fn, rb_fn)
    mm_mod_row = lambda i: mod_row(i * (MM_TM // ROW_TILE))

    mods = _ada_mods(c, c_ctx, ada_w, ada_b)
    cos_b, sin_b = _rope_tables(n_lat, n_ctx, HEAD_DIM, 1, split_pairs=True)
    qk0, qk1 = 2 * A_WIDTH, 2 * A_WIDTH + B_WIDTH + B_KV_HEADS * HEAD_DIM
    cos_c, sin_c = _rope_tables(n_lat, n_ctx, C_ROPE, LANES // C_ROPE)

    lat_mm_tiles = t_lat // MM_TM
    streams = [(x.reshape(t_lat, d), 0), (ctx.reshape(batch * n_ctx, d), lat_mm_tiles)]
    for l in range(depth):
        need_ctx = l < depth - 1
        t_rows = t_all if need_ctx else t_lat
        qk_cols = _split_pairs(w_in[l][:, qk0:qk1].reshape(d, (qk1 - qk0) // HEAD_DIM, HEAD_DIM), 2)
        w_in_l = jnp.concatenate([w_in[l][:, :qk0], qk_cols.reshape(d, qk1 - qk0), w_in[l][:, qk1:]], axis=1)
        w_in_l = jnp.pad(w_in_l.astype(BF16), ((0, 0), (0, IN_COLS_PAD - w_in.shape[2])))

        proj = None
        for rows, tile0 in streams:
            proj = _in_proj(rows, norm1_w[l], mods, l, mm_mod_row, w_in_l, tile0, t_all, proj)
        a_out = _gmlp(proj, t_rows, gm_v_norm[l], gm_w_s[l], gm_b_s[l])
        qkv = _gqa_prep(proj, geom, gqa_q_norm[l], gqa_k_norm[l], cos_b, sin_b)
        b_out = _attention(_score_bound(gqa_q_norm[l], gqa_k_norm[l], HEAD_DIM), qkv, qkv, qkv,
                           (0, B_HEADS, B_HEADS + B_KV_HEADS), B_KV_HEADS, B_GROUP, GQA_TQ,
                           n_lat, n_ctx, need_ctx, t_rows, "gqa_attn")
        qc, kc, vc = _mla_prep(proj, geom, mla_q_a_norm[l], mla_w_uq[l], mla_kv_a_norm[l], mla_w_ukv[l],
                               mla_q_norm[l], mla_k_norm[l], cos_c, sin_c)
        c_out = _attention(_score_bound(mla_q_norm[l], mla_k_norm[l], C_QK), qc, kc, vc,
                           (0, 0, 0), C_HEADS, 1, MLA_TQ, n_lat, n_ctx, need_ctx, t_rows, "mla_attn")
        xs = None
        for rows, tile0 in streams:
            n_tiles = min(rows.shape[0], t_rows - tile0 * MM_TM) // MM_TM
            if n_tiles > 0:
                xs = _out_proj(a_out, b_out, c_out, w_out[l].astype(BF16), rows, n_tiles, mods, l, mm_mod_row,
                               tile0, t_rows, xs)
        xs = _moe(xs, t_rows, norm2_w[l], mods, l, mod_row, router_w[l], router_bias[l],
                  exp_w_gate, exp_w_up, exp_w_down, sh_w_gate[l], sh_w_up[l], sh_w_down[l])
        streams = [(xs, 0)]
    return xs[:t_lat].reshape(batch, n_lat, d)
```

```python
import functools

import jax
import jax.numpy as jnp
from jax import lax
from jax.experimental import pallas as pl
from jax.experimental.pallas import tpu as pltpu

F32 = jnp.float32
BF16 = jnp.bfloat16
U32 = jnp.uint32
I32 = jnp.int32

GRID_W = 64
CHUNK = 128
HEAD_DIM = 128
EPS = 1e-6
ROPE_THETA = 10000.0
A_HEADS = 8
B_HEADS = 12
B_KV_HEADS = 4
B_GROUP = B_HEADS // B_KV_HEADS
C_HEADS = 12
C_Q_RANK = 768
C_KV_RANK = 512
C_NOPE = 128
C_ROPE = 64
C_V = 128
C_QK = C_NOPE + C_ROPE
A_WIDTH = A_HEADS * HEAD_DIM
B_WIDTH = B_HEADS * HEAD_DIM
C_WIDTH = C_HEADS * C_V
N_EXPERTS = 64
TOP_K = 6
N_GROUPS = 8
GROUP_SIZE = N_EXPERTS // N_GROUPS
TOPK_GROUPS = 4
ROUTED_SCALE = 2.5
N_MOD = 6

LANES = 128
SUBLANES = 8
ROW_TILE = 256
MM_TM = 512
MM_TN = 1024
GQA_TQ = 1024
MLA_TQ = 2048
ATT_TK = 512
LOG2E = 1.4426950408889634
BOUND_SLACK = 1.02
SCORE_BOUND_LIMIT = 60.0
EXP_TM = 256
DISP_T = 256
COMB_T = 128
COMB_ROWS = 32
IN_COLS_PAD = 6144
VMEM_LIMIT = 56 * 1024 * 1024


def _params(sem, vmem=VMEM_LIMIT):
    return pltpu.CompilerParams(dimension_semantics=sem, vmem_limit_bytes=vmem)


def _rms(x, w):
    return x * lax.rsqrt(jnp.mean(x * x, axis=-1, keepdims=True) + EPS) * w


def _pack_bf16_pair(a, b):
    ua = lax.bitcast_convert_type(a.astype(BF16).astype(F32), U32)
    ub = lax.bitcast_convert_type(b.astype(BF16).astype(F32), U32)
    return (ua >> 16) | ub


def _unpack_bf16_pair(u):
    a = lax.bitcast_convert_type(u << 16, F32)
    b = lax.bitcast_convert_type(u & jnp.uint32(0xFFFF0000), F32)
    return a, b


def _slab_rows(d):
    return d // 2 // LANES


def _slab_piece(ref, row0, n_tok, n_sl, a):
    return ref[pl.ds(row0 + a, n_tok, stride=n_sl), :]


def _store_slabs(ref, row0, v):
    n_tok, d = v.shape
    n_sl = _slab_rows(d)
    for a in range(n_sl):
        piece = _pack_bf16_pair(v[:, a * LANES:(a + 1) * LANES], v[:, d // 2 + a * LANES:d // 2 + (a + 1) * LANES])
        ref[pl.ds(row0 + a, n_tok, stride=n_sl), :] = piece


def _slab_weight_rows(w):
    d = w.shape[0]
    parts = []
    for a in range(_slab_rows(d)):
        parts += [w[a * LANES:(a + 1) * LANES], w[d // 2 + a * LANES:d // 2 + (a + 1) * LANES]]
    return parts


def _ffn_slabs(ref, row0, n_tok, n_sl, wg_ref, wu_ref, wd_ref):
    hg = None
    for a in range(n_sl):
        lo, hi = _unpack_bf16_pair(_slab_piece(ref, row0, n_tok, n_sl, a))
        xa = jnp.concatenate([lo.astype(BF16), hi.astype(BF16)], axis=1)
        rows = slice(2 * a * LANES, 2 * (a + 1) * LANES)
        g = jnp.dot(xa, wg_ref[rows, :], preferred_element_type=F32)
        u = jnp.dot(xa, wu_ref[rows, :], preferred_element_type=F32)
        hg, hu = (g, u) if hg is None else (hg + g, hu + u)
    hb = (jax.nn.silu(hg) * hu).astype(BF16)
    return jnp.dot(hb, wd_ref[...], preferred_element_type=F32)


def _ada_kernel(cb_ref, w_ref, b_ref, o_ref, silu_sc, *, n_rows, d, tn):
    nj = tn // LANES

    @pl.when(jnp.logical_and(pl.program_id(0) == 0, pl.program_id(1) == 0))
    def _():
        cv = cb_ref[...]
        silu_sc[...] = cv * jax.nn.sigmoid(cv)

    def body(kc, accs):
        k0 = pl.multiple_of(kc * SUBLANES, SUBLANES)
        wk = w_ref[0, pl.ds(k0, SUBLANES), :]
        out = []
        for r in range(n_rows):
            sv = silu_sc[r, pl.ds(k0, SUBLANES), :]
            for j in range(nj):
                out.append(accs[r * nj + j] + wk[:, j * LANES:(j + 1) * LANES] * sv)
        return tuple(out)

    init = tuple(jnp.zeros((SUBLANES, LANES), F32) for _ in range(n_rows * nj))
    accs = lax.fori_loop(0, d // SUBLANES, body, init, unroll=8)
    o_ref[0] = jnp.zeros((SUBLANES, tn), F32)
    for r in range(n_rows):
        for j in range(nj):
            o_ref[0, r:r + 1, j * LANES:(j + 1) * LANES] = (
                jnp.sum(accs[r * nj + j], axis=0, keepdims=True) + b_ref[0, :, j * LANES:(j + 1) * LANES])


def _ada_mods(c, c_ctx, ada_w, ada_b):
    n_layers, d, nd = ada_w.shape
    n_rows = c.shape[0] + 1
    assert n_rows <= SUBLANES
    cvec = jnp.concatenate([c, c_ctx[None]], axis=0)
    cb = jnp.broadcast_to(cvec[:, :, None], (n_rows, d, LANES))
    tn = 1024
    out = pl.pallas_call(
        functools.partial(_ada_kernel, n_rows=n_rows, d=d, tn=tn),
        grid=(n_layers, nd // tn),
        in_specs=[pl.BlockSpec((n_rows, d, LANES), lambda l, j: (0, 0, 0)),
                  pl.BlockSpec((1, d, tn), lambda l, j: (l, 0, j)),
                  pl.BlockSpec((1, 1, tn), lambda l, j: (l, 0, j))],
        out_specs=pl.BlockSpec((1, SUBLANES, tn), lambda l, j: (l, 0, j)),
        out_shape=jax.ShapeDtypeStruct((n_layers, SUBLANES, nd), F32),
        scratch_shapes=[pltpu.VMEM((n_rows, d, LANES), F32)],
        compiler_params=_params(("arbitrary", "arbitrary")),
        name="ada_mods",
    )(cb, ada_w, ada_b.reshape(n_layers, 1, nd))
    return out.reshape(n_layers, SUBLANES, N_MOD, 1, d)


def _mod_spec(layer, m, d, row_fn, tn=None, col_fn=None):
    if tn is None:
        return pl.BlockSpec((None, None, None, 1, d), lambda *a: (layer, row_fn(*a), m, 0, 0))
    return pl.BlockSpec((None, None, None, 1, tn), lambda *a: (layer, row_fn(*a), m, 0, col_fn(*a)))


def _in_proj_kernel(x_ref, nw_ref, shift_ref, scale_ref, w_ref, *rest):
    o_ref = rest[-1]
    y = _rms(x_ref[...], nw_ref[...])
    h = (y * (1.0 + scale_ref[...]) + shift_ref[...]).astype(BF16)
    o_ref[...] = jnp.dot(h, w_ref[...], preferred_element_type=F32).astype(o_ref.dtype)


def _alias_prev(in_specs, args, prev):
    if prev is None:
        return {}
    in_specs.append(pl.BlockSpec(memory_space=pl.ANY))
    args.append(prev)
    return {len(args) - 1: 0}


def _in_proj(x, norm_w, mods, layer, mod_row, w, tile0, total_rows, prev):
    m, d = x.shape
    n = w.shape[1]
    in_specs = [pl.BlockSpec((MM_TM, d), lambda j, i: (i, 0)),
                pl.BlockSpec((1, d), lambda j, i: (0, 0)),
                _mod_spec(layer, 0, d, lambda j, i: mod_row(i + tile0)),
                _mod_spec(layer, 1, d, lambda j, i: mod_row(i + tile0)),
                pl.BlockSpec((d, MM_TN), lambda j, i: (0, j))]
    args = [x, norm_w.reshape(1, d), mods, mods, w]
    aliases = _alias_prev(in_specs, args, prev)
    return pl.pallas_call(
        _in_proj_kernel,
        grid=(n // MM_TN, m // MM_TM),
        in_specs=in_specs,
        out_specs=pl.BlockSpec((MM_TM, MM_TN), lambda j, i: (i + tile0, j)),
        out_shape=jax.ShapeDtypeStruct((total_rows, n), BF16),
        input_output_aliases=aliases,
        compiler_params=_params(("arbitrary", "arbitrary")),
        name="in_proj",
    )(*args)


def _gmlp_kernel(p_ref, vn_ref, ws_ref, bt_ref, o_ref):
    for h in range(A_HEADS):
        hs = slice(h * HEAD_DIM, (h + 1) * HEAD_DIM)
        u = jax.nn.gelu(p_ref[:, hs].astype(F32))
        v = jax.nn.gelu(p_ref[:, A_WIDTH + h * HEAD_DIM:A_WIDTH + (h + 1) * HEAD_DIM].astype(F32))
        vb = _rms(v, vn_ref[:, hs]).astype(BF16)
        for c in range(ROW_TILE // CHUNK):
            cs = slice(c * CHUNK, (c + 1) * CHUNK)
            s = jnp.dot(ws_ref[h], vb[cs], preferred_element_type=F32) + bt_ref[:, h:h + 1]
            o_ref[cs, hs] = (u[cs] * s).astype(o_ref.dtype)


def _gmlp(proj, t_rows, v_norm, w_s, b_s):
    return pl.pallas_call(
        _gmlp_kernel,
        grid=(t_rows // ROW_TILE,),
        in_specs=[pl.BlockSpec((ROW_TILE, 2 * A_WIDTH), lambda i: (i, 0)),
                  pl.BlockSpec((1, A_WIDTH), lambda i: (0, 0)),
                  pl.BlockSpec((A_HEADS, CHUNK, CHUNK), lambda i: (0, 0, 0)),
                  pl.BlockSpec((CHUNK, A_HEADS), lambda i: (0, 0))],
        out_specs=pl.BlockSpec((ROW_TILE, A_WIDTH), lambda i: (i, 0)),
        out_shape=jax.ShapeDtypeStruct((t_rows, A_WIDTH), BF16),
        compiler_params=_params(("arbitrary",)),
        name="gmlp",
    )(proj, v_norm.reshape(1, A_WIDTH), w_s.astype(BF16), jnp.transpose(b_s))


def _rope_tables(n_lat, n_ctx, d_rot, reps):
    half = d_rot // 2
    nf = half // 2
    rows = n_lat // GRID_W
    tpos = jnp.arange(n_lat, dtype=I32)
    row = tpos // GRID_W - rows // 2
    col = tpos % GRID_W - GRID_W // 2
    freq = ROPE_THETA ** (-jnp.arange(nf, dtype=F32) / nf)
    ang_r = row.astype(F32)[:, None] * freq
    ang_c = col.astype(F32)[:, None] * freq
    cos = jnp.concatenate([jnp.cos(ang_r), jnp.cos(ang_r), jnp.cos(ang_c), jnp.cos(ang_c)], axis=1)
    sin = jnp.concatenate([-jnp.sin(ang_r), jnp.sin(ang_r), -jnp.sin(ang_c), jnp.sin(ang_c)], axis=1)
    cos = jnp.concatenate([cos, jnp.ones((n_ctx, d_rot), F32)], axis=0)
    sin = jnp.concatenate([sin, jnp.zeros((n_ctx, d_rot), F32)], axis=0)
    return jnp.tile(cos, (1, reps)), jnp.tile(sin, (1, reps))


def _rope(y, cos, sin, nf):
    lane = lax.broadcasted_iota(I32, y.shape, 1)
    partner = jnp.where((lane % (2 * nf)) < nf,
                        pltpu.roll(y, LANES - nf, 1), pltpu.roll(y, nf, 1))
    return y * cos + partner * sin


def _gqa_prep_kernel(p_ref, w_ref, cos_ref, sin_ref, o_ref):
    c = pl.program_id(1)
    heads = p_ref.shape[1] // HEAD_DIM

    @pl.when(c < 4)
    def _():
        for hh in range(heads):
            x = p_ref[:, hh * HEAD_DIM:(hh + 1) * HEAD_DIM].astype(F32)
            y = _rope(_rms(x, w_ref[...]), cos_ref[...], sin_ref[...], HEAD_DIM // 4)
            o_ref[hh] = y.astype(o_ref.dtype)

    @pl.when(c == 4)
    def _():
        for hh in range(heads):
            o_ref[hh] = p_ref[:, hh * HEAD_DIM:(hh + 1) * HEAD_DIM]


def _gqa_prep(proj, geom, q_norm, k_norm, cos, sin):
    batch, n_tok, b_fn, rb_fn = geom
    wvec = jnp.stack([q_norm * (HEAD_DIM ** -0.5 * LOG2E)] * 3 + [k_norm, jnp.ones_like(k_norm)])
    wvec = wvec.reshape(5, 1, HEAD_DIM)
    first = (2 * A_WIDTH) // 512
    return pl.pallas_call(
        _gqa_prep_kernel,
        grid=(proj.shape[0] // ROW_TILE, 5),
        in_specs=[pl.BlockSpec((ROW_TILE, 512), lambda i, c: (i, first + c)),
                  pl.BlockSpec((None, 1, HEAD_DIM), lambda i, c: (c, 0, 0)),
                  pl.BlockSpec((ROW_TILE, HEAD_DIM), lambda i, c: (rb_fn(i), 0)),
                  pl.BlockSpec((ROW_TILE, HEAD_DIM), lambda i, c: (rb_fn(i), 0))],
        out_specs=pl.BlockSpec((None, 4, ROW_TILE, HEAD_DIM), lambda i, c: (b_fn(i), c, rb_fn(i), 0)),
        out_shape=jax.ShapeDtypeStruct((batch, B_HEADS + 2 * B_KV_HEADS, n_tok, HEAD_DIM), BF16),
        compiler_params=_params(("arbitrary", "arbitrary")),
        name="gqa_prep",
    )(proj, wvec, cos, sin)


def _mla_prep_kernel(p_ref, qa_ref, kva_ref, wqn_ref, wqr_ref, wkk_ref, wkv_ref,
                     nqn_ref, nqr_ref, nkn_ref, nkr_ref, cos_ref, sin_ref,
                     qc_ref, kc_ref, vc_ref):
    cq = p_ref[:, :C_Q_RANK].astype(F32)
    ckv = p_ref[:, C_Q_RANK:C_Q_RANK + C_KV_RANK].astype(F32)
    kr = p_ref[:, C_Q_RANK + C_KV_RANK:C_Q_RANK + C_KV_RANK + LANES].astype(F32)
    cqn = _rms(cq, qa_ref[...]).astype(BF16)
    ckvn = _rms(ckv, kva_ref[...]).astype(BF16)
    qn = jnp.dot(cqn, wqn_ref[...], preferred_element_type=F32)
    qr = jnp.dot(cqn, wqr_ref[...], preferred_element_type=F32)
    kn = jnp.dot(ckvn, wkk_ref[...], preferred_element_type=F32)
    vv = jnp.dot(ckvn, wkv_ref[...], preferred_element_type=F32)
    cos = cos_ref[...]
    sin = sin_ref[...]
    lane = lax.broadcasted_iota(I32, kr.shape, 1)
    low = lane < C_ROPE
    inv_qk = 1.0 / C_QK

    kr_rot = _rope(kr * nkr_ref[...], cos, sin, C_ROPE // 4)
    kr_ss = jnp.sum(kr * kr, axis=-1, keepdims=True)
    for h in range(C_HEADS):
        hs = slice(h * C_NOPE, (h + 1) * C_NOPE)
        knh = kn[:, hs]
        r = lax.rsqrt((jnp.sum(knh * knh, axis=-1, keepdims=True) + kr_ss) * inv_qk + EPS)
        kc_ref[h, :, 0:C_NOPE] = (knh * r * nkn_ref[...]).astype(kc_ref.dtype)
        kc_ref[h, :, C_NOPE:C_QK] = (kr_rot * r)[:, 0:C_ROPE].astype(kc_ref.dtype)
        vc_ref[h] = vv[:, h * C_V:(h + 1) * C_V].astype(vc_ref.dtype)

    for pr in range(C_HEADS // 2):
        qrp = qr[:, pr * LANES:(pr + 1) * LANES]
        sq = qrp * qrp
        ss_pair = (jnp.sum(jnp.where(low, sq, 0.0), axis=-1, keepdims=True),
                   jnp.sum(jnp.where(low, 0.0, sq), axis=-1, keepdims=True))
        rs = []
        for e in range(2):
            h = 2 * pr + e
            qnh = qn[:, h * C_NOPE:(h + 1) * C_NOPE]
            r = lax.rsqrt((jnp.sum(qnh * qnh, axis=-1, keepdims=True) + ss_pair[e]) * inv_qk + EPS)
            rs.append(r)
            qc_ref[h, :, 0:C_NOPE] = (qnh * r * nqn_ref[...]).astype(qc_ref.dtype)
        q_rot = _rope(qrp * jnp.where(low, rs[0], rs[1]) * nqr_ref[...], cos, sin, C_ROPE // 4)
        qc_ref[2 * pr, :, C_NOPE:C_QK] = q_rot[:, 0:C_ROPE].astype(qc_ref.dtype)
        qc_ref[2 * pr + 1, :, C_NOPE:C_QK] = q_rot[:, C_ROPE:2 * C_ROPE].astype(qc_ref.dtype)


def _mla_prep(proj, geom, q_a_norm, w_uq, kv_a_norm, w_ukv, q_norm, k_norm, cos, sin):
    batch, n_tok, b_fn, rb_fn = geom
    first = (2 * A_WIDTH + B_WIDTH + 2 * B_KV_HEADS * HEAD_DIM)
    width = IN_COLS_PAD - first
    assert first % width == 0
    wq = w_uq.reshape(C_Q_RANK, C_HEADS, C_QK)
    wqn = wq[:, :, :C_NOPE].reshape(C_Q_RANK, C_HEADS * C_NOPE).astype(BF16)
    wqr = wq[:, :, C_NOPE:].reshape(C_Q_RANK, C_HEADS * C_ROPE).astype(BF16)
    wkv = w_ukv.reshape(C_KV_RANK, C_HEADS, C_NOPE + C_V)
    wkk = wkv[:, :, :C_NOPE].reshape(C_KV_RANK, C_HEADS * C_NOPE).astype(BF16)
    wkvv = wkv[:, :, C_NOPE:].reshape(C_KV_RANK, C_HEADS * C_V).astype(BF16)
    qs = C_QK ** -0.5 * LOG2E
    nqn = (q_norm[:C_NOPE] * qs).reshape(1, C_NOPE)
    nqr = jnp.tile(q_norm[C_NOPE:] * qs, 2).reshape(1, LANES)
    nkn = k_norm[:C_NOPE].reshape(1, C_NOPE)
    nkr = jnp.tile(k_norm[C_NOPE:], 2).reshape(1, LANES)
    const = lambda shape: pl.BlockSpec(shape, lambda i: tuple(0 for _ in shape))
    head_spec = lambda dd: pl.BlockSpec((None, C_HEADS, ROW_TILE, dd), lambda i: (b_fn(i), 0, rb_fn(i), 0))
    return pl.pallas_call(
        _mla_prep_kernel,
        grid=(proj.shape[0] // ROW_TILE,),
        in_specs=[pl.BlockSpec((ROW_TILE, width), lambda i: (i, first // width)),
                  const((1, C_Q_RANK)), const((1, C_KV_RANK)),
                  const(wqn.shape), const(wqr.shape), const(wkk.shape), const(wkvv.shape),
                  const((1, C_NOPE)), const((1, LANES)), const((1, C_NOPE)), const((1, LANES)),
                  pl.BlockSpec((ROW_TILE, LANES), lambda i: (rb_fn(i), 0)),
                  pl.BlockSpec((ROW_TILE, LANES), lambda i: (rb_fn(i), 0))],
        out_specs=[head_spec(C_QK), head_spec(C_QK), head_spec(C_V)],
        out_shape=[jax.ShapeDtypeStruct((batch, C_HEADS, n_tok, C_QK), BF16),
                   jax.ShapeDtypeStruct((batch, C_HEADS, n_tok, C_QK), BF16),
                   jax.ShapeDtypeStruct((batch, C_HEADS, n_tok, C_V), BF16)],
        compiler_params=_params(("arbitrary",)),
        name="mla_prep",
    )(proj, q_a_norm.reshape(1, C_Q_RANK), kv_a_norm.reshape(1, C_KV_RANK), wqn, wqr, wkk, wkvv,
      nqn, nqr, nkn, nkr, cos, sin)


def _attn_kernel(bound_ref, q_ref, k_ref, v_ref, *rest, group, n_lat):
    use_bound = bound_ref[0] <= SCORE_BOUND_LIMIT
    for bounded, pred in ((True, use_bound), (False, jnp.logical_not(use_bound))):
        pl.when(pred)(functools.partial(_attn_body, bound_ref, q_ref, k_ref, v_ref, rest,
                                        group=group, n_lat=n_lat, bounded=bounded))


def _attn_body(bound_ref, q_ref, k_ref, v_ref, rest, *, group, n_lat, bounded):
    o_ref, m_sc, l_sc, acc_sc = rest[-4:]
    tq, dk = q_ref.shape[1], q_ref.shape[2]
    dv = v_ref.shape[1]
    n_tail = k_ref.shape[0] - n_lat
    q = q_ref[...].reshape(group * tq, dk)
    l_sc[...] = jnp.zeros(l_sc.shape, F32)
    acc_sc[...] = jnp.zeros(acc_sc.shape, F32)
    if not bounded:
        m_sc[...] = jnp.full(m_sc.shape, -jnp.inf, F32)

    def lane_fold(p):
        out = p[:, 0:LANES]
        for j in range(1, p.shape[1] // LANES):
            out = out + p[:, j * LANES:(j + 1) * LANES]
        return out

    def chunk(start, size):
        k = k_ref[pl.ds(start, size), :]
        v = v_ref[pl.ds(start, size), :]
        s = lax.dot_general(q, k, (((1,), (1,)), ((), ())), preferred_element_type=F32)
        if bounded:
            p = jnp.exp2(s - bound_ref[0])
            l_sc[...] += lane_fold(p)
            acc_sc[...] += jnp.dot(p.astype(BF16), v, preferred_element_type=F32)
        else:
            m_prev = m_sc[...]
            m_new = jnp.maximum(m_prev, jnp.max(s, axis=-1, keepdims=True))
            alpha = jnp.exp2(m_prev - m_new)
            p = jnp.exp2(s - m_new[:, 0:1])
            l_sc[...] = alpha * l_sc[...] + lane_fold(p)
            acc_sc[...] = alpha[:, 0:1] * acc_sc[...] + jnp.dot(p.astype(BF16), v, preferred_element_type=F32)
            m_sc[...] = m_new

    if n_lat:
        def body(c, carry):
            chunk(pl.multiple_of(c * ATT_TK, ATT_TK), ATT_TK)
            return carry
        lax.fori_loop(0, n_lat // ATT_TK, body, 0, unroll=2)
    chunk(n_lat, n_tail)
    o = acc_sc[...] / jnp.sum(l_sc[...], axis=-1, keepdims=True)
    for g in range(group):
        o_ref[:, g * dv:(g + 1) * dv] = o[g * tq:(g + 1) * tq].astype(o_ref.dtype)


def _attention_call(bound, q_arr, k_arr, v_arr, prev_out, heads, n_kv, group, tq, n_lat, n_ctx, ctx_only,
                    t_rows, name):
    q_head0, k_head0, v_head0 = heads
    batch, _, n_tok, dk = q_arr.shape
    dv = v_arr.shape[-1]
    m = group * tq
    if ctx_only:
        assert tq == n_ctx and n_lat % n_ctx == 0
        nq, q0, key_rows, key_blk = 1, n_lat // tq, n_ctx, n_lat // n_ctx
        out_row = lambda b, qi: batch * (n_lat // tq) + b
    else:
        assert n_lat % tq == 0 and n_lat % ATT_TK == 0
        nq, q0, key_rows, key_blk = n_lat // tq, 0, n_tok, 0
        out_row = lambda b, qi: b * nq + qi
    in_specs = [pl.BlockSpec((None, group, tq, dk), lambda b, g, qi, *_: (b, q_head0 // group + g, q0 + qi, 0)),
                pl.BlockSpec((None, None, key_rows, dk), lambda b, g, qi, *_: (b, k_head0 + g, key_blk, 0)),
                pl.BlockSpec((None, None, key_rows, dv), lambda b, g, qi, *_: (b, v_head0 + g, key_blk, 0))]
    args = [bound, q_arr, k_arr, v_arr]
    aliases = {}
    if prev_out is not None:
        in_specs.append(pl.BlockSpec(memory_space=pl.ANY))
        args.append(prev_out)
        aliases = {len(args) - 1: 0}
    grid_spec = pltpu.PrefetchScalarGridSpec(
        num_scalar_prefetch=1,
        grid=(batch, n_kv, nq),
        in_specs=in_specs,
        out_specs=pl.BlockSpec((tq, group * dv), lambda b, g, qi, *_: (out_row(b, qi), g)),
        scratch_shapes=[pltpu.VMEM((m, LANES), F32), pltpu.VMEM((m, LANES), F32), pltpu.VMEM((m, dv), F32)],
    )
    return pl.pallas_call(
        functools.partial(_attn_kernel, group=group, n_lat=0 if ctx_only else n_lat),
        grid_spec=grid_spec,
        out_shape=jax.ShapeDtypeStruct((t_rows, n_kv * group * dv), BF16),
        input_output_aliases=aliases,
        compiler_params=_params(("arbitrary", "arbitrary", "arbitrary")),
        name=name,
    )(*args)


def _attention(bound, q_arr, k_arr, v_arr, heads, n_kv, group, tq, n_lat, n_ctx, with_ctx, t_rows, name):
    tq = min(tq, n_lat)
    out = _attention_call(bound, q_arr, k_arr, v_arr, None, heads, n_kv, group, tq, n_lat, n_ctx,
                          False, t_rows, name)
    if with_ctx:
        out = _attention_call(bound, q_arr, k_arr, v_arr, out, heads, n_kv, group, n_ctx, n_lat, n_ctx,
                              True, t_rows, name + "_ctx")
    return out


def _score_bound(q_norm, k_norm, dk):
    b = dk ** 0.5 * jnp.max(jnp.abs(q_norm)) * jnp.max(jnp.abs(k_norm)) * (LOG2E * BOUND_SLACK)
    return b.reshape(1).astype(F32)


def _wout_kernel(a_ref, b_ref, c_ref, w_ref, x_ref, g_ref, *rest):
    o_ref = rest[-1]
    acc = jnp.dot(a_ref[...], w_ref[0:A_WIDTH, :], preferred_element_type=F32)
    acc += jnp.dot(b_ref[...], w_ref[A_WIDTH:A_WIDTH + B_WIDTH, :], preferred_element_type=F32)
    acc += jnp.dot(c_ref[...], w_ref[A_WIDTH + B_WIDTH:, :], preferred_element_type=F32)
    o_ref[...] = x_ref[...] + g_ref[...] * acc


def _out_proj(a_out, b_out, c_out, w, x, n_tiles, mods, layer, mod_row, tile0, total_rows, prev):
    d = w.shape[1]
    in_specs = [pl.BlockSpec((MM_TM, A_WIDTH), lambda j, i: (i + tile0, 0)),
                pl.BlockSpec((MM_TM, B_WIDTH), lambda j, i: (i + tile0, 0)),
                pl.BlockSpec((MM_TM, C_WIDTH), lambda j, i: (i + tile0, 0)),
                pl.BlockSpec((w.shape[0], MM_TN), lambda j, i: (0, j)),
                pl.BlockSpec((MM_TM, MM_TN), lambda j, i: (i, j)),
                _mod_spec(layer, 2, d, lambda j, i: mod_row(i + tile0), MM_TN, lambda j, i: j)]
    args = [a_out, b_out, c_out, w, x, mods]
    aliases = _alias_prev(in_specs, args, prev)
    return pl.pallas_call(
        _wout_kernel,
        grid=(d // MM_TN, n_tiles),
        in_specs=in_specs,
        out_specs=pl.BlockSpec((MM_TM, MM_TN), lambda j, i: (i + tile0, j)),
        out_shape=jax.ShapeDtypeStruct((total_rows, d), F32),
        input_output_aliases=aliases,
        compiler_params=_params(("arbitrary", "arbitrary")),
        name="out_proj",
    )(*args)


def _first_index(hit_val, x, iota, big):
    return jnp.min(jnp.where(x == hit_val, iota, big), axis=0, keepdims=True)


def _router_kernel(x_ref, nw_ref, shift_ref, scale_ref, rwt_ref, bias_ref,
                   hp_ref, eidx_ref, gate_ref, rank_ref, cnt_ref, carry_sc):
    i = pl.program_id(0)
    tm, d = x_ref.shape
    h = _rms(x_ref[...], nw_ref[...]) * (1.0 + scale_ref[...]) + shift_ref[...]
    _store_slabs(hp_ref, 0, h)

    logits = lax.dot_general(rwt_ref[...], h, (((1,), (1,)), ((), ())),
                             precision=lax.Precision.HIGHEST, preferred_element_type=F32)
    scores = jax.nn.sigmoid(logits)
    sel = scores + bias_ref[:, 0:1]
    neg = jnp.float32(-jnp.inf)

    iota_n = lax.broadcasted_iota(I32, (N_GROUPS, tm), 0).astype(F32)
    iota_e = lax.broadcasted_iota(I32, (N_EXPERTS, tm), 0).astype(F32)

    gscore = jnp.zeros((N_GROUPS, tm), F32)
    for g in range(N_GROUPS):
        blk = sel[g * GROUP_SIZE:(g + 1) * GROUP_SIZE]
        m1 = jnp.max(blk, axis=0, keepdims=True)
        first = _first_index(m1, blk, iota_n, float(GROUP_SIZE))
        m2 = jnp.max(jnp.where(iota_n == first, neg, blk), axis=0, keepdims=True)
        gscore = jnp.where(iota_n == float(g), m1 + m2, gscore)

    gsel = jnp.zeros((N_GROUPS, tm), F32)
    work = gscore
    for _ in range(TOPK_GROUPS):
        m = jnp.max(work, axis=0, keepdims=True)
        hit = iota_n == _first_index(m, work, iota_n, float(N_GROUPS))
        gsel = jnp.where(hit, 1.0, gsel)
        work = jnp.where(hit, neg, work)
    emask = jnp.concatenate(
        [jnp.broadcast_to(gsel[g:g + 1], (GROUP_SIZE, tm)) for g in range(N_GROUPS)], axis=0)

    work = jnp.where(emask > 0.5, sel, neg)
    picks = []
    onehot = jnp.zeros((N_EXPERTS, tm), F32)
    for _ in range(TOP_K):
        m = jnp.max(work, axis=0, keepdims=True)
        pick = _first_index(m, work, iota_e, float(N_EXPERTS))
        picks.append(pick)
        hit = iota_e == pick
        onehot = jnp.where(hit, 1.0, onehot)
        work = jnp.where(hit, neg, work)

    w_un = onehot * scores
    gates = w_un / jnp.sum(w_un, axis=0, keepdims=True) * ROUTED_SCALE

    @pl.when(i == 0)
    def _():
        carry_sc[...] = jnp.zeros(carry_sc.shape, F32)

    before = (lax.broadcasted_iota(I32, (tm, tm), 0) < lax.broadcasted_iota(I32, (tm, tm), 1)).astype(BF16)
    rank = carry_sc[:, 0:1] + jnp.dot(onehot.astype(BF16), before, preferred_element_type=F32)
    carry_sc[...] = carry_sc[...] + jnp.sum(onehot, axis=1, keepdims=True)
    cnt_ref[...] = carry_sc[...]

    eidx_ref[...] = jnp.zeros(eidx_ref.shape, I32)
    gate_ref[...] = jnp.zeros(gate_ref.shape, F32)
    rank_ref[...] = jnp.zeros(rank_ref.shape, I32)
    for k in range(TOP_K):
        hit = iota_e == picks[k]
        eidx_ref[k:k + 1, :] = picks[k].astype(I32)
        gate_ref[k:k + 1, :] = jnp.sum(jnp.where(hit, gates, 0.0), axis=0, keepdims=True)
        rank_ref[k:k + 1, :] = jnp.sum(jnp.where(hit, rank, 0.0), axis=0, keepdims=True).astype(I32)


def _norm_router(x, t_rows, norm_w, mods, layer, mod_row, router_w, router_bias):
    d = x.shape[1]
    col = lambda dt: (pl.BlockSpec((SUBLANES, ROW_TILE), lambda i: (0, i)),
                      jax.ShapeDtypeStruct((SUBLANES, t_rows), dt))
    specs, shapes = zip(col(I32), col(F32), col(I32))
    return pl.pallas_call(
        _router_kernel,
        grid=(t_rows // ROW_TILE,),
        in_specs=[pl.BlockSpec((ROW_TILE, d), lambda i: (i, 0)),
                  pl.BlockSpec((1, d), lambda i: (0, 0)),
                  _mod_spec(layer, 3, d, mod_row),
                  _mod_spec(layer, 4, d, mod_row),
                  pl.BlockSpec((N_EXPERTS, d), lambda i: (0, 0)),
                  pl.BlockSpec((N_EXPERTS, LANES), lambda i: (0, 0))],
        out_specs=[pl.BlockSpec((ROW_TILE * _slab_rows(d), LANES), lambda i: (i, 0)), *specs,
                   pl.BlockSpec((N_EXPERTS, LANES), lambda i: (0, 0))],
        out_shape=[jax.ShapeDtypeStruct((t_rows * _slab_rows(d), LANES), U32), *shapes,
                   jax.ShapeDtypeStruct((N_EXPERTS, LANES), F32)],
        scratch_shapes=[pltpu.VMEM((N_EXPERTS, LANES), F32)],
        compiler_params=_params(("arbitrary",)),
        name="norm_router",
    )(x, norm_w.reshape(1, d), mods, mods, jnp.transpose(router_w),
      jnp.broadcast_to(router_bias[:, None], (N_EXPERTS, LANES)))


def _slab_copy(src, src_tok, dst, dst_tok, n_tok, n_sl, sem):
    rows = n_tok * n_sl
    first = lambda tok: tok * n_sl if isinstance(tok, int) else pl.multiple_of(tok * n_sl, n_sl)
    return pltpu.make_async_copy(src.at[pl.ds(first(src_tok), rows)], dst.at[pl.ds(first(dst_tok), rows)], sem)


_FILL_PIECES = tuple(1 << b for b in range(EXP_TM.bit_length() - 2, -1, -1))


def _zero_fill(pstart_ref, plen_ref, tail_ref, xs_hbm, zero_sc, n_sl, sem_z):
    zero_sc[...] = jnp.zeros(zero_sc.shape, zero_sc.dtype)
    n_tail = (xs_hbm.shape[0] // n_sl - tail_ref[0]) // _FILL_PIECES[0]

    def pad_copies(e, wait):
        off = pstart_ref[e]
        for piece in _FILL_PIECES:
            on = (plen_ref[e] & piece) != 0
            cp = _slab_copy(zero_sc, 0, xs_hbm, off, piece, n_sl, sem_z)

            @pl.when(on)
            def _():
                cp.wait() if wait else cp.start()

            off = off + jnp.where(on, piece, 0)

    def tail_copy(j, wait):
        cp = _slab_copy(zero_sc, 0, xs_hbm, tail_ref[0] + j * _FILL_PIECES[0], _FILL_PIECES[0], n_sl, sem_z)
        cp.wait() if wait else cp.start()

    for wait in (False, True):
        lax.fori_loop(0, N_EXPERTS, lambda e, c: (pad_copies(e, wait), c)[1], 0)
        lax.fori_loop(0, n_tail, lambda j, c: (tail_copy(j, wait), c)[1], 0)


def _dispatch_kernel(pstart_ref, plen_ref, tail_ref, slot_hbm, h_ref, xs_hbm, slot_sm, zero_sc, sem_s, sem, sem_z):
    i = pl.program_id(0)
    n = slot_sm.shape[0]
    n_sl = h_ref.shape[0] // DISP_T
    cp = pltpu.make_async_copy(slot_hbm.at[pl.ds(pl.multiple_of(i * n, n), n)], slot_sm, sem_s)
    cp.start()

    @pl.when(i == 0)
    def _():
        _zero_fill(pstart_ref, plen_ref, tail_ref, xs_hbm, zero_sc, n_sl, sem_z)

    cp.wait()

    def issue(t, carry):
        for k in range(TOP_K):
            _slab_copy(h_ref, t, xs_hbm, slot_sm[t * SUBLANES + k], 1, n_sl, sem).start(priority=k % 2)
        return carry

    lax.fori_loop(0, DISP_T, issue, 0)

    def drain(t, carry):
        for k in range(TOP_K):
            _slab_copy(h_ref, 0, xs_hbm, 0, 1, n_sl, sem).wait()
        return carry

    lax.fori_loop(0, DISP_T, drain, 0)


def _dispatch(hp, slots, pad_start, pad_len, tail_start, n_slots, n_sl):
    t = hp.shape[0] // n_sl
    grid_spec = pltpu.PrefetchScalarGridSpec(
        num_scalar_prefetch=3,
        grid=(t // DISP_T,),
        in_specs=[pl.BlockSpec(memory_space=pl.ANY), pl.BlockSpec((DISP_T * n_sl, LANES), lambda i, *_: (i, 0))],
        out_specs=pl.BlockSpec(memory_space=pl.ANY),
        scratch_shapes=[pltpu.SMEM((DISP_T * SUBLANES,), I32),
                        pltpu.VMEM((_FILL_PIECES[0] * n_sl, LANES), U32),
                        pltpu.SemaphoreType.DMA(()), pltpu.SemaphoreType.DMA(()), pltpu.SemaphoreType.DMA(())],
    )
    return pl.pallas_call(
        _dispatch_kernel,
        grid_spec=grid_spec,
        out_shape=jax.ShapeDtypeStruct((n_slots * n_sl, LANES), U32),
        compiler_params=_params(("arbitrary",)),
        name="moe_dispatch",
    )(pad_start, pad_len, tail_start, slots, hp)


def _expert_kernel(te_ref, par_ref, nxt_ref, nu_ref, xs_ref, wg_hbm, wu_hbm, wd_hbm, ys_ref,
                   wg_f, wu_f, wd_f, wg_sc, wu_sc, wd_sc, sem, *, layer):
    i = pl.program_id(0)
    expert = te_ref[i]
    par = par_ref[i]

    def weight_copies(e, slot):
        return (pltpu.make_async_copy(wg_hbm.at[layer, e], wg_f.at[slot], sem.at[slot, 0]),
                pltpu.make_async_copy(wu_hbm.at[layer, e], wu_f.at[slot], sem.at[slot, 1]),
                pltpu.make_async_copy(wd_hbm.at[layer, e], wd_f.at[slot], sem.at[slot, 2]))

    @pl.when(i == 0)
    def _():
        for cp in weight_copies(expert, par):
            cp.start()

    @pl.when(jnp.logical_or(i == 0, expert != te_ref[jnp.maximum(i - 1, 0)]))
    def _():
        for cp in weight_copies(expert, par):
            cp.wait()

        @pl.when(nxt_ref[i] >= 0)
        def _():
            for cp in weight_copies(nxt_ref[i], 1 - par):
                cp.start()

        d = wg_sc.shape[0]
        for a in range(_slab_rows(d)):
            for half, src0 in enumerate((a * LANES, d // 2 + a * LANES)):
                dst = slice((2 * a + half) * LANES, (2 * a + half + 1) * LANES)
                wg_sc[dst, :] = wg_f[par, src0:src0 + LANES, :].astype(BF16)
                wu_sc[dst, :] = wu_f[par, src0:src0 + LANES, :].astype(BF16)
        wd_sc[...] = wd_f[par].astype(BF16)

    @pl.when(i < nu_ref[0])
    def _():
        n_sl = xs_ref.shape[0] // EXP_TM
        y = _ffn_slabs(xs_ref, 0, EXP_TM, n_sl, wg_sc, wu_sc, wd_sc)
        _store_slabs(ys_ref, 0, y)

    @pl.when(i >= nu_ref[0])
    def _():
        ys_ref[...] = jnp.zeros(ys_ref.shape, ys_ref.dtype)


def _experts(xs, tile_expert, tile_slot, tile_next, n_used, w_gate, w_up, w_down, layer):
    _, _, d, f = w_gate.shape
    n_sl = _slab_rows(d)
    n_tiles = xs.shape[0] // n_sl // EXP_TM
    hbm = pl.BlockSpec(memory_space=pl.ANY)
    tile = (EXP_TM * n_sl, LANES)
    grid_spec = pltpu.PrefetchScalarGridSpec(
        num_scalar_prefetch=4,
        grid=(n_tiles,),
        in_specs=[pl.BlockSpec(tile, lambda i, te, par, nxt, nu: (jnp.minimum(i, nu[0] - 1), 0)),
                  hbm, hbm, hbm],
        out_specs=pl.BlockSpec(tile, lambda i, te, par, nxt, nu: (i, 0)),
        scratch_shapes=[pltpu.VMEM((2, d, f), F32), pltpu.VMEM((2, d, f), F32), pltpu.VMEM((2, f, d), F32),
                        pltpu.VMEM((d, f), BF16), pltpu.VMEM((d, f), BF16), pltpu.VMEM((f, d), BF16),
                        pltpu.SemaphoreType.DMA((2, 3))],
    )
    return pl.pallas_call(
        functools.partial(_expert_kernel, layer=layer),
        grid_spec=grid_spec,
        out_shape=jax.ShapeDtypeStruct(xs.shape, U32),
        compiler_params=_params(("arbitrary",)),
        name="moe_experts",
    )(tile_expert, tile_slot, tile_next, n_used, xs, w_gate, w_up, w_down)


def _combine_kernel(slot_hbm, ys_hbm, hp_ref, gate_ref, sg_ref, su_ref, sd_ref, x_ref, g5_ref,
                    o_ref, slot_sm, buf, y_sc, sem_s, sem):
    i = pl.program_id(0)
    n_steps = pl.num_programs(0)
    n = COMB_T * SUBLANES
    half = x_ref.shape[1] // 2

    def load_slots(h, half_index):
        cp = pltpu.make_async_copy(slot_hbm.at[pl.ds(pl.multiple_of(half_index * n, n), n)],
                                   slot_sm.at[h], sem_s.at[h])
        cp.start()
        cp.wait()

    n_sl = half // LANES

    def issue(h, tokens):
        for t in tokens:
            for k in range(TOP_K):
                _slab_copy(ys_hbm, slot_sm[h, t * SUBLANES + k], buf.at[h, k], t, 1, n_sl,
                           sem.at[h]).start(priority=k % 2)

    def drain(h):
        def body(t, carry):
            for k in range(TOP_K):
                _slab_copy(ys_hbm, 0, buf.at[h, k], 0, 1, n_sl, sem.at[h]).wait()
            return carry

        lax.fori_loop(0, COMB_T, body, 0)

    def finish(h, other):
        rows = slice(h * COMB_T, (h + 1) * COMB_T)
        issue(other, range(0, COMB_T // 2))
        y = _ffn_slabs(hp_ref, h * COMB_T * n_sl, COMB_T, n_sl, sg_ref, su_ref, sd_ref)
        drain(h)
        issue(other, range(COMB_T // 2, COMB_T))
        y_sc[...] = y
        for q in range(COMB_T // COMB_ROWS):
            tok = slice(h * COMB_T + q * COMB_ROWS, h * COMB_T + (q + 1) * COMB_ROWS)
            loc = slice(q * COMB_ROWS, (q + 1) * COMB_ROWS)
            gates = [gate_ref[tok, k:k + 1] for k in range(TOP_K)]
            for a in range(n_sl):
                lo_cols = slice(a * LANES, (a + 1) * LANES)
                hi_cols = slice(half + a * LANES, half + (a + 1) * LANES)
                r_lo = y_sc[loc, lo_cols]
                r_hi = y_sc[loc, hi_cols]
                for k in range(TOP_K):
                    piece = _slab_piece(buf.at[h, k], q * COMB_ROWS * n_sl, COMB_ROWS, n_sl, a)
                    e_lo, e_hi = _unpack_bf16_pair(piece)
                    r_lo = r_lo + gates[k] * e_lo
                    r_hi = r_hi + gates[k] * e_hi
                o_ref[tok, lo_cols] = x_ref[tok, lo_cols] + g5_ref[:, lo_cols] * r_lo
                o_ref[tok, hi_cols] = x_ref[tok, hi_cols] + g5_ref[:, hi_cols] * r_hi

    @pl.when(i == 0)
    def _():
        load_slots(0, 0)
        lax.fori_loop(0, COMB_T, lambda t, c: (issue(0, (t,)), c)[1], 0)

    load_slots(1, 2 * i + 1)
    finish(0, 1)
    load_slots(0, jnp.minimum(2 * i + 2, 2 * n_steps - 2))
    finish(1, 0)

    @pl.when(i == n_steps - 1)
    def _():
        drain(0)


def _combine(ys, hp, slots, gates, sh_gate, sh_up, sh_down, x, mods, layer, t_rows, mod_row):
    d = x.shape[1]
    n_sl = _slab_rows(d)
    f = sh_gate.shape[1]
    rows = 2 * COMB_T
    slab_order = lambda w: jnp.concatenate(_slab_weight_rows(w), axis=0).astype(BF16)
    return pl.pallas_call(
        _combine_kernel,
        grid=(t_rows // rows,),
        in_specs=[pl.BlockSpec(memory_space=pl.ANY), pl.BlockSpec(memory_space=pl.ANY),
                  pl.BlockSpec((rows * n_sl, LANES), lambda i: (i, 0)),
                  pl.BlockSpec((rows, SUBLANES), lambda i: (i, 0)),
                  pl.BlockSpec((d, f), lambda i: (0, 0)),
                  pl.BlockSpec((d, f), lambda i: (0, 0)),
                  pl.BlockSpec((f, d), lambda i: (0, 0)),
                  pl.BlockSpec((rows, d), lambda i: (i, 0)),
                  _mod_spec(layer, 5, d, lambda i: mod_row(i * rows // ROW_TILE))],
        out_specs=pl.BlockSpec((rows, d), lambda i: (i, 0)),
        out_shape=jax.ShapeDtypeStruct((t_rows, d), F32),
        scratch_shapes=[pltpu.SMEM((2, COMB_T * SUBLANES), I32),
                        pltpu.VMEM((2, TOP_K, COMB_T * n_sl, LANES), U32),
                        pltpu.VMEM((COMB_T, d), F32),
                        pltpu.SemaphoreType.DMA((2,)), pltpu.SemaphoreType.DMA((2,))],
        compiler_params=_params(("arbitrary",)),
        name="moe_combine",
    )(slots, ys, hp, gates, slab_order(sh_gate), slab_order(sh_up), sh_down.astype(BF16), x, mods)


def _moe(x, t_rows, norm_w, mods, layer, mod_row, router_w, router_bias,
         w_gate, w_up, w_down, sh_gate, sh_up, sh_down):
    hp, eidx, gate, rank, cnt = _norm_router(x, t_rows, norm_w, mods, layer, mod_row, router_w, router_bias)

    n_tiles = (t_rows * TOP_K + N_EXPERTS * (EXP_TM - 1) + EXP_TM - 1) // EXP_TM
    counts = cnt[:, 0].astype(I32)
    padded = (counts + EXP_TM - 1) // EXP_TM * EXP_TM
    ends = jnp.cumsum(padded)
    base = ends - padded
    experts = jnp.arange(N_EXPERTS, dtype=I32)
    base_of = jnp.sum(jnp.where(eidx[:, :, None] == experts, base, 0), axis=-1)
    slots = jnp.transpose(base_of + rank).reshape(-1)
    n_used = (ends[-1] // EXP_TM).astype(I32)
    tile_start = jnp.arange(n_tiles, dtype=I32) * EXP_TM
    last_start = (n_used - 1) * EXP_TM
    te = jnp.sum(jnp.minimum(tile_start, last_start)[:, None] >= ends[None, :], axis=1).astype(I32)
    te = jnp.minimum(te, N_EXPERTS - 1)
    used = counts > 0
    slot_of = ((jnp.cumsum(used.astype(I32)) - 1) % 2).astype(I32)
    later = jnp.logical_and(experts[None, :] > experts[:, None], used[None, :])
    next_of = jnp.min(jnp.where(later, experts[None, :], N_EXPERTS), axis=1)
    next_of = jnp.where(next_of == N_EXPERTS, -1, next_of).astype(I32)

    n_sl = _slab_rows(x.shape[1])
    assert n_sl % SUBLANES == 0
    xs = _dispatch(hp, slots, base + counts, padded - counts, ends[-1:], n_tiles * EXP_TM, n_sl)
    ys = _experts(xs, te, slot_of[te], next_of[te], n_used.reshape(1), w_gate, w_up, w_down, layer)
    return _combine(ys, hp, slots, jnp.transpose(gate), sh_gate, sh_up, sh_down, x, mods, layer, t_rows, mod_row)


def kernel(x, c, ctx, c_ctx, ada_w, ada_b, norm1_w, norm2_w, w_in, gm_v_norm, gm_w_s, gm_b_s, gqa_q_norm, gqa_k_norm, mla_q_a_norm, mla_w_uq, mla_kv_a_norm, mla_w_ukv, mla_q_norm, mla_k_norm, w_out, router_w, router_bias, exp_w_gate, exp_w_up, exp_w_down, sh_w_gate, sh_w_up, sh_w_down):
    batch, n_lat, d = x.shape
    n_ctx = ctx.shape[1]
    depth = ada_w.shape[0]
    assert n_ctx == ROW_TILE and n_lat % MM_TM == 0 and (batch * n_ctx) % MM_TM == 0
    t_lat = batch * n_lat
    t_all = t_lat + batch * n_ctx
    n_tok = n_lat + n_ctx
    lat_tiles = n_lat // ROW_TILE

    mod_row = lambda i: jnp.minimum(i // lat_tiles, batch)
    is_lat = lambda i: i < batch * lat_tiles
    b_fn = lambda i: jnp.where(is_lat(i), i // lat_tiles, i - batch * lat_tiles)
    rb_fn = lambda i: jnp.where(is_lat(i), i % lat_tiles, lat_tiles)
    geom = (batch, n_tok, b_fn, rb_fn)
    mm_mod_row = lambda i: mod_row(i * (MM_TM // ROW_TILE))

    mods = _ada_mods(c, c_ctx, ada_w, ada_b)
    cos_b, sin_b = _rope_tables(n_lat, n_ctx, HEAD_DIM, 1)
    cos_c, sin_c = _rope_tables(n_lat, n_ctx, C_ROPE, LANES // C_ROPE)

    lat_mm_tiles = t_lat // MM_TM
    streams = [(x.reshape(t_lat, d), 0), (ctx.reshape(batch * n_ctx, d), lat_mm_tiles)]
    for l in range(depth):
        need_ctx = l < depth - 1
        t_rows = t_all if need_ctx else t_lat
        w_in_l = jnp.pad(w_in[l].astype(BF16), ((0, 0), (0, IN_COLS_PAD - w_in.shape[2])))

        proj = None
        for rows, tile0 in streams:
            proj = _in_proj(rows, norm1_w[l], mods, l, mm_mod_row, w_in_l, tile0, t_all, proj)
        a_out = _gmlp(proj, t_rows, gm_v_norm[l], gm_w_s[l], gm_b_s[l])
        qkv = _gqa_prep(proj, geom, gqa_q_norm[l], gqa_k_norm[l], cos_b, sin_b)
        b_out = _attention(_score_bound(gqa_q_norm[l], gqa_k_norm[l], HEAD_DIM), qkv, qkv, qkv,
                           (0, B_HEADS, B_HEADS + B_KV_HEADS), B_KV_HEADS, B_GROUP, GQA_TQ,
                           n_lat, n_ctx, need_ctx, t_rows, "gqa_attn")
        qc, kc, vc = _mla_prep(proj, geom, mla_q_a_norm[l], mla_w_uq[l], mla_kv_a_norm[l], mla_w_ukv[l],
                               mla_q_norm[l], mla_k_norm[l], cos_c, sin_c)
        c_out = _attention(_score_bound(mla_q_norm[l], mla_k_norm[l], C_QK), qc, kc, vc,
                           (0, 0, 0), C_HEADS, 1, MLA_TQ, n_lat, n_ctx, need_ctx, t_rows, "mla_attn")
        xs = None
        for rows, tile0 in streams:
            n_tiles = min(rows.shape[0], t_rows - tile0 * MM_TM) // MM_TM
            if n_tiles > 0:
                xs = _out_proj(a_out, b_out, c_out, w_out[l].astype(BF16), rows, n_tiles, mods, l, mm_mod_row,
                               tile0, t_rows, xs)
        xs = _moe(xs, t_rows, norm2_w[l], mods, l, mod_row, router_w[l], router_bias[l],
                  exp_w_gate, exp_w_up, exp_w_down, sh_w_gate[l], sh_w_up[l], sh_w_down[l])
        streams = [(xs, 0)]
    return xs[:t_lat].reshape(batch, n_lat, d)
```

```python
import functools

import jax
import jax.numpy as jnp
from jax import lax
from jax.experimental import pallas as pl
from jax.experimental.pallas import tpu as pltpu

F32 = jnp.float32
BF16 = jnp.bfloat16
U32 = jnp.uint32
I32 = jnp.int32

GRID_W = 64
CHUNK = 128
HEAD_DIM = 128
EPS = 1e-6
ROPE_THETA = 10000.0
A_HEADS = 8
B_HEADS = 12
B_KV_HEADS = 4
B_GROUP = B_HEADS // B_KV_HEADS
C_HEADS = 12
C_Q_RANK = 768
C_KV_RANK = 512
C_NOPE = 128
C_ROPE = 64
C_V = 128
C_QK = C_NOPE + C_ROPE
A_WIDTH = A_HEADS * HEAD_DIM
B_WIDTH = B_HEADS * HEAD_DIM
C_WIDTH = C_HEADS * C_V
N_EXPERTS = 64
TOP_K = 6
N_GROUPS = 8
GROUP_SIZE = N_EXPERTS // N_GROUPS
TOPK_GROUPS = 4
ROUTED_SCALE = 2.5
N_MOD = 6

LANES = 128
SUBLANES = 8
ROW_TILE = 256
MM_TM = 512
MM_TN = 1024
GQA_TQ = 1024
MLA_TQ = 2048
ATT_TK = 512
LOG2E = 1.4426950408889634
BOUND_SLACK = 1.02
SCORE_BOUND_LIMIT = 60.0
EXP_TM = 256
DISP_T = 256
COMB_T = 128
COMB_COLS = 128
IN_COLS_PAD = 6144
VMEM_LIMIT = 56 * 1024 * 1024


def _params(sem, vmem=VMEM_LIMIT):
    return pltpu.CompilerParams(dimension_semantics=sem, vmem_limit_bytes=vmem)


def _rms(x, w):
    return x * lax.rsqrt(jnp.mean(x * x, axis=-1, keepdims=True) + EPS) * w


def _pack_bf16_pair(a, b):
    ua = lax.bitcast_convert_type(a.astype(BF16).astype(F32), U32)
    ub = lax.bitcast_convert_type(b.astype(BF16).astype(F32), U32)
    return (ua >> 16) | ub


def _unpack_bf16_pair(u):
    a = lax.bitcast_convert_type(u << 16, F32)
    b = lax.bitcast_convert_type(u & jnp.uint32(0xFFFF0000), F32)
    return a, b


def _ffn(lo, hi, wg, wu, wd):
    half = lo.shape[1]
    hg = jnp.dot(lo, wg[:half], preferred_element_type=F32) + jnp.dot(hi, wg[half:], preferred_element_type=F32)
    hu = jnp.dot(lo, wu[:half], preferred_element_type=F32) + jnp.dot(hi, wu[half:], preferred_element_type=F32)
    hb = (jax.nn.silu(hg) * hu).astype(BF16)
    return jnp.dot(hb, wd, preferred_element_type=F32)


def _ada_kernel(cb_ref, w_ref, b_ref, o_ref, silu_sc, *, n_rows, d, tn):
    nj = tn // LANES

    @pl.when(jnp.logical_and(pl.program_id(0) == 0, pl.program_id(1) == 0))
    def _():
        cv = cb_ref[...]
        silu_sc[...] = cv * jax.nn.sigmoid(cv)

    def body(kc, accs):
        k0 = pl.multiple_of(kc * SUBLANES, SUBLANES)
        wk = w_ref[0, pl.ds(k0, SUBLANES), :]
        out = []
        for r in range(n_rows):
            sv = silu_sc[r, pl.ds(k0, SUBLANES), :]
            for j in range(nj):
                out.append(accs[r * nj + j] + wk[:, j * LANES:(j + 1) * LANES] * sv)
        return tuple(out)

    init = tuple(jnp.zeros((SUBLANES, LANES), F32) for _ in range(n_rows * nj))
    accs = lax.fori_loop(0, d // SUBLANES, body, init, unroll=8)
    o_ref[0] = jnp.zeros((SUBLANES, tn), F32)
    for r in range(n_rows):
        for j in range(nj):
            o_ref[0, r:r + 1, j * LANES:(j + 1) * LANES] = (
                jnp.sum(accs[r * nj + j], axis=0, keepdims=True) + b_ref[0, :, j * LANES:(j + 1) * LANES])


def _ada_mods(c, c_ctx, ada_w, ada_b):
    n_layers, d, nd = ada_w.shape
    n_rows = c.shape[0] + 1
    assert n_rows <= SUBLANES
    cvec = jnp.concatenate([c, c_ctx[None]], axis=0)
    cb = jnp.broadcast_to(cvec[:, :, None], (n_rows, d, LANES))
    tn = 1024
    out = pl.pallas_call(
        functools.partial(_ada_kernel, n_rows=n_rows, d=d, tn=tn),
        grid=(n_layers, nd // tn),
        in_specs=[pl.BlockSpec((n_rows, d, LANES), lambda l, j: (0, 0, 0)),
                  pl.BlockSpec((1, d, tn), lambda l, j: (l, 0, j)),
                  pl.BlockSpec((1, 1, tn), lambda l, j: (l, 0, j))],
        out_specs=pl.BlockSpec((1, SUBLANES, tn), lambda l, j: (l, 0, j)),
        out_shape=jax.ShapeDtypeStruct((n_layers, SUBLANES, nd), F32),
        scratch_shapes=[pltpu.VMEM((n_rows, d, LANES), F32)],
        compiler_params=_params(("arbitrary", "arbitrary")),
        name="ada_mods",
    )(cb, ada_w, ada_b.reshape(n_layers, 1, nd))
    return out.reshape(n_layers, SUBLANES, N_MOD, 1, d)


def _mod_spec(layer, m, d, row_fn, tn=None, col_fn=None):
    if tn is None:
        return pl.BlockSpec((None, None, None, 1, d), lambda *a: (layer, row_fn(*a), m, 0, 0))
    return pl.BlockSpec((None, None, None, 1, tn), lambda *a: (layer, row_fn(*a), m, 0, col_fn(*a)))


def _in_proj_kernel(x_ref, nw_ref, shift_ref, scale_ref, w_ref, *rest):
    o_ref = rest[-1]
    y = _rms(x_ref[...], nw_ref[...])
    h = (y * (1.0 + scale_ref[...]) + shift_ref[...]).astype(BF16)
    o_ref[...] = jnp.dot(h, w_ref[...], preferred_element_type=F32).astype(o_ref.dtype)


def _alias_prev(in_specs, args, prev):
    if prev is None:
        return {}
    in_specs.append(pl.BlockSpec(memory_space=pl.ANY))
    args.append(prev)
    return {len(args) - 1: 0}


def _in_proj(x, norm_w, mods, layer, mod_row, w, tile0, total_rows, prev):
    m, d = x.shape
    n = w.shape[1]
    in_specs = [pl.BlockSpec((MM_TM, d), lambda j, i: (i, 0)),
                pl.BlockSpec((1, d), lambda j, i: (0, 0)),
                _mod_spec(layer, 0, d, lambda j, i: mod_row(i + tile0)),
                _mod_spec(layer, 1, d, lambda j, i: mod_row(i + tile0)),
                pl.BlockSpec((d, MM_TN), lambda j, i: (0, j))]
    args = [x, norm_w.reshape(1, d), mods, mods, w]
    aliases = _alias_prev(in_specs, args, prev)
    return pl.pallas_call(
        _in_proj_kernel,
        grid=(n // MM_TN, m // MM_TM),
        in_specs=in_specs,
        out_specs=pl.BlockSpec((MM_TM, MM_TN), lambda j, i: (i + tile0, j)),
        out_shape=jax.ShapeDtypeStruct((total_rows, n), BF16),
        input_output_aliases=aliases,
        compiler_params=_params(("arbitrary", "arbitrary")),
        name="in_proj",
    )(*args)


def _gmlp_kernel(p_ref, vn_ref, ws_ref, bt_ref, o_ref):
    for h in range(A_HEADS):
        hs = slice(h * HEAD_DIM, (h + 1) * HEAD_DIM)
        u = jax.nn.gelu(p_ref[:, hs].astype(F32))
        v = jax.nn.gelu(p_ref[:, A_WIDTH + h * HEAD_DIM:A_WIDTH + (h + 1) * HEAD_DIM].astype(F32))
        vb = _rms(v, vn_ref[:, hs]).astype(BF16)
        for c in range(ROW_TILE // CHUNK):
            cs = slice(c * CHUNK, (c + 1) * CHUNK)
            s = jnp.dot(ws_ref[h], vb[cs], preferred_element_type=F32) + bt_ref[:, h:h + 1]
            o_ref[cs, hs] = (u[cs] * s).astype(o_ref.dtype)


def _gmlp(proj, t_rows, v_norm, w_s, b_s):
    return pl.pallas_call(
        _gmlp_kernel,
        grid=(t_rows // ROW_TILE,),
        in_specs=[pl.BlockSpec((ROW_TILE, 2 * A_WIDTH), lambda i: (i, 0)),
                  pl.BlockSpec((1, A_WIDTH), lambda i: (0, 0)),
                  pl.BlockSpec((A_HEADS, CHUNK, CHUNK), lambda i: (0, 0, 0)),
                  pl.BlockSpec((CHUNK, A_HEADS), lambda i: (0, 0))],
        out_specs=pl.BlockSpec((ROW_TILE, A_WIDTH), lambda i: (i, 0)),
        out_shape=jax.ShapeDtypeStruct((t_rows, A_WIDTH), BF16),
        compiler_params=_params(("arbitrary",)),
        name="gmlp",
    )(proj, v_norm.reshape(1, A_WIDTH), w_s.astype(BF16), jnp.transpose(b_s))


def _rope_tables(n_lat, n_ctx, d_rot, reps):
    half = d_rot // 2
    nf = half // 2
    rows = n_lat // GRID_W
    tpos = jnp.arange(n_lat, dtype=I32)
    row = tpos // GRID_W - rows // 2
    col = tpos % GRID_W - GRID_W // 2
    freq = ROPE_THETA ** (-jnp.arange(nf, dtype=F32) / nf)
    ang_r = row.astype(F32)[:, None] * freq
    ang_c = col.astype(F32)[:, None] * freq
    cos = jnp.concatenate([jnp.cos(ang_r), jnp.cos(ang_r), jnp.cos(ang_c), jnp.cos(ang_c)], axis=1)
    sin = jnp.concatenate([-jnp.sin(ang_r), jnp.sin(ang_r), -jnp.sin(ang_c), jnp.sin(ang_c)], axis=1)
    cos = jnp.concatenate([cos, jnp.ones((n_ctx, d_rot), F32)], axis=0)
    sin = jnp.concatenate([sin, jnp.zeros((n_ctx, d_rot), F32)], axis=0)
    return jnp.tile(cos, (1, reps)), jnp.tile(sin, (1, reps))


def _rope(y, cos, sin, nf):
    lane = lax.broadcasted_iota(I32, y.shape, 1)
    partner = jnp.where((lane % (2 * nf)) < nf,
                        pltpu.roll(y, LANES - nf, 1), pltpu.roll(y, nf, 1))
    return y * cos + partner * sin


def _gqa_prep_kernel(p_ref, w_ref, cos_ref, sin_ref, o_ref):
    c = pl.program_id(1)
    heads = p_ref.shape[1] // HEAD_DIM

    @pl.when(c < 4)
    def _():
        for hh in range(heads):
            x = p_ref[:, hh * HEAD_DIM:(hh + 1) * HEAD_DIM].astype(F32)
            y = _rope(_rms(x, w_ref[...]), cos_ref[...], sin_ref[...], HEAD_DIM // 4)
            o_ref[hh] = y.astype(o_ref.dtype)

    @pl.when(c == 4)
    def _():
        for hh in range(heads):
            o_ref[hh] = p_ref[:, hh * HEAD_DIM:(hh + 1) * HEAD_DIM]


def _gqa_prep(proj, geom, q_norm, k_norm, cos, sin):
    batch, n_tok, b_fn, rb_fn = geom
    wvec = jnp.stack([q_norm * (HEAD_DIM ** -0.5 * LOG2E)] * 3 + [k_norm, jnp.ones_like(k_norm)])
    wvec = wvec.reshape(5, 1, HEAD_DIM)
    first = (2 * A_WIDTH) // 512
    return pl.pallas_call(
        _gqa_prep_kernel,
        grid=(proj.shape[0] // ROW_TILE, 5),
        in_specs=[pl.BlockSpec((ROW_TILE, 512), lambda i, c: (i, first + c)),
                  pl.BlockSpec((None, 1, HEAD_DIM), lambda i, c: (c, 0, 0)),
                  pl.BlockSpec((ROW_TILE, HEAD_DIM), lambda i, c: (rb_fn(i), 0)),
                  pl.BlockSpec((ROW_TILE, HEAD_DIM), lambda i, c: (rb_fn(i), 0))],
        out_specs=pl.BlockSpec((None, 4, ROW_TILE, HEAD_DIM), lambda i, c: (b_fn(i), c, rb_fn(i), 0)),
        out_shape=jax.ShapeDtypeStruct((batch, B_HEADS + 2 * B_KV_HEADS, n_tok, HEAD_DIM), BF16),
        compiler_params=_params(("arbitrary", "arbitrary")),
        name="gqa_prep",
    )(proj, wvec, cos, sin)


def _mla_prep_kernel(p_ref, qa_ref, kva_ref, wqn_ref, wqr_ref, wkk_ref, wkv_ref,
                     nqn_ref, nqr_ref, nkn_ref, nkr_ref, cos_ref, sin_ref,
                     qc_ref, kc_ref, vc_ref):
    cq = p_ref[:, :C_Q_RANK].astype(F32)
    ckv = p_ref[:, C_Q_RANK:C_Q_RANK + C_KV_RANK].astype(F32)
    kr = p_ref[:, C_Q_RANK + C_KV_RANK:C_Q_RANK + C_KV_RANK + LANES].astype(F32)
    cqn = _rms(cq, qa_ref[...]).astype(BF16)
    ckvn = _rms(ckv, kva_ref[...]).astype(BF16)
    qn = jnp.dot(cqn, wqn_ref[...], preferred_element_type=F32)
    qr = jnp.dot(cqn, wqr_ref[...], preferred_element_type=F32)
    kn = jnp.dot(ckvn, wkk_ref[...], preferred_element_type=F32)
    vv = jnp.dot(ckvn, wkv_ref[...], preferred_element_type=F32)
    cos = cos_ref[...]
    sin = sin_ref[...]
    lane = lax.broadcasted_iota(I32, kr.shape, 1)
    low = lane < C_ROPE
    inv_qk = 1.0 / C_QK

    kr_rot = _rope(kr * nkr_ref[...], cos, sin, C_ROPE // 4)
    kr_ss = jnp.sum(kr * kr, axis=-1, keepdims=True)
    for h in range(C_HEADS):
        hs = slice(h * C_NOPE, (h + 1) * C_NOPE)
        knh = kn[:, hs]
        r = lax.rsqrt((jnp.sum(knh * knh, axis=-1, keepdims=True) + kr_ss) * inv_qk + EPS)
        kc_ref[h, :, 0:C_NOPE] = (knh * r * nkn_ref[...]).astype(kc_ref.dtype)
        kc_ref[h, :, C_NOPE:C_QK] = (kr_rot * r)[:, 0:C_ROPE].astype(kc_ref.dtype)
        vc_ref[h] = vv[:, h * C_V:(h + 1) * C_V].astype(vc_ref.dtype)

    for pr in range(C_HEADS // 2):
        qrp = qr[:, pr * LANES:(pr + 1) * LANES]
        sq = qrp * qrp
        ss_pair = (jnp.sum(jnp.where(low, sq, 0.0), axis=-1, keepdims=True),
                   jnp.sum(jnp.where(low, 0.0, sq), axis=-1, keepdims=True))
        rs = []
        for e in range(2):
            h = 2 * pr + e
            qnh = qn[:, h * C_NOPE:(h + 1) * C_NOPE]
            r = lax.rsqrt((jnp.sum(qnh * qnh, axis=-1, keepdims=True) + ss_pair[e]) * inv_qk + EPS)
            rs.append(r)
            qc_ref[h, :, 0:C_NOPE] = (qnh * r * nqn_ref[...]).astype(qc_ref.dtype)
        q_rot = _rope(qrp * jnp.where(low, rs[0], rs[1]) * nqr_ref[...], cos, sin, C_ROPE // 4)
        qc_ref[2 * pr, :, C_NOPE:C_QK] = q_rot[:, 0:C_ROPE].astype(qc_ref.dtype)
        qc_ref[2 * pr + 1, :, C_NOPE:C_QK] = q_rot[:, C_ROPE:2 * C_ROPE].astype(qc_ref.dtype)


def _mla_prep(proj, geom, q_a_norm, w_uq, kv_a_norm, w_ukv, q_norm, k_norm, cos, sin):
    batch, n_tok, b_fn, rb_fn = geom
    first = (2 * A_WIDTH + B_WIDTH + 2 * B_KV_HEADS * HEAD_DIM)
    width = IN_COLS_PAD - first
    assert first % width == 0
    wq = w_uq.reshape(C_Q_RANK, C_HEADS, C_QK)
    wqn = wq[:, :, :C_NOPE].reshape(C_Q_RANK, C_HEADS * C_NOPE).astype(BF16)
    wqr = wq[:, :, C_NOPE:].reshape(C_Q_RANK, C_HEADS * C_ROPE).astype(BF16)
    wkv = w_ukv.reshape(C_KV_RANK, C_HEADS, C_NOPE + C_V)
    wkk = wkv[:, :, :C_NOPE].reshape(C_KV_RANK, C_HEADS * C_NOPE).astype(BF16)
    wkvv = wkv[:, :, C_NOPE:].reshape(C_KV_RANK, C_HEADS * C_V).astype(BF16)
    qs = C_QK ** -0.5 * LOG2E
    nqn = (q_norm[:C_NOPE] * qs).reshape(1, C_NOPE)
    nqr = jnp.tile(q_norm[C_NOPE:] * qs, 2).reshape(1, LANES)
    nkn = k_norm[:C_NOPE].reshape(1, C_NOPE)
    nkr = jnp.tile(k_norm[C_NOPE:], 2).reshape(1, LANES)
    const = lambda shape: pl.BlockSpec(shape, lambda i: tuple(0 for _ in shape))
    head_spec = lambda dd: pl.BlockSpec((None, C_HEADS, ROW_TILE, dd), lambda i: (b_fn(i), 0, rb_fn(i), 0))
    return pl.pallas_call(
        _mla_prep_kernel,
        grid=(proj.shape[0] // ROW_TILE,),
        in_specs=[pl.BlockSpec((ROW_TILE, width), lambda i: (i, first // width)),
                  const((1, C_Q_RANK)), const((1, C_KV_RANK)),
                  const(wqn.shape), const(wqr.shape), const(wkk.shape), const(wkvv.shape),
                  const((1, C_NOPE)), const((1, LANES)), const((1, C_NOPE)), const((1, LANES)),
                  pl.BlockSpec((ROW_TILE, LANES), lambda i: (rb_fn(i), 0)),
                  pl.BlockSpec((ROW_TILE, LANES), lambda i: (rb_fn(i), 0))],
        out_specs=[head_spec(C_QK), head_spec(C_QK), head_spec(C_V)],
        out_shape=[jax.ShapeDtypeStruct((batch, C_HEADS, n_tok, C_QK), BF16),
                   jax.ShapeDtypeStruct((batch, C_HEADS, n_tok, C_QK), BF16),
                   jax.ShapeDtypeStruct((batch, C_HEADS, n_tok, C_V), BF16)],
        compiler_params=_params(("arbitrary",)),
        name="mla_prep",
    )(proj, q_a_norm.reshape(1, C_Q_RANK), kv_a_norm.reshape(1, C_KV_RANK), wqn, wqr, wkk, wkvv,
      nqn, nqr, nkn, nkr, cos, sin)


def _attn_kernel(bound_ref, q_ref, k_ref, v_ref, *rest, group, n_lat):
    use_bound = bound_ref[0] <= SCORE_BOUND_LIMIT
    for bounded, pred in ((True, use_bound), (False, jnp.logical_not(use_bound))):
        pl.when(pred)(functools.partial(_attn_body, bound_ref, q_ref, k_ref, v_ref, rest,
                                        group=group, n_lat=n_lat, bounded=bounded))


def _attn_body(bound_ref, q_ref, k_ref, v_ref, rest, *, group, n_lat, bounded):
    o_ref, m_sc, l_sc, acc_sc = rest[-4:]
    tq, dk = q_ref.shape[1], q_ref.shape[2]
    dv = v_ref.shape[1]
    n_tail = k_ref.shape[0] - n_lat
    q = q_ref[...].reshape(group * tq, dk)
    l_sc[...] = jnp.zeros(l_sc.shape, F32)
    acc_sc[...] = jnp.zeros(acc_sc.shape, F32)
    if not bounded:
        m_sc[...] = jnp.full(m_sc.shape, -jnp.inf, F32)

    def lane_fold(p):
        out = p[:, 0:LANES]
        for j in range(1, p.shape[1] // LANES):
            out = out + p[:, j * LANES:(j + 1) * LANES]
        return out

    def chunk(start, size):
        k = k_ref[pl.ds(start, size), :]
        v = v_ref[pl.ds(start, size), :]
        s = lax.dot_general(q, k, (((1,), (1,)), ((), ())), preferred_element_type=F32)
        if bounded:
            p = jnp.exp2(s - bound_ref[0])
            l_sc[...] += lane_fold(p)
            acc_sc[...] += jnp.dot(p.astype(BF16), v, preferred_element_type=F32)
        else:
            m_prev = m_sc[...]
            m_new = jnp.maximum(m_prev, jnp.max(s, axis=-1, keepdims=True))
            alpha = jnp.exp2(m_prev - m_new)
            p = jnp.exp2(s - m_new[:, 0:1])
            l_sc[...] = alpha * l_sc[...] + lane_fold(p)
            acc_sc[...] = alpha[:, 0:1] * acc_sc[...] + jnp.dot(p.astype(BF16), v, preferred_element_type=F32)
            m_sc[...] = m_new

    if n_lat:
        def body(c, carry):
            chunk(pl.multiple_of(c * ATT_TK, ATT_TK), ATT_TK)
            return carry
        lax.fori_loop(0, n_lat // ATT_TK, body, 0, unroll=2)
    chunk(n_lat, n_tail)
    o = acc_sc[...] / jnp.sum(l_sc[...], axis=-1, keepdims=True)
    for g in range(group):
        o_ref[:, g * dv:(g + 1) * dv] = o[g * tq:(g + 1) * tq].astype(o_ref.dtype)


def _attention_call(bound, q_arr, k_arr, v_arr, prev_out, heads, n_kv, group, tq, n_lat, n_ctx, ctx_only,
                    t_rows, name):
    q_head0, k_head0, v_head0 = heads
    batch, _, n_tok, dk = q_arr.shape
    dv = v_arr.shape[-1]
    m = group * tq
    if ctx_only:
        assert tq == n_ctx and n_lat % n_ctx == 0
        nq, q0, key_rows, key_blk = 1, n_lat // tq, n_ctx, n_lat // n_ctx
        out_row = lambda b, qi: batch * (n_lat // tq) + b
    else:
        assert n_lat % tq == 0 and n_lat % ATT_TK == 0
        nq, q0, key_rows, key_blk = n_lat // tq, 0, n_tok, 0
        out_row = lambda b, qi: b * nq + qi
    in_specs = [pl.BlockSpec((None, group, tq, dk), lambda b, g, qi, *_: (b, q_head0 // group + g, q0 + qi, 0)),
                pl.BlockSpec((None, None, key_rows, dk), lambda b, g, qi, *_: (b, k_head0 + g, key_blk, 0)),
                pl.BlockSpec((None, None, key_rows, dv), lambda b, g, qi, *_: (b, v_head0 + g, key_blk, 0))]
    args = [bound, q_arr, k_arr, v_arr]
    aliases = {}
    if prev_out is not None:
        in_specs.append(pl.BlockSpec(memory_space=pl.ANY))
        args.append(prev_out)
        aliases = {len(args) - 1: 0}
    grid_spec = pltpu.PrefetchScalarGridSpec(
        num_scalar_prefetch=1,
        grid=(batch, n_kv, nq),
        in_specs=in_specs,
        out_specs=pl.BlockSpec((tq, group * dv), lambda b, g, qi, *_: (out_row(b, qi), g)),
        scratch_shapes=[pltpu.VMEM((m, LANES), F32), pltpu.VMEM((m, LANES), F32), pltpu.VMEM((m, dv), F32)],
    )
    return pl.pallas_call(
        functools.partial(_attn_kernel, group=group, n_lat=0 if ctx_only else n_lat),
        grid_spec=grid_spec,
        out_shape=jax.ShapeDtypeStruct((t_rows, n_kv * group * dv), BF16),
        input_output_aliases=aliases,
        compiler_params=_params(("arbitrary", "arbitrary", "arbitrary")),
        name=name,
    )(*args)


def _attention(bound, q_arr, k_arr, v_arr, heads, n_kv, group, tq, n_lat, n_ctx, with_ctx, t_rows, name):
    tq = min(tq, n_lat)
    out = _attention_call(bound, q_arr, k_arr, v_arr, None, heads, n_kv, group, tq, n_lat, n_ctx,
                          False, t_rows, name)
    if with_ctx:
        out = _attention_call(bound, q_arr, k_arr, v_arr, out, heads, n_kv, group, n_ctx, n_lat, n_ctx,
                              True, t_rows, name + "_ctx")
    return out


def _score_bound(q_norm, k_norm, dk):
    b = dk ** 0.5 * jnp.max(jnp.abs(q_norm)) * jnp.max(jnp.abs(k_norm)) * (LOG2E * BOUND_SLACK)
    return b.reshape(1).astype(F32)


def _wout_kernel(a_ref, b_ref, c_ref, w_ref, x_ref, g_ref, *rest):
    o_ref = rest[-1]
    acc = jnp.dot(a_ref[...], w_ref[0:A_WIDTH, :], preferred_element_type=F32)
    acc += jnp.dot(b_ref[...], w_ref[A_WIDTH:A_WIDTH + B_WIDTH, :], preferred_element_type=F32)
    acc += jnp.dot(c_ref[...], w_ref[A_WIDTH + B_WIDTH:, :], preferred_element_type=F32)
    o_ref[...] = x_ref[...] + g_ref[...] * acc


def _out_proj(a_out, b_out, c_out, w, x, n_tiles, mods, layer, mod_row, tile0, total_rows, prev):
    d = w.shape[1]
    in_specs = [pl.BlockSpec((MM_TM, A_WIDTH), lambda j, i: (i + tile0, 0)),
                pl.BlockSpec((MM_TM, B_WIDTH), lambda j, i: (i + tile0, 0)),
                pl.BlockSpec((MM_TM, C_WIDTH), lambda j, i: (i + tile0, 0)),
                pl.BlockSpec((w.shape[0], MM_TN), lambda j, i: (0, j)),
                pl.BlockSpec((MM_TM, MM_TN), lambda j, i: (i, j)),
                _mod_spec(layer, 2, d, lambda j, i: mod_row(i + tile0), MM_TN, lambda j, i: j)]
    args = [a_out, b_out, c_out, w, x, mods]
    aliases = _alias_prev(in_specs, args, prev)
    return pl.pallas_call(
        _wout_kernel,
        grid=(d // MM_TN, n_tiles),
        in_specs=in_specs,
        out_specs=pl.BlockSpec((MM_TM, MM_TN), lambda j, i: (i + tile0, j)),
        out_shape=jax.ShapeDtypeStruct((total_rows, d), F32),
        input_output_aliases=aliases,
        compiler_params=_params(("arbitrary", "arbitrary")),
        name="out_proj",
    )(*args)


def _first_index(hit_val, x, iota, big):
    return jnp.min(jnp.where(x == hit_val, iota, big), axis=0, keepdims=True)


def _router_kernel(x_ref, nw_ref, shift_ref, scale_ref, rwt_ref, bias_ref,
                   hp_ref, eidx_ref, gate_ref, rank_ref, cnt_ref, carry_sc):
    i = pl.program_id(0)
    tm, d = x_ref.shape
    h = _rms(x_ref[...], nw_ref[...]) * (1.0 + scale_ref[...]) + shift_ref[...]
    hp_ref[...] = _pack_bf16_pair(h[:, :d // 2], h[:, d // 2:])

    logits = lax.dot_general(rwt_ref[...], h, (((1,), (1,)), ((), ())),
                             precision=lax.Precision.HIGHEST, preferred_element_type=F32)
    scores = jax.nn.sigmoid(logits)
    sel = scores + bias_ref[:, 0:1]
    neg = jnp.float32(-jnp.inf)

    iota_n = lax.broadcasted_iota(I32, (N_GROUPS, tm), 0).astype(F32)
    iota_e = lax.broadcasted_iota(I32, (N_EXPERTS, tm), 0).astype(F32)

    gscore = jnp.zeros((N_GROUPS, tm), F32)
    for g in range(N_GROUPS):
        blk = sel[g * GROUP_SIZE:(g + 1) * GROUP_SIZE]
        m1 = jnp.max(blk, axis=0, keepdims=True)
        first = _first_index(m1, blk, iota_n, float(GROUP_SIZE))
        m2 = jnp.max(jnp.where(iota_n == first, neg, blk), axis=0, keepdims=True)
        gscore = jnp.where(iota_n == float(g), m1 + m2, gscore)

    gsel = jnp.zeros((N_GROUPS, tm), F32)
    work = gscore
    for _ in range(TOPK_GROUPS):
        m = jnp.max(work, axis=0, keepdims=True)
        hit = iota_n == _first_index(m, work, iota_n, float(N_GROUPS))
        gsel = jnp.where(hit, 1.0, gsel)
        work = jnp.where(hit, neg, work)
    emask = jnp.concatenate(
        [jnp.broadcast_to(gsel[g:g + 1], (GROUP_SIZE, tm)) for g in range(N_GROUPS)], axis=0)

    work = jnp.where(emask > 0.5, sel, neg)
    picks = []
    onehot = jnp.zeros((N_EXPERTS, tm), F32)
    for _ in range(TOP_K):
        m = jnp.max(work, axis=0, keepdims=True)
        pick = _first_index(m, work, iota_e, float(N_EXPERTS))
        picks.append(pick)
        hit = iota_e == pick
        onehot = jnp.where(hit, 1.0, onehot)
        work = jnp.where(hit, neg, work)

    w_un = onehot * scores
    gates = w_un / jnp.sum(w_un, axis=0, keepdims=True) * ROUTED_SCALE

    @pl.when(i == 0)
    def _():
        carry_sc[...] = jnp.zeros(carry_sc.shape, F32)

    before = (lax.broadcasted_iota(I32, (tm, tm), 0) < lax.broadcasted_iota(I32, (tm, tm), 1)).astype(BF16)
    rank = carry_sc[:, 0:1] + jnp.dot(onehot.astype(BF16), before, preferred_element_type=F32)
    carry_sc[...] = carry_sc[...] + jnp.sum(onehot, axis=1, keepdims=True)
    cnt_ref[...] = carry_sc[...]

    eidx_ref[...] = jnp.zeros(eidx_ref.shape, I32)
    gate_ref[...] = jnp.zeros(gate_ref.shape, F32)
    rank_ref[...] = jnp.zeros(rank_ref.shape, I32)
    for k in range(TOP_K):
        hit = iota_e == picks[k]
        eidx_ref[k:k + 1, :] = picks[k].astype(I32)
        gate_ref[k:k + 1, :] = jnp.sum(jnp.where(hit, gates, 0.0), axis=0, keepdims=True)
        rank_ref[k:k + 1, :] = jnp.sum(jnp.where(hit, rank, 0.0), axis=0, keepdims=True).astype(I32)


def _norm_router(x, t_rows, norm_w, mods, layer, mod_row, router_w, router_bias):
    d = x.shape[1]
    col = lambda dt: (pl.BlockSpec((SUBLANES, ROW_TILE), lambda i: (0, i)),
                      jax.ShapeDtypeStruct((SUBLANES, t_rows), dt))
    specs, shapes = zip(col(I32), col(F32), col(I32))
    return pl.pallas_call(
        _router_kernel,
        grid=(t_rows // ROW_TILE,),
        in_specs=[pl.BlockSpec((ROW_TILE, d), lambda i: (i, 0)),
                  pl.BlockSpec((1, d), lambda i: (0, 0)),
                  _mod_spec(layer, 3, d, mod_row),
                  _mod_spec(layer, 4, d, mod_row),
                  pl.BlockSpec((N_EXPERTS, d), lambda i: (0, 0)),
                  pl.BlockSpec((N_EXPERTS, LANES), lambda i: (0, 0))],
        out_specs=[pl.BlockSpec((ROW_TILE, d // 2), lambda i: (i, 0)), *specs,
                   pl.BlockSpec((N_EXPERTS, LANES), lambda i: (0, 0))],
        out_shape=[jax.ShapeDtypeStruct((t_rows, d // 2), U32), *shapes,
                   jax.ShapeDtypeStruct((N_EXPERTS, LANES), F32)],
        scratch_shapes=[pltpu.VMEM((N_EXPERTS, LANES), F32)],
        compiler_params=_params(("arbitrary",)),
        name="norm_router",
    )(x, norm_w.reshape(1, d), mods, mods, jnp.transpose(router_w),
      jnp.broadcast_to(router_bias[:, None], (N_EXPERTS, LANES)))


def _row_copy(src, src_row, dst, dst_row, sem):
    return pltpu.make_async_copy(src.at[pl.ds(src_row, 1)], dst.at[pl.ds(dst_row, 1)], sem)


_FILL_PIECES = tuple(1 << b for b in range(EXP_TM.bit_length() - 2, -1, -1))


def _zero_fill(pstart_ref, plen_ref, tail_ref, xs_hbm, zero_sc, sem_z):
    zero_sc[...] = jnp.zeros(zero_sc.shape, zero_sc.dtype)
    n_tail = (xs_hbm.shape[0] - tail_ref[0]) // _FILL_PIECES[0]

    def pad_copies(e, wait):
        start = pstart_ref[e]
        head = (-start) & (SUBLANES - 1)
        for r in range(SUBLANES - 1):
            cp = pltpu.make_async_copy(zero_sc.at[pl.ds(0, 1)], xs_hbm.at[pl.ds(start + r, 1)], sem_z)

            @pl.when(r < head)
            def _():
                cp.wait() if wait else cp.start()

        off = start + head
        rem = plen_ref[e] - head
        for piece in _FILL_PIECES:
            if piece < SUBLANES:
                break
            on = (rem & piece) != 0
            cp = pltpu.make_async_copy(zero_sc.at[pl.ds(0, piece)],
                                       xs_hbm.at[pl.ds(pl.multiple_of(off, SUBLANES), piece)], sem_z)

            @pl.when(on)
            def _():
                cp.wait() if wait else cp.start()

            off = off + jnp.where(on, piece, 0)

    def tail_copy(j, wait):
        row = pl.multiple_of(tail_ref[0] + j * _FILL_PIECES[0], _FILL_PIECES[0])
        cp = pltpu.make_async_copy(zero_sc, xs_hbm.at[pl.ds(row, _FILL_PIECES[0])], sem_z)
        cp.wait() if wait else cp.start()

    for wait in (False, True):
        lax.fori_loop(0, N_EXPERTS, lambda e, c: (pad_copies(e, wait), c)[1], 0)
        lax.fori_loop(0, n_tail, lambda j, c: (tail_copy(j, wait), c)[1], 0)


def _dispatch_kernel(pstart_ref, plen_ref, tail_ref, slot_hbm, h_ref, xs_hbm, slot_sm, zero_sc, sem_s, sem, sem_z):
    i = pl.program_id(0)
    n = slot_sm.shape[0]
    cp = pltpu.make_async_copy(slot_hbm.at[pl.ds(pl.multiple_of(i * n, n), n)], slot_sm, sem_s)
    cp.start()

    @pl.when(i == 0)
    def _():
        _zero_fill(pstart_ref, plen_ref, tail_ref, xs_hbm, zero_sc, sem_z)

    cp.wait()

    def issue(t, carry):
        for k in range(TOP_K):
            _row_copy(h_ref, t, xs_hbm, slot_sm[t * SUBLANES + k], sem).start(priority=k % 2)
        return carry

    lax.fori_loop(0, DISP_T, issue, 0)

    def drain(t, carry):
        for k in range(TOP_K):
            _row_copy(h_ref, 0, xs_hbm, 0, sem).wait()
        return carry

    lax.fori_loop(0, DISP_T, drain, 0)


def _dispatch(hp, slots, pad_start, pad_len, tail_start, n_slots):
    t, w = hp.shape
    grid_spec = pltpu.PrefetchScalarGridSpec(
        num_scalar_prefetch=3,
        grid=(t // DISP_T,),
        in_specs=[pl.BlockSpec(memory_space=pl.ANY), pl.BlockSpec((DISP_T, w), lambda i, *_: (i, 0))],
        out_specs=pl.BlockSpec(memory_space=pl.ANY),
        scratch_shapes=[pltpu.SMEM((DISP_T * SUBLANES,), I32),
                        pltpu.VMEM((_FILL_PIECES[0], w), U32),
                        pltpu.SemaphoreType.DMA(()), pltpu.SemaphoreType.DMA(()), pltpu.SemaphoreType.DMA(())],
    )
    return pl.pallas_call(
        _dispatch_kernel,
        grid_spec=grid_spec,
        out_shape=jax.ShapeDtypeStruct((n_slots, w), U32),
        compiler_params=_params(("arbitrary",)),
        name="moe_dispatch",
    )(pad_start, pad_len, tail_start, slots, hp)


def _expert_kernel(te_ref, par_ref, nxt_ref, nu_ref, xs_ref, wg_hbm, wu_hbm, wd_hbm, ys_ref,
                   wg_f, wu_f, wd_f, wg_sc, wu_sc, wd_sc, sem, *, layer):
    i = pl.program_id(0)
    expert = te_ref[i]
    par = par_ref[i]

    def weight_copies(e, slot):
        return (pltpu.make_async_copy(wg_hbm.at[layer, e], wg_f.at[slot], sem.at[slot, 0]),
                pltpu.make_async_copy(wu_hbm.at[layer, e], wu_f.at[slot], sem.at[slot, 1]),
                pltpu.make_async_copy(wd_hbm.at[layer, e], wd_f.at[slot], sem.at[slot, 2]))

    @pl.when(i == 0)
    def _():
        for cp in weight_copies(expert, par):
            cp.start()

    @pl.when(jnp.logical_or(i == 0, expert != te_ref[jnp.maximum(i - 1, 0)]))
    def _():
        for cp in weight_copies(expert, par):
            cp.wait()

        @pl.when(nxt_ref[i] >= 0)
        def _():
            for cp in weight_copies(nxt_ref[i], 1 - par):
                cp.start()

        wg_sc[...] = wg_f[par].astype(BF16)
        wu_sc[...] = wu_f[par].astype(BF16)
        wd_sc[...] = wd_f[par].astype(BF16)

    @pl.when(i < nu_ref[0])
    def _():
        lo, hi = _unpack_bf16_pair(xs_ref[...])
        y = _ffn(lo.astype(BF16), hi.astype(BF16), wg_sc[...], wu_sc[...], wd_sc[...])
        half = y.shape[1] // 2
        ys_ref[...] = _pack_bf16_pair(y[:, :half], y[:, half:])

    @pl.when(i >= nu_ref[0])
    def _():
        ys_ref[...] = jnp.zeros(ys_ref.shape, ys_ref.dtype)


def _experts(xs, tile_expert, tile_slot, tile_next, n_used, w_gate, w_up, w_down, layer):
    _, _, d, f = w_gate.shape
    n_tiles = xs.shape[0] // EXP_TM
    hbm = pl.BlockSpec(memory_space=pl.ANY)
    tile = (EXP_TM, xs.shape[1])
    grid_spec = pltpu.PrefetchScalarGridSpec(
        num_scalar_prefetch=4,
        grid=(n_tiles,),
        in_specs=[pl.BlockSpec(tile, lambda i, te, par, nxt, nu: (jnp.minimum(i, nu[0] - 1), 0)),
                  hbm, hbm, hbm],
        out_specs=pl.BlockSpec(tile, lambda i, te, par, nxt, nu: (i, 0)),
        scratch_shapes=[pltpu.VMEM((2, d, f), F32), pltpu.VMEM((2, d, f), F32), pltpu.VMEM((2, f, d), F32),
                        pltpu.VMEM((d, f), BF16), pltpu.VMEM((d, f), BF16), pltpu.VMEM((f, d), BF16),
                        pltpu.SemaphoreType.DMA((2, 3))],
    )
    return pl.pallas_call(
        functools.partial(_expert_kernel, layer=layer),
        grid_spec=grid_spec,
        out_shape=jax.ShapeDtypeStruct(xs.shape, U32),
        compiler_params=_params(("arbitrary",)),
        name="moe_experts",
    )(tile_expert, tile_slot, tile_next, n_used, xs, w_gate, w_up, w_down)


def _combine_kernel(slot_hbm, ys_hbm, hp_ref, gate_ref, sg_ref, su_ref, sd_ref, x_ref, g5_ref,
                    o_ref, slot_sm, buf, sem_s, sem):
    i = pl.program_id(0)
    n_steps = pl.num_programs(0)
    n = COMB_T * SUBLANES
    half = x_ref.shape[1] // 2

    def load_slots(h, half_index):
        cp = pltpu.make_async_copy(slot_hbm.at[pl.ds(pl.multiple_of(half_index * n, n), n)],
                                   slot_sm.at[h], sem_s.at[h])
        cp.start()
        cp.wait()

    def issue(h, tokens):
        for t in tokens:
            for k in range(TOP_K):
                _row_copy(ys_hbm, slot_sm[h, t * SUBLANES + k], buf.at[h, k], t, sem.at[h]).start(priority=k % 2)

    def drain(h):
        def body(t, carry):
            for k in range(TOP_K):
                _row_copy(ys_hbm, 0, buf.at[h, k], 0, sem.at[h]).wait()
            return carry

        lax.fori_loop(0, COMB_T, body, 0)

    def finish(h, other):
        rows = slice(h * COMB_T, (h + 1) * COMB_T)
        issue(other, range(0, COMB_T // 2))
        lo, hi = _unpack_bf16_pair(hp_ref[rows, :])
        y = _ffn(lo.astype(BF16), hi.astype(BF16), sg_ref[...], su_ref[...], sd_ref[...])
        drain(h)
        issue(other, range(COMB_T // 2, COMB_T))
        gates = [gate_ref[rows, k:k + 1] for k in range(TOP_K)]
        for c in range(half // COMB_COLS):
            lo_cols = slice(c * COMB_COLS, (c + 1) * COMB_COLS)
            hi_cols = slice(half + c * COMB_COLS, half + (c + 1) * COMB_COLS)
            r_lo = y[:, lo_cols]
            r_hi = y[:, hi_cols]
            for k in range(TOP_K):
                e_lo, e_hi = _unpack_bf16_pair(buf[h, k, :, lo_cols])
                r_lo = r_lo + gates[k] * e_lo
                r_hi = r_hi + gates[k] * e_hi
            o_ref[rows, lo_cols] = x_ref[rows, lo_cols] + g5_ref[:, lo_cols] * r_lo
            o_ref[rows, hi_cols] = x_ref[rows, hi_cols] + g5_ref[:, hi_cols] * r_hi

    @pl.when(i == 0)
    def _():
        load_slots(0, 0)
        lax.fori_loop(0, COMB_T, lambda t, c: (issue(0, (t,)), c)[1], 0)

    load_slots(1, 2 * i + 1)
    finish(0, 1)
    load_slots(0, jnp.minimum(2 * i + 2, 2 * n_steps - 2))
    finish(1, 0)

    @pl.when(i == n_steps - 1)
    def _():
        drain(0)


def _combine(ys, hp, slots, gates, sh_gate, sh_up, sh_down, x, mods, layer, t_rows, mod_row):
    d = x.shape[1]
    w = hp.shape[1]
    f = sh_gate.shape[1]
    rows = 2 * COMB_T
    return pl.pallas_call(
        _combine_kernel,
        grid=(t_rows // rows,),
        in_specs=[pl.BlockSpec(memory_space=pl.ANY), pl.BlockSpec(memory_space=pl.ANY),
                  pl.BlockSpec((rows, w), lambda i: (i, 0)),
                  pl.BlockSpec((rows, SUBLANES), lambda i: (i, 0)),
                  pl.BlockSpec((d, f), lambda i: (0, 0)),
                  pl.BlockSpec((d, f), lambda i: (0, 0)),
                  pl.BlockSpec((f, d), lambda i: (0, 0)),
                  pl.BlockSpec((rows, d), lambda i: (i, 0)),
                  _mod_spec(layer, 5, d, lambda i: mod_row(i * rows // ROW_TILE))],
        out_specs=pl.BlockSpec((rows, d), lambda i: (i, 0)),
        out_shape=jax.ShapeDtypeStruct((t_rows, d), F32),
        scratch_shapes=[pltpu.SMEM((2, COMB_T * SUBLANES), I32),
                        pltpu.VMEM((2, TOP_K, COMB_T, w), U32),
                        pltpu.SemaphoreType.DMA((2,)), pltpu.SemaphoreType.DMA((2,))],
        compiler_params=_params(("arbitrary",)),
        name="moe_combine",
    )(slots, ys, hp, gates, sh_gate.astype(BF16), sh_up.astype(BF16), sh_down.astype(BF16), x, mods)


def _moe(x, t_rows, norm_w, mods, layer, mod_row, router_w, router_bias,
         w_gate, w_up, w_down, sh_gate, sh_up, sh_down):
    hp, eidx, gate, rank, cnt = _norm_router(x, t_rows, norm_w, mods, layer, mod_row, router_w, router_bias)

    n_tiles = (t_rows * TOP_K + N_EXPERTS * (EXP_TM - 1) + EXP_TM - 1) // EXP_TM
    counts = cnt[:, 0].astype(I32)
    padded = (counts + EXP_TM - 1) // EXP_TM * EXP_TM
    ends = jnp.cumsum(padded)
    base = ends - padded
    experts = jnp.arange(N_EXPERTS, dtype=I32)
    base_of = jnp.sum(jnp.where(eidx[:, :, None] == experts, base, 0), axis=-1)
    slots = jnp.transpose(base_of + rank).reshape(-1)
    n_used = (ends[-1] // EXP_TM).astype(I32)
    tile_start = jnp.arange(n_tiles, dtype=I32) * EXP_TM
    last_start = (n_used - 1) * EXP_TM
    te = jnp.sum(jnp.minimum(tile_start, last_start)[:, None] >= ends[None, :], axis=1).astype(I32)
    te = jnp.minimum(te, N_EXPERTS - 1)
    used = counts > 0
    slot_of = ((jnp.cumsum(used.astype(I32)) - 1) % 2).astype(I32)
    later = jnp.logical_and(experts[None, :] > experts[:, None], used[None, :])
    next_of = jnp.min(jnp.where(later, experts[None, :], N_EXPERTS), axis=1)
    next_of = jnp.where(next_of == N_EXPERTS, -1, next_of).astype(I32)

    xs = _dispatch(hp, slots, base + counts, padded - counts, ends[-1:], n_tiles * EXP_TM)
    ys = _experts(xs, te, slot_of[te], next_of[te], n_used.reshape(1), w_gate, w_up, w_down, layer)
    return _combine(ys, hp, slots, jnp.transpose(gate), sh_gate, sh_up, sh_down, x, mods, layer, t_rows, mod_row)


def kernel(x, c, ctx, c_ctx, ada_w, ada_b, norm1_w, norm2_w, w_in, gm_v_norm, gm_w_s, gm_b_s, gqa_q_norm, gqa_k_norm, mla_q_a_norm, mla_w_uq, mla_kv_a_norm, mla_w_ukv, mla_q_norm, mla_k_norm, w_out, router_w, router_bias, exp_w_gate, exp_w_up, exp_w_down, sh_w_gate, sh_w_up, sh_w_down):
    batch, n_lat, d = x.shape
    n_ctx = ctx.shape[1]
    depth = ada_w.shape[0]
    assert n_ctx == ROW_TILE and n_lat % MM_TM == 0 and (batch * n_ctx) % MM_TM == 0
    t_lat = batch * n_lat
    t_all = t_lat + batch * n_ctx
    n_tok = n_lat + n_ctx
    lat_tiles = n_lat // ROW_TILE

    mod_row = lambda i: jnp.minimum(i // lat_tiles, batch)
    is_lat = lambda i: i < batch * lat_tiles
    b_fn = lambda i: jnp.where(is_lat(i), i // lat_tiles, i - batch * lat_tiles)
    rb_fn = lambda i: jnp.where(is_lat(i), i % lat_tiles, lat_tiles)
    geom = (batch, n_tok, b_fn, rb_fn)
    mm_mod_row = lambda i: mod_row(i * (MM_TM // ROW_TILE))

    mods = _ada_mods(c, c_ctx, ada_w, ada_b)
    cos_b, sin_b = _rope_tables(n_lat, n_ctx, HEAD_DIM, 1)
    cos_c, sin_c = _rope_tables(n_lat, n_ctx, C_ROPE, LANES // C_ROPE)

    lat_mm_tiles = t_lat // MM_TM
    streams = [(x.reshape(t_lat, d), 0), (ctx.reshape(batch * n_ctx, d), lat_mm_tiles)]
    for l in range(depth):
        need_ctx = l < depth - 1
        t_rows = t_all if need_ctx else t_lat
        w_in_l = jnp.pad(w_in[l].astype(BF16), ((0, 0), (0, IN_COLS_PAD - w_in.shape[2])))

        proj = None
        for rows, tile0 in streams:
            proj = _in_proj(rows, norm1_w[l], mods, l, mm_mod_row, w_in_l, tile0, t_all, proj)
        a_out = _gmlp(proj, t_rows, gm_v_norm[l], gm_w_s[l], gm_b_s[l])
        qkv = _gqa_prep(proj, geom, gqa_q_norm[l], gqa_k_norm[l], cos_b, sin_b)
        b_out = _attention(_score_bound(gqa_q_norm[l], gqa_k_norm[l], HEAD_DIM), qkv, qkv, qkv,
                           (0, B_HEADS, B_HEADS + B_KV_HEADS), B_KV_HEADS, B_GROUP, GQA_TQ,
                           n_lat, n_ctx, need_ctx, t_rows, "gqa_attn")
        qc, kc, vc = _mla_prep(proj, geom, mla_q_a_norm[l], mla_w_uq[l], mla_kv_a_norm[l], mla_w_ukv[l],
                               mla_q_norm[l], mla_k_norm[l], cos_c, sin_c)
        c_out = _attention(_score_bound(mla_q_norm[l], mla_k_norm[l], C_QK), qc, kc, vc,
                           (0, 0, 0), C_HEADS, 1, MLA_TQ, n_lat, n_ctx, need_ctx, t_rows, "mla_attn")
        xs = None
        for rows, tile0 in streams:
            n_tiles = min(rows.shape[0], t_rows - tile0 * MM_TM) // MM_TM
            if n_tiles > 0:
                xs = _out_proj(a_out, b_out, c_out, w_out[l].astype(BF16), rows, n_tiles, mods, l, mm_mod_row,
                               tile0, t_rows, xs)
        xs = _moe(xs, t_rows, norm2_w[l], mods, l, mod_row, router_w[l], router_bias[l],
                  exp_w_gate, exp_w_up, exp_w_down, sh_w_gate[l], sh_w_up[l], sh_w_down[l])
        streams = [(xs, 0)]
    return xs[:t_lat].reshape(batch, n_lat, d)
```

```python
import functools

import jax
import jax.numpy as jnp
from jax import lax
from jax.experimental import pallas as pl
from jax.experimental.pallas import tpu as pltpu

F32 = jnp.float32
BF16 = jnp.bfloat16
U32 = jnp.uint32
I32 = jnp.int32

GRID_W = 64
CHUNK = 128
HEAD_DIM = 128
EPS = 1e-6
ROPE_THETA = 10000.0
A_HEADS = 8
B_HEADS = 12
B_KV_HEADS = 4
B_GROUP = B_HEADS // B_KV_HEADS
C_HEADS = 12
C_Q_RANK = 768
C_KV_RANK = 512
C_NOPE = 128
C_ROPE = 64
C_V = 128
C_QK = C_NOPE + C_ROPE
A_WIDTH = A_HEADS * HEAD_DIM
B_WIDTH = B_HEADS * HEAD_DIM
C_WIDTH = C_HEADS * C_V
N_EXPERTS = 64
TOP_K = 6
N_GROUPS = 8
GROUP_SIZE = N_EXPERTS // N_GROUPS
TOPK_GROUPS = 4
ROUTED_SCALE = 2.5
N_MOD = 6

LANES = 128
SUBLANES = 8
ROW_TILE = 256
MM_TM = 512
MM_TN = 1024
GQA_TQ = 1024
MLA_TQ = 2048
ATT_TK = 512
LOG2E = 1.4426950408889634
BOUND_SLACK = 1.02
SCORE_BOUND_LIMIT = 60.0
EXP_TM = 256
DISP_T = 256
COMB_T = 128
COMB_COLS = 128
IN_COLS_PAD = 6144
VMEM_LIMIT = 56 * 1024 * 1024


def _params(sem, vmem=VMEM_LIMIT):
    return pltpu.CompilerParams(dimension_semantics=sem, vmem_limit_bytes=vmem)


def _rms(x, w):
    return x * lax.rsqrt(jnp.mean(x * x, axis=-1, keepdims=True) + EPS) * w


def _pack_bf16_pair(a, b):
    ua = lax.bitcast_convert_type(a.astype(BF16).astype(F32), U32)
    ub = lax.bitcast_convert_type(b.astype(BF16).astype(F32), U32)
    return (ua >> 16) | ub


def _unpack_bf16_pair(u):
    a = lax.bitcast_convert_type(u << 16, F32)
    b = lax.bitcast_convert_type(u & jnp.uint32(0xFFFF0000), F32)
    return a, b


def _ffn(lo, hi, wg, wu, wd):
    half = lo.shape[1]
    hg = jnp.dot(lo, wg[:half], preferred_element_type=F32) + jnp.dot(hi, wg[half:], preferred_element_type=F32)
    hu = jnp.dot(lo, wu[:half], preferred_element_type=F32) + jnp.dot(hi, wu[half:], preferred_element_type=F32)
    hb = (jax.nn.silu(hg) * hu).astype(BF16)
    return jnp.dot(hb, wd, preferred_element_type=F32)


def _ada_kernel(cb_ref, w_ref, b_ref, o_ref, silu_sc, *, n_rows, d, tn):
    nj = tn // LANES

    @pl.when(jnp.logical_and(pl.program_id(0) == 0, pl.program_id(1) == 0))
    def _():
        cv = cb_ref[...]
        silu_sc[...] = cv * jax.nn.sigmoid(cv)

    def body(kc, accs):
        k0 = pl.multiple_of(kc * SUBLANES, SUBLANES)
        wk = w_ref[0, pl.ds(k0, SUBLANES), :]
        out = []
        for r in range(n_rows):
            sv = silu_sc[r, pl.ds(k0, SUBLANES), :]
            for j in range(nj):
                out.append(accs[r * nj + j] + wk[:, j * LANES:(j + 1) * LANES] * sv)
        return tuple(out)

    init = tuple(jnp.zeros((SUBLANES, LANES), F32) for _ in range(n_rows * nj))
    accs = lax.fori_loop(0, d // SUBLANES, body, init, unroll=8)
    o_ref[0] = jnp.zeros((SUBLANES, tn), F32)
    for r in range(n_rows):
        for j in range(nj):
            o_ref[0, r:r + 1, j * LANES:(j + 1) * LANES] = (
                jnp.sum(accs[r * nj + j], axis=0, keepdims=True) + b_ref[0, :, j * LANES:(j + 1) * LANES])


def _ada_mods(c, c_ctx, ada_w, ada_b):
    n_layers, d, nd = ada_w.shape
    n_rows = c.shape[0] + 1
    assert n_rows <= SUBLANES
    cvec = jnp.concatenate([c, c_ctx[None]], axis=0)
    cb = jnp.broadcast_to(cvec[:, :, None], (n_rows, d, LANES))
    tn = 1024
    out = pl.pallas_call(
        functools.partial(_ada_kernel, n_rows=n_rows, d=d, tn=tn),
        grid=(n_layers, nd // tn),
        in_specs=[pl.BlockSpec((n_rows, d, LANES), lambda l, j: (0, 0, 0)),
                  pl.BlockSpec((1, d, tn), lambda l, j: (l, 0, j)),
                  pl.BlockSpec((1, 1, tn), lambda l, j: (l, 0, j))],
        out_specs=pl.BlockSpec((1, SUBLANES, tn), lambda l, j: (l, 0, j)),
        out_shape=jax.ShapeDtypeStruct((n_layers, SUBLANES, nd), F32),
        scratch_shapes=[pltpu.VMEM((n_rows, d, LANES), F32)],
        compiler_params=_params(("arbitrary", "arbitrary")),
        name="ada_mods",
    )(cb, ada_w, ada_b.reshape(n_layers, 1, nd))
    return out.reshape(n_layers, SUBLANES, N_MOD, 1, d)


def _mod_spec(layer, m, d, row_fn, tn=None, col_fn=None):
    if tn is None:
        return pl.BlockSpec((None, None, None, 1, d), lambda *a: (layer, row_fn(*a), m, 0, 0))
    return pl.BlockSpec((None, None, None, 1, tn), lambda *a: (layer, row_fn(*a), m, 0, col_fn(*a)))


def _in_proj_kernel(x_ref, nw_ref, shift_ref, scale_ref, w_ref, *rest):
    o_ref = rest[-1]
    y = _rms(x_ref[...], nw_ref[...])
    h = (y * (1.0 + scale_ref[...]) + shift_ref[...]).astype(BF16)
    o_ref[...] = jnp.dot(h, w_ref[...], preferred_element_type=F32).astype(o_ref.dtype)


def _alias_prev(in_specs, args, prev):
    if prev is None:
        return {}
    in_specs.append(pl.BlockSpec(memory_space=pl.ANY))
    args.append(prev)
    return {len(args) - 1: 0}


def _in_proj(x, norm_w, mods, layer, mod_row, w, tile0, total_rows, prev):
    m, d = x.shape
    n = w.shape[1]
    in_specs = [pl.BlockSpec((MM_TM, d), lambda j, i: (i, 0)),
                pl.BlockSpec((1, d), lambda j, i: (0, 0)),
                _mod_spec(layer, 0, d, lambda j, i: mod_row(i + tile0)),
                _mod_spec(layer, 1, d, lambda j, i: mod_row(i + tile0)),
                pl.BlockSpec((d, MM_TN), lambda j, i: (0, j))]
    args = [x, norm_w.reshape(1, d), mods, mods, w]
    aliases = _alias_prev(in_specs, args, prev)
    return pl.pallas_call(
        _in_proj_kernel,
        grid=(n // MM_TN, m // MM_TM),
        in_specs=in_specs,
        out_specs=pl.BlockSpec((MM_TM, MM_TN), lambda j, i: (i + tile0, j)),
        out_shape=jax.ShapeDtypeStruct((total_rows, n), BF16),
        input_output_aliases=aliases,
        compiler_params=_params(("arbitrary", "arbitrary")),
        name="in_proj",
    )(*args)


def _gmlp_kernel(p_ref, vn_ref, ws_ref, bt_ref, o_ref):
    for h in range(A_HEADS):
        hs = slice(h * HEAD_DIM, (h + 1) * HEAD_DIM)
        u = jax.nn.gelu(p_ref[:, hs].astype(F32))
        v = jax.nn.gelu(p_ref[:, A_WIDTH + h * HEAD_DIM:A_WIDTH + (h + 1) * HEAD_DIM].astype(F32))
        vb = _rms(v, vn_ref[:, hs]).astype(BF16)
        for c in range(ROW_TILE // CHUNK):
            cs = slice(c * CHUNK, (c + 1) * CHUNK)
            s = jnp.dot(ws_ref[h], vb[cs], preferred_element_type=F32) + bt_ref[:, h:h + 1]
            o_ref[cs, hs] = (u[cs] * s).astype(o_ref.dtype)


def _gmlp(proj, t_rows, v_norm, w_s, b_s):
    return pl.pallas_call(
        _gmlp_kernel,
        grid=(t_rows // ROW_TILE,),
        in_specs=[pl.BlockSpec((ROW_TILE, 2 * A_WIDTH), lambda i: (i, 0)),
                  pl.BlockSpec((1, A_WIDTH), lambda i: (0, 0)),
                  pl.BlockSpec((A_HEADS, CHUNK, CHUNK), lambda i: (0, 0, 0)),
                  pl.BlockSpec((CHUNK, A_HEADS), lambda i: (0, 0))],
        out_specs=pl.BlockSpec((ROW_TILE, A_WIDTH), lambda i: (i, 0)),
        out_shape=jax.ShapeDtypeStruct((t_rows, A_WIDTH), BF16),
        compiler_params=_params(("arbitrary",)),
        name="gmlp",
    )(proj, v_norm.reshape(1, A_WIDTH), w_s.astype(BF16), jnp.transpose(b_s))


def _rope_tables(n_lat, n_ctx, d_rot, reps):
    half = d_rot // 2
    nf = half // 2
    rows = n_lat // GRID_W
    tpos = jnp.arange(n_lat, dtype=I32)
    row = tpos // GRID_W - rows // 2
    col = tpos % GRID_W - GRID_W // 2
    freq = ROPE_THETA ** (-jnp.arange(nf, dtype=F32) / nf)
    ang_r = row.astype(F32)[:, None] * freq
    ang_c = col.astype(F32)[:, None] * freq
    cos = jnp.concatenate([jnp.cos(ang_r), jnp.cos(ang_r), jnp.cos(ang_c), jnp.cos(ang_c)], axis=1)
    sin = jnp.concatenate([-jnp.sin(ang_r), jnp.sin(ang_r), -jnp.sin(ang_c), jnp.sin(ang_c)], axis=1)
    cos = jnp.concatenate([cos, jnp.ones((n_ctx, d_rot), F32)], axis=0)
    sin = jnp.concatenate([sin, jnp.zeros((n_ctx, d_rot), F32)], axis=0)
    return jnp.tile(cos, (1, reps)), jnp.tile(sin, (1, reps))


def _rope(y, cos, sin, nf):
    lane = lax.broadcasted_iota(I32, y.shape, 1)
    partner = jnp.where((lane % (2 * nf)) < nf,
                        pltpu.roll(y, LANES - nf, 1), pltpu.roll(y, nf, 1))
    return y * cos + partner * sin


def _gqa_prep_kernel(p_ref, w_ref, cos_ref, sin_ref, o_ref):
    c = pl.program_id(1)
    heads = p_ref.shape[1] // HEAD_DIM

    @pl.when(c < 4)
    def _():
        for hh in range(heads):
            x = p_ref[:, hh * HEAD_DIM:(hh + 1) * HEAD_DIM].astype(F32)
            y = _rope(_rms(x, w_ref[...]), cos_ref[...], sin_ref[...], HEAD_DIM // 4)
            o_ref[hh] = y.astype(o_ref.dtype)

    @pl.when(c == 4)
    def _():
        for hh in range(heads):
            o_ref[hh] = p_ref[:, hh * HEAD_DIM:(hh + 1) * HEAD_DIM]


def _gqa_prep(proj, geom, q_norm, k_norm, cos, sin):
    batch, n_tok, b_fn, rb_fn = geom
    wvec = jnp.stack([q_norm * (HEAD_DIM ** -0.5 * LOG2E)] * 3 + [k_norm, jnp.ones_like(k_norm)])
    wvec = wvec.reshape(5, 1, HEAD_DIM)
    first = (2 * A_WIDTH) // 512
    return pl.pallas_call(
        _gqa_prep_kernel,
        grid=(proj.shape[0] // ROW_TILE, 5),
        in_specs=[pl.BlockSpec((ROW_TILE, 512), lambda i, c: (i, first + c)),
                  pl.BlockSpec((None, 1, HEAD_DIM), lambda i, c: (c, 0, 0)),
                  pl.BlockSpec((ROW_TILE, HEAD_DIM), lambda i, c: (rb_fn(i), 0)),
                  pl.BlockSpec((ROW_TILE, HEAD_DIM), lambda i, c: (rb_fn(i), 0))],
        out_specs=pl.BlockSpec((None, 4, ROW_TILE, HEAD_DIM), lambda i, c: (b_fn(i), c, rb_fn(i), 0)),
        out_shape=jax.ShapeDtypeStruct((batch, B_HEADS + 2 * B_KV_HEADS, n_tok, HEAD_DIM), BF16),
        compiler_params=_params(("arbitrary", "arbitrary")),
        name="gqa_prep",
    )(proj, wvec, cos, sin)


def _mla_prep_kernel(p_ref, qa_ref, kva_ref, wqn_ref, wqr_ref, wkk_ref, wkv_ref,
                     nqn_ref, nqr_ref, nkn_ref, nkr_ref, cos_ref, sin_ref,
                     qc_ref, kc_ref, vc_ref):
    cq = p_ref[:, :C_Q_RANK].astype(F32)
    ckv = p_ref[:, C_Q_RANK:C_Q_RANK + C_KV_RANK].astype(F32)
    kr = p_ref[:, C_Q_RANK + C_KV_RANK:C_Q_RANK + C_KV_RANK + LANES].astype(F32)
    cqn = _rms(cq, qa_ref[...]).astype(BF16)
    ckvn = _rms(ckv, kva_ref[...]).astype(BF16)
    qn = jnp.dot(cqn, wqn_ref[...], preferred_element_type=F32)
    qr = jnp.dot(cqn, wqr_ref[...], preferred_element_type=F32)
    kn = jnp.dot(ckvn, wkk_ref[...], preferred_element_type=F32)
    vv = jnp.dot(ckvn, wkv_ref[...], preferred_element_type=F32)
    cos = cos_ref[...]
    sin = sin_ref[...]
    lane = lax.broadcasted_iota(I32, kr.shape, 1)
    low = lane < C_ROPE
    inv_qk = 1.0 / C_QK

    kr_rot = _rope(kr * nkr_ref[...], cos, sin, C_ROPE // 4)
    kr_ss = jnp.sum(kr * kr, axis=-1, keepdims=True)
    for h in range(C_HEADS):
        hs = slice(h * C_NOPE, (h + 1) * C_NOPE)
        knh = kn[:, hs]
        r = lax.rsqrt((jnp.sum(knh * knh, axis=-1, keepdims=True) + kr_ss) * inv_qk + EPS)
        kc_ref[h, :, 0:C_NOPE] = (knh * r * nkn_ref[...]).astype(kc_ref.dtype)
        kc_ref[h, :, C_NOPE:C_QK] = (kr_rot * r)[:, 0:C_ROPE].astype(kc_ref.dtype)
        vc_ref[h] = vv[:, h * C_V:(h + 1) * C_V].astype(vc_ref.dtype)

    for pr in range(C_HEADS // 2):
        qrp = qr[:, pr * LANES:(pr + 1) * LANES]
        sq = qrp * qrp
        ss_pair = (jnp.sum(jnp.where(low, sq, 0.0), axis=-1, keepdims=True),
                   jnp.sum(jnp.where(low, 0.0, sq), axis=-1, keepdims=True))
        rs = []
        for e in range(2):
            h = 2 * pr + e
            qnh = qn[:, h * C_NOPE:(h + 1) * C_NOPE]
            r = lax.rsqrt((jnp.sum(qnh * qnh, axis=-1, keepdims=True) + ss_pair[e]) * inv_qk + EPS)
            rs.append(r)
            qc_ref[h, :, 0:C_NOPE] = (qnh * r * nqn_ref[...]).astype(qc_ref.dtype)
        q_rot = _rope(qrp * jnp.where(low, rs[0], rs[1]) * nqr_ref[...], cos, sin, C_ROPE // 4)
        qc_ref[2 * pr, :, C_NOPE:C_QK] = q_rot[:, 0:C_ROPE].astype(qc_ref.dtype)
        qc_ref[2 * pr + 1, :, C_NOPE:C_QK] = q_rot[:, C_ROPE:2 * C_ROPE].astype(qc_ref.dtype)


def _mla_prep(proj, geom, q_a_norm, w_uq, kv_a_norm, w_ukv, q_norm, k_norm, cos, sin):
    batch, n_tok, b_fn, rb_fn = geom
    first = (2 * A_WIDTH + B_WIDTH + 2 * B_KV_HEADS * HEAD_DIM)
    width = IN_COLS_PAD - first
    assert first % width == 0
    wq = w_uq.reshape(C_Q_RANK, C_HEADS, C_QK)
    wqn = wq[:, :, :C_NOPE].reshape(C_Q_RANK, C_HEADS * C_NOPE).astype(BF16)
    wqr = wq[:, :, C_NOPE:].reshape(C_Q_RANK, C_HEADS * C_ROPE).astype(BF16)
    wkv = w_ukv.reshape(C_KV_RANK, C_HEADS, C_NOPE + C_V)
    wkk = wkv[:, :, :C_NOPE].reshape(C_KV_RANK, C_HEADS * C_NOPE).astype(BF16)
    wkvv = wkv[:, :, C_NOPE:].reshape(C_KV_RANK, C_HEADS * C_V).astype(BF16)
    qs = C_QK ** -0.5 * LOG2E
    nqn = (q_norm[:C_NOPE] * qs).reshape(1, C_NOPE)
    nqr = jnp.tile(q_norm[C_NOPE:] * qs, 2).reshape(1, LANES)
    nkn = k_norm[:C_NOPE].reshape(1, C_NOPE)
    nkr = jnp.tile(k_norm[C_NOPE:], 2).reshape(1, LANES)
    const = lambda shape: pl.BlockSpec(shape, lambda i: tuple(0 for _ in shape))
    head_spec = lambda dd: pl.BlockSpec((None, C_HEADS, ROW_TILE, dd), lambda i: (b_fn(i), 0, rb_fn(i), 0))
    return pl.pallas_call(
        _mla_prep_kernel,
        grid=(proj.shape[0] // ROW_TILE,),
        in_specs=[pl.BlockSpec((ROW_TILE, width), lambda i: (i, first // width)),
                  const((1, C_Q_RANK)), const((1, C_KV_RANK)),
                  const(wqn.shape), const(wqr.shape), const(wkk.shape), const(wkvv.shape),
                  const((1, C_NOPE)), const((1, LANES)), const((1, C_NOPE)), const((1, LANES)),
                  pl.BlockSpec((ROW_TILE, LANES), lambda i: (rb_fn(i), 0)),
                  pl.BlockSpec((ROW_TILE, LANES), lambda i: (rb_fn(i), 0))],
        out_specs=[head_spec(C_QK), head_spec(C_QK), head_spec(C_V)],
        out_shape=[jax.ShapeDtypeStruct((batch, C_HEADS, n_tok, C_QK), BF16),
                   jax.ShapeDtypeStruct((batch, C_HEADS, n_tok, C_QK), BF16),
                   jax.ShapeDtypeStruct((batch, C_HEADS, n_tok, C_V), BF16)],
        compiler_params=_params(("arbitrary",)),
        name="mla_prep",
    )(proj, q_a_norm.reshape(1, C_Q_RANK), kv_a_norm.reshape(1, C_KV_RANK), wqn, wqr, wkk, wkvv,
      nqn, nqr, nkn, nkr, cos, sin)


def _attn_kernel(bound_ref, q_ref, k_ref, v_ref, *rest, group, n_lat):
    use_bound = bound_ref[0] <= SCORE_BOUND_LIMIT
    for bounded, pred in ((True, use_bound), (False, jnp.logical_not(use_bound))):
        pl.when(pred)(functools.partial(_attn_body, bound_ref, q_ref, k_ref, v_ref, rest,
                                        group=group, n_lat=n_lat, bounded=bounded))


def _attn_body(bound_ref, q_ref, k_ref, v_ref, rest, *, group, n_lat, bounded):
    o_ref, m_sc, l_sc, acc_sc = rest[-4:]
    tq, dk = q_ref.shape[1], q_ref.shape[2]
    dv = v_ref.shape[1]
    n_tail = k_ref.shape[0] - n_lat
    q = q_ref[...].reshape(group * tq, dk)
    l_sc[...] = jnp.zeros(l_sc.shape, F32)
    acc_sc[...] = jnp.zeros(acc_sc.shape, F32)
    if not bounded:
        m_sc[...] = jnp.full(m_sc.shape, -jnp.inf, F32)

    def lane_fold(p):
        out = p[:, 0:LANES]
        for j in range(1, p.shape[1] // LANES):
            out = out + p[:, j * LANES:(j + 1) * LANES]
        return out

    def chunk(start, size):
        k = k_ref[pl.ds(start, size), :]
        v = v_ref[pl.ds(start, size), :]
        s = lax.dot_general(q, k, (((1,), (1,)), ((), ())), preferred_element_type=F32)
        if bounded:
            p = jnp.exp2(s - bound_ref[0])
            l_sc[...] += lane_fold(p)
            acc_sc[...] += jnp.dot(p.astype(BF16), v, preferred_element_type=F32)
        else:
            m_prev = m_sc[...]
            m_new = jnp.maximum(m_prev, jnp.max(s, axis=-1, keepdims=True))
            alpha = jnp.exp2(m_prev - m_new)
            p = jnp.exp2(s - m_new[:, 0:1])
            l_sc[...] = alpha * l_sc[...] + lane_fold(p)
            acc_sc[...] = alpha[:, 0:1] * acc_sc[...] + jnp.dot(p.astype(BF16), v, preferred_element_type=F32)
            m_sc[...] = m_new

    if n_lat:
        def body(c, carry):
            chunk(pl.multiple_of(c * ATT_TK, ATT_TK), ATT_TK)
            return carry
        lax.fori_loop(0, n_lat // ATT_TK, body, 0, unroll=2)
    chunk(n_lat, n_tail)
    o = acc_sc[...] / jnp.sum(l_sc[...], axis=-1, keepdims=True)
    for g in range(group):
        o_ref[:, g * dv:(g + 1) * dv] = o[g * tq:(g + 1) * tq].astype(o_ref.dtype)


def _attention_call(bound, q_arr, k_arr, v_arr, prev_out, heads, n_kv, group, tq, n_lat, n_ctx, ctx_only,
                    t_rows, name):
    q_head0, k_head0, v_head0 = heads
    batch, _, n_tok, dk = q_arr.shape
    dv = v_arr.shape[-1]
    m = group * tq
    if ctx_only:
        assert tq == n_ctx and n_lat % n_ctx == 0
        nq, q0, key_rows, key_blk = 1, n_lat // tq, n_ctx, n_lat // n_ctx
        out_row = lambda b, qi: batch * (n_lat // tq) + b
    else:
        assert n_lat % tq == 0 and n_lat % ATT_TK == 0
        nq, q0, key_rows, key_blk = n_lat // tq, 0, n_tok, 0
        out_row = lambda b, qi: b * nq + qi
    in_specs = [pl.BlockSpec((None, group, tq, dk), lambda b, g, qi, *_: (b, q_head0 // group + g, q0 + qi, 0)),
                pl.BlockSpec((None, None, key_rows, dk), lambda b, g, qi, *_: (b, k_head0 + g, key_blk, 0)),
                pl.BlockSpec((None, None, key_rows, dv), lambda b, g, qi, *_: (b, v_head0 + g, key_blk, 0))]
    args = [bound, q_arr, k_arr, v_arr]
    aliases = {}
    if prev_out is not None:
        in_specs.append(pl.BlockSpec(memory_space=pl.ANY))
        args.append(prev_out)
        aliases = {len(args) - 1: 0}
    grid_spec = pltpu.PrefetchScalarGridSpec(
        num_scalar_prefetch=1,
        grid=(batch, n_kv, nq),
        in_specs=in_specs,
        out_specs=pl.BlockSpec((tq, group * dv), lambda b, g, qi, *_: (out_row(b, qi), g)),
        scratch_shapes=[pltpu.VMEM((m, LANES), F32), pltpu.VMEM((m, LANES), F32), pltpu.VMEM((m, dv), F32)],
    )
    return pl.pallas_call(
        functools.partial(_attn_kernel, group=group, n_lat=0 if ctx_only else n_lat),
        grid_spec=grid_spec,
        out_shape=jax.ShapeDtypeStruct((t_rows, n_kv * group * dv), BF16),
        input_output_aliases=aliases,
        compiler_params=_params(("arbitrary", "arbitrary", "arbitrary")),
        name=name,
    )(*args)


def _attention(bound, q_arr, k_arr, v_arr, heads, n_kv, group, tq, n_lat, n_ctx, with_ctx, t_rows, name):
    tq = min(tq, n_lat)
    out = _attention_call(bound, q_arr, k_arr, v_arr, None, heads, n_kv, group, tq, n_lat, n_ctx,
                          False, t_rows, name)
    if with_ctx:
        out = _attention_call(bound, q_arr, k_arr, v_arr, out, heads, n_kv, group, n_ctx, n_lat, n_ctx,
                              True, t_rows, name + "_ctx")
    return out


def _score_bound(q_norm, k_norm, dk):
    b = dk ** 0.5 * jnp.max(jnp.abs(q_norm)) * jnp.max(jnp.abs(k_norm)) * (LOG2E * BOUND_SLACK)
    return b.reshape(1).astype(F32)


def _wout_kernel(a_ref, b_ref, c_ref, w_ref, x_ref, g_ref, *rest):
    o_ref = rest[-1]
    acc = jnp.dot(a_ref[...], w_ref[0:A_WIDTH, :], preferred_element_type=F32)
    acc += jnp.dot(b_ref[...], w_ref[A_WIDTH:A_WIDTH + B_WIDTH, :], preferred_element_type=F32)
    acc += jnp.dot(c_ref[...], w_ref[A_WIDTH + B_WIDTH:, :], preferred_element_type=F32)
    o_ref[...] = x_ref[...] + g_ref[...] * acc


def _out_proj(a_out, b_out, c_out, w, x, n_tiles, mods, layer, mod_row, tile0, total_rows, prev):
    d = w.shape[1]
    in_specs = [pl.BlockSpec((MM_TM, A_WIDTH), lambda j, i: (i + tile0, 0)),
                pl.BlockSpec((MM_TM, B_WIDTH), lambda j, i: (i + tile0, 0)),
                pl.BlockSpec((MM_TM, C_WIDTH), lambda j, i: (i + tile0, 0)),
                pl.BlockSpec((w.shape[0], MM_TN), lambda j, i: (0, j)),
                pl.BlockSpec((MM_TM, MM_TN), lambda j, i: (i, j)),
                _mod_spec(layer, 2, d, lambda j, i: mod_row(i + tile0), MM_TN, lambda j, i: j)]
    args = [a_out, b_out, c_out, w, x, mods]
    aliases = _alias_prev(in_specs, args, prev)
    return pl.pallas_call(
        _wout_kernel,
        grid=(d // MM_TN, n_tiles),
        in_specs=in_specs,
        out_specs=pl.BlockSpec((MM_TM, MM_TN), lambda j, i: (i + tile0, j)),
        out_shape=jax.ShapeDtypeStruct((total_rows, d), F32),
        input_output_aliases=aliases,
        compiler_params=_params(("arbitrary", "arbitrary")),
        name="out_proj",
    )(*args)


def _first_index(hit_val, x, iota, big):
    return jnp.min(jnp.where(x == hit_val, iota, big), axis=0, keepdims=True)


def _router_kernel(x_ref, nw_ref, shift_ref, scale_ref, rwt_ref, bias_ref,
                   hp_ref, eidx_ref, gate_ref, rank_ref, cnt_ref, carry_sc):
    i = pl.program_id(0)
    tm, d = x_ref.shape
    h = _rms(x_ref[...], nw_ref[...]) * (1.0 + scale_ref[...]) + shift_ref[...]
    hp_ref[...] = _pack_bf16_pair(h[:, :d // 2], h[:, d // 2:])

    logits = lax.dot_general(rwt_ref[...], h, (((1,), (1,)), ((), ())),
                             precision=lax.Precision.HIGHEST, preferred_element_type=F32)
    scores = jax.nn.sigmoid(logits)
    sel = scores + bias_ref[:, 0:1]
    neg = jnp.float32(-jnp.inf)

    iota_n = lax.broadcasted_iota(I32, (N_GROUPS, tm), 0).astype(F32)
    iota_e = lax.broadcasted_iota(I32, (N_EXPERTS, tm), 0).astype(F32)

    gscore = jnp.zeros((N_GROUPS, tm), F32)
    for g in range(N_GROUPS):
        blk = sel[g * GROUP_SIZE:(g + 1) * GROUP_SIZE]
        m1 = jnp.max(blk, axis=0, keepdims=True)
        first = _first_index(m1, blk, iota_n, float(GROUP_SIZE))
        m2 = jnp.max(jnp.where(iota_n == first, neg, blk), axis=0, keepdims=True)
        gscore = jnp.where(iota_n == float(g), m1 + m2, gscore)

    gsel = jnp.zeros((N_GROUPS, tm), F32)
    work = gscore
    for _ in range(TOPK_GROUPS):
        m = jnp.max(work, axis=0, keepdims=True)
        hit = iota_n == _first_index(m, work, iota_n, float(N_GROUPS))
        gsel = jnp.where(hit, 1.0, gsel)
        work = jnp.where(hit, neg, work)
    emask = jnp.concatenate(
        [jnp.broadcast_to(gsel[g:g + 1], (GROUP_SIZE, tm)) for g in range(N_GROUPS)], axis=0)

    work = jnp.where(emask > 0.5, sel, neg)
    picks = []
    onehot = jnp.zeros((N_EXPERTS, tm), F32)
    for _ in range(TOP_K):
        m = jnp.max(work, axis=0, keepdims=True)
        pick = _first_index(m, work, iota_e, float(N_EXPERTS))
        picks.append(pick)
        hit = iota_e == pick
        onehot = jnp.where(hit, 1.0, onehot)
        work = jnp.where(hit, neg, work)

    w_un = onehot * scores
    gates = w_un / jnp.sum(w_un, axis=0, keepdims=True) * ROUTED_SCALE

    @pl.when(i == 0)
    def _():
        carry_sc[...] = jnp.zeros(carry_sc.shape, F32)

    before = (lax.broadcasted_iota(I32, (tm, tm), 0) < lax.broadcasted_iota(I32, (tm, tm), 1)).astype(BF16)
    rank = carry_sc[:, 0:1] + jnp.dot(onehot.astype(BF16), before, preferred_element_type=F32)
    carry_sc[...] = carry_sc[...] + jnp.sum(onehot, axis=1, keepdims=True)
    cnt_ref[...] = carry_sc[...]

    eidx_ref[...] = jnp.zeros(eidx_ref.shape, I32)
    gate_ref[...] = jnp.zeros(gate_ref.shape, F32)
    rank_ref[...] = jnp.zeros(rank_ref.shape, I32)
    for k in range(TOP_K):
        hit = iota_e == picks[k]
        eidx_ref[k:k + 1, :] = picks[k].astype(I32)
        gate_ref[k:k + 1, :] = jnp.sum(jnp.where(hit, gates, 0.0), axis=0, keepdims=True)
        rank_ref[k:k + 1, :] = jnp.sum(jnp.where(hit, rank, 0.0), axis=0, keepdims=True).astype(I32)


def _norm_router(x, t_rows, norm_w, mods, layer, mod_row, router_w, router_bias):
    d = x.shape[1]
    col = lambda dt: (pl.BlockSpec((SUBLANES, ROW_TILE), lambda i: (0, i)),
                      jax.ShapeDtypeStruct((SUBLANES, t_rows), dt))
    specs, shapes = zip(col(I32), col(F32), col(I32))
    return pl.pallas_call(
        _router_kernel,
        grid=(t_rows // ROW_TILE,),
        in_specs=[pl.BlockSpec((ROW_TILE, d), lambda i: (i, 0)),
                  pl.BlockSpec((1, d), lambda i: (0, 0)),
                  _mod_spec(layer, 3, d, mod_row),
                  _mod_spec(layer, 4, d, mod_row),
                  pl.BlockSpec((N_EXPERTS, d), lambda i: (0, 0)),
                  pl.BlockSpec((N_EXPERTS, LANES), lambda i: (0, 0))],
        out_specs=[pl.BlockSpec((ROW_TILE, d // 2), lambda i: (i, 0)), *specs,
                   pl.BlockSpec((N_EXPERTS, LANES), lambda i: (0, 0))],
        out_shape=[jax.ShapeDtypeStruct((t_rows, d // 2), U32), *shapes,
                   jax.ShapeDtypeStruct((N_EXPERTS, LANES), F32)],
        scratch_shapes=[pltpu.VMEM((N_EXPERTS, LANES), F32)],
        compiler_params=_params(("arbitrary",)),
        name="norm_router",
    )(x, norm_w.reshape(1, d), mods, mods, jnp.transpose(router_w),
      jnp.broadcast_to(router_bias[:, None], (N_EXPERTS, LANES)))


def _row_copy(src, src_row, dst, dst_row, sem):
    return pltpu.make_async_copy(src.at[pl.ds(src_row, 1)], dst.at[pl.ds(dst_row, 1)], sem)


_FILL_PIECES = tuple(1 << b for b in range(EXP_TM.bit_length() - 2, -1, -1))


def _zero_fill(pstart_ref, plen_ref, tail_ref, xs_hbm, zero_sc, sem_z):
    zero_sc[...] = jnp.zeros(zero_sc.shape, zero_sc.dtype)
    n_tail = (xs_hbm.shape[0] - tail_ref[0]) // _FILL_PIECES[0]

    def pad_copies(e, wait):
        start = pstart_ref[e]
        head = (-start) & (SUBLANES - 1)
        for r in range(SUBLANES - 1):
            cp = pltpu.make_async_copy(zero_sc.at[pl.ds(0, 1)], xs_hbm.at[pl.ds(start + r, 1)], sem_z)

            @pl.when(r < head)
            def _():
                cp.wait() if wait else cp.start()

        off = start + head
        rem = plen_ref[e] - head
        for piece in _FILL_PIECES:
            if piece < SUBLANES:
                break
            on = (rem & piece) != 0
            cp = pltpu.make_async_copy(zero_sc.at[pl.ds(0, piece)],
                                       xs_hbm.at[pl.ds(pl.multiple_of(off, SUBLANES), piece)], sem_z)

            @pl.when(on)
            def _():
                cp.wait() if wait else cp.start()

            off = off + jnp.where(on, piece, 0)

    def tail_copy(j, wait):
        row = pl.multiple_of(tail_ref[0] + j * _FILL_PIECES[0], _FILL_PIECES[0])
        cp = pltpu.make_async_copy(zero_sc, xs_hbm.at[pl.ds(row, _FILL_PIECES[0])], sem_z)
        cp.wait() if wait else cp.start()

    for wait in (False, True):
        lax.fori_loop(0, N_EXPERTS, lambda e, c: (pad_copies(e, wait), c)[1], 0)
        lax.fori_loop(0, n_tail, lambda j, c: (tail_copy(j, wait), c)[1], 0)


def _dispatch_kernel(pstart_ref, plen_ref, tail_ref, slot_hbm, h_ref, xs_hbm, slot_sm, zero_sc, sem_s, sem, sem_z):
    i = pl.program_id(0)
    n = slot_sm.shape[0]
    cp = pltpu.make_async_copy(slot_hbm.at[pl.ds(pl.multiple_of(i * n, n), n)], slot_sm, sem_s)
    cp.start()

    @pl.when(i == 0)
    def _():
        _zero_fill(pstart_ref, plen_ref, tail_ref, xs_hbm, zero_sc, sem_z)

    cp.wait()

    def issue(t, carry):
        for k in range(TOP_K):
            _row_copy(h_ref, t, xs_hbm, slot_sm[t * SUBLANES + k], sem).start(priority=k % 2)
        return carry

    lax.fori_loop(0, DISP_T, issue, 0)

    def drain(t, carry):
        for k in range(TOP_K):
            _row_copy(h_ref, 0, xs_hbm, 0, sem).wait()
        return carry

    lax.fori_loop(0, DISP_T, drain, 0)


def _dispatch(hp, slots, pad_start, pad_len, tail_start, n_slots):
    t, w = hp.shape
    grid_spec = pltpu.PrefetchScalarGridSpec(
        num_scalar_prefetch=3,
        grid=(t // DISP_T,),
        in_specs=[pl.BlockSpec(memory_space=pl.ANY), pl.BlockSpec((DISP_T, w), lambda i, *_: (i, 0))],
        out_specs=pl.BlockSpec(memory_space=pl.ANY),
        scratch_shapes=[pltpu.SMEM((DISP_T * SUBLANES,), I32),
                        pltpu.VMEM((_FILL_PIECES[0], w), U32),
                        pltpu.SemaphoreType.DMA(()), pltpu.SemaphoreType.DMA(()), pltpu.SemaphoreType.DMA(())],
    )
    return pl.pallas_call(
        _dispatch_kernel,
        grid_spec=grid_spec,
        out_shape=jax.ShapeDtypeStruct((n_slots, w), U32),
        compiler_params=_params(("arbitrary",)),
        name="moe_dispatch",
    )(pad_start, pad_len, tail_start, slots, hp)


def _expert_kernel(te_ref, par_ref, nxt_ref, nu_ref, xs_ref, wg_hbm, wu_hbm, wd_hbm, ys_ref,
                   wg_f, wu_f, wd_f, wg_sc, wu_sc, wd_sc, sem, *, layer):
    i = pl.program_id(0)
    expert = te_ref[i]
    par = par_ref[i]

    def weight_copies(e, slot):
        return (pltpu.make_async_copy(wg_hbm.at[layer, e], wg_f.at[slot], sem.at[slot, 0]),
                pltpu.make_async_copy(wu_hbm.at[layer, e], wu_f.at[slot], sem.at[slot, 1]),
                pltpu.make_async_copy(wd_hbm.at[layer, e], wd_f.at[slot], sem.at[slot, 2]))

    @pl.when(i == 0)
    def _():
        for cp in weight_copies(expert, par):
            cp.start()

    @pl.when(jnp.logical_or(i == 0, expert != te_ref[jnp.maximum(i - 1, 0)]))
    def _():
        for cp in weight_copies(expert, par):
            cp.wait()

        @pl.when(nxt_ref[i] >= 0)
        def _():
            for cp in weight_copies(nxt_ref[i], 1 - par):
                cp.start(priority=1)

        wg_sc[...] = wg_f[par].astype(BF16)
        wu_sc[...] = wu_f[par].astype(BF16)
        wd_sc[...] = wd_f[par].astype(BF16)

    @pl.when(i < nu_ref[0])
    def _():
        lo, hi = _unpack_bf16_pair(xs_ref[...])
        y = _ffn(lo.astype(BF16), hi.astype(BF16), wg_sc[...], wu_sc[...], wd_sc[...])
        half = y.shape[1] // 2
        ys_ref[...] = _pack_bf16_pair(y[:, :half], y[:, half:])

    @pl.when(i >= nu_ref[0])
    def _():
        ys_ref[...] = jnp.zeros(ys_ref.shape, ys_ref.dtype)


def _experts(xs, tile_expert, tile_slot, tile_next, n_used, w_gate, w_up, w_down, layer):
    _, _, d, f = w_gate.shape
    n_tiles = xs.shape[0] // EXP_TM
    hbm = pl.BlockSpec(memory_space=pl.ANY)
    tile = (EXP_TM, xs.shape[1])
    grid_spec = pltpu.PrefetchScalarGridSpec(
        num_scalar_prefetch=4,
        grid=(n_tiles,),
        in_specs=[pl.BlockSpec(tile, lambda i, te, par, nxt, nu: (jnp.minimum(i, nu[0] - 1), 0)),
                  hbm, hbm, hbm],
        out_specs=pl.BlockSpec(tile, lambda i, te, par, nxt, nu: (i, 0)),
        scratch_shapes=[pltpu.VMEM((2, d, f), F32), pltpu.VMEM((2, d, f), F32), pltpu.VMEM((2, f, d), F32),
                        pltpu.VMEM((d, f), BF16), pltpu.VMEM((d, f), BF16), pltpu.VMEM((f, d), BF16),
                        pltpu.SemaphoreType.DMA((2, 3))],
    )
    return pl.pallas_call(
        functools.partial(_expert_kernel, layer=layer),
        grid_spec=grid_spec,
        out_shape=jax.ShapeDtypeStruct(xs.shape, U32),
        compiler_params=_params(("arbitrary",)),
        name="moe_experts",
    )(tile_expert, tile_slot, tile_next, n_used, xs, w_gate, w_up, w_down)


def _combine_kernel(slot_hbm, ys_hbm, hp_ref, gate_ref, sg_ref, su_ref, sd_ref, x_ref, g5_ref,
                    o_ref, slot_sm, buf, sem_s, sem):
    i = pl.program_id(0)
    n_steps = pl.num_programs(0)
    n = COMB_T * SUBLANES
    half = x_ref.shape[1] // 2

    def load_slots(h, half_index):
        cp = pltpu.make_async_copy(slot_hbm.at[pl.ds(pl.multiple_of(half_index * n, n), n)],
                                   slot_sm.at[h], sem_s.at[h])
        cp.start()
        cp.wait()

    def issue(h, tokens):
        for t in tokens:
            for k in range(TOP_K):
                _row_copy(ys_hbm, slot_sm[h, t * SUBLANES + k], buf.at[h, k], t, sem.at[h]).start(priority=k % 2)

    def drain(h):
        def body(t, carry):
            for k in range(TOP_K):
                _row_copy(ys_hbm, 0, buf.at[h, k], 0, sem.at[h]).wait()
            return carry

        lax.fori_loop(0, COMB_T, body, 0)

    def finish(h, other):
        rows = slice(h * COMB_T, (h + 1) * COMB_T)
        issue(other, range(0, COMB_T // 2))
        lo, hi = _unpack_bf16_pair(hp_ref[rows, :])
        y = _ffn(lo.astype(BF16), hi.astype(BF16), sg_ref[...], su_ref[...], sd_ref[...])
        drain(h)
        issue(other, range(COMB_T // 2, COMB_T))
        gates = [gate_ref[rows, k:k + 1] for k in range(TOP_K)]
        for c in range(half // COMB_COLS):
            lo_cols = slice(c * COMB_COLS, (c + 1) * COMB_COLS)
            hi_cols = slice(half + c * COMB_COLS, half + (c + 1) * COMB_COLS)
            r_lo = y[:, lo_cols]
            r_hi = y[:, hi_cols]
            for k in range(TOP_K):
                e_lo, e_hi = _unpack_bf16_pair(buf[h, k, :, lo_cols])
                r_lo = r_lo + gates[k] * e_lo
                r_hi = r_hi + gates[k] * e_hi
            o_ref[rows, lo_cols] = x_ref[rows, lo_cols] + g5_ref[:, lo_cols] * r_lo
            o_ref[rows, hi_cols] = x_ref[rows, hi_cols] + g5_ref[:, hi_cols] * r_hi

    @pl.when(i == 0)
    def _():
        load_slots(0, 0)
        lax.fori_loop(0, COMB_T, lambda t, c: (issue(0, (t,)), c)[1], 0)

    load_slots(1, 2 * i + 1)
    finish(0, 1)
    load_slots(0, jnp.minimum(2 * i + 2, 2 * n_steps - 2))
    finish(1, 0)

    @pl.when(i == n_steps - 1)
    def _():
        drain(0)


def _combine(ys, hp, slots, gates, sh_gate, sh_up, sh_down, x, mods, layer, t_rows, mod_row):
    d = x.shape[1]
    w = hp.shape[1]
    f = sh_gate.shape[1]
    rows = 2 * COMB_T
    return pl.pallas_call(
        _combine_kernel,
        grid=(t_rows // rows,),
        in_specs=[pl.BlockSpec(memory_space=pl.ANY), pl.BlockSpec(memory_space=pl.ANY),
                  pl.BlockSpec((rows, w), lambda i: (i, 0)),
                  pl.BlockSpec((rows, SUBLANES), lambda i: (i, 0)),
                  pl.BlockSpec((d, f), lambda i: (0, 0)),
                  pl.BlockSpec((d, f), lambda i: (0, 0)),
                  pl.BlockSpec((f, d), lambda i: (0, 0)),
                  pl.BlockSpec((rows, d), lambda i: (i, 0)),
                  _mod_spec(layer, 5, d, lambda i: mod_row(i * rows // ROW_TILE))],
        out_specs=pl.BlockSpec((rows, d), lambda i: (i, 0)),
        out_shape=jax.ShapeDtypeStruct((t_rows, d), F32),
        scratch_shapes=[pltpu.SMEM((2, COMB_T * SUBLANES), I32),
                        pltpu.VMEM((2, TOP_K, COMB_T, w), U32),
                        pltpu.SemaphoreType.DMA((2,)), pltpu.SemaphoreType.DMA((2,))],
        compiler_params=_params(("arbitrary",)),
        name="moe_combine",
    )(slots, ys, hp, gates, sh_gate.astype(BF16), sh_up.astype(BF16), sh_down.astype(BF16), x, mods)


def _moe(x, t_rows, norm_w, mods, layer, mod_row, router_w, router_bias,
         w_gate, w_up, w_down, sh_gate, sh_up, sh_down):
    hp, eidx, gate, rank, cnt = _norm_router(x, t_rows, norm_w, mods, layer, mod_row, router_w, router_bias)

    n_tiles = (t_rows * TOP_K + N_EXPERTS * (EXP_TM - 1) + EXP_TM - 1) // EXP_TM
    counts = cnt[:, 0].astype(I32)
    padded = (counts + EXP_TM - 1) // EXP_TM * EXP_TM
    ends = jnp.cumsum(padded)
    base = ends - padded
    experts = jnp.arange(N_EXPERTS, dtype=I32)
    base_of = jnp.sum(jnp.where(eidx[:, :, None] == experts, base, 0), axis=-1)
    slots = jnp.transpose(base_of + rank).reshape(-1)
    n_used = (ends[-1] // EXP_TM).astype(I32)
    tile_start = jnp.arange(n_tiles, dtype=I32) * EXP_TM
    last_start = (n_used - 1) * EXP_TM
    te = jnp.sum(jnp.minimum(tile_start, last_start)[:, None] >= ends[None, :], axis=1).astype(I32)
    te = jnp.minimum(te, N_EXPERTS - 1)
    used = counts > 0
    slot_of = ((jnp.cumsum(used.astype(I32)) - 1) % 2).astype(I32)
    later = jnp.logical_and(experts[None, :] > experts[:, None], used[None, :])
    next_of = jnp.min(jnp.where(later, experts[None, :], N_EXPERTS), axis=1)
    next_of = jnp.where(next_of == N_EXPERTS, -1, next_of).astype(I32)

    xs = _dispatch(hp, slots, base + counts, padded - counts, ends[-1:], n_tiles * EXP_TM)
    ys = _experts(xs, te, slot_of[te], next_of[te], n_used.reshape(1), w_gate, w_up, w_down, layer)
    return _combine(ys, hp, slots, jnp.transpose(gate), sh_gate, sh_up, sh_down, x, mods, layer, t_rows, mod_row)


def kernel(x, c, ctx, c_ctx, ada_w, ada_b, norm1_w, norm2_w, w_in, gm_v_norm, gm_w_s, gm_b_s, gqa_q_norm, gqa_k_norm, mla_q_a_norm, mla_w_uq, mla_kv_a_norm, mla_w_ukv, mla_q_norm, mla_k_norm, w_out, router_w, router_bias, exp_w_gate, exp_w_up, exp_w_down, sh_w_gate, sh_w_up, sh_w_down):
    batch, n_lat, d = x.shape
    n_ctx = ctx.shape[1]
    depth = ada_w.shape[0]
    assert n_ctx == ROW_TILE and n_lat % MM_TM == 0 and (batch * n_ctx) % MM_TM == 0
    t_lat = batch * n_lat
    t_all = t_lat + batch * n_ctx
    n_tok = n_lat + n_ctx
    lat_tiles = n_lat // ROW_TILE

    mod_row = lambda i: jnp.minimum(i // lat_tiles, batch)
    is_lat = lambda i: i < batch * lat_tiles
    b_fn = lambda i: jnp.where(is_lat(i), i // lat_tiles, i - batch * lat_tiles)
    rb_fn = lambda i: jnp.where(is_lat(i), i % lat_tiles, lat_tiles)
    geom = (batch, n_tok, b_fn, rb_fn)
    mm_mod_row = lambda i: mod_row(i * (MM_TM // ROW_TILE))

    mods = _ada_mods(c, c_ctx, ada_w, ada_b)
    cos_b, sin_b = _rope_tables(n_lat, n_ctx, HEAD_DIM, 1)
    cos_c, sin_c = _rope_tables(n_lat, n_ctx, C_ROPE, LANES // C_ROPE)

    lat_mm_tiles = t_lat // MM_TM
    streams = [(x.reshape(t_lat, d), 0), (ctx.reshape(batch * n_ctx, d), lat_mm_tiles)]
    for l in range(depth):
        need_ctx = l < depth - 1
        t_rows = t_all if need_ctx else t_lat
        w_in_l = jnp.pad(w_in[l].astype(BF16), ((0, 0), (0, IN_COLS_PAD - w_in.shape[2])))

        proj = None
        for rows, tile0 in streams:
            proj = _in_proj(rows, norm1_w[l], mods, l, mm_mod_row, w_in_l, tile0, t_all, proj)
        a_out = _gmlp(proj, t_rows, gm_v_norm[l], gm_w_s[l], gm_b_s[l])
        qkv = _gqa_prep(proj, geom, gqa_q_norm[l], gqa_k_norm[l], cos_b, sin_b)
        b_out = _attention(_score_bound(gqa_q_norm[l], gqa_k_norm[l], HEAD_DIM), qkv, qkv, qkv,
                           (0, B_HEADS, B_HEADS + B_KV_HEADS), B_KV_HEADS, B_GROUP, GQA_TQ,
                           n_lat, n_ctx, need_ctx, t_rows, "gqa_attn")
        qc, kc, vc = _mla_prep(proj, geom, mla_q_a_norm[l], mla_w_uq[l], mla_kv_a_norm[l], mla_w_ukv[l],
                               mla_q_norm[l], mla_k_norm[l], cos_c, sin_c)
        c_out = _attention(_score_bound(mla_q_norm[l], mla_k_norm[l], C_QK), qc, kc, vc,
                           (0, 0, 0), C_HEADS, 1, MLA_TQ, n_lat, n_ctx, need_ctx, t_rows, "mla_attn")
        xs = None
        for rows, tile0 in streams:
            n_tiles = min(rows.shape[0], t_rows - tile0 * MM_TM) // MM_TM
            if n_tiles > 0:
                xs = _out_proj(a_out, b_out, c_out, w_out[l].astype(BF16), rows, n_tiles, mods, l, mm_mod_row,
                               tile0, t_rows, xs)
        xs = _moe(xs, t_rows, norm2_w[l], mods, l, mod_row, router_w[l], router_bias[l],
                  exp_w_gate, exp_w_up, exp_w_down, sh_w_gate[l], sh_w_up[l], sh_w_down[l])
        streams = [(xs, 0)]
    return xs[:t_lat].reshape(batch, n_lat, d)
```

```python
import functools

import jax
import jax.numpy as jnp
from jax import lax
from jax.experimental import pallas as pl
from jax.experimental.pallas import tpu as pltpu

F32 = jnp.float32
BF16 = jnp.bfloat16
U32 = jnp.uint32
I32 = jnp.int32

GRID_W = 64
CHUNK = 128
HEAD_DIM = 128
EPS = 1e-6
ROPE_THETA = 10000.0
A_HEADS = 8
B_HEADS = 12
B_KV_HEADS = 4
B_GROUP = B_HEADS // B_KV_HEADS
C_HEADS = 12
C_Q_RANK = 768
C_KV_RANK = 512
C_NOPE = 128
C_ROPE = 64
C_V = 128
C_QK = C_NOPE + C_ROPE
A_WIDTH = A_HEADS * HEAD_DIM
B_WIDTH = B_HEADS * HEAD_DIM
C_WIDTH = C_HEADS * C_V
N_EXPERTS = 64
TOP_K = 6
N_GROUPS = 8
GROUP_SIZE = N_EXPERTS // N_GROUPS
TOPK_GROUPS = 4
ROUTED_SCALE = 2.5
N_MOD = 6

LANES = 128
SUBLANES = 8
ROW_TILE = 256
MM_TM = 512
MM_TN = 1024
GQA_TQ = 1024
MLA_TQ = 2048
ATT_TK = 512
LOG2E = 1.4426950408889634
BOUND_SLACK = 1.02
SCORE_BOUND_LIMIT = 60.0
EXP_TM = 256
DISP_T = 256
COMB_T = 128
COMB_COLS = 128
IN_COLS_PAD = 6144
VMEM_LIMIT = 56 * 1024 * 1024


def _params(sem, vmem=VMEM_LIMIT):
    return pltpu.CompilerParams(dimension_semantics=sem, vmem_limit_bytes=vmem)


def _rms(x, w):
    return x * lax.rsqrt(jnp.mean(x * x, axis=-1, keepdims=True) + EPS) * w


def _pack_bf16_pair(a, b):
    ua = lax.bitcast_convert_type(a.astype(BF16).astype(F32), U32)
    ub = lax.bitcast_convert_type(b.astype(BF16).astype(F32), U32)
    return (ua >> 16) | ub


def _unpack_bf16_pair(u):
    a = lax.bitcast_convert_type(u << 16, F32)
    b = lax.bitcast_convert_type(u & jnp.uint32(0xFFFF0000), F32)
    return a, b


def _ffn(lo, hi, wg, wu, wd):
    half = lo.shape[1]
    hg = jnp.dot(lo, wg[:half], preferred_element_type=F32) + jnp.dot(hi, wg[half:], preferred_element_type=F32)
    hu = jnp.dot(lo, wu[:half], preferred_element_type=F32) + jnp.dot(hi, wu[half:], preferred_element_type=F32)
    hb = (jax.nn.silu(hg) * hu).astype(BF16)
    return jnp.dot(hb, wd, preferred_element_type=F32)


def _ada_kernel(cb_ref, w_ref, b_ref, o_ref, silu_sc, *, n_rows, d, tn):
    nj = tn // LANES

    @pl.when(jnp.logical_and(pl.program_id(0) == 0, pl.program_id(1) == 0))
    def _():
        cv = cb_ref[...]
        silu_sc[...] = cv * jax.nn.sigmoid(cv)

    def body(kc, accs):
        k0 = pl.multiple_of(kc * SUBLANES, SUBLANES)
        wk = w_ref[0, pl.ds(k0, SUBLANES), :]
        out = []
        for r in range(n_rows):
            sv = silu_sc[r, pl.ds(k0, SUBLANES), :]
            for j in range(nj):
                out.append(accs[r * nj + j] + wk[:, j * LANES:(j + 1) * LANES] * sv)
        return tuple(out)

    init = tuple(jnp.zeros((SUBLANES, LANES), F32) for _ in range(n_rows * nj))
    accs = lax.fori_loop(0, d // SUBLANES, body, init, unroll=8)
    o_ref[0] = jnp.zeros((SUBLANES, tn), F32)
    for r in range(n_rows):
        for j in range(nj):
            o_ref[0, r:r + 1, j * LANES:(j + 1) * LANES] = (
                jnp.sum(accs[r * nj + j], axis=0, keepdims=True) + b_ref[0, :, j * LANES:(j + 1) * LANES])


def _ada_mods(c, c_ctx, ada_w, ada_b):
    n_layers, d, nd = ada_w.shape
    n_rows = c.shape[0] + 1
    assert n_rows <= SUBLANES
    cvec = jnp.concatenate([c, c_ctx[None]], axis=0)
    cb = jnp.broadcast_to(cvec[:, :, None], (n_rows, d, LANES))
    tn = 1024
    out = pl.pallas_call(
        functools.partial(_ada_kernel, n_rows=n_rows, d=d, tn=tn),
        grid=(n_layers, nd // tn),
        in_specs=[pl.BlockSpec((n_rows, d, LANES), lambda l, j: (0, 0, 0)),
                  pl.BlockSpec((1, d, tn), lambda l, j: (l, 0, j)),
                  pl.BlockSpec((1, 1, tn), lambda l, j: (l, 0, j))],
        out_specs=pl.BlockSpec((1, SUBLANES, tn), lambda l, j: (l, 0, j)),
        out_shape=jax.ShapeDtypeStruct((n_layers, SUBLANES, nd), F32),
        scratch_shapes=[pltpu.VMEM((n_rows, d, LANES), F32)],
        compiler_params=_params(("arbitrary", "arbitrary")),
        name="ada_mods",
    )(cb, ada_w, ada_b.reshape(n_layers, 1, nd))
    return out.reshape(n_layers, SUBLANES, N_MOD, 1, d)


def _mod_spec(layer, m, d, row_fn, tn=None, col_fn=None):
    if tn is None:
        return pl.BlockSpec((None, None, None, 1, d), lambda *a: (layer, row_fn(*a), m, 0, 0))
    return pl.BlockSpec((None, None, None, 1, tn), lambda *a: (layer, row_fn(*a), m, 0, col_fn(*a)))


def _in_proj_kernel(x_ref, nw_ref, shift_ref, scale_ref, w_ref, *rest):
    o_ref = rest[-1]
    y = _rms(x_ref[...], nw_ref[...])
    h = (y * (1.0 + scale_ref[...]) + shift_ref[...]).astype(BF16)
    o_ref[...] = jnp.dot(h, w_ref[...], preferred_element_type=F32).astype(o_ref.dtype)


def _alias_prev(in_specs, args, prev):
    if prev is None:
        return {}
    in_specs.append(pl.BlockSpec(memory_space=pl.ANY))
    args.append(prev)
    return {len(args) - 1: 0}


def _in_proj(x, norm_w, mods, layer, mod_row, w, tile0, total_rows, prev):
    m, d = x.shape
    n = w.shape[1]
    in_specs = [pl.BlockSpec((MM_TM, d), lambda j, i: (i, 0)),
                pl.BlockSpec((1, d), lambda j, i: (0, 0)),
                _mod_spec(layer, 0, d, lambda j, i: mod_row(i + tile0)),
                _mod_spec(layer, 1, d, lambda j, i: mod_row(i + tile0)),
                pl.BlockSpec((d, MM_TN), lambda j, i: (0, j))]
    args = [x, norm_w.reshape(1, d), mods, mods, w]
    aliases = _alias_prev(in_specs, args, prev)
    return pl.pallas_call(
        _in_proj_kernel,
        grid=(n // MM_TN, m // MM_TM),
        in_specs=in_specs,
        out_specs=pl.BlockSpec((MM_TM, MM_TN), lambda j, i: (i + tile0, j)),
        out_shape=jax.ShapeDtypeStruct((total_rows, n), BF16),
        input_output_aliases=aliases,
        compiler_params=_params(("arbitrary", "arbitrary")),
        name="in_proj",
    )(*args)


def _gmlp_kernel(p_ref, vn_ref, ws_ref, bt_ref, o_ref):
    for h in range(A_HEADS):
        hs = slice(h * HEAD_DIM, (h + 1) * HEAD_DIM)
        u = jax.nn.gelu(p_ref[:, hs].astype(F32))
        v = jax.nn.gelu(p_ref[:, A_WIDTH + h * HEAD_DIM:A_WIDTH + (h + 1) * HEAD_DIM].astype(F32))
        vb = _rms(v, vn_ref[:, hs]).astype(BF16)
        for c in range(ROW_TILE // CHUNK):
            cs = slice(c * CHUNK, (c + 1) * CHUNK)
            s = jnp.dot(ws_ref[h], vb[cs], preferred_element_type=F32) + bt_ref[:, h:h + 1]
            o_ref[cs, hs] = (u[cs] * s).astype(o_ref.dtype)


def _gmlp(proj, t_rows, v_norm, w_s, b_s):
    return pl.pallas_call(
        _gmlp_kernel,
        grid=(t_rows // ROW_TILE,),
        in_specs=[pl.BlockSpec((ROW_TILE, 2 * A_WIDTH), lambda i: (i, 0)),
                  pl.BlockSpec((1, A_WIDTH), lambda i: (0, 0)),
                  pl.BlockSpec((A_HEADS, CHUNK, CHUNK), lambda i: (0, 0, 0)),
                  pl.BlockSpec((CHUNK, A_HEADS), lambda i: (0, 0))],
        out_specs=pl.BlockSpec((ROW_TILE, A_WIDTH), lambda i: (i, 0)),
        out_shape=jax.ShapeDtypeStruct((t_rows, A_WIDTH), BF16),
        compiler_params=_params(("arbitrary",)),
        name="gmlp",
    )(proj, v_norm.reshape(1, A_WIDTH), w_s.astype(BF16), jnp.transpose(b_s))


def _rope_tables(n_lat, n_ctx, d_rot, reps):
    half = d_rot // 2
    nf = half // 2
    rows = n_lat // GRID_W
    tpos = jnp.arange(n_lat, dtype=I32)
    row = tpos // GRID_W - rows // 2
    col = tpos % GRID_W - GRID_W // 2
    freq = ROPE_THETA ** (-jnp.arange(nf, dtype=F32) / nf)
    ang_r = row.astype(F32)[:, None] * freq
    ang_c = col.astype(F32)[:, None] * freq
    cos = jnp.concatenate([jnp.cos(ang_r), jnp.cos(ang_r), jnp.cos(ang_c), jnp.cos(ang_c)], axis=1)
    sin = jnp.concatenate([-jnp.sin(ang_r), jnp.sin(ang_r), -jnp.sin(ang_c), jnp.sin(ang_c)], axis=1)
    cos = jnp.concatenate([cos, jnp.ones((n_ctx, d_rot), F32)], axis=0)
    sin = jnp.concatenate([sin, jnp.zeros((n_ctx, d_rot), F32)], axis=0)
    return jnp.tile(cos, (1, reps)), jnp.tile(sin, (1, reps))


def _rope(y, cos, sin, nf):
    lane = lax.broadcasted_iota(I32, y.shape, 1)
    partner = jnp.where((lane % (2 * nf)) < nf,
                        pltpu.roll(y, LANES - nf, 1), pltpu.roll(y, nf, 1))
    return y * cos + partner * sin


def _gqa_prep_kernel(p_ref, w_ref, cos_ref, sin_ref, o_ref):
    c = pl.program_id(1)
    heads = p_ref.shape[1] // HEAD_DIM

    @pl.when(c < 4)
    def _():
        for hh in range(heads):
            x = p_ref[:, hh * HEAD_DIM:(hh + 1) * HEAD_DIM].astype(F32)
            y = _rope(_rms(x, w_ref[...]), cos_ref[...], sin_ref[...], HEAD_DIM // 4)
            o_ref[hh] = y.astype(o_ref.dtype)

    @pl.when(c == 4)
    def _():
        for hh in range(heads):
            o_ref[hh] = p_ref[:, hh * HEAD_DIM:(hh + 1) * HEAD_DIM]


def _gqa_prep(proj, geom, q_norm, k_norm, cos, sin):
    batch, n_tok, b_fn, rb_fn = geom
    wvec = jnp.stack([q_norm * (HEAD_DIM ** -0.5 * LOG2E)] * 3 + [k_norm, jnp.ones_like(k_norm)])
    wvec = wvec.reshape(5, 1, HEAD_DIM)
    first = (2 * A_WIDTH) // 512
    return pl.pallas_call(
        _gqa_prep_kernel,
        grid=(proj.shape[0] // ROW_TILE, 5),
        in_specs=[pl.BlockSpec((ROW_TILE, 512), lambda i, c: (i, first + c)),
                  pl.BlockSpec((None, 1, HEAD_DIM), lambda i, c: (c, 0, 0)),
                  pl.BlockSpec((ROW_TILE, HEAD_DIM), lambda i, c: (rb_fn(i), 0)),
                  pl.BlockSpec((ROW_TILE, HEAD_DIM), lambda i, c: (rb_fn(i), 0))],
        out_specs=pl.BlockSpec((None, 4, ROW_TILE, HEAD_DIM), lambda i, c: (b_fn(i), c, rb_fn(i), 0)),
        out_shape=jax.ShapeDtypeStruct((batch, B_HEADS + 2 * B_KV_HEADS, n_tok, HEAD_DIM), BF16),
        compiler_params=_params(("arbitrary", "arbitrary")),
        name="gqa_prep",
    )(proj, wvec, cos, sin)


def _mla_prep_kernel(p_ref, qa_ref, kva_ref, wqn_ref, wqr_ref, wkk_ref, wkv_ref,
                     nqn_ref, nqr_ref, nkn_ref, nkr_ref, cos_ref, sin_ref,
                     qc_ref, kc_ref, vc_ref):
    cq = p_ref[:, :C_Q_RANK].astype(F32)
    ckv = p_ref[:, C_Q_RANK:C_Q_RANK + C_KV_RANK].astype(F32)
    kr = p_ref[:, C_Q_RANK + C_KV_RANK:C_Q_RANK + C_KV_RANK + LANES].astype(F32)
    cqn = _rms(cq, qa_ref[...]).astype(BF16)
    ckvn = _rms(ckv, kva_ref[...]).astype(BF16)
    qn = jnp.dot(cqn, wqn_ref[...], preferred_element_type=F32)
    qr = jnp.dot(cqn, wqr_ref[...], preferred_element_type=F32)
    kn = jnp.dot(ckvn, wkk_ref[...], preferred_element_type=F32)
    vv = jnp.dot(ckvn, wkv_ref[...], preferred_element_type=F32)
    cos = cos_ref[...]
    sin = sin_ref[...]
    lane = lax.broadcasted_iota(I32, kr.shape, 1)
    low = lane < C_ROPE
    inv_qk = 1.0 / C_QK

    kr_rot = _rope(kr * nkr_ref[...], cos, sin, C_ROPE // 4)
    kr_ss = jnp.sum(kr * kr, axis=-1, keepdims=True)
    for h in range(C_HEADS):
        hs = slice(h * C_NOPE, (h + 1) * C_NOPE)
        knh = kn[:, hs]
        r = lax.rsqrt((jnp.sum(knh * knh, axis=-1, keepdims=True) + kr_ss) * inv_qk + EPS)
        kc_ref[h, :, 0:C_NOPE] = (knh * r * nkn_ref[...]).astype(kc_ref.dtype)
        kc_ref[h, :, C_NOPE:C_QK] = (kr_rot * r)[:, 0:C_ROPE].astype(kc_ref.dtype)
        vc_ref[h] = vv[:, h * C_V:(h + 1) * C_V].astype(vc_ref.dtype)

    for pr in range(C_HEADS // 2):
        qrp = qr[:, pr * LANES:(pr + 1) * LANES]
        sq = qrp * qrp
        ss_pair = (jnp.sum(jnp.where(low, sq, 0.0), axis=-1, keepdims=True),
                   jnp.sum(jnp.where(low, 0.0, sq), axis=-1, keepdims=True))
        rs = []
        for e in range(2):
            h = 2 * pr + e
            qnh = qn[:, h * C_NOPE:(h + 1) * C_NOPE]
            r = lax.rsqrt((jnp.sum(qnh * qnh, axis=-1, keepdims=True) + ss_pair[e]) * inv_qk + EPS)
            rs.append(r)
            qc_ref[h, :, 0:C_NOPE] = (qnh * r * nqn_ref[...]).astype(qc_ref.dtype)
        q_rot = _rope(qrp * jnp.where(low, rs[0], rs[1]) * nqr_ref[...], cos, sin, C_ROPE // 4)
        qc_ref[2 * pr, :, C_NOPE:C_QK] = q_rot[:, 0:C_ROPE].astype(qc_ref.dtype)
        qc_ref[2 * pr + 1, :, C_NOPE:C_QK] = q_rot[:, C_ROPE:2 * C_ROPE].astype(qc_ref.dtype)


def _mla_prep(proj, geom, q_a_norm, w_uq, kv_a_norm, w_ukv, q_norm, k_norm, cos, sin):
    batch, n_tok, b_fn, rb_fn = geom
    first = (2 * A_WIDTH + B_WIDTH + 2 * B_KV_HEADS * HEAD_DIM)
    width = IN_COLS_PAD - first
    assert first % width == 0
    wq = w_uq.reshape(C_Q_RANK, C_HEADS, C_QK)
    wqn = wq[:, :, :C_NOPE].reshape(C_Q_RANK, C_HEADS * C_NOPE).astype(BF16)
    wqr = wq[:, :, C_NOPE:].reshape(C_Q_RANK, C_HEADS * C_ROPE).astype(BF16)
    wkv = w_ukv.reshape(C_KV_RANK, C_HEADS, C_NOPE + C_V)
    wkk = wkv[:, :, :C_NOPE].reshape(C_KV_RANK, C_HEADS * C_NOPE).astype(BF16)
    wkvv = wkv[:, :, C_NOPE:].reshape(C_KV_RANK, C_HEADS * C_V).astype(BF16)
    qs = C_QK ** -0.5 * LOG2E
    nqn = (q_norm[:C_NOPE] * qs).reshape(1, C_NOPE)
    nqr = jnp.tile(q_norm[C_NOPE:] * qs, 2).reshape(1, LANES)
    nkn = k_norm[:C_NOPE].reshape(1, C_NOPE)
    nkr = jnp.tile(k_norm[C_NOPE:], 2).reshape(1, LANES)
    const = lambda shape: pl.BlockSpec(shape, lambda i: tuple(0 for _ in shape))
    head_spec = lambda dd: pl.BlockSpec((None, C_HEADS, ROW_TILE, dd), lambda i: (b_fn(i), 0, rb_fn(i), 0))
    return pl.pallas_call(
        _mla_prep_kernel,
        grid=(proj.shape[0] // ROW_TILE,),
        in_specs=[pl.BlockSpec((ROW_TILE, width), lambda i: (i, first // width)),
                  const((1, C_Q_RANK)), const((1, C_KV_RANK)),
                  const(wqn.shape), const(wqr.shape), const(wkk.shape), const(wkvv.shape),
                  const((1, C_NOPE)), const((1, LANES)), const((1, C_NOPE)), const((1, LANES)),
                  pl.BlockSpec((ROW_TILE, LANES), lambda i: (rb_fn(i), 0)),
                  pl.BlockSpec((ROW_TILE, LANES), lambda i: (rb_fn(i), 0))],
        out_specs=[head_spec(C_QK), head_spec(C_QK), head_spec(C_V)],
        out_shape=[jax.ShapeDtypeStruct((batch, C_HEADS, n_tok, C_QK), BF16),
                   jax.ShapeDtypeStruct((batch, C_HEADS, n_tok, C_QK), BF16),
                   jax.ShapeDtypeStruct((batch, C_HEADS, n_tok, C_V), BF16)],
        compiler_params=_params(("arbitrary",)),
        name="mla_prep",
    )(proj, q_a_norm.reshape(1, C_Q_RANK), kv_a_norm.reshape(1, C_KV_RANK), wqn, wqr, wkk, wkvv,
      nqn, nqr, nkn, nkr, cos, sin)


def _attn_kernel(bound_ref, q_ref, k_ref, v_ref, *rest, group, n_lat):
    use_bound = bound_ref[0] <= SCORE_BOUND_LIMIT
    for bounded, pred in ((True, use_bound), (False, jnp.logical_not(use_bound))):
        pl.when(pred)(functools.partial(_attn_body, bound_ref, q_ref, k_ref, v_ref, rest,
                                        group=group, n_lat=n_lat, bounded=bounded))


def _attn_body(bound_ref, q_ref, k_ref, v_ref, rest, *, group, n_lat, bounded):
    o_ref, m_sc, l_sc, acc_sc = rest[-4:]
    tq, dk = q_ref.shape[1], q_ref.shape[2]
    dv = v_ref.shape[1]
    n_tail = k_ref.shape[0] - n_lat
    q = q_ref[...].reshape(group * tq, dk)
    l_sc[...] = jnp.zeros(l_sc.shape, F32)
    acc_sc[...] = jnp.zeros(acc_sc.shape, F32)
    if not bounded:
        m_sc[...] = jnp.full(m_sc.shape, -jnp.inf, F32)

    def lane_fold(p):
        out = p[:, 0:LANES]
        for j in range(1, p.shape[1] // LANES):
            out = out + p[:, j * LANES:(j + 1) * LANES]
        return out

    def chunk(start, size):
        k = k_ref[pl.ds(start, size), :]
        v = v_ref[pl.ds(start, size), :]
        s = lax.dot_general(q, k, (((1,), (1,)), ((), ())), preferred_element_type=F32)
        if bounded:
            p = jnp.exp2(s - bound_ref[0])
            l_sc[...] += lane_fold(p)
            acc_sc[...] += jnp.dot(p.astype(BF16), v, preferred_element_type=F32)
        else:
            m_prev = m_sc[...]
            m_new = jnp.maximum(m_prev, jnp.max(s, axis=-1, keepdims=True))
            alpha = jnp.exp2(m_prev - m_new)
            p = jnp.exp2(s - m_new[:, 0:1])
            l_sc[...] = alpha * l_sc[...] + lane_fold(p)
            acc_sc[...] = alpha[:, 0:1] * acc_sc[...] + jnp.dot(p.astype(BF16), v, preferred_element_type=F32)
            m_sc[...] = m_new

    if n_lat:
        def body(c, carry):
            chunk(pl.multiple_of(c * ATT_TK, ATT_TK), ATT_TK)
            return carry
        lax.fori_loop(0, n_lat // ATT_TK, body, 0, unroll=2)
    chunk(n_lat, n_tail)
    o = acc_sc[...] / jnp.sum(l_sc[...], axis=-1, keepdims=True)
    for g in range(group):
        o_ref[:, g * dv:(g + 1) * dv] = o[g * tq:(g + 1) * tq].astype(o_ref.dtype)


def _attention_call(bound, q_arr, k_arr, v_arr, prev_out, heads, n_kv, group, tq, n_lat, n_ctx, ctx_only,
                    t_rows, name):
    q_head0, k_head0, v_head0 = heads
    batch, _, n_tok, dk = q_arr.shape
    dv = v_arr.shape[-1]
    m = group * tq
    if ctx_only:
        assert tq == n_ctx and n_lat % n_ctx == 0
        nq, q0, key_rows, key_blk = 1, n_lat // tq, n_ctx, n_lat // n_ctx
        out_row = lambda b, qi: batch * (n_lat // tq) + b
    else:
        assert n_lat % tq == 0 and n_lat % ATT_TK == 0
        nq, q0, key_rows, key_blk = n_lat // tq, 0, n_tok, 0
        out_row = lambda b, qi: b * nq + qi
    in_specs = [pl.BlockSpec((None, group, tq, dk), lambda b, g, qi, *_: (b, q_head0 // group + g, q0 + qi, 0)),
                pl.BlockSpec((None, None, key_rows, dk), lambda b, g, qi, *_: (b, k_head0 + g, key_blk, 0)),
                pl.BlockSpec((None, None, key_rows, dv), lambda b, g, qi, *_: (b, v_head0 + g, key_blk, 0))]
    args = [bound, q_arr, k_arr, v_arr]
    aliases = {}
    if prev_out is not None:
        in_specs.append(pl.BlockSpec(memory_space=pl.ANY))
        args.append(prev_out)
        aliases = {len(args) - 1: 0}
    grid_spec = pltpu.PrefetchScalarGridSpec(
        num_scalar_prefetch=1,
        grid=(batch, n_kv, nq),
        in_specs=in_specs,
        out_specs=pl.BlockSpec((tq, group * dv), lambda b, g, qi, *_: (out_row(b, qi), g)),
        scratch_shapes=[pltpu.VMEM((m, LANES), F32), pltpu.VMEM((m, LANES), F32), pltpu.VMEM((m, dv), F32)],
    )
    return pl.pallas_call(
        functools.partial(_attn_kernel, group=group, n_lat=0 if ctx_only else n_lat),
        grid_spec=grid_spec,
        out_shape=jax.ShapeDtypeStruct((t_rows, n_kv * group * dv), BF16),
        input_output_aliases=aliases,
        compiler_params=_params(("arbitrary", "arbitrary", "arbitrary")),
        name=name,
    )(*args)


def _attention(bound, q_arr, k_arr, v_arr, heads, n_kv, group, tq, n_lat, n_ctx, with_ctx, t_rows, name):
    tq = min(tq, n_lat)
    out = _attention_call(bound, q_arr, k_arr, v_arr, None, heads, n_kv, group, tq, n_lat, n_ctx,
                          False, t_rows, name)
    if with_ctx:
        out = _attention_call(bound, q_arr, k_arr, v_arr, out, heads, n_kv, group, n_ctx, n_lat, n_ctx,
                              True, t_rows, name + "_ctx")
    return out


def _score_bound(q_norm, k_norm, dk):
    b = dk ** 0.5 * jnp.max(jnp.abs(q_norm)) * jnp.max(jnp.abs(k_norm)) * (LOG2E * BOUND_SLACK)
    return b.reshape(1).astype(F32)


def _wout_kernel(a_ref, b_ref, c_ref, w_ref, x_ref, g_ref, *rest):
    o_ref = rest[-1]
    acc = jnp.dot(a_ref[...], w_ref[0:A_WIDTH, :], preferred_element_type=F32)
    acc += jnp.dot(b_ref[...], w_ref[A_WIDTH:A_WIDTH + B_WIDTH, :], preferred_element_type=F32)
    acc += jnp.dot(c_ref[...], w_ref[A_WIDTH + B_WIDTH:, :], preferred_element_type=F32)
    o_ref[...] = x_ref[...] + g_ref[...] * acc


def _out_proj(a_out, b_out, c_out, w, x, n_tiles, mods, layer, mod_row, tile0, total_rows, prev):
    d = w.shape[1]
    in_specs = [pl.BlockSpec((MM_TM, A_WIDTH), lambda j, i: (i + tile0, 0)),
                pl.BlockSpec((MM_TM, B_WIDTH), lambda j, i: (i + tile0, 0)),
                pl.BlockSpec((MM_TM, C_WIDTH), lambda j, i: (i + tile0, 0)),
                pl.BlockSpec((w.shape[0], MM_TN), lambda j, i: (0, j)),
                pl.BlockSpec((MM_TM, MM_TN), lambda j, i: (i, j)),
                _mod_spec(layer, 2, d, lambda j, i: mod_row(i + tile0), MM_TN, lambda j, i: j)]
    args = [a_out, b_out, c_out, w, x, mods]
    aliases = _alias_prev(in_specs, args, prev)
    return pl.pallas_call(
        _wout_kernel,
        grid=(d // MM_TN, n_tiles),
        in_specs=in_specs,
        out_specs=pl.BlockSpec((MM_TM, MM_TN), lambda j, i: (i + tile0, j)),
        out_shape=jax.ShapeDtypeStruct((total_rows, d), F32),
        input_output_aliases=aliases,
        compiler_params=_params(("arbitrary", "arbitrary")),
        name="out_proj",
    )(*args)


def _first_index(hit_val, x, iota, big):
    return jnp.min(jnp.where(x == hit_val, iota, big), axis=0, keepdims=True)


def _router_kernel(x_ref, nw_ref, shift_ref, scale_ref, rwt_ref, bias_ref,
                   hp_ref, eidx_ref, gate_ref, rank_ref, cnt_ref, carry_sc):
    i = pl.program_id(0)
    tm, d = x_ref.shape
    h = _rms(x_ref[...], nw_ref[...]) * (1.0 + scale_ref[...]) + shift_ref[...]
    hp_ref[...] = _pack_bf16_pair(h[:, :d // 2], h[:, d // 2:])

    logits = lax.dot_general(rwt_ref[...], h, (((1,), (1,)), ((), ())),
                             precision=lax.Precision.HIGHEST, preferred_element_type=F32)
    scores = jax.nn.sigmoid(logits)
    sel = scores + bias_ref[:, 0:1]
    neg = jnp.float32(-jnp.inf)

    iota_n = lax.broadcasted_iota(I32, (N_GROUPS, tm), 0).astype(F32)
    iota_e = lax.broadcasted_iota(I32, (N_EXPERTS, tm), 0).astype(F32)

    gscore = jnp.zeros((N_GROUPS, tm), F32)
    for g in range(N_GROUPS):
        blk = sel[g * GROUP_SIZE:(g + 1) * GROUP_SIZE]
        m1 = jnp.max(blk, axis=0, keepdims=True)
        first = _first_index(m1, blk, iota_n, float(GROUP_SIZE))
        m2 = jnp.max(jnp.where(iota_n == first, neg, blk), axis=0, keepdims=True)
        gscore = jnp.where(iota_n == float(g), m1 + m2, gscore)

    gsel = jnp.zeros((N_GROUPS, tm), F32)
    work = gscore
    for _ in range(TOPK_GROUPS):
        m = jnp.max(work, axis=0, keepdims=True)
        hit = iota_n == _first_index(m, work, iota_n, float(N_GROUPS))
        gsel = jnp.where(hit, 1.0, gsel)
        work = jnp.where(hit, neg, work)
    emask = jnp.concatenate(
        [jnp.broadcast_to(gsel[g:g + 1], (GROUP_SIZE, tm)) for g in range(N_GROUPS)], axis=0)

    work = jnp.where(emask > 0.5, sel, neg)
    picks = []
    onehot = jnp.zeros((N_EXPERTS, tm), F32)
    for _ in range(TOP_K):
        m = jnp.max(work, axis=0, keepdims=True)
        pick = _first_index(m, work, iota_e, float(N_EXPERTS))
        picks.append(pick)
        hit = iota_e == pick
        onehot = jnp.where(hit, 1.0, onehot)
        work = jnp.where(hit, neg, work)

    w_un = onehot * scores
    gates = w_un / jnp.sum(w_un, axis=0, keepdims=True) * ROUTED_SCALE

    @pl.when(i == 0)
    def _():
        carry_sc[...] = jnp.zeros(carry_sc.shape, F32)

    before = (lax.broadcasted_iota(I32, (tm, tm), 0) < lax.broadcasted_iota(I32, (tm, tm), 1)).astype(BF16)
    rank = carry_sc[:, 0:1] + jnp.dot(onehot.astype(BF16), before, preferred_element_type=F32)
    carry_sc[...] = carry_sc[...] + jnp.sum(onehot, axis=1, keepdims=True)
    cnt_ref[...] = carry_sc[...]

    eidx_ref[...] = jnp.zeros(eidx_ref.shape, I32)
    gate_ref[...] = jnp.zeros(gate_ref.shape, F32)
    rank_ref[...] = jnp.zeros(rank_ref.shape, I32)
    for k in range(TOP_K):
        hit = iota_e == picks[k]
        eidx_ref[k:k + 1, :] = picks[k].astype(I32)
        gate_ref[k:k + 1, :] = jnp.sum(jnp.where(hit, gates, 0.0), axis=0, keepdims=True)
        rank_ref[k:k + 1, :] = jnp.sum(jnp.where(hit, rank, 0.0), axis=0, keepdims=True).astype(I32)


def _norm_router(x, t_rows, norm_w, mods, layer, mod_row, router_w, router_bias):
    d = x.shape[1]
    col = lambda dt: (pl.BlockSpec((SUBLANES, ROW_TILE), lambda i: (0, i)),
                      jax.ShapeDtypeStruct((SUBLANES, t_rows), dt))
    specs, shapes = zip(col(I32), col(F32), col(I32))
    return pl.pallas_call(
        _router_kernel,
        grid=(t_rows // ROW_TILE,),
        in_specs=[pl.BlockSpec((ROW_TILE, d), lambda i: (i, 0)),
                  pl.BlockSpec((1, d), lambda i: (0, 0)),
                  _mod_spec(layer, 3, d, mod_row),
                  _mod_spec(layer, 4, d, mod_row),
                  pl.BlockSpec((N_EXPERTS, d), lambda i: (0, 0)),
                  pl.BlockSpec((N_EXPERTS, LANES), lambda i: (0, 0))],
        out_specs=[pl.BlockSpec((ROW_TILE, d // 2), lambda i: (i, 0)), *specs,
                   pl.BlockSpec((N_EXPERTS, LANES), lambda i: (0, 0))],
        out_shape=[jax.ShapeDtypeStruct((t_rows, d // 2), U32), *shapes,
                   jax.ShapeDtypeStruct((N_EXPERTS, LANES), F32)],
        scratch_shapes=[pltpu.VMEM((N_EXPERTS, LANES), F32)],
        compiler_params=_params(("arbitrary",)),
        name="norm_router",
    )(x, norm_w.reshape(1, d), mods, mods, jnp.transpose(router_w),
      jnp.broadcast_to(router_bias[:, None], (N_EXPERTS, LANES)))


def _row_copy(src, src_row, dst, dst_row, sem):
    return pltpu.make_async_copy(src.at[pl.ds(src_row, 1)], dst.at[pl.ds(dst_row, 1)], sem)


_FILL_PIECES = tuple(1 << b for b in range(EXP_TM.bit_length() - 2, -1, -1))


def _zero_fill(pstart_ref, plen_ref, tail_ref, xs_hbm, zero_sc, sem_z):
    zero_sc[...] = jnp.zeros(zero_sc.shape, zero_sc.dtype)
    n_tail = (xs_hbm.shape[0] - tail_ref[0]) // _FILL_PIECES[0]

    def pad_copies(e, wait):
        start = pstart_ref[e]
        head = (-start) & (SUBLANES - 1)
        for r in range(SUBLANES - 1):
            cp = pltpu.make_async_copy(zero_sc.at[pl.ds(0, 1)], xs_hbm.at[pl.ds(start + r, 1)], sem_z)

            @pl.when(r < head)
            def _():
                cp.wait() if wait else cp.start()

        off = start + head
        rem = plen_ref[e] - head
        for piece in _FILL_PIECES:
            if piece < SUBLANES:
                break
            on = (rem & piece) != 0
            cp = pltpu.make_async_copy(zero_sc.at[pl.ds(0, piece)],
                                       xs_hbm.at[pl.ds(pl.multiple_of(off, SUBLANES), piece)], sem_z)

            @pl.when(on)
            def _():
                cp.wait() if wait else cp.start()

            off = off + jnp.where(on, piece, 0)

    def tail_copy(j, wait):
        row = pl.multiple_of(tail_ref[0] + j * _FILL_PIECES[0], _FILL_PIECES[0])
        cp = pltpu.make_async_copy(zero_sc, xs_hbm.at[pl.ds(row, _FILL_PIECES[0])], sem_z)
        cp.wait() if wait else cp.start()

    for wait in (False, True):
        lax.fori_loop(0, N_EXPERTS, lambda e, c: (pad_copies(e, wait), c)[1], 0)
        lax.fori_loop(0, n_tail, lambda j, c: (tail_copy(j, wait), c)[1], 0)


def _dispatch_kernel(pstart_ref, plen_ref, tail_ref, slot_hbm, h_ref, xs_hbm, slot_sm, zero_sc, sem_s, sem, sem_z):
    i = pl.program_id(0)
    n = slot_sm.shape[0]
    cp = pltpu.make_async_copy(slot_hbm.at[pl.ds(pl.multiple_of(i * n, n), n)], slot_sm, sem_s)
    cp.start()

    @pl.when(i == 0)
    def _():
        _zero_fill(pstart_ref, plen_ref, tail_ref, xs_hbm, zero_sc, sem_z)

    cp.wait()

    def issue(t, carry):
        for k in range(TOP_K):
            _row_copy(h_ref, t, xs_hbm, slot_sm[t * SUBLANES + k], sem).start(priority=1)
        return carry

    lax.fori_loop(0, DISP_T, issue, 0)

    def drain(t, carry):
        for k in range(TOP_K):
            _row_copy(h_ref, 0, xs_hbm, 0, sem).wait()
        return carry

    lax.fori_loop(0, DISP_T, drain, 0)


def _dispatch(hp, slots, pad_start, pad_len, tail_start, n_slots):
    t, w = hp.shape
    grid_spec = pltpu.PrefetchScalarGridSpec(
        num_scalar_prefetch=3,
        grid=(t // DISP_T,),
        in_specs=[pl.BlockSpec(memory_space=pl.ANY), pl.BlockSpec((DISP_T, w), lambda i, *_: (i, 0))],
        out_specs=pl.BlockSpec(memory_space=pl.ANY),
        scratch_shapes=[pltpu.SMEM((DISP_T * SUBLANES,), I32),
                        pltpu.VMEM((_FILL_PIECES[0], w), U32),
                        pltpu.SemaphoreType.DMA(()), pltpu.SemaphoreType.DMA(()), pltpu.SemaphoreType.DMA(())],
    )
    return pl.pallas_call(
        _dispatch_kernel,
        grid_spec=grid_spec,
        out_shape=jax.ShapeDtypeStruct((n_slots, w), U32),
        compiler_params=_params(("arbitrary",)),
        name="moe_dispatch",
    )(pad_start, pad_len, tail_start, slots, hp)


def _expert_kernel(te_ref, par_ref, nxt_ref, nu_ref, xs_ref, wg_hbm, wu_hbm, wd_hbm, ys_ref,
                   wg_f, wu_f, wd_f, wg_sc, wu_sc, wd_sc, sem, *, layer):
    i = pl.program_id(0)
    expert = te_ref[i]
    par = par_ref[i]

    def weight_copies(e, slot):
        return (pltpu.make_async_copy(wg_hbm.at[layer, e], wg_f.at[slot], sem.at[slot, 0]),
                pltpu.make_async_copy(wu_hbm.at[layer, e], wu_f.at[slot], sem.at[slot, 1]),
                pltpu.make_async_copy(wd_hbm.at[layer, e], wd_f.at[slot], sem.at[slot, 2]))

    @pl.when(i == 0)
    def _():
        for cp in weight_copies(expert, par):
            cp.start()

    @pl.when(jnp.logical_or(i == 0, expert != te_ref[jnp.maximum(i - 1, 0)]))
    def _():
        for cp in weight_copies(expert, par):
            cp.wait()

        @pl.when(nxt_ref[i] >= 0)
        def _():
            for cp in weight_copies(nxt_ref[i], 1 - par):
                cp.start(priority=1)

        wg_sc[...] = wg_f[par].astype(BF16)
        wu_sc[...] = wu_f[par].astype(BF16)
        wd_sc[...] = wd_f[par].astype(BF16)

    @pl.when(i < nu_ref[0])
    def _():
        lo, hi = _unpack_bf16_pair(xs_ref[...])
        y = _ffn(lo.astype(BF16), hi.astype(BF16), wg_sc[...], wu_sc[...], wd_sc[...])
        half = y.shape[1] // 2
        ys_ref[...] = _pack_bf16_pair(y[:, :half], y[:, half:])

    @pl.when(i >= nu_ref[0])
    def _():
        ys_ref[...] = jnp.zeros(ys_ref.shape, ys_ref.dtype)


def _experts(xs, tile_expert, tile_slot, tile_next, n_used, w_gate, w_up, w_down, layer):
    _, _, d, f = w_gate.shape
    n_tiles = xs.shape[0] // EXP_TM
    hbm = pl.BlockSpec(memory_space=pl.ANY)
    tile = (EXP_TM, xs.shape[1])
    grid_spec = pltpu.PrefetchScalarGridSpec(
        num_scalar_prefetch=4,
        grid=(n_tiles,),
        in_specs=[pl.BlockSpec(tile, lambda i, te, par, nxt, nu: (jnp.minimum(i, nu[0] - 1), 0)),
                  hbm, hbm, hbm],
        out_specs=pl.BlockSpec(tile, lambda i, te, par, nxt, nu: (i, 0)),
        scratch_shapes=[pltpu.VMEM((2, d, f), F32), pltpu.VMEM((2, d, f), F32), pltpu.VMEM((2, f, d), F32),
                        pltpu.VMEM((d, f), BF16), pltpu.VMEM((d, f), BF16), pltpu.VMEM((f, d), BF16),
                        pltpu.SemaphoreType.DMA((2, 3))],
    )
    return pl.pallas_call(
        functools.partial(_expert_kernel, layer=layer),
        grid_spec=grid_spec,
        out_shape=jax.ShapeDtypeStruct(xs.shape, U32),
        compiler_params=_params(("arbitrary",)),
        name="moe_experts",
    )(tile_expert, tile_slot, tile_next, n_used, xs, w_gate, w_up, w_down)


def _combine_kernel(slot_hbm, ys_hbm, hp_ref, gate_ref, sg_ref, su_ref, sd_ref, x_ref, g5_ref,
                    o_ref, slot_sm, buf, sem_s, sem):
    i = pl.program_id(0)
    n_steps = pl.num_programs(0)
    n = COMB_T * SUBLANES
    half = x_ref.shape[1] // 2

    def load_slots(h, half_index):
        cp = pltpu.make_async_copy(slot_hbm.at[pl.ds(pl.multiple_of(half_index * n, n), n)],
                                   slot_sm.at[h], sem_s.at[h])
        cp.start()
        cp.wait()

    def issue(h, tokens):
        for t in tokens:
            for k in range(TOP_K):
                _row_copy(ys_hbm, slot_sm[h, t * SUBLANES + k], buf.at[h, k], t, sem.at[h]).start(priority=1)

    def drain(h):
        def body(t, carry):
            for k in range(TOP_K):
                _row_copy(ys_hbm, 0, buf.at[h, k], 0, sem.at[h]).wait()
            return carry

        lax.fori_loop(0, COMB_T, body, 0)

    def finish(h, other):
        rows = slice(h * COMB_T, (h + 1) * COMB_T)
        issue(other, range(0, COMB_T // 2))
        lo, hi = _unpack_bf16_pair(hp_ref[rows, :])
        y = _ffn(lo.astype(BF16), hi.astype(BF16), sg_ref[...], su_ref[...], sd_ref[...])
        drain(h)
        issue(other, range(COMB_T // 2, COMB_T))
        gates = [gate_ref[rows, k:k + 1] for k in range(TOP_K)]
        for c in range(half // COMB_COLS):
            lo_cols = slice(c * COMB_COLS, (c + 1) * COMB_COLS)
            hi_cols = slice(half + c * COMB_COLS, half + (c + 1) * COMB_COLS)
            r_lo = y[:, lo_cols]
            r_hi = y[:, hi_cols]
            for k in range(TOP_K):
                e_lo, e_hi = _unpack_bf16_pair(buf[h, k, :, lo_cols])
                r_lo = r_lo + gates[k] * e_lo
                r_hi = r_hi + gates[k] * e_hi
            o_ref[rows, lo_cols] = x_ref[rows, lo_cols] + g5_ref[:, lo_cols] * r_lo
            o_ref[rows, hi_cols] = x_ref[rows, hi_cols] + g5_ref[:, hi_cols] * r_hi

    @pl.when(i == 0)
    def _():
        load_slots(0, 0)
        lax.fori_loop(0, COMB_T, lambda t, c: (issue(0, (t,)), c)[1], 0)

    load_slots(1, 2 * i + 1)
    finish(0, 1)
    load_slots(0, jnp.minimum(2 * i + 2, 2 * n_steps - 2))
    finish(1, 0)

    @pl.when(i == n_steps - 1)
    def _():
        drain(0)


def _combine(ys, hp, slots, gates, sh_gate, sh_up, sh_down, x, mods, layer, t_rows, mod_row):
    d = x.shape[1]
    w = hp.shape[1]
    f = sh_gate.shape[1]
    rows = 2 * COMB_T
    return pl.pallas_call(
        _combine_kernel,
        grid=(t_rows // rows,),
        in_specs=[pl.BlockSpec(memory_space=pl.ANY), pl.BlockSpec(memory_space=pl.ANY),
                  pl.BlockSpec((rows, w), lambda i: (i, 0)),
                  pl.BlockSpec((rows, SUBLANES), lambda i: (i, 0)),
                  pl.BlockSpec((d, f), lambda i: (0, 0)),
                  pl.BlockSpec((d, f), lambda i: (0, 0)),
                  pl.BlockSpec((f, d), lambda i: (0, 0)),
                  pl.BlockSpec((rows, d), lambda i: (i, 0)),
                  _mod_spec(layer, 5, d, lambda i: mod_row(i * rows // ROW_TILE))],
        out_specs=pl.BlockSpec((rows, d), lambda i: (i, 0)),
        out_shape=jax.ShapeDtypeStruct((t_rows, d), F32),
        scratch_shapes=[pltpu.SMEM((2, COMB_T * SUBLANES), I32),
                        pltpu.VMEM((2, TOP_K, COMB_T, w), U32),
                        pltpu.SemaphoreType.DMA((2,)), pltpu.SemaphoreType.DMA((2,))],
        compiler_params=_params(("arbitrary",)),
        name="moe_combine",
    )(slots, ys, hp, gates, sh_gate.astype(BF16), sh_up.astype(BF16), sh_down.astype(BF16), x, mods)


def _moe(x, t_rows, norm_w, mods, layer, mod_row, router_w, router_bias,
         w_gate, w_up, w_down, sh_gate, sh_up, sh_down):
    hp, eidx, gate, rank, cnt = _norm_router(x, t_rows, norm_w, mods, layer, mod_row, router_w, router_bias)

    n_tiles = (t_rows * TOP_K + N_EXPERTS * (EXP_TM - 1) + EXP_TM - 1) // EXP_TM
    counts = cnt[:, 0].astype(I32)
    padded = (counts + EXP_TM - 1) // EXP_TM * EXP_TM
    ends = jnp.cumsum(padded)
    base = ends - padded
    experts = jnp.arange(N_EXPERTS, dtype=I32)
    base_of = jnp.sum(jnp.where(eidx[:, :, None] == experts, base, 0), axis=-1)
    slots = jnp.transpose(base_of + rank).reshape(-1)
    n_used = (ends[-1] // EXP_TM).astype(I32)
    tile_start = jnp.arange(n_tiles, dtype=I32) * EXP_TM
    last_start = (n_used - 1) * EXP_TM
    te = jnp.sum(jnp.minimum(tile_start, last_start)[:, None] >= ends[None, :], axis=1).astype(I32)
    te = jnp.minimum(te, N_EXPERTS - 1)
    used = counts > 0
    slot_of = ((jnp.cumsum(used.astype(I32)) - 1) % 2).astype(I32)
    later = jnp.logical_and(experts[None, :] > experts[:, None], used[None, :])
    next_of = jnp.min(jnp.where(later, experts[None, :], N_EXPERTS), axis=1)
    next_of = jnp.where(next_of == N_EXPERTS, -1, next_of).astype(I32)

    xs = _dispatch(hp, slots, base + counts, padded - counts, ends[-1:], n_tiles * EXP_TM)
    ys = _experts(xs, te, slot_of[te], next_of[te], n_used.reshape(1), w_gate, w_up, w_down, layer)
    return _combine(ys, hp, slots, jnp.transpose(gate), sh_gate, sh_up, sh_down, x, mods, layer, t_rows, mod_row)


def kernel(x, c, ctx, c_ctx, ada_w, ada_b, norm1_w, norm2_w, w_in, gm_v_norm, gm_w_s, gm_b_s, gqa_q_norm, gqa_k_norm, mla_q_a_norm, mla_w_uq, mla_kv_a_norm, mla_w_ukv, mla_q_norm, mla_k_norm, w_out, router_w, router_bias, exp_w_gate, exp_w_up, exp_w_down, sh_w_gate, sh_w_up, sh_w_down):
    batch, n_lat, d = x.shape
    n_ctx = ctx.shape[1]
    depth = ada_w.shape[0]
    assert n_ctx == ROW_TILE and n_lat % MM_TM == 0 and (batch * n_ctx) % MM_TM == 0
    t_lat = batch * n_lat
    t_all = t_lat + batch * n_ctx
    n_tok = n_lat + n_ctx
    lat_tiles = n_lat // ROW_TILE

    mod_row = lambda i: jnp.minimum(i // lat_tiles, batch)
    is_lat = lambda i: i < batch * lat_tiles
    b_fn = lambda i: jnp.where(is_lat(i), i // lat_tiles, i - batch * lat_tiles)
    rb_fn = lambda i: jnp.where(is_lat(i), i % lat_tiles, lat_tiles)
    geom = (batch, n_tok, b_fn, rb_fn)
    mm_mod_row = lambda i: mod_row(i * (MM_TM // ROW_TILE))

    mods = _ada_mods(c, c_ctx, ada_w, ada_b)
    cos_b, sin_b = _rope_tables(n_lat, n_ctx, HEAD_DIM, 1)
    cos_c, sin_c = _rope_tables(n_lat, n_ctx, C_ROPE, LANES // C_ROPE)

    lat_mm_tiles = t_lat // MM_TM
    streams = [(x.reshape(t_lat, d), 0), (ctx.reshape(batch * n_ctx, d), lat_mm_tiles)]
    for l in range(depth):
        need_ctx = l < depth - 1
        t_rows = t_all if need_ctx else t_lat
        w_in_l = jnp.pad(w_in[l].astype(BF16), ((0, 0), (0, IN_COLS_PAD - w_in.shape[2])))

        proj = None
        for rows, tile0 in streams:
            proj = _in_proj(rows, norm1_w[l], mods, l, mm_mod_row, w_in_l, tile0, t_all, proj)
        a_out = _gmlp(proj, t_rows, gm_v_norm[l], gm_w_s[l], gm_b_s[l])
        qkv = _gqa_prep(proj, geom, gqa_q_norm[l], gqa_k_norm[l], cos_b, sin_b)
        b_out = _attention(_score_bound(gqa_q_norm[l], gqa_k_norm[l], HEAD_DIM), qkv, qkv, qkv,
                           (0, B_HEADS, B_HEADS + B_KV_HEADS), B_KV_HEADS, B_GROUP, GQA_TQ,
                           n_lat, n_ctx, need_ctx, t_rows, "gqa_attn")
        qc, kc, vc = _mla_prep(proj, geom, mla_q_a_norm[l], mla_w_uq[l], mla_kv_a_norm[l], mla_w_ukv[l],
                               mla_q_norm[l], mla_k_norm[l], cos_c, sin_c)
        c_out = _attention(_score_bound(mla_q_norm[l], mla_k_norm[l], C_QK), qc, kc, vc,
                           (0, 0, 0), C_HEADS, 1, MLA_TQ, n_lat, n_ctx, need_ctx, t_rows, "mla_attn")
        xs = None
        for rows, tile0 in streams:
            n_tiles = min(rows.shape[0], t_rows - tile0 * MM_TM) // MM_TM
            if n_tiles > 0:
                xs = _out_proj(a_out, b_out, c_out, w_out[l].astype(BF16), rows, n_tiles, mods, l, mm_mod_row,
                               tile0, t_rows, xs)
        xs = _moe(xs, t_rows, norm2_w[l], mods, l, mod_row, router_w[l], router_bias[l],
                  exp_w_gate, exp_w_up, exp_w_down, sh_w_gate[l], sh_w_up[l], sh_w_down[l])
        streams = [(xs, 0)]
    return xs[:t_lat].reshape(batch, n_lat, d)
```
